```python
import math
import jax, jax.numpy as jnp
from jax import lax
import numpy as np

D_MODEL = 1024
BATCH = 8
SEQ = 16384
DEPTH = 1

MEM_LEN = 256
EPS = 1e-6

SSD_HEADS = 16
SSD_HEAD_DIM = 64
SSD_DIM = SSD_HEADS * SSD_HEAD_DIM
SSD_GROUPS = 2
SSD_HEADS_PER_GROUP = SSD_HEADS // SSD_GROUPS
SSD_STATE = 128
SSD_CONV = 4
SSD_CHUNK = 128
SSD_CONV_CH = SSD_DIM + 2 * SSD_GROUPS * SSD_STATE

HG_HEADS = 8
HG_K = 128
HG_V = 128
HG_KDIM = HG_HEADS * HG_K
HG_VDIM = HG_HEADS * HG_V
HG_CHUNK = 64

D_MIX = SSD_DIM + HG_VDIM
N_IN = SSD_DIM + SSD_CONV_CH + SSD_HEADS + 2 * HG_KDIM + 2 * HG_VDIM

XA_HEADS = 4
XA_HEAD_DIM = D_MODEL // XA_HEADS

FFN_DIM = -(-8 * D_MODEL // (3 * 256)) * 256

kernel_name = "hybrid_ssd_hgrn2_xattn_block"


def rmsnorm(x, w):
    xf = x.astype(jnp.float32)
    y = xf * lax.rsqrt(jnp.mean(xf * xf, axis=-1, keepdims=True) + EPS)
    return (y * w.astype(jnp.float32)).astype(x.dtype)


def causal_depthwise_conv(u, w, b):
    ch = u.shape[-1]
    out = lax.conv_general_dilated(
        u, w[:, None, :].astype(u.dtype), window_strides=(1,),
        padding=[(w.shape[0] - 1, 0)], dimension_numbers=("NWC", "WIO", "NWC"),
        feature_group_count=ch)
    return out + b.astype(u.dtype)


def ssd_mixer(xs, bm, cm, dt, a_log, d_skip):
    bsz, seqlen = xs.shape[0], xs.shape[1]
    nc, q = seqlen // SSD_CHUNK, SSD_CHUNK
    g, e, p, n = SSD_GROUPS, SSD_HEADS_PER_GROUP, SSD_HEAD_DIM, SSD_STATE
    a = -jnp.exp(a_log.astype(jnp.float32)).reshape(g, e)
    x_c = xs.reshape(bsz, nc, q, g, e, p)
    b_c = bm.reshape(bsz, nc, q, g, n)
    c_c = cm.reshape(bsz, nc, q, g, n)
    dt_c = dt.reshape(bsz, nc, q, g, e)
    acum = jnp.cumsum(jnp.moveaxis(dt_c * a, 2, -1), axis=-1)
    causal = jnp.tril(jnp.ones((q, q), dtype=bool))
    seg = acum[..., :, None] - acum[..., None, :]
    l_dec = jnp.exp(jnp.where(causal, seg, -jnp.inf))
    xdt = x_c * dt_c[..., None]
    cb = jnp.einsum("bclgn,bcsgn->bcgls", c_c, b_c)
    y_diag = jnp.einsum("bcgels,bcsgep->bclgep", cb[:, :, :, None] * l_dec, xdt)
    dec_to_end = jnp.moveaxis(jnp.exp(acum[..., -1:] - acum), -1, 2)
    states = jnp.einsum("bcsgn,bcsgep->bcgepn", b_c, xdt * dec_to_end[..., None])
    chunk_decay = jnp.exp(acum[..., -1])

    def step(s, inp):
        st, dec = inp
        return s * dec[..., None, None] + st, s

    s0 = jnp.zeros((bsz, g, e, p, n), jnp.float32)
    _, s_in = lax.scan(step, s0, (jnp.moveaxis(states, 1, 0).astype(jnp.float32),
                                  jnp.moveaxis(chunk_decay, 1, 0)))
    s_in = jnp.moveaxis(s_in, 0, 1)
    dec_from_start = jnp.moveaxis(jnp.exp(acum), -1, 2)
    y_off = jnp.einsum("bclgn,bcgepn->bclgep", c_c, s_in) * dec_from_start[..., None]
    y = (y_diag + y_off).reshape(bsz, seqlen, SSD_HEADS, p)
    return y + d_skip.astype(jnp.float32)[:, None] * xs


def hgrn2_mixer(q_raw, f_raw, i_val, lb):
    bsz, seqlen = q_raw.shape[0], q_raw.shape[1]
    nc, c = seqlen // HG_CHUNK, HG_CHUNK
    qf = jax.nn.silu(q_raw)
    fg = lb + (1.0 - lb) * jax.nn.sigmoid(f_raw.astype(jnp.float32))
    kf = 1.0 - fg
    gl = jnp.log(fg)

    def to_chunks(t):
        return t.reshape(bsz, nc, c, t.shape[2], t.shape[3]).transpose(1, 0, 3, 2, 4)

    causal = jnp.tril(jnp.ones((c, c), dtype=bool))[:, :, None]

    def step(s, inp):
        qc, kc, vc, gc = inp
        bcum = jnp.cumsum(gc, axis=2)
        o_inter = jnp.einsum("bhqk,bhkv->bhqv", qc * jnp.exp(bcum), s)
        seg = bcum[:, :, :, None, :] - bcum[:, :, None, :, :]
        dec = jnp.exp(jnp.where(causal, seg, -jnp.inf))
        att = jnp.einsum("bhik,bhijk->bhij", qc, dec * kc[:, :, None, :, :])
        o_intra = jnp.einsum("bhij,bhjv->bhiv", att, vc)
        b_last = bcum[:, :, -1:, :]
        s_new = s * jnp.exp(b_last[:, :, 0, :])[..., None] + jnp.einsum(
            "bhjk,bhjv->bhkv", kc * jnp.exp(b_last - bcum), vc)
        return s_new, (o_inter + o_intra).astype(jnp.float32)

    s0 = jnp.zeros((bsz, HG_HEADS, HG_K, HG_V), jnp.float32)
    _, o = lax.scan(step, s0, (to_chunks(qf), to_chunks(kf), to_chunks(i_val), to_chunks(gl)))
    return o.transpose(1, 0, 3, 2, 4).reshape(bsz, seqlen, HG_HEADS, HG_V)


def _fwd_setup_inputs(seed: int = 0) -> dict:
    key = jax.random.key(seed)
    ks = jax.random.split(key, 24)
    f32 = jnp.float32

    def nrm(k, shape, scale):
        return jax.random.normal(k, shape, f32) * scale

    def gain(k, shape):
        return 1.0 + 0.02 * jax.random.normal(k, shape, f32)

    dt = jnp.exp(jax.random.uniform(ks[5], (DEPTH, SSD_HEADS), f32)
                 * (math.log(0.1) - math.log(0.001)) + math.log(0.001))
    return {
        "x": nrm(ks[0], (BATCH, SEQ, D_MODEL), 1.0),
        "mem": nrm(ks[1], (BATCH, MEM_LEN, D_MODEL), 1.0),
        "norm_mix_w": gain(ks[2], (DEPTH, D_MODEL)),
        "w_in": nrm(ks[3], (DEPTH, D_MODEL, N_IN), D_MODEL ** -0.5),
        "conv_w": nrm(ks[4], (DEPTH, SSD_CONV, SSD_CONV_CH), SSD_CONV ** -0.5),
        "conv_b": nrm(ks[6], (DEPTH, SSD_CONV_CH), 0.02),
        "dt_bias": dt + jnp.log(-jnp.expm1(-dt)),
        "a_log": jnp.log(jax.random.uniform(ks[7], (DEPTH, SSD_HEADS), f32, 1.0, 16.0)),
        "d_skip": 1.0 + 0.1 * jax.random.normal(ks[8], (DEPTH, SSD_HEADS), f32),
        "ssd_norm_w": gain(ks[9], (DEPTH, SSD_DIM)),
        "hg_lower_bounds": nrm(ks[10], (DEPTH + 1, HG_KDIM), 0.5),
        "hg_norm_w": gain(ks[11], (DEPTH, HG_V)),
        "w_out": nrm(ks[12], (DEPTH, D_MIX, D_MODEL), D_MIX ** -0.5),
        "norm_xa_w": gain(ks[13], (DEPTH, D_MODEL)),
        "norm_mem_w": gain(ks[14], (DEPTH, D_MODEL)),
        "xa_wq": nrm(ks[15], (DEPTH, D_MODEL, D_MODEL), D_MODEL ** -0.5),
        "xa_wkv": nrm(ks[16], (DEPTH, D_MODEL, 2 * D_MODEL), D_MODEL ** -0.5),
        "xa_wo": nrm(ks[17], (DEPTH, D_MODEL, D_MODEL), D_MODEL ** -0.5),
        "norm_ffn_w": gain(ks[18], (DEPTH, D_MODEL)),
        "ffn_w_gate": nrm(ks[19], (DEPTH, D_MODEL, FFN_DIM), D_MODEL ** -0.5),
        "ffn_w_up": nrm(ks[20], (DEPTH, D_MODEL, FFN_DIM), D_MODEL ** -0.5),
        "ffn_w_down": nrm(ks[21], (DEPTH, FFN_DIM, D_MODEL), FFN_DIM ** -0.5),
        "norm_final_w": gain(ks[22], (D_MODEL,)),
    }


def _fwd_reference(x, mem, norm_mix_w, w_in, conv_w, conv_b, dt_bias, a_log, d_skip, ssd_norm_w,
              hg_lower_bounds, hg_norm_w, w_out, norm_xa_w, norm_mem_w, xa_wq, xa_wkv, xa_wo,
              norm_ffn_w, ffn_w_gate, ffn_w_up, ffn_w_down, norm_final_w):
    bsz, seqlen, _ = x.shape
    lb_all = jnp.cumsum(jax.nn.softmax(hg_lower_bounds.astype(jnp.float32), axis=0), axis=0)
    s1 = SSD_DIM
    s2 = s1 + SSD_CONV_CH
    s3 = s2 + SSD_HEADS
    s4 = s3 + HG_KDIM
    s5 = s4 + HG_KDIM
    s6 = s5 + HG_VDIM
    for l in range(DEPTH):
        h = rmsnorm(x, norm_mix_w[l])
        proj = h @ w_in[l]
        z, xbc, dt_raw, hq, hf, hi, hgate = jnp.split(proj, [s1, s2, s3, s4, s5, s6], axis=-1)
        xbc = jax.nn.silu(causal_depthwise_conv(xbc, conv_w[l], conv_b[l]))
        xs, bm, cm = jnp.split(xbc, [SSD_DIM, SSD_DIM + SSD_GROUPS * SSD_STATE], axis=-1)
        dt = jax.nn.softplus((dt_raw + dt_bias[l]).astype(jnp.float32))
        y_a = ssd_mixer(xs.reshape(bsz, seqlen, SSD_HEADS, SSD_HEAD_DIM),
                        bm.reshape(bsz, seqlen, SSD_GROUPS, SSD_STATE),
                        cm.reshape(bsz, seqlen, SSD_GROUPS, SSD_STATE),
                        dt, a_log[l], d_skip[l])
        yz = (y_a.reshape(bsz, seqlen, SSD_DIM) * jax.nn.silu(z)).reshape(
            bsz, seqlen, SSD_GROUPS, SSD_DIM // SSD_GROUPS)
        y_a = rmsnorm(yz, ssd_norm_w[l].reshape(SSD_GROUPS, -1)).reshape(bsz, seqlen, SSD_DIM)
        o_b = hgrn2_mixer(hq.reshape(bsz, seqlen, HG_HEADS, HG_K),
                          hf.reshape(bsz, seqlen, HG_HEADS, HG_K),
                          hi.reshape(bsz, seqlen, HG_HEADS, HG_V),
                          lb_all[l].reshape(HG_HEADS, HG_K))
        o_b = rmsnorm(o_b, hg_norm_w[l]) * jax.nn.silu(hgate.reshape(bsz, seqlen, HG_HEADS, HG_V))
        mixed = jnp.concatenate([y_a, o_b.reshape(bsz, seqlen, HG_VDIM)], axis=-1).astype(x.dtype)
        x = x + mixed @ w_out[l]
        h = rmsnorm(x, norm_xa_w[l])
        m = rmsnorm(mem, norm_mem_w[l])
        qx = (h @ xa_wq[l]).reshape(bsz, seqlen, XA_HEADS, XA_HEAD_DIM)
        km, vm = jnp.split(m @ xa_wkv[l], 2, axis=-1)
        km = km.reshape(bsz, MEM_LEN, XA_HEADS, XA_HEAD_DIM)
        vm = vm.reshape(bsz, MEM_LEN, XA_HEADS, XA_HEAD_DIM)
        sc = jnp.einsum("bqhd,bkhd->bhqk", qx, km, preferred_element_type=jnp.float32)
        pr = jax.nn.softmax(sc * (XA_HEAD_DIM ** -0.5), axis=-1).astype(vm.dtype)
        ox = jnp.einsum("bhqk,bkhd->bqhd", pr, vm).reshape(bsz, seqlen, D_MODEL)
        x = x + ox @ xa_wo[l]
        h = rmsnorm(x, norm_ffn_w[l])
        x = x + (jax.nn.silu(h @ ffn_w_gate[l]) * (h @ ffn_w_up[l])) @ ffn_w_down[l]
    return rmsnorm(x, norm_final_w)


import jax as _jax
import jax.numpy as _jnp

TWIN_FORMAT = 'train_step'
FWD_PARAMS = ['x', 'mem', 'norm_mix_w', 'w_in', 'conv_w', 'conv_b', 'dt_bias', 'a_log', 'd_skip', 'ssd_norm_w', 'hg_lower_bounds', 'hg_norm_w', 'w_out', 'norm_xa_w', 'norm_mem_w', 'xa_wq', 'xa_wkv', 'xa_wo', 'norm_ffn_w', 'ffn_w_gate', 'ffn_w_up', 'ffn_w_down', 'norm_final_w']
TWIN_WEIGHTS = ['norm_mix_w', 'w_in', 'conv_w', 'conv_b', 'dt_bias', 'a_log', 'd_skip', 'ssd_norm_w', 'hg_lower_bounds', 'hg_norm_w', 'w_out', 'norm_xa_w', 'norm_mem_w', 'xa_wq', 'xa_wkv', 'xa_wo', 'norm_ffn_w', 'ffn_w_gate', 'ffn_w_up', 'ffn_w_down', 'norm_final_w']
TWIN_DIFF_INPUT = 'x'
TWIN_INPUTS = ['x', 'mem', 'norm_mix_w', 'w_in', 'conv_w', 'conv_b', 'dt_bias', 'a_log', 'd_skip', 'ssd_norm_w', 'hg_lower_bounds', 'hg_norm_w', 'w_out', 'norm_xa_w', 'norm_mem_w', 'xa_wq', 'xa_wkv', 'xa_wo', 'norm_ffn_w', 'ffn_w_gate', 'ffn_w_up', 'ffn_w_down', 'norm_final_w', 'loss_target', 'm_norm_mix_w', 'm_w_in', 'm_conv_w', 'm_conv_b', 'm_dt_bias', 'm_a_log', 'm_d_skip', 'm_ssd_norm_w', 'm_hg_lower_bounds', 'm_hg_norm_w', 'm_w_out', 'm_norm_xa_w', 'm_norm_mem_w', 'm_xa_wq', 'm_xa_wkv', 'm_xa_wo', 'm_norm_ffn_w', 'm_ffn_w_gate', 'm_ffn_w_up', 'm_ffn_w_down', 'm_norm_final_w', 'v_norm_mix_w', 'v_w_in', 'v_conv_w', 'v_conv_b', 'v_dt_bias', 'v_a_log', 'v_d_skip', 'v_ssd_norm_w', 'v_hg_lower_bounds', 'v_hg_norm_w', 'v_w_out', 'v_norm_xa_w', 'v_norm_mem_w', 'v_xa_wq', 'v_xa_wkv', 'v_xa_wo', 'v_norm_ffn_w', 'v_ffn_w_gate', 'v_ffn_w_up', 'v_ffn_w_down', 'v_norm_final_w']
TWIN_OUTPUTS = ['loss', 'grad_x', 'grad_norm_mix_w', 'grad_w_in', 'grad_conv_w', 'grad_conv_b', 'grad_dt_bias', 'grad_a_log', 'grad_d_skip', 'grad_ssd_norm_w', 'grad_hg_lower_bounds', 'grad_hg_norm_w', 'grad_w_out', 'grad_norm_xa_w', 'grad_norm_mem_w', 'grad_xa_wq', 'grad_xa_wkv', 'grad_xa_wo', 'grad_norm_ffn_w', 'grad_ffn_w_gate', 'grad_ffn_w_up', 'grad_ffn_w_down', 'grad_norm_final_w', 'delta_norm_mix_w', 'delta_w_in', 'delta_conv_w', 'delta_conv_b', 'delta_dt_bias', 'delta_a_log', 'delta_d_skip', 'delta_ssd_norm_w', 'delta_hg_lower_bounds', 'delta_hg_norm_w', 'delta_w_out', 'delta_norm_xa_w', 'delta_norm_mem_w', 'delta_xa_wq', 'delta_xa_wkv', 'delta_xa_wo', 'delta_norm_ffn_w', 'delta_ffn_w_gate', 'delta_ffn_w_up', 'delta_ffn_w_down', 'delta_norm_final_w', 'new_m_norm_mix_w', 'new_m_w_in', 'new_m_conv_w', 'new_m_conv_b', 'new_m_dt_bias', 'new_m_a_log', 'new_m_d_skip', 'new_m_ssd_norm_w', 'new_m_hg_lower_bounds', 'new_m_hg_norm_w', 'new_m_w_out', 'new_m_norm_xa_w', 'new_m_norm_mem_w', 'new_m_xa_wq', 'new_m_xa_wkv', 'new_m_xa_wo', 'new_m_norm_ffn_w', 'new_m_ffn_w_gate', 'new_m_ffn_w_up', 'new_m_ffn_w_down', 'new_m_norm_final_w', 'new_v_norm_mix_w', 'new_v_w_in', 'new_v_conv_w', 'new_v_conv_b', 'new_v_dt_bias', 'new_v_a_log', 'new_v_d_skip', 'new_v_ssd_norm_w', 'new_v_hg_lower_bounds', 'new_v_hg_norm_w', 'new_v_w_out', 'new_v_norm_xa_w', 'new_v_norm_mem_w', 'new_v_xa_wq', 'new_v_xa_wkv', 'new_v_xa_wo', 'new_v_norm_ffn_w', 'new_v_ffn_w_gate', 'new_v_ffn_w_up', 'new_v_ffn_w_down', 'new_v_norm_final_w']
TWIN_LEAF_KINDS = {'loss': 'loss', 'grad_x': 'grad_x', 'grad_norm_mix_w': 'grad_w', 'grad_w_in': 'grad_w', 'grad_conv_w': 'grad_w', 'grad_conv_b': 'grad_w', 'grad_dt_bias': 'grad_w', 'grad_a_log': 'grad_w', 'grad_d_skip': 'grad_w', 'grad_ssd_norm_w': 'grad_w', 'grad_hg_lower_bounds': 'grad_w', 'grad_hg_norm_w': 'grad_w', 'grad_w_out': 'grad_w', 'grad_norm_xa_w': 'grad_w', 'grad_norm_mem_w': 'grad_w', 'grad_xa_wq': 'grad_w', 'grad_xa_wkv': 'grad_w', 'grad_xa_wo': 'grad_w', 'grad_norm_ffn_w': 'grad_w', 'grad_ffn_w_gate': 'grad_w', 'grad_ffn_w_up': 'grad_w', 'grad_ffn_w_down': 'grad_w', 'grad_norm_final_w': 'grad_w', 'delta_norm_mix_w': 'delta_w', 'delta_w_in': 'delta_w', 'delta_conv_w': 'delta_w', 'delta_conv_b': 'delta_w', 'delta_dt_bias': 'delta_w', 'delta_a_log': 'delta_w', 'delta_d_skip': 'delta_w', 'delta_ssd_norm_w': 'delta_w', 'delta_hg_lower_bounds': 'delta_w', 'delta_hg_norm_w': 'delta_w', 'delta_w_out': 'delta_w', 'delta_norm_xa_w': 'delta_w', 'delta_norm_mem_w': 'delta_w', 'delta_xa_wq': 'delta_w', 'delta_xa_wkv': 'delta_w', 'delta_xa_wo': 'delta_w', 'delta_norm_ffn_w': 'delta_w', 'delta_ffn_w_gate': 'delta_w', 'delta_ffn_w_up': 'delta_w', 'delta_ffn_w_down': 'delta_w', 'delta_norm_final_w': 'delta_w', 'new_m_norm_mix_w': 'new_m', 'new_m_w_in': 'new_m', 'new_m_conv_w': 'new_m', 'new_m_conv_b': 'new_m', 'new_m_dt_bias': 'new_m', 'new_m_a_log': 'new_m', 'new_m_d_skip': 'new_m', 'new_m_ssd_norm_w': 'new_m', 'new_m_hg_lower_bounds': 'new_m', 'new_m_hg_norm_w': 'new_m', 'new_m_w_out': 'new_m', 'new_m_norm_xa_w': 'new_m', 'new_m_norm_mem_w': 'new_m', 'new_m_xa_wq': 'new_m', 'new_m_xa_wkv': 'new_m', 'new_m_xa_wo': 'new_m', 'new_m_norm_ffn_w': 'new_m', 'new_m_ffn_w_gate': 'new_m', 'new_m_ffn_w_up': 'new_m', 'new_m_ffn_w_down': 'new_m', 'new_m_norm_final_w': 'new_m', 'new_v_norm_mix_w': 'new_v', 'new_v_w_in': 'new_v', 'new_v_conv_w': 'new_v', 'new_v_conv_b': 'new_v', 'new_v_dt_bias': 'new_v', 'new_v_a_log': 'new_v', 'new_v_d_skip': 'new_v', 'new_v_ssd_norm_w': 'new_v', 'new_v_hg_lower_bounds': 'new_v', 'new_v_hg_norm_w': 'new_v', 'new_v_w_out': 'new_v', 'new_v_norm_xa_w': 'new_v', 'new_v_norm_mem_w': 'new_v', 'new_v_xa_wq': 'new_v', 'new_v_xa_wkv': 'new_v', 'new_v_xa_wo': 'new_v', 'new_v_norm_ffn_w': 'new_v', 'new_v_ffn_w_gate': 'new_v', 'new_v_ffn_w_up': 'new_v', 'new_v_ffn_w_down': 'new_v', 'new_v_norm_final_w': 'new_v'}


def _forward(args):
    return _fwd_reference(*[args[k] for k in FWD_PARAMS])


def _output_shape():
    def fwd():
        inp = _fwd_setup_inputs(0)
        return _fwd_reference(*[inp[k] for k in FWD_PARAMS])
    out = _jax.eval_shape(fwd)
    return out.shape, out.dtype

N_MICROBATCH = 1
ADAM_LR = 0.001
ADAM_B1 = 0.9
ADAM_B2 = 0.999
ADAM_EPS = 1e-08
ADAM_WD = 0.01
ADAM_STEP = 10
PER_EXAMPLE_BATCH_AXIS = {'x': 0, 'mem': 0, 'loss_target': 0}
SHARED_INPUTS = []
_WEIGHT_DTYPES = {'norm_mix_w': _jnp.float32, 'w_in': _jnp.float32, 'conv_w': _jnp.float32, 'conv_b': _jnp.float32, 'dt_bias': _jnp.float32, 'a_log': _jnp.float32, 'd_skip': _jnp.float32, 'ssd_norm_w': _jnp.float32, 'hg_lower_bounds': _jnp.float32, 'hg_norm_w': _jnp.float32, 'w_out': _jnp.float32, 'norm_xa_w': _jnp.float32, 'norm_mem_w': _jnp.float32, 'xa_wq': _jnp.float32, 'xa_wkv': _jnp.float32, 'xa_wo': _jnp.float32, 'norm_ffn_w': _jnp.float32, 'ffn_w_gate': _jnp.float32, 'ffn_w_up': _jnp.float32, 'ffn_w_down': _jnp.float32, 'norm_final_w': _jnp.float32}
MOMENT_SCALE = {'norm_mix_w': 3.747473e-01, 'w_in': 1.442760e-01, 'conv_w': 1.831785e-01, 'conv_b': 2.554288e-01, 'dt_bias': 4.177635e-01, 'a_log': 6.582568e-01, 'd_skip': 7.824278e-01, 'ssd_norm_w': 2.070649e-01, 'hg_lower_bounds': 1.138008e-02, 'hg_norm_w': 3.763105e-01, 'w_out': 2.488825e-01, 'norm_xa_w': 3.128051e-02, 'norm_mem_w': 4.737590e-02, 'xa_wq': 3.146441e-02, 'xa_wkv': 3.161526e-02, 'xa_wo': 3.170552e-02, 'norm_ffn_w': 2.199702e-01, 'ffn_w_gate': 9.237687e-02, 'ffn_w_up': 8.936499e-02, 'ffn_w_down': 1.485475e-01, 'norm_final_w': 1.280847e+02}


def _to_microbatches(a, axis):
    t = _jnp.moveaxis(a, axis, 0)
    t = t.reshape((N_MICROBATCH, t.shape[0] // N_MICROBATCH) + t.shape[1:])
    return _jnp.moveaxis(t, 1, axis + 1)


def setup_inputs(seed: int = 0) -> dict:
    inp = _fwd_setup_inputs(seed)
    key = _jax.random.fold_in(_jax.random.key(seed), 7919)
    shape, _ = _output_shape()
    out = dict(inp)
    out["loss_target"] = _jax.random.normal(_jax.random.fold_in(key, 0), shape, _jnp.float32)
    for i, name in enumerate(TWIN_WEIGHTS):
        w = inp[name].astype(_jnp.float32)
        if MOMENT_SCALE is None:
            s = _jnp.sqrt(_jnp.mean(_jnp.square(w)) + 1e-30)
        else:
            s = MOMENT_SCALE[name]
        km, kv = _jax.random.split(_jax.random.fold_in(key, i + 1))
        out[name] = w
        out["m_" + name] = s * _jax.random.normal(km, w.shape, _jnp.float32)
        out["v_" + name] = (s * s) * _jax.random.uniform(kv, w.shape, _jnp.float32, 0.5, 1.5)
    if N_MICROBATCH > 1:
        for name, axis in PER_EXAMPLE_BATCH_AXIS.items():
            out[name] = _to_microbatches(out[name], axis)
    return {'x': out['x'], 'mem': out['mem'], 'norm_mix_w': out['norm_mix_w'], 'w_in': out['w_in'], 'conv_w': out['conv_w'], 'conv_b': out['conv_b'], 'dt_bias': out['dt_bias'], 'a_log': out['a_log'], 'd_skip': out['d_skip'], 'ssd_norm_w': out['ssd_norm_w'], 'hg_lower_bounds': out['hg_lower_bounds'], 'hg_norm_w': out['hg_norm_w'], 'w_out': out['w_out'], 'norm_xa_w': out['norm_xa_w'], 'norm_mem_w': out['norm_mem_w'], 'xa_wq': out['xa_wq'], 'xa_wkv': out['xa_wkv'], 'xa_wo': out['xa_wo'], 'norm_ffn_w': out['norm_ffn_w'], 'ffn_w_gate': out['ffn_w_gate'], 'ffn_w_up': out['ffn_w_up'], 'ffn_w_down': out['ffn_w_down'], 'norm_final_w': out['norm_final_w'], 'loss_target': out['loss_target'], 'm_norm_mix_w': out['m_norm_mix_w'], 'm_w_in': out['m_w_in'], 'm_conv_w': out['m_conv_w'], 'm_conv_b': out['m_conv_b'], 'm_dt_bias': out['m_dt_bias'], 'm_a_log': out['m_a_log'], 'm_d_skip': out['m_d_skip'], 'm_ssd_norm_w': out['m_ssd_norm_w'], 'm_hg_lower_bounds': out['m_hg_lower_bounds'], 'm_hg_norm_w': out['m_hg_norm_w'], 'm_w_out': out['m_w_out'], 'm_norm_xa_w': out['m_norm_xa_w'], 'm_norm_mem_w': out['m_norm_mem_w'], 'm_xa_wq': out['m_xa_wq'], 'm_xa_wkv': out['m_xa_wkv'], 'm_xa_wo': out['m_xa_wo'], 'm_norm_ffn_w': out['m_norm_ffn_w'], 'm_ffn_w_gate': out['m_ffn_w_gate'], 'm_ffn_w_up': out['m_ffn_w_up'], 'm_ffn_w_down': out['m_ffn_w_down'], 'm_norm_final_w': out['m_norm_final_w'], 'v_norm_mix_w': out['v_norm_mix_w'], 'v_w_in': out['v_w_in'], 'v_conv_w': out['v_conv_w'], 'v_conv_b': out['v_conv_b'], 'v_dt_bias': out['v_dt_bias'], 'v_a_log': out['v_a_log'], 'v_d_skip': out['v_d_skip'], 'v_ssd_norm_w': out['v_ssd_norm_w'], 'v_hg_lower_bounds': out['v_hg_lower_bounds'], 'v_hg_norm_w': out['v_hg_norm_w'], 'v_w_out': out['v_w_out'], 'v_norm_xa_w': out['v_norm_xa_w'], 'v_norm_mem_w': out['v_norm_mem_w'], 'v_xa_wq': out['v_xa_wq'], 'v_xa_wkv': out['v_xa_wkv'], 'v_xa_wo': out['v_xa_wo'], 'v_norm_ffn_w': out['v_norm_ffn_w'], 'v_ffn_w_gate': out['v_ffn_w_gate'], 'v_ffn_w_up': out['v_ffn_w_up'], 'v_ffn_w_down': out['v_ffn_w_down'], 'v_norm_final_w': out['v_norm_final_w']}


def _loss(weights, diff, rest, loss_target):
    with _jax.named_scope("forward"):
        args = {**rest, TWIN_DIFF_INPUT: diff, **{k: w.astype(_WEIGHT_DTYPES[k]) for k, w in weights.items()}}
        y = _forward(args)
    with _jax.named_scope("loss_head"):
        err = _jnp.square(y.astype(_jnp.float32) - loss_target)
        return 0.5 * _jnp.sum(_jnp.mean(err, axis=-1)) if err.ndim else 0.5 * err


def _adamw(w, g, m, v):
    m = ADAM_B1 * m + (1.0 - ADAM_B1) * g
    v = ADAM_B2 * v + (1.0 - ADAM_B2) * _jnp.square(g)
    m_hat = m / (1.0 - ADAM_B1 ** ADAM_STEP)
    v_hat = v / (1.0 - ADAM_B2 ** ADAM_STEP)
    delta = -ADAM_LR * (m_hat / (_jnp.sqrt(v_hat) + ADAM_EPS) + ADAM_WD * w)
    return delta, m, v


def reference(x, mem, norm_mix_w, w_in, conv_w, conv_b, dt_bias, a_log, d_skip, ssd_norm_w, hg_lower_bounds, hg_norm_w, w_out, norm_xa_w, norm_mem_w, xa_wq, xa_wkv, xa_wo, norm_ffn_w, ffn_w_gate, ffn_w_up, ffn_w_down, norm_final_w, loss_target, m_norm_mix_w, m_w_in, m_conv_w, m_conv_b, m_dt_bias, m_a_log, m_d_skip, m_ssd_norm_w, m_hg_lower_bounds, m_hg_norm_w, m_w_out, m_norm_xa_w, m_norm_mem_w, m_xa_wq, m_xa_wkv, m_xa_wo, m_norm_ffn_w, m_ffn_w_gate, m_ffn_w_up, m_ffn_w_down, m_norm_final_w, v_norm_mix_w, v_w_in, v_conv_w, v_conv_b, v_dt_bias, v_a_log, v_d_skip, v_ssd_norm_w, v_hg_lower_bounds, v_hg_norm_w, v_w_out, v_norm_xa_w, v_norm_mem_w, v_xa_wq, v_xa_wkv, v_xa_wo, v_norm_ffn_w, v_ffn_w_gate, v_ffn_w_up, v_ffn_w_down, v_norm_final_w):
    given = dict(x=x, mem=mem, norm_mix_w=norm_mix_w, w_in=w_in, conv_w=conv_w, conv_b=conv_b, dt_bias=dt_bias, a_log=a_log, d_skip=d_skip, ssd_norm_w=ssd_norm_w, hg_lower_bounds=hg_lower_bounds, hg_norm_w=hg_norm_w, w_out=w_out, norm_xa_w=norm_xa_w, norm_mem_w=norm_mem_w, xa_wq=xa_wq, xa_wkv=xa_wkv, xa_wo=xa_wo, norm_ffn_w=norm_ffn_w, ffn_w_gate=ffn_w_gate, ffn_w_up=ffn_w_up, ffn_w_down=ffn_w_down, norm_final_w=norm_final_w, loss_target=loss_target, m_norm_mix_w=m_norm_mix_w, m_w_in=m_w_in, m_conv_w=m_conv_w, m_conv_b=m_conv_b, m_dt_bias=m_dt_bias, m_a_log=m_a_log, m_d_skip=m_d_skip, m_ssd_norm_w=m_ssd_norm_w, m_hg_lower_bounds=m_hg_lower_bounds, m_hg_norm_w=m_hg_norm_w, m_w_out=m_w_out, m_norm_xa_w=m_norm_xa_w, m_norm_mem_w=m_norm_mem_w, m_xa_wq=m_xa_wq, m_xa_wkv=m_xa_wkv, m_xa_wo=m_xa_wo, m_norm_ffn_w=m_norm_ffn_w, m_ffn_w_gate=m_ffn_w_gate, m_ffn_w_up=m_ffn_w_up, m_ffn_w_down=m_ffn_w_down, m_norm_final_w=m_norm_final_w, v_norm_mix_w=v_norm_mix_w, v_w_in=v_w_in, v_conv_w=v_conv_w, v_conv_b=v_conv_b, v_dt_bias=v_dt_bias, v_a_log=v_a_log, v_d_skip=v_d_skip, v_ssd_norm_w=v_ssd_norm_w, v_hg_lower_bounds=v_hg_lower_bounds, v_hg_norm_w=v_hg_norm_w, v_w_out=v_w_out, v_norm_xa_w=v_norm_xa_w, v_norm_mem_w=v_norm_mem_w, v_xa_wq=v_xa_wq, v_xa_wkv=v_xa_wkv, v_xa_wo=v_xa_wo, v_norm_ffn_w=v_norm_ffn_w, v_ffn_w_gate=v_ffn_w_gate, v_ffn_w_up=v_ffn_w_up, v_ffn_w_down=v_ffn_w_down, v_norm_final_w=v_norm_final_w)
    weights = {n: given[n] for n in TWIN_WEIGHTS}
    shared = {n: given[n] for n in SHARED_INPUTS}
    per_example = {n: given[n] for n in ['x', 'mem']}
    grad_fn = _jax.value_and_grad(_loss, argnums=(0, 1))

    def one_microbatch(ex, loss_target):
        ex = dict(ex)
        diff = ex.pop(TWIN_DIFF_INPUT)
        return grad_fn(weights, diff, {**shared, **ex}, loss_target)

    if N_MICROBATCH == 1:
        loss, (grad_w, grad_x) = one_microbatch(per_example, given["loss_target"])
    else:
        def body(carry, xs):
            loss_sum, grad_sum = carry
            l_k, (gw_k, gx_k) = one_microbatch(xs[0], xs[1])
            with _jax.named_scope("update"):
                return (loss_sum + l_k, _jax.tree.map(_jnp.add, grad_sum, gw_k)), gx_k

        init = (_jnp.zeros((), _jnp.float32), _jax.tree.map(_jnp.zeros_like, weights))
        (loss, grad_w), grad_x = _jax.lax.scan(body, init, (per_example, given["loss_target"]))
    with _jax.named_scope("update"):
        delta_w, new_m, new_v = {}, {}, {}
        for n in TWIN_WEIGHTS:
            delta_w[n], new_m[n], new_v[n] = _adamw(weights[n], grad_w[n], given["m_" + n], given["v_" + n])
    return (loss, grad_x, *[grad_w[n] for n in TWIN_WEIGHTS], *[delta_w[n] for n in TWIN_WEIGHTS],
            *[new_m[n] for n in TWIN_WEIGHTS], *[new_v[n] for n in TWIN_WEIGHTS])
```

```python
import jax
import jax.numpy as jnp
from jax import lax
from jax.experimental import pallas as pl
from jax.experimental.pallas import tpu as pltpu

F32, BF16 = jnp.float32, jnp.bfloat16
HI = lax.Precision.HIGHEST
MESH = pl.DeviceIdType.MESH

D = 1024
EPS = 1e-6
SSD_HEADS, SSD_P, SSD_N, SSD_Q = 16, 64, 128, 128
HG_HEADS, HG_K, HG_STEP, HG_SUB = 8, 128, 128, 64
XA_HEADS, XA_DH, MEM_LEN = 4, 256, 256
FFN = 2816
N_IN = 6672
N_DEV = 8
WA, WB, WC = 4096, 1536, 1536
ADAM_LR, ADAM_B1, ADAM_B2, ADAM_EPS, ADAM_WD, ADAM_STEP = 0.001, 0.9, 0.999, 1e-08, 0.01, 10
VMEM_MB = 2 ** 20

PACK = (("w_in", 834), ("w_out", 256), ("xa_wq", 128), ("xa_wkv", 256), ("xa_wo", 128),
        ("ffn_w_gate", 352), ("ffn_w_up", 352), ("ffn_w_down", 352))
ROW_TILE = 16
TAIL_ROWS = ROW_TILE


def _rows_padded(r):
    return -(-r // ROW_TILE) * ROW_TILE


PACK_ROWS = sum(_rows_padded(r) for _, r in PACK) + TAIL_ROWS
SMALL = (("norm_mix_w", 1024), ("conv_w", 6144), ("conv_b", 1536), ("dt_bias", 16), ("a_log", 16),
         ("d_skip", 16), ("ssd_norm_w", 1024), ("hg_lower_bounds", 2048), ("hg_norm_w", 128),
         ("norm_xa_w", 1024), ("norm_mem_w", 1024), ("norm_ffn_w", 1024), ("norm_final_w", 1024),
         ("loss", 1))


def _small_layout():
    off, lay = 0, {}
    for n, s in SMALL:
        lay[n] = (off, s)
        off += -(-s // 128) * 128
    return lay, off // 128


SMALL_LAY, SMALL_ROWS = _small_layout()
SMALL_ROWS_PAD = -(-SMALL_ROWS // 8) * 8


def _cparams(sem=None, vmem_mb=48):
    return pltpu.CompilerParams(dimension_semantics=sem, vmem_limit_bytes=vmem_mb * VMEM_MB)


def _dot(a, b):
    return jnp.dot(a.astype(BF16), b.astype(BF16), preferred_element_type=F32)


def _dot_nt(a, b):
    return lax.dot_general(a.astype(BF16), b.astype(BF16), (((1,), (1,)), ((), ())), preferred_element_type=F32)


def _dot_tn(a, b):
    return lax.dot_general(a.astype(BF16), b.astype(BF16), (((0,), (0,)), ((), ())), preferred_element_type=F32)


def _dot_hi(a, b):
    return jnp.dot(a, b, precision=HI, preferred_element_type=F32)


def _dot_nt_hi(a, b):
    return lax.dot_general(a, b, (((1,), (1,)), ((), ())), precision=HI, preferred_element_type=F32)


def _sigmoid(x):
    return 1.0 / (1.0 + jnp.exp(-x))


def _silu(x):
    return x * _sigmoid(x)


def _dsilu(x):
    s = _sigmoid(x)
    return s * (1.0 + x * (1.0 - s))


def _softplus(x):
    return jnp.maximum(x, 0.0) + jnp.log(1.0 + jnp.exp(-jnp.abs(x)))


def _rms_fwd(x, w):
    r = lax.rsqrt(jnp.mean(x * x, axis=1, keepdims=True) + EPS)
    return x * r * w


def _rms_bwd(dy, x, w):
    r = lax.rsqrt(jnp.mean(x * x, axis=1, keepdims=True) + EPS)
    xh = x * r
    g = dy * w
    dx = r * (g - xh * jnp.mean(g * xh, axis=1, keepdims=True))
    return dx, jnp.sum(dy * xh, axis=0, keepdims=True)


def _iota2(shape, dim):
    return lax.broadcasted_iota(jnp.int32, shape, dim)


def _tril(n):
    return (_iota2((n, n), 0) >= _iota2((n, n), 1)).astype(F32)


def _triu(n):
    return (_iota2((n, n), 0) <= _iota2((n, n), 1)).astype(F32)


def _norm_mm(x, nw, w, *, tm, tn, name, emit_h=False):
    L, K = x.shape
    N = w.shape[1]
    tm, tn = min(tm, L), min(tn, N)

    def body(x_ref, nw_ref, w_ref, *rest):
        if emit_h:
            o_ref, h_ref, hs_ref = rest
        else:
            o_ref, hs_ref = rest

        @pl.when(pl.program_id(1) == 0)
        def _():
            h = _rms_fwd(x_ref[...], nw_ref[...]).astype(BF16)
            hs_ref[...] = h
            if emit_h:
                h_ref[...] = h

        o_ref[...] = jnp.dot(hs_ref[...], w_ref[...], preferred_element_type=F32)

    out_shape = [jax.ShapeDtypeStruct((L, N), F32)]
    out_specs = [pl.BlockSpec((tm, tn), lambda i, j: (i, j))]
    if emit_h:
        out_shape.append(jax.ShapeDtypeStruct((L, K), BF16))
        out_specs.append(pl.BlockSpec((tm, K), lambda i, j: (i, 0)))
    res = pl.pallas_call(
        body, name=name, grid=(L // tm, N // tn),
        in_specs=[pl.BlockSpec((tm, K), lambda i, j: (i, 0)), pl.BlockSpec((1, K), lambda i, j: (0, 0)),
                  pl.BlockSpec((K, tn), lambda i, j: (0, j))],
        out_specs=out_specs, out_shape=out_shape,
        scratch_shapes=[pltpu.VMEM((tm, K), BF16)],
        compiler_params=_cparams(("parallel", "arbitrary")),
    )(x, nw, w)
    return res if emit_h else res[0]


def _mm2_res(res, a1, a2, w, *, tm, name):
    L, N = res.shape
    K = a1.shape[1]
    tm = min(tm, L)

    def body(r_ref, a1_ref, a2_ref, w_ref, o_ref):
        acc = jnp.dot(a1_ref[...], w_ref[0:K, :], preferred_element_type=F32)
        acc += jnp.dot(a2_ref[...], w_ref[K:2 * K, :], preferred_element_type=F32)
        o_ref[...] = r_ref[...] + acc

    return pl.pallas_call(
        body, name=name, grid=(L // tm,),
        in_specs=[pl.BlockSpec((tm, N), lambda i: (i, 0)), pl.BlockSpec((tm, K), lambda i: (i, 0)),
                  pl.BlockSpec((tm, K), lambda i: (i, 0)), pl.BlockSpec((2 * K, N), lambda i: (0, 0))],
        out_specs=pl.BlockSpec((tm, N), lambda i: (i, 0)),
        out_shape=jax.ShapeDtypeStruct((L, N), F32),
        compiler_params=_cparams(("parallel",)),
    )(res, a1, a2, w)


def _mm_nt2(a, w, *, tm, name):
    L, N = a.shape
    K = w.shape[0] // 2
    tm = min(tm, L)

    def body(a_ref, w_ref, o1_ref, o2_ref):
        av = a_ref[...].astype(BF16)
        o1_ref[...] = _dot_nt(av, w_ref[0:K, :])
        o2_ref[...] = _dot_nt(av, w_ref[K:2 * K, :])

    return pl.pallas_call(
        body, name=name, grid=(L // tm,),
        in_specs=[pl.BlockSpec((tm, N), lambda i: (i, 0)), pl.BlockSpec((2 * K, N), lambda i: (0, 0))],
        out_specs=[pl.BlockSpec((tm, K), lambda i: (i, 0)), pl.BlockSpec((tm, K), lambda i: (i, 0))],
        out_shape=[jax.ShapeDtypeStruct((L, K), F32), jax.ShapeDtypeStruct((L, K), F32)],
        compiler_params=_cparams(("parallel",)),
    )(a, w)


def _dw(a, b, *, tM, tN, tl, name):
    L, M = a.shape
    N = b.shape[1]
    tM, tN, tl = min(tM, M), min(tN, N), min(tl, L)

    def body(a_ref, b_ref, o_ref):
        @pl.when(pl.program_id(2) == 0)
        def _():
            o_ref[...] = jnp.zeros_like(o_ref)

        o_ref[...] += _dot_tn(a_ref[...], b_ref[...])

    return pl.pallas_call(
        body, name=name, grid=(M // tM, N // tN, L // tl),
        in_specs=[pl.BlockSpec((tl, tM), lambda i, j, l: (l, i)), pl.BlockSpec((tl, tN), lambda i, j, l: (l, j))],
        out_specs=pl.BlockSpec((tM, tN), lambda i, j, l: (i, j)),
        out_shape=jax.ShapeDtypeStruct((M, N), F32),
        compiler_params=_cparams(("parallel", "parallel", "arbitrary")),
    )(a, b)


def _mmnt_normbwd(a_list, w_list, x, nw, res, *, tm, name):
    L, Dm = x.shape
    tm = min(tm, L)
    n = len(a_list)
    has_res = res is not None

    def body(*refs):
        a_refs, w_refs = refs[:n], refs[n:2 * n]
        x_ref, nw_ref = refs[2 * n], refs[2 * n + 1]
        k = 2 * n + 2
        r_ref = refs[k] if has_res else None
        dx_ref, dnw_ref = refs[k + has_res], refs[k + has_res + 1]
        dh = _dot_nt(a_refs[0][...], w_refs[0][...])
        for a_ref, w_ref in zip(a_refs[1:], w_refs[1:]):
            dh += _dot_nt(a_ref[...], w_ref[...])
        dx, dnw = _rms_bwd(dh, x_ref[...], nw_ref[...])
        dx_ref[...] = dx + r_ref[...] if has_res else dx

        @pl.when(pl.program_id(0) == 0)
        def _():
            dnw_ref[...] = jnp.zeros_like(dnw_ref)

        dnw_ref[...] += dnw

    in_specs = [pl.BlockSpec((tm, a.shape[1]), lambda i: (i, 0)) for a in a_list]
    in_specs += [pl.BlockSpec(w.shape, lambda i: (0, 0)) for w in w_list]
    in_specs += [pl.BlockSpec((tm, Dm), lambda i: (i, 0)), pl.BlockSpec((1, Dm), lambda i: (0, 0))]
    args = [*a_list, *w_list, x, nw]
    if has_res:
        in_specs.append(pl.BlockSpec((tm, Dm), lambda i: (i, 0)))
        args.append(res)
    return pl.pallas_call(
        body, name=name, grid=(L // tm,), in_specs=in_specs,
        out_specs=[pl.BlockSpec((tm, Dm), lambda i: (i, 0)), pl.BlockSpec((1, Dm), lambda i: (0, 0))],
        out_shape=[jax.ShapeDtypeStruct((L, Dm), F32), jax.ShapeDtypeStruct((1, Dm), F32)],
        compiler_params=_cparams(("arbitrary",), vmem_mb=56),
    )(*args)


CONV_TN = 512


def _conv_pre(cat, w_ref, b_ref, rows):
    shifted = [pltpu.roll(cat, 3 - k, 0)[8:8 + rows] for k in range(3)] + [cat[8:8 + rows]]
    pre = b_ref[...] + w_ref[3:4, :] * shifted[3]
    for k in range(3):
        pre += w_ref[k:k + 1, :] * shifted[k]
    return pre, shifted


def _conv_fwd(pc, cw, cb, *, tm):
    L, C = pc.shape
    tm = min(tm, L)
    tn = CONV_TN

    def body(u_ref, halo_ref, w_ref, b_ref, o_ref):
        halo = jnp.where(pl.program_id(1) > 0, halo_ref[...], 0.0)
        cat = jnp.concatenate([halo, u_ref[...]], axis=0)
        pre, _ = _conv_pre(cat, w_ref, b_ref, tm)
        o_ref[...] = _silu(pre)

    return pl.pallas_call(
        body, name="conv_fwd", grid=(C // tn, L // tm),
        in_specs=[pl.BlockSpec((tm, tn), lambda j, i: (i, j)),
                  pl.BlockSpec((8, tn), lambda j, i: (jnp.maximum(i * (tm // 8) - 1, 0), j)),
                  pl.BlockSpec((4, tn), lambda j, i: (0, j)), pl.BlockSpec((1, tn), lambda j, i: (0, j))],
        out_specs=pl.BlockSpec((tm, tn), lambda j, i: (i, j)),
        out_shape=jax.ShapeDtypeStruct((L, C), F32),
        compiler_params=_cparams(("parallel", "parallel")),
    )(pc, pc, cw, cb)


def _conv_bwd(pc, dact, cw, cb, *, tm):
    L, C = pc.shape
    tm = min(tm, L)
    tn = CONV_TN
    nt = L // tm

    def body(u_ref, halo_ref, unext_ref, da_ref, danext_ref, w_ref, b_ref, du_ref, dw_ref, db_ref):
        i = pl.program_id(1)
        halo = jnp.where(i > 0, halo_ref[...], 0.0)
        cat = jnp.concatenate([halo, u_ref[...], unext_ref[...]], axis=0)
        pre, shifted = _conv_pre(cat, w_ref, b_ref, tm + 8)
        da = jnp.concatenate([da_ref[...], jnp.where(i < nt - 1, danext_ref[...], 0.0)], axis=0)
        dpre = da * _dsilu(pre)
        du = w_ref[3:4, :] * dpre[0:tm]
        for k in range(3):
            du += w_ref[k:k + 1, :] * pltpu.roll(dpre, tm + 8 - (3 - k), 0)[0:tm]
        du_ref[...] = du.astype(BF16)

        @pl.when(i == 0)
        def _():
            dw_ref[...] = jnp.zeros_like(dw_ref)
            db_ref[...] = jnp.zeros_like(db_ref)

        dp = dpre[0:tm]
        dw_ref[...] += jnp.concatenate(
            [jnp.sum(dp * shifted[k][0:tm], axis=0, keepdims=True) for k in range(4)], axis=0)
        db_ref[...] += jnp.sum(dp, axis=0, keepdims=True)

    nb8 = L // 8
    return pl.pallas_call(
        body, name="conv_bwd", grid=(C // tn, nt),
        in_specs=[pl.BlockSpec((tm, tn), lambda j, i: (i, j)),
                  pl.BlockSpec((8, tn), lambda j, i: (jnp.maximum(i * (tm // 8) - 1, 0), j)),
                  pl.BlockSpec((8, tn), lambda j, i: (jnp.minimum((i + 1) * (tm // 8), nb8 - 1), j)),
                  pl.BlockSpec((tm, tn), lambda j, i: (i, j)),
                  pl.BlockSpec((8, tn), lambda j, i: (jnp.minimum((i + 1) * (tm // 8), nb8 - 1), j)),
                  pl.BlockSpec((4, tn), lambda j, i: (0, j)), pl.BlockSpec((1, tn), lambda j, i: (0, j))],
        out_specs=[pl.BlockSpec((tm, tn), lambda j, i: (i, j)), pl.BlockSpec((4, tn), lambda j, i: (0, j)),
                   pl.BlockSpec((1, tn), lambda j, i: (0, j))],
        out_shape=[jax.ShapeDtypeStruct((L, C), BF16), jax.ShapeDtypeStruct((4, C), F32),
                   jax.ShapeDtypeStruct((1, C), F32)],
        compiler_params=_cparams(("parallel", "arbitrary")),
    )(pc, pc, pc, dact, dact, cw, cb)


def _ssd_common(dtr_ref, dtb_ref, avec_ref, aexp_ref, e_ref, acx_ref, acol_ref, arow_ref):
    q = SSD_Q
    tril = _tril(q)
    dtpre = dtr_ref[...] + dtb_ref[...]
    dt = _softplus(dtpre)
    dtx = _dot_hi(dt, e_ref[...])
    acx_ref[...] = _dot_hi(tril, dtx * aexp_ref[...])
    acol = _dot_hi(tril, dt * avec_ref[...])
    acol_ref[...] = acol
    arow_ref[...] = acol.T
    return dtpre, dt, dtx


def _ssd_fwd(xbc, pb, dtb, avec, aexp, dexp, nw, emat):
    L = xbc.shape[0]
    q = SSD_Q
    nc = L // q

    def body(xbc_ref, z_ref, dtr_ref, dtb_ref, avec_ref, aexp_ref, dexp_ref, nw_ref, e_ref,
             ya_ref, ypre_ref, sin_ref, st_ref, acx_ref, acol_ref, arow_ref, xdt_ref, y_ref):
        @pl.when(pl.program_id(0) == 0)
        def _():
            st_ref[...] = jnp.zeros_like(st_ref)

        sin_ref[...] = st_ref[...]
        _, _, dtx = _ssd_common(dtr_ref, dtb_ref, avec_ref, aexp_ref, e_ref, acx_ref, acol_ref, arow_ref)
        xs = xbc_ref[:, 0:D]
        xdt = xs * dtx
        xdt_ref[...] = xdt
        acx = acx_ref[...]
        alast = acx_ref[q - 1:q, :]
        xdtd = xdt * jnp.exp(alast - acx)
        eac = jnp.exp(acx)
        ealast = jnp.exp(alast)
        causal = _iota2((q, q), 0) >= _iota2((q, q), 1)
        for g in range(2):
            gs = slice(512 * g, 512 * g + 512)
            bm = xbc_ref[:, D + 128 * g:D + 128 * g + 128]
            cm = xbc_ref[:, D + 256 + 128 * g:D + 256 + 128 * g + 128]
            stg = st_ref[:, gs]
            yoff = _dot(cm, stg) * eac[:, gs]
            gmat = _dot_nt(cm, bm)
            for e in range(8):
                h = 8 * g + e
                hs = slice(64 * h, 64 * h + 64)
                col = acol_ref[:, h:h + 1]
                row = arow_ref[h:h + 1, :]
                lm = jnp.exp(jnp.where(causal, col - row, -1e30))
                y_ref[:, hs] = _dot(gmat * lm, xdt_ref[:, hs])
            y_ref[:, gs] += yoff + dexp_ref[:, gs] * xs[:, gs]
            st_ref[:, gs] = stg * ealast[:, gs] + _dot_tn(bm, xdtd[:, gs])
        ypre_ref[...] = y_ref[...]
        for g in range(2):
            gs = slice(512 * g, 512 * g + 512)
            yz = y_ref[:, gs] * _silu(z_ref[:, gs])
            ya_ref[:, gs] = _rms_fwd(yz, nw_ref[:, gs]).astype(BF16)

    vec = lambda n: pl.BlockSpec((1, n), lambda c: (0, 0))
    return pl.pallas_call(
        body, name="ssd_fwd", grid=(nc,),
        in_specs=[pl.BlockSpec((q, 1536), lambda c: (c, 0)), pl.BlockSpec((q, D), lambda c: (c, 0)),
                  pl.BlockSpec((q, 128), lambda c: (c, 8)), vec(128), vec(128), vec(D), vec(D), vec(D),
                  pl.BlockSpec((128, D), lambda c: (0, 0))],
        out_specs=[pl.BlockSpec((q, D), lambda c: (c, 0)), pl.BlockSpec((q, D), lambda c: (c, 0)),
                   pl.BlockSpec((128, D), lambda c: (c, 0))],
        out_shape=[jax.ShapeDtypeStruct((L, D), BF16), jax.ShapeDtypeStruct((L, D), F32),
                   jax.ShapeDtypeStruct((nc * 128, D), F32)],
        scratch_shapes=[pltpu.VMEM((128, D), F32), pltpu.VMEM((q, D), F32), pltpu.VMEM((q, 128), F32),
                        pltpu.VMEM((128, q), F32), pltpu.VMEM((q, D), F32), pltpu.VMEM((q, D), F32)],
        compiler_params=_cparams(("arbitrary",)),
    )(xbc, pb, pb, dtb, avec, aexp, dexp, nw, emat)


def _ssd_bwd(xbc, pb, ypre, sin, dya, dtb, avec, aexp, dexp, nw, emat):
    L = xbc.shape[0]
    q = SSD_Q
    nc = L // q

    def body(xbc_ref, z_ref, dtr_ref, ypre_ref, sin_ref, dya_ref, dtb_ref, avec_ref, aexp_ref, dexp_ref, nw_ref,
             e_ref, dpb_ref, dxbc_ref, dnw_ref, ddtb_ref, da_ref, ddx_ref,
             dst_ref, acx_ref, acol_ref, arow_ref, xdt_ref, dxdt_ref, dy_ref, dacx_ref):
        @pl.when(pl.program_id(0) == 0)
        def _():
            dst_ref[...] = jnp.zeros_like(dst_ref)
            dnw_ref[...] = jnp.zeros_like(dnw_ref)
            ddtb_ref[...] = jnp.zeros_like(ddtb_ref)
            da_ref[...] = jnp.zeros_like(da_ref)
            ddx_ref[...] = jnp.zeros_like(ddx_ref)

        dtpre, dt, dtx = _ssd_common(dtr_ref, dtb_ref, avec_ref, aexp_ref, e_ref, acx_ref, acol_ref, arow_ref)
        xs = xbc_ref[:, 0:D]
        xdt = xs * dtx
        xdt_ref[...] = xdt
        acx = acx_ref[...]
        alast = acx_ref[q - 1:q, :]
        dec_end = jnp.exp(alast - acx)
        xdtd = xdt * dec_end
        eac = jnp.exp(acx)
        ealast = jnp.exp(alast)
        for g in range(2):
            gs = slice(512 * g, 512 * g + 512)
            y = ypre_ref[:, gs]
            z = z_ref[:, gs]
            sz = _silu(z)
            dyz, dnw = _rms_bwd(dya_ref[:, gs], y * sz, nw_ref[:, gs])
            dnw_ref[:, gs] += dnw
            dy_ref[:, gs] = dyz * sz
            dpb_ref[:, gs] = (dyz * y * _dsilu(z)).astype(BF16)
        dy = dy_ref[...]
        ddx_ref[...] += jnp.sum(dy * xs, axis=0, keepdims=True)
        ri = _iota2((q, q), 0)
        ci = _iota2((q, q), 1)
        causal = ri >= ci
        causal_t = ri <= ci
        dacol = jnp.zeros((q, 128), F32)
        dacol_t = jnp.zeros((128, q), F32)
        last_row = _iota2((q, 512), 0) == q - 1
        for g in range(2):
            gs = slice(512 * g, 512 * g + 512)
            bm = xbc_ref[:, D + 128 * g:D + 128 * g + 128]
            cm = xbc_ref[:, D + 256 + 128 * g:D + 256 + 128 * g + 128]
            stg = sin_ref[:, gs]
            dstg = dst_ref[:, gs]
            dyg = dy[:, gs]
            yoff = _dot(cm, stg) * eac[:, gs]
            dwm = dyg * eac[:, gs]
            dcm = _dot_nt(dwm, stg)
            dstin = _dot_tn(cm, dwm)
            dacx_g = dyg * yoff
            dxdtd = _dot(bm, dstg)
            dbm = _dot_nt(xdtd[:, gs], dstg)
            t = dxdtd * xdtd[:, gs]
            dacx_g -= t
            dalast = jnp.sum(t, axis=0, keepdims=True) + jnp.sum(dstg * stg, axis=0, keepdims=True) * ealast[:, gs]
            dst_ref[:, gs] = dstin + dstg * ealast[:, gs]
            dacx_ref[:, gs] = dacx_g + jnp.where(last_row, dalast, 0.0)
            gmat = _dot_nt(cm, bm)
            gmat_t = _dot_nt(bm, cm)
            dg = jnp.zeros((q, q), F32)
            for e in range(8):
                h = 8 * g + e
                hs = slice(64 * h, 64 * h + 64)
                col = acol_ref[:, h:h + 1]
                row = arow_ref[h:h + 1, :]
                lm = jnp.exp(jnp.where(causal, col - row, -1e30))
                lm_t = jnp.exp(jnp.where(causal_t, row - col, -1e30))
                dyh = dy_ref[:, hs]
                dm = _dot_nt(dyh, xdt_ref[:, hs])
                dxdt_ref[:, hs] = _dot(gmat_t * lm_t, dyh)
                dml = dm * lm
                dg += dml
                p = dml * gmat
                dacol += jnp.where(ci == h, jnp.sum(p, axis=1, keepdims=True), 0.0)
                dacol_t -= jnp.where(ri == h, jnp.sum(p, axis=0, keepdims=True), 0.0)
            dcm += _dot(dg, bm)
            dbm += _dot_tn(dg, cm)
            dxbc_ref[:, D + 128 * g:D + 128 * g + 128] = dbm
            dxbc_ref[:, D + 256 + 128 * g:D + 256 + 128 * g + 128] = dcm
            dxdt_ref[:, gs] += dxdtd * dec_end[:, gs]
        dxdt = dxdt_ref[...]
        dacum = dacol + dacol_t.T + _dot_nt_hi(dacx_ref[...], e_ref[...])
        da = _dot_hi(_triu(q), dacum)
        ddt = da * avec_ref[...] + _dot_nt_hi(dxdt * xs, e_ref[...])
        da_ref[...] += jnp.sum(da * dt, axis=0, keepdims=True) * avec_ref[...]
        dxbc_ref[:, 0:D] = dexp_ref[...] * dy + dxdt * dtx
        ddtr = ddt * _sigmoid(dtpre)
        ddtb_ref[...] += jnp.sum(ddtr, axis=0, keepdims=True)
        dpb_ref[:, D:D + 128] = ddtr.astype(BF16)
        dpb_ref[:, D + 128:1536] = jnp.zeros((q, 384), BF16)

    rev = lambda c: nc - 1 - c
    vec = lambda n: pl.BlockSpec((1, n), lambda c: (0, 0))
    return pl.pallas_call(
        body, name="ssd_bwd", grid=(nc,),
        in_specs=[pl.BlockSpec((q, 1536), lambda c: (rev(c), 0)), pl.BlockSpec((q, D), lambda c: (rev(c), 0)),
                  pl.BlockSpec((q, 128), lambda c: (rev(c), 8)), pl.BlockSpec((q, D), lambda c: (rev(c), 0)),
                  pl.BlockSpec((128, D), lambda c: (rev(c), 0)), pl.BlockSpec((q, D), lambda c: (rev(c), 0)),
                  vec(128), vec(128), vec(D), vec(D), vec(D), pl.BlockSpec((128, D), lambda c: (0, 0))],
        out_specs=[pl.BlockSpec((q, 1536), lambda c: (rev(c), 0)), pl.BlockSpec((q, 1536), lambda c: (rev(c), 0)),
                   vec(D), vec(128), vec(128), vec(D)],
        out_shape=[jax.ShapeDtypeStruct((L, 1536), BF16), jax.ShapeDtypeStruct((L, 1536), F32),
                   jax.ShapeDtypeStruct((1, D), F32), jax.ShapeDtypeStruct((1, 128), F32),
                   jax.ShapeDtypeStruct((1, 128), F32), jax.ShapeDtypeStruct((1, D), F32)],
        scratch_shapes=[pltpu.VMEM((128, D), F32), pltpu.VMEM((q, D), F32), pltpu.VMEM((q, 128), F32),
                        pltpu.VMEM((128, q), F32), pltpu.VMEM((q, D), F32), pltpu.VMEM((q, D), F32),
                        pltpu.VMEM((q, D), F32), pltpu.VMEM((q, D), F32)],
        compiler_params=_cparams(("arbitrary",)),
    )(xbc, pb, pb, ypre, sin, dya, dtb, avec, aexp, dexp, nw, emat)


def _hg_gates(hq, hf, hgl_ref, b_ref):
    lb = 1.0 / (1.0 + jnp.exp(hgl_ref[1:2, :] - hgl_ref[0:1, :]))
    qf = _silu(hq)
    sg = _sigmoid(hf)
    f = lb + (1.0 - lb) * sg
    b_ref[...] = _dot_hi(_tril(HG_STEP), jnp.log(f))
    return lb, qf, sg, f


def _hg_factors(qf, kf, b_ref):
    s, n = HG_SUB, HG_STEP
    b = b_ref[...]
    blast = b_ref[n - 1:n, :]
    m0, mb, m1 = b_ref[s // 2 - 1:s // 2, :], b_ref[s - 1:s, :], b_ref[s + s // 2 - 1:s + s // 2, :]
    b0, b1 = b[0:s], b[s:n]
    q0, q1, k0, k1 = qf[0:s], qf[s:n], kf[0:s], kf[s:n]
    fac = dict(
        eb=jnp.exp(blast), eq=jnp.exp(b), ek=jnp.exp(blast - b),
        eq0=jnp.exp(b0 - m0), ek0=jnp.exp(m0 - b0), eq1=jnp.exp(b1 - m1), ek1=jnp.exp(m1 - b1),
        eqb=jnp.exp(b1 - mb), ekb=jnp.exp(mb - b0))
    rd = lambda t: t.astype(BF16).astype(F32)
    val = dict(qe=qf * fac["eq"], ke=kf * fac["ek"], qm0=rd(q0 * fac["eq0"]), km0=rd(k0 * fac["ek0"]),
               qm1=rd(q1 * fac["eq1"]), km1=rd(k1 * fac["ek1"]), qb=rd(q1 * fac["eqb"]), kb=rd(k0 * fac["ekb"]))
    return fac, val


def _hgrn_fwd(pa, hgl, nwx):
    L = pa.shape[0]
    n, s = HG_STEP, HG_SUB
    nc = L // n

    def body(hq_ref, hf_ref, hi_ref, hg_ref, hgl_ref, nw_ref, ob_ref, opre_ref, sin_ref, st_ref, b_ref):
        @pl.when(pl.program_id(0) == 0)
        def _():
            st_ref[...] = jnp.zeros_like(st_ref)

        sin_ref[...] = st_ref[...]
        _, qf, _, f = _hg_gates(hq_ref[...], hf_ref[...], hgl_ref, b_ref)
        fac, val = _hg_factors(qf, 1.0 - f, b_ref)
        causal = _iota2((s, s), 0) >= _iota2((s, s), 1)
        for h in range(HG_HEADS):
            hs = slice(128 * h, 128 * h + 128)
            sth = st_ref[:, hs]
            v = hi_ref[:, hs]
            v0, v1 = v[0:s], v[s:n]
            a00 = jnp.where(causal, _dot_nt(val["qm0"][:, hs], val["km0"][:, hs]), 0.0)
            a11 = jnp.where(causal, _dot_nt(val["qm1"][:, hs], val["km1"][:, hs]), 0.0)
            a10 = _dot_nt(val["qb"][:, hs], val["kb"][:, hs])
            o = _dot_nt(val["qe"][:, hs], sth) + jnp.concatenate(
                [_dot(a00, v0), _dot(a10, v0) + _dot(a11, v1)], axis=0)
            st_ref[:, hs] = sth * fac["eb"][:, hs] + _dot_tn(v, val["ke"][:, hs])
            opre_ref[:, hs] = o
            ob_ref[:, hs] = (_rms_fwd(o, nw_ref[:, hs]) * _silu(hg_ref[:, hs])).astype(BF16)

    blk = lambda j: pl.BlockSpec((n, D), lambda c: (c, j))
    return pl.pallas_call(
        body, name="hgrn_fwd", grid=(nc,),
        in_specs=[blk(0), blk(1), blk(2), blk(3), pl.BlockSpec((2, D), lambda c: (0, 0)),
                  pl.BlockSpec((1, D), lambda c: (0, 0))],
        out_specs=[blk(0), blk(0), blk(0)],
        out_shape=[jax.ShapeDtypeStruct((L, D), BF16), jax.ShapeDtypeStruct((L, D), F32),
                   jax.ShapeDtypeStruct((nc * 128, D), F32)],
        scratch_shapes=[pltpu.VMEM((128, D), F32), pltpu.VMEM((n, D), F32)],
        compiler_params=_cparams(("arbitrary",)),
    )(pa, pa, pa, pa, hgl, nwx)


def _hgrn_bwd(pa, opre, sin, dob, hgl, nwx):
    L = pa.shape[0]
    n, s = HG_STEP, HG_SUB
    nc = L // n

    def body(hq_ref, hf_ref, hi_ref, hg_ref, opre_ref, sin_ref, dob_ref, hgl_ref, nw_ref,
             dpa_ref, dhgl_ref, dnw_ref, dst_ref, b_ref, dlb_ref, dq_ref, dk_ref, db_ref):
        i = pl.program_id(0)

        @pl.when(i == 0)
        def _():
            dst_ref[...] = jnp.zeros_like(dst_ref)
            dlb_ref[...] = jnp.zeros_like(dlb_ref)
            dnw_ref[...] = jnp.zeros_like(dnw_ref)

        hq = hq_ref[...]
        lb, qf, sg, f = _hg_gates(hq, hf_ref[...], hgl_ref, b_ref)
        kf = 1.0 - f
        fac, val = _hg_factors(qf, kf, b_ref)
        ri, ci = _iota2((s, s), 0), _iota2((s, s), 1)
        causal, causal_t = ri >= ci, ri <= ci
        last_row = _iota2((n, 128), 0) == n - 1
        for h in range(HG_HEADS):
            hs = slice(128 * h, 128 * h + 128)
            o = opre_ref[:, hs]
            gate = hg_ref[:, hs]
            dout = dob_ref[:, hs]
            sgate = _silu(gate)
            do, dnw = _rms_bwd(dout * sgate, o, nw_ref[:, hs])
            dnw_ref[:, hs] += dnw
            dpa_ref[:, 3 * D + 128 * h:3 * D + 128 * h + 128] = (
                dout * _rms_fwd(o, nw_ref[:, hs]) * _dsilu(gate)).astype(BF16)
            sth = sin_ref[:, hs]
            dsth = dst_ref[:, hs]
            v = hi_ref[:, hs]
            v0, v1 = v[0:s], v[s:n]
            do0, do1 = do[0:s], do[s:n]
            qe, ke = val["qe"][:, hs], val["ke"][:, hs]
            qm0, km0, qm1, km1 = val["qm0"][:, hs], val["km0"][:, hs], val["qm1"][:, hs], val["km1"][:, hs]
            qb, kb = val["qb"][:, hs], val["kb"][:, hs]
            dqe = _dot(do, sth)
            dstin = _dot_tn(do, qe)
            a00t = jnp.where(causal_t, _dot_nt(km0, qm0), 0.0)
            a11t = jnp.where(causal_t, _dot_nt(km1, qm1), 0.0)
            a10t = _dot_nt(kb, qb)
            dat00 = jnp.where(causal, _dot_nt(do0, v0), 0.0)
            dat11 = jnp.where(causal, _dot_nt(do1, v1), 0.0)
            dat10 = _dot_nt(do1, v0)
            dat00t = jnp.where(causal_t, _dot_nt(v0, do0), 0.0)
            dat11t = jnp.where(causal_t, _dot_nt(v1, do1), 0.0)
            dat10t = _dot_nt(v0, do1)
            dv = jnp.concatenate([_dot(a00t, do0) + _dot(a10t, do1), _dot(a11t, do1)], axis=0)
            dqm0, dkm0 = _dot(dat00, km0), _dot(dat00t, qm0)
            dqm1, dkm1 = _dot(dat11, km1), _dot(dat11t, qm1)
            dqb, dkb = _dot(dat10, kb), _dot(dat10t, qb)
            dke = _dot(v, dsth)
            dv += _dot_nt(ke, dsth)
            deb = jnp.sum(dsth * sth, axis=0, keepdims=True)
            dst_ref[:, hs] = dstin + dsth * fac["eb"][:, hs]
            dq = dqe * fac["eq"][:, hs] + jnp.concatenate(
                [dqm0 * fac["eq0"][:, hs], dqm1 * fac["eq1"][:, hs] + dqb * fac["eqb"][:, hs]], axis=0)
            dk = dke * fac["ek"][:, hs] + jnp.concatenate(
                [dkm0 * fac["ek0"][:, hs] + dkb * fac["ekb"][:, hs], dkm1 * fac["ek1"][:, hs]], axis=0)
            tke = dke * ke
            db = dqe * qe - tke + jnp.concatenate(
                [dqm0 * qm0 - dkm0 * km0 - dkb * kb, dqm1 * qm1 - dkm1 * km1 + dqb * qb], axis=0)
            dblast = jnp.sum(tke, axis=0, keepdims=True) + deb * fac["eb"][:, hs]
            db_ref[:, hs] = db + jnp.where(last_row, dblast, 0.0)
            dq_ref[:, hs] = dq
            dk_ref[:, hs] = dk
            dpa_ref[:, 2 * D + 128 * h:2 * D + 128 * h + 128] = dv.astype(BF16)
        dg = _dot_hi(_triu(n), db_ref[...])
        df = dg / f - dk_ref[...]
        dpa_ref[:, D:2 * D] = (df * (1.0 - lb) * sg * (1.0 - sg)).astype(BF16)
        dpa_ref[:, 0:D] = (dq_ref[...] * _dsilu(hq)).astype(BF16)
        dlb_ref[...] += jnp.sum(df * (1.0 - sg), axis=0, keepdims=True)

        @pl.when(i == nc - 1)
        def _():
            d0 = dlb_ref[...] * lb * (1.0 - lb)
            dhgl_ref[...] = jnp.concatenate([d0, -d0], axis=0)

    rev = lambda c: nc - 1 - c
    blk = lambda j: pl.BlockSpec((n, D), lambda c: (rev(c), j))
    return pl.pallas_call(
        body, name="hgrn_bwd", grid=(nc,),
        in_specs=[blk(0), blk(1), blk(2), blk(3), blk(0), blk(0), blk(0), pl.BlockSpec((2, D), lambda c: (0, 0)),
                  pl.BlockSpec((1, D), lambda c: (0, 0))],
        out_specs=[pl.BlockSpec((n, 4 * D), lambda c: (rev(c), 0)), pl.BlockSpec((2, D), lambda c: (0, 0)),
                   pl.BlockSpec((1, D), lambda c: (0, 0))],
        out_shape=[jax.ShapeDtypeStruct((L, 4 * D), BF16), jax.ShapeDtypeStruct((2, D), F32),
                   jax.ShapeDtypeStruct((1, D), F32)],
        scratch_shapes=[pltpu.VMEM((128, D), F32), pltpu.VMEM((n, D), F32), pltpu.VMEM((1, D), F32),
                        pltpu.VMEM((n, D), F32), pltpu.VMEM((n, D), F32), pltpu.VMEM((n, D), F32)],
        compiler_params=_cparams(("arbitrary",)),
    )(pa, pa, pa, pa, opre, sin, dob, hgl, nwx)


XA_SCALE = XA_DH ** -0.5


def _xa_probs(qh, kmh):
    sc = _dot_nt(qh, kmh) * XA_SCALE
    p = jnp.exp(sc - jnp.max(sc, axis=1, keepdims=True))
    return p / jnp.sum(p, axis=1, keepdims=True)


def _xattn_fwd(x1, nw, wq, kv, wo, *, tm):
    L = x1.shape[0]
    tm = min(tm, L)

    def body(x_ref, nw_ref, wq_ref, kv_ref, wo_ref, o_ref, ox_ref):
        x = x_ref[...]
        q = _dot(_rms_fwd(x, nw_ref[...]), wq_ref[...])
        for h in range(XA_HEADS):
            hs = slice(XA_DH * h, XA_DH * h + XA_DH)
            p = _xa_probs(q[:, hs], kv_ref[:, hs])
            ox_ref[:, hs] = _dot(p, kv_ref[:, D + XA_DH * h:D + XA_DH * h + XA_DH])
        o_ref[...] = x + _dot(ox_ref[...], wo_ref[...])

    full = lambda a: pl.BlockSpec(a.shape, lambda i: (0, 0))
    return pl.pallas_call(
        body, name="xattn_fwd", grid=(L // tm,),
        in_specs=[pl.BlockSpec((tm, D), lambda i: (i, 0)), full(nw), full(wq), full(kv), full(wo)],
        out_specs=pl.BlockSpec((tm, D), lambda i: (i, 0)),
        out_shape=jax.ShapeDtypeStruct((L, D), F32),
        scratch_shapes=[pltpu.VMEM((tm, D), F32)],
        compiler_params=_cparams(("parallel",)),
    )(x1, nw, wq, kv, wo)


def _xattn_bwd(x1, dx2, nw, wq, kv, wo, *, tm):
    L = x1.shape[0]
    tm = min(tm, L)

    def body(x_ref, dx2_ref, nw_ref, wq_ref, kv_ref, wo_ref, dx1_ref, h_ref, dq_ref, ox_ref, dkv_ref, dnw_ref,
             dqs_ref):
        @pl.when(pl.program_id(0) == 0)
        def _():
            dkv_ref[...] = jnp.zeros_like(dkv_ref)
            dnw_ref[...] = jnp.zeros_like(dnw_ref)

        x = x_ref[...]
        dx2 = dx2_ref[...]
        hn = _rms_fwd(x, nw_ref[...]).astype(BF16)
        h_ref[...] = hn
        q = _dot(hn, wq_ref[...])
        dox = _dot_nt(dx2, wo_ref[...])
        for h in range(XA_HEADS):
            hs = slice(XA_DH * h, XA_DH * h + XA_DH)
            vs = slice(D + XA_DH * h, D + XA_DH * h + XA_DH)
            qh, kmh, vmh, doxh = q[:, hs], kv_ref[:, hs], kv_ref[:, vs], dox[:, hs]
            p = _xa_probs(qh, kmh)
            ox_ref[:, hs] = _dot(p, vmh).astype(BF16)
            dp = _dot_nt(doxh, vmh)
            dkv_ref[:, vs] += _dot_tn(p, doxh)
            ds = p * (dp - jnp.sum(dp * p, axis=1, keepdims=True)) * XA_SCALE
            dqs_ref[:, hs] = _dot(ds, kmh)
            dkv_ref[:, hs] += _dot_tn(ds, qh)
        dq = dqs_ref[...]
        dq_ref[...] = dq.astype(BF16)
        dx, dnw = _rms_bwd(_dot_nt(dq, wq_ref[...]), x, nw_ref[...])
        dx1_ref[...] = dx2 + dx
        dnw_ref[...] += dnw

    full = lambda a: pl.BlockSpec(a.shape, lambda i: (0, 0))
    row = pl.BlockSpec((tm, D), lambda i: (i, 0))
    return pl.pallas_call(
        body, name="xattn_bwd", grid=(L // tm,),
        in_specs=[row, row, full(nw), full(wq), full(kv), full(wo)],
        out_specs=[row, row, row, row, pl.BlockSpec((MEM_LEN, 2 * D), lambda i: (0, 0)),
                   pl.BlockSpec((1, D), lambda i: (0, 0))],
        out_shape=[jax.ShapeDtypeStruct((L, D), F32), jax.ShapeDtypeStruct((L, D), BF16),
                   jax.ShapeDtypeStruct((L, D), BF16), jax.ShapeDtypeStruct((L, D), BF16),
                   jax.ShapeDtypeStruct((MEM_LEN, 2 * D), F32), jax.ShapeDtypeStruct((1, D), F32)],
        scratch_shapes=[pltpu.VMEM((tm, D), F32)],
        compiler_params=_cparams(("arbitrary",)),
    )(x1, dx2, nw, wq, kv, wo)


FFN_TF = 256


def _ffn_fwd(x2, nw, wg, wu, wd, *, tm):
    L = x2.shape[0]
    tm = min(tm, L)
    tf = FFN_TF
    nf = FFN // tf

    def body(x_ref, nw_ref, wg_ref, wu_ref, wd_ref, o_ref, h_ref, acc_ref):
        j = pl.program_id(1)

        @pl.when(j == 0)
        def _():
            h_ref[...] = _rms_fwd(x_ref[...], nw_ref[...]).astype(BF16)
            acc_ref[...] = jnp.zeros_like(acc_ref)

        h = h_ref[...]
        a = _silu(_dot(h, wg_ref[...])) * _dot(h, wu_ref[...])
        acc_ref[...] += _dot(a, wd_ref[...])

        @pl.when(j == nf - 1)
        def _():
            o_ref[...] = x_ref[...] + acc_ref[...]

    return pl.pallas_call(
        body, name="ffn_fwd", grid=(L // tm, nf),
        in_specs=[pl.BlockSpec((tm, D), lambda i, j: (i, 0)), pl.BlockSpec((1, D), lambda i, j: (0, 0)),
                  pl.BlockSpec((D, tf), lambda i, j: (0, j)), pl.BlockSpec((D, tf), lambda i, j: (0, j)),
                  pl.BlockSpec((tf, D), lambda i, j: (j, 0))],
        out_specs=pl.BlockSpec((tm, D), lambda i, j: (i, 0)),
        out_shape=jax.ShapeDtypeStruct((L, D), F32),
        scratch_shapes=[pltpu.VMEM((tm, D), BF16), pltpu.VMEM((tm, D), F32)],
        compiler_params=_cparams(("parallel", "arbitrary")),
    )(x2, nw, wg, wu, wd)


def _ffn_bwd(x2, dx3, nw, wg, wu, wd, *, tm):
    L = x2.shape[0]
    tm = min(tm, L)
    tf = FFN_TF
    nf = FFN // tf

    def body(x_ref, dx3_ref, nw_ref, wg_ref, wu_ref, wd_ref, dx2_ref, hn_ref, a_ref, dg_ref, du_ref, dnw_ref,
             h_ref, d3_ref, acc_ref):
        i, j = pl.program_id(0), pl.program_id(1)

        @pl.when(j == 0)
        def _():
            hn = _rms_fwd(x_ref[...], nw_ref[...]).astype(BF16)
            h_ref[...] = hn
            hn_ref[...] = hn
            d3_ref[...] = dx3_ref[...].astype(BF16)
            acc_ref[...] = jnp.zeros_like(acc_ref)

        h = h_ref[...]
        g = _dot(h, wg_ref[...])
        u = _dot(h, wu_ref[...])
        sg = _silu(g)
        a_ref[...] = (sg * u).astype(BF16)
        da = _dot_nt(d3_ref[...], wd_ref[...])
        dg = (da * u * _dsilu(g)).astype(BF16)
        du = (da * sg).astype(BF16)
        dg_ref[...] = dg
        du_ref[...] = du
        acc_ref[...] += _dot_nt(dg, wg_ref[...]) + _dot_nt(du, wu_ref[...])

        @pl.when(jnp.logical_and(i == 0, j == 0))
        def _():
            dnw_ref[...] = jnp.zeros_like(dnw_ref)

        @pl.when(j == nf - 1)
        def _():
            dx, dnw = _rms_bwd(acc_ref[...], x_ref[...], nw_ref[...])
            dx2_ref[...] = dx3_ref[...] + dx
            dnw_ref[...] += dnw

    row = pl.BlockSpec((tm, D), lambda i, j: (i, 0))
    colblk = pl.BlockSpec((tm, tf), lambda i, j: (i, j))
    return pl.pallas_call(
        body, name="ffn_bwd", grid=(L // tm, nf),
        in_specs=[row, row, pl.BlockSpec((1, D), lambda i, j: (0, 0)),
                  pl.BlockSpec((D, tf), lambda i, j: (0, j)), pl.BlockSpec((D, tf), lambda i, j: (0, j)),
                  pl.BlockSpec((tf, D), lambda i, j: (j, 0))],
        out_specs=[row, row, colblk, colblk, colblk, pl.BlockSpec((1, D), lambda i, j: (0, 0))],
        out_shape=[jax.ShapeDtypeStruct((L, D), F32), jax.ShapeDtypeStruct((L, D), BF16),
                   jax.ShapeDtypeStruct((L, FFN), BF16), jax.ShapeDtypeStruct((L, FFN), BF16),
                   jax.ShapeDtypeStruct((L, FFN), BF16), jax.ShapeDtypeStruct((1, D), F32)],
        scratch_shapes=[pltpu.VMEM((tm, D), BF16), pltpu.VMEM((tm, D), BF16), pltpu.VMEM((tm, D), F32)],
        compiler_params=_cparams(("arbitrary", "arbitrary")),
    )(x2, dx3, nw, wg, wu, wd)


def _final(x3, tgt, nw, *, tm):
    L = x3.shape[0]
    tm = min(tm, L)

    def body(x_ref, t_ref, nw_ref, dx_ref, dxb_ref, loss_ref, dnw_ref):
        @pl.when(pl.program_id(0) == 0)
        def _():
            loss_ref[...] = jnp.zeros_like(loss_ref)
            dnw_ref[...] = jnp.zeros_like(dnw_ref)

        x = x_ref[...]
        w = nw_ref[...]
        err = _rms_fwd(x, w) - t_ref[...]
        part = 0.5 * jnp.sum(jnp.sum(err * err, axis=1, keepdims=True), axis=0, keepdims=True) * (1.0 / D)
        loss_ref[...] += jnp.where(_iota2((1, 128), 1) == 0, part, 0.0)
        dx, dnw = _rms_bwd(err * (1.0 / D), x, w)
        dx_ref[...] = dx
        dxb_ref[...] = dx.astype(BF16)
        dnw_ref[...] += dnw

    row = pl.BlockSpec((tm, D), lambda i: (i, 0))
    return pl.pallas_call(
        body, name="final_loss", grid=(L // tm,),
        in_specs=[row, row, pl.BlockSpec((1, D), lambda i: (0, 0))],
        out_specs=[row, row, pl.BlockSpec((1, 128), lambda i: (0, 0)), pl.BlockSpec((1, D), lambda i: (0, 0))],
        out_shape=[jax.ShapeDtypeStruct((L, D), F32), jax.ShapeDtypeStruct((L, D), BF16),
                   jax.ShapeDtypeStruct((1, 128), F32), jax.ShapeDtypeStruct((1, D), F32)],
        compiler_params=_cparams(("arbitrary",)),
    )(x3, tgt, nw)


def _adamw(parts, w, m, v, *, tr, name):
    n_parts, R, C = parts.shape
    tr = min(tr, R)
    c1 = 1.0 / (1.0 - ADAM_B1 ** ADAM_STEP)
    c2 = 1.0 / (1.0 - ADAM_B2 ** ADAM_STEP)

    def body(p_ref, w_ref, m_ref, v_ref, g_ref, d_ref, nm_ref, nv_ref):
        g = p_ref[0]
        for k in range(1, n_parts):
            g = g + p_ref[k]
        nm = ADAM_B1 * m_ref[...] + (1.0 - ADAM_B1) * g
        nv = ADAM_B2 * v_ref[...] + (1.0 - ADAM_B2) * (g * g)
        g_ref[...] = g
        nm_ref[...] = nm
        nv_ref[...] = nv
        d_ref[...] = -ADAM_LR * ((nm * c1) / (jnp.sqrt(nv * c2) + ADAM_EPS) + ADAM_WD * w_ref[...])

    blk = pl.BlockSpec((tr, C), lambda i: (i, 0))
    sds = jax.ShapeDtypeStruct((R, C), F32)
    return pl.pallas_call(
        body, name=name, grid=(R // tr,),
        in_specs=[pl.BlockSpec((n_parts, tr, C), lambda i: (0, i, 0)), blk, blk, blk],
        out_specs=[blk, blk, blk, blk], out_shape=[sds, sds, sds, sds],
        compiler_params=_cparams(("parallel",)),
    )(parts, w, m, v)


def _position():
    return lax.axis_index("x"), lax.axis_index("y"), lax.axis_index("c")


def _allgather(xp):
    R, C = xp.shape

    def body(x_ref, out_ref, send_sems, recv_sems, local_sem):
        x, y, c = _position()
        me, sibling = (x, y, c), (x, y, 1 - c)
        chips = [(1 - x, y), (x, 1 - y), (1 - x, 1 - y)]

        def rows(px, py, pc):
            return out_ref.at[4 * px + 2 * py + pc]

        def copy(k, block, to, src=None):
            return pltpu.make_async_remote_copy(
                src_ref=rows(*block) if src is None else src, dst_ref=rows(*block),
                send_sem=send_sems.at[k], recv_sem=recv_sems.at[k], device_id=to, device_id_type=MESH)

        mine = pltpu.make_async_copy(x_ref, rows(*me), local_sem)
        mine.start()
        first = [copy(0, me, sibling, src=x_ref)]
        first += [copy(1 + j, me, (*chip, c), src=x_ref) for j, chip in enumerate(chips)]
        for cp in first:
            cp.start()
        passed = [copy(4 + j, (*chip, c), sibling) for j, chip in enumerate(chips)]
        for j, chip in enumerate(chips):
            copy(1 + j, (*chip, c), me).wait_recv()
            passed[j].start()
        copy(0, sibling, me).wait_recv()
        for j, chip in enumerate(chips):
            copy(4 + j, (*chip, 1 - c), me).wait_recv()
        for cp in first + passed:
            cp.wait_send()
        mine.wait()

    return pl.pallas_call(
        body, name="allgather_weights",
        out_shape=jax.ShapeDtypeStruct((N_DEV, R, C), xp.dtype),
        in_specs=[pl.BlockSpec(memory_space=pltpu.HBM)], out_specs=pl.BlockSpec(memory_space=pltpu.HBM),
        scratch_shapes=[pltpu.SemaphoreType.DMA((7,)), pltpu.SemaphoreType.DMA((7,)), pltpu.SemaphoreType.DMA],
    )(xp)


def _exchange(gr):
    _, R, C = gr.shape

    def body(g_ref, out_ref, send_sems, recv_sems, local_sem):
        x, y, c = _position()
        me = 4 * x + 2 * y + c
        mine = pltpu.make_async_copy(g_ref.at[me], out_ref.at[me], local_sem)
        mine.start()
        copies = []
        for k in range(1, N_DEV):
            px = 1 - x if k & 4 else x
            py = 1 - y if k & 2 else y
            pc = 1 - c if k & 1 else c
            cp = pltpu.make_async_remote_copy(
                src_ref=g_ref.at[4 * px + 2 * py + pc], dst_ref=out_ref.at[me],
                send_sem=send_sems.at[k - 1], recv_sem=recv_sems.at[k - 1],
                device_id=(px, py, pc), device_id_type=MESH)
            cp.start()
            copies.append(cp)
        for cp in copies:
            cp.wait()
        mine.wait()

    return pl.pallas_call(
        body, name="exchange_grads",
        out_shape=jax.ShapeDtypeStruct(gr.shape, gr.dtype),
        in_specs=[pl.BlockSpec(memory_space=pltpu.HBM)], out_specs=pl.BlockSpec(memory_space=pltpu.HBM),
        scratch_shapes=[pltpu.SemaphoreType.DMA((7,)), pltpu.SemaphoreType.DMA((7,)), pltpu.SemaphoreType.DMA],
    )(gr)


def _small_allreduce(sp):
    R, C = sp.shape

    def body(s_ref, out_ref, buf_ref, send_sems, recv_sems):
        x, y, c = _position()
        me = 4 * x + 2 * y + c
        buf_ref[me] = s_ref[...]
        copies = []
        for k in range(1, N_DEV):
            px = 1 - x if k & 4 else x
            py = 1 - y if k & 2 else y
            pc = 1 - c if k & 1 else c
            cp = pltpu.make_async_remote_copy(
                src_ref=s_ref, dst_ref=buf_ref.at[me], send_sem=send_sems.at[k - 1], recv_sem=recv_sems.at[k - 1],
                device_id=(px, py, pc), device_id_type=MESH)
            cp.start()
            copies.append(cp)
        for cp in copies:
            cp.wait()
        tot = buf_ref[0]
        for k in range(1, N_DEV):
            tot = tot + buf_ref[k]
        out_ref[...] = tot

    return pl.pallas_call(
        body, name="allreduce_small",
        out_shape=jax.ShapeDtypeStruct((R, C), F32),
        in_specs=[pl.BlockSpec(memory_space=pltpu.VMEM)], out_specs=pl.BlockSpec(memory_space=pltpu.VMEM),
        scratch_shapes=[pltpu.VMEM((N_DEV, R, C), F32), pltpu.SemaphoreType.DMA((7,)), pltpu.SemaphoreType.DMA((7,))],
    )(sp)


def _col_shards(t, rows):
    n, cols = t.shape[0], t.shape[1] * t.shape[2] // rows
    return t.reshape(n, rows, cols).transpose(1, 0, 2).reshape(rows, n * cols)


def _to_col_shards(w, n):
    rows, tot = w.shape
    cols = tot // n
    return w.reshape(rows, n, cols).transpose(1, 0, 2).reshape(n, rows * cols // D, D)


def _local_step(x, mem, tgt, wt, small):
    L = x.shape[0]
    w_in = wt["w_in"]
    zpad = jnp.zeros((D, WB - D - SSD_HEADS), BF16)
    w_a = w_in[:, 2576:6672]
    w_b = jnp.concatenate([w_in[:, 0:D], w_in[:, 2560:2576], zpad], axis=1)
    w_c = w_in[:, D:2560]
    a_log, d_skip = small["a_log"], small["d_skip"]
    avec = jnp.pad(-jnp.exp(a_log), ((0, 0), (0, 128 - SSD_HEADS)))
    aexp = jnp.repeat(-jnp.exp(a_log), SSD_P, axis=1)
    dexp = jnp.repeat(d_skip, SSD_P, axis=1)
    dtb = jnp.pad(small["dt_bias"], ((0, 0), (0, 128 - SSD_HEADS)))
    emat = (lax.broadcasted_iota(jnp.int32, (128, D), 0) == lax.broadcasted_iota(jnp.int32, (128, D), 1) // SSD_P
            ).astype(F32)
    hg_nwx = jnp.tile(small["hg_norm_w"], (1, HG_HEADS))
    hgl = small["hg_lower_bounds"]
    nfw = small["norm_final_w"].reshape(1, D)

    pa, hn_mix = _norm_mm(x, small["norm_mix_w"], w_a, tm=512, tn=1024, name="inproj_a", emit_h=True)
    pb = _norm_mm(x, small["norm_mix_w"], w_b, tm=512, tn=512, name="inproj_b")
    pc = _norm_mm(x, small["norm_mix_w"], w_c, tm=512, tn=512, name="inproj_c")
    xbc = _conv_fwd(pc, small["conv_w"], small["conv_b"], tm=512)
    ya, ypre, ssd_sin = _ssd_fwd(xbc, pb, dtb, avec, aexp, dexp, small["ssd_norm_w"], emat)
    ob, opre, hg_sin = _hgrn_fwd(pa, hgl, hg_nwx)
    x1 = _mm2_res(x, ya, ob, wt["w_out"], tm=512, name="outproj")
    kv, mn = _norm_mm(mem, small["norm_mem_w"], wt["xa_wkv"], tm=256, tn=1024, name="mem_kv", emit_h=True)
    kvb = kv.astype(BF16)
    x2 = _xattn_fwd(x1, small["norm_xa_w"], wt["xa_wq"], kvb, wt["xa_wo"], tm=512)
    x3 = _ffn_fwd(x2, small["norm_ffn_w"], wt["ffn_w_gate"], wt["ffn_w_up"], wt["ffn_w_down"], tm=1024)

    dx3, dx3b, loss, g_nf = _final(x3, tgt, nfw, tm=512)
    dx2, hn_ffn, act, dg, du, g_nffn = _ffn_bwd(x2, dx3, small["norm_ffn_w"], wt["ffn_w_gate"], wt["ffn_w_up"],
                                                wt["ffn_w_down"], tm=512)
    g_wg = _dw(hn_ffn, dg, tM=1024, tN=1408, tl=512, name="dw_gate")
    g_wu = _dw(hn_ffn, du, tM=1024, tN=1408, tl=512, name="dw_up")
    g_wd = _dw(act, dx3b, tM=1408, tN=1024, tl=512, name="dw_down")
    dx1, hn_xa, dq, ox, dkv, g_nxa = _xattn_bwd(x1, dx2, small["norm_xa_w"], wt["xa_wq"], kvb, wt["xa_wo"], tm=512)
    g_wq = _dw(hn_xa, dq, tM=1024, tN=1024, tl=512, name="dw_q")
    g_wo = _dw(ox, dx2.astype(BF16), tM=1024, tN=1024, tl=512, name="dw_o")
    g_wkv = _dw(mn, dkv.astype(BF16), tM=1024, tN=1024, tl=256, name="dw_kv")
    _, g_nmem = _mmnt_normbwd([dkv.astype(BF16)], [wt["xa_wkv"]], mem, small["norm_mem_w"], None, tm=256,
                              name="mem_bwd")
    dya, dob = _mm_nt2(dx1, wt["w_out"], tm=512, name="outproj_bwd")
    dx1b = dx1.astype(BF16)
    g_wout = jnp.concatenate([_dw(ya, dx1b, tM=1024, tN=1024, tl=512, name="dw_out_a"),
                              _dw(ob, dx1b, tM=1024, tN=1024, tl=512, name="dw_out_b")], axis=0)
    dpa, g_hgl, g_hgn_x = _hgrn_bwd(pa, opre, hg_sin, dob, hgl, hg_nwx)
    dpb, dxbc, g_ssdn, g_dtb, g_alog, g_dx = _ssd_bwd(xbc, pb, ypre, ssd_sin, dya, dtb, avec, aexp, dexp,
                                                     small["ssd_norm_w"], emat)
    dpc, g_cw, g_cb = _conv_bwd(pc, dxbc, small["conv_w"], small["conv_b"], tm=512)
    grad_x, g_nmix = _mmnt_normbwd([dpa, dpb, dpc], [w_a, w_b, w_c], x, small["norm_mix_w"], dx1, tm=256,
                                   name="inproj_bwd")
    g_wa = _dw(hn_mix, dpa, tM=1024, tN=1024, tl=512, name="dw_in_a")
    g_wb = _dw(hn_mix, dpb, tM=1024, tN=512, tl=512, name="dw_in_b")
    g_wc = _dw(hn_mix, dpc, tM=1024, tN=512, tl=512, name="dw_in_c")
    g_win = jnp.concatenate([g_wb[:, 0:D], g_wc, g_wb[:, D:D + SSD_HEADS], g_wa], axis=1)

    big = {"w_in": g_win, "w_out": g_wout, "xa_wq": g_wq, "xa_wkv": g_wkv, "xa_wo": g_wo,
           "ffn_w_gate": g_wg, "ffn_w_up": g_wu, "ffn_w_down": g_wd}
    smallg = {
        "norm_mix_w": g_nmix, "conv_w": g_cw, "conv_b": g_cb, "dt_bias": g_dtb[:, 0:SSD_HEADS],
        "a_log": g_alog[:, 0:SSD_HEADS], "d_skip": g_dx.reshape(SSD_HEADS, SSD_P).sum(axis=1).reshape(1, SSD_HEADS),
        "ssd_norm_w": g_ssdn, "hg_lower_bounds": g_hgl,
        "hg_norm_w": g_hgn_x.reshape(HG_HEADS, HG_K).sum(axis=0).reshape(1, HG_K),
        "norm_xa_w": g_nxa, "norm_mem_w": g_nmem, "norm_ffn_w": g_nffn, "norm_final_w": g_nf,
        "loss": loss[:, 0:1]}
    return grad_x, big, smallg


COL_SHARDED = ("w_in", "xa_wkv", "ffn_w_gate", "ffn_w_up")
FULL_ROWS = {"w_in": D, "xa_wkv": D, "ffn_w_gate": D, "ffn_w_up": D}


def _pad_rows(t, rows):
    return jnp.pad(t, [(0, 0)] * (t.ndim - 2) + [(0, rows - t.shape[-2]), (0, 0)])


def _pack_big_shards(shards, dtype, tail=None):
    parts = [_pad_rows(shards[n].astype(dtype).reshape(r, D), _rows_padded(r)) for n, r in PACK]
    parts.append(jnp.zeros((TAIL_ROWS, D), dtype) if tail is None else tail)
    return jnp.concatenate(parts, axis=0)


def _unpack_big_shards(packed, shapes):
    out, off = {}, 0
    for n, r in PACK:
        out[n] = packed[off:off + r].reshape(shapes[n])
        off += _rows_padded(r)
    return out


def _pack_small(vals):
    rows = []
    for n, s in SMALL:
        v = vals[n].reshape(-1)
        rows.append(jnp.pad(v, (0, -(-s // 128) * 128 - s)))
    flat = jnp.concatenate(rows)
    return jnp.pad(flat, (0, SMALL_ROWS_PAD * 128 - flat.shape[0])).reshape(SMALL_ROWS_PAD, 128)


def _unpack_small(packed, shapes):
    flat = packed.reshape(-1)
    return {n: flat[SMALL_LAY[n][0]:SMALL_LAY[n][0] + SMALL_LAY[n][1]].reshape(shapes[n]) for n, _ in SMALL}


WEIGHTS = ['norm_mix_w', 'w_in', 'conv_w', 'conv_b', 'dt_bias', 'a_log', 'd_skip', 'ssd_norm_w', 'hg_lower_bounds',
           'hg_norm_w', 'w_out', 'norm_xa_w', 'norm_mem_w', 'xa_wq', 'xa_wkv', 'xa_wo', 'norm_ffn_w', 'ffn_w_gate',
           'ffn_w_up', 'ffn_w_down', 'norm_final_w']
BIG = tuple(n for n, _ in PACK)


def kernel(x, mem, norm_mix_w, w_in, conv_w, conv_b, dt_bias, a_log, d_skip, ssd_norm_w, hg_lower_bounds, hg_norm_w, w_out, norm_xa_w, norm_mem_w, xa_wq, xa_wkv, xa_wo, norm_ffn_w, ffn_w_gate, ffn_w_up, ffn_w_down, norm_final_w, loss_target, m_norm_mix_w, m_w_in, m_conv_w, m_conv_b, m_dt_bias, m_a_log, m_d_skip, m_ssd_norm_w, m_hg_lower_bounds, m_hg_norm_w, m_w_out, m_norm_xa_w, m_norm_mem_w, m_xa_wq, m_xa_wkv, m_xa_wo, m_norm_ffn_w, m_ffn_w_gate, m_ffn_w_up, m_ffn_w_down, m_norm_final_w, v_norm_mix_w, v_w_in, v_conv_w, v_conv_b, v_dt_bias, v_a_log, v_d_skip, v_ssd_norm_w, v_hg_lower_bounds, v_hg_norm_w, v_w_out, v_norm_xa_w, v_norm_mem_w, v_xa_wq, v_xa_wkv, v_xa_wo, v_norm_ffn_w, v_ffn_w_gate, v_ffn_w_up, v_ffn_w_down, v_norm_final_w):
    args = dict(locals())
    w = {n: args[n] for n in WEIGHTS}
    mo = {n: args["m_" + n] for n in WEIGHTS}
    vo = {n: args["v_" + n] for n in WEIGHTS}
    me = 4 * lax.axis_index("x") + 2 * lax.axis_index("y") + lax.axis_index("c")

    big_sh = {n: w[n][0] for n in BIG}
    cw_bits = lax.bitcast_convert_type(conv_w[0], BF16).reshape(-1)
    cw_rows = jnp.pad(cw_bits, (0, TAIL_ROWS * D - cw_bits.shape[0])).reshape(TAIL_ROWS, D)
    gath = _allgather(_pack_big_shards(big_sh, BF16, tail=cw_rows))
    wt, off = {}, 0
    for n, r in PACK:
        t = gath[:, off:off + r]
        wt[n] = _col_shards(t, FULL_ROWS[n]) if n in COL_SHARDED else t.reshape(N_DEV * r, D)
        off += _rows_padded(r)
    cw_all = lax.bitcast_convert_type(gath[:, off:off + 2].reshape(N_DEV, 2 * D)[:, 0:1536].reshape(N_DEV, 4, 192, 2),
                                      F32)
    conv_w_full = cw_all.transpose(1, 0, 2).reshape(4, 1536)

    small = {n: w[n][0] if w[n].ndim == 3 else w[n] for n in WEIGHTS if n not in BIG}
    small["conv_w"] = conv_w_full
    small["hg_lower_bounds"] = hg_lower_bounds
    grad_x, gbig, gsmall = _local_step(x[0], mem[0], loss_target[0], wt, small)

    pieces = []
    for n, r in PACK:
        g = gbig[n]
        pieces.append(_pad_rows(_to_col_shards(g, N_DEV) if n in COL_SHARDED else g.reshape(N_DEV, r, D),
                                _rows_padded(r)))
    pieces.append(jnp.zeros((N_DEV, TAIL_ROWS, D), F32))
    recv = _exchange(jnp.concatenate(pieces, axis=1))
    wp = _pack_big_shards(big_sh, F32)
    mp = _pack_big_shards({n: mo[n][0] for n in BIG}, F32)
    vp = _pack_big_shards({n: vo[n][0] for n in BIG}, F32)
    gp, dp, nmp, nvp = _adamw(recv, wp, mp, vp, tr=PACK_ROWS // 8, name="adamw_big")
    shapes = {n: w[n].shape for n in BIG}
    out_g, out_d = _unpack_big_shards(gp, shapes), _unpack_big_shards(dp, shapes)
    out_m, out_v = _unpack_big_shards(nmp, shapes), _unpack_big_shards(nvp, shapes)

    tot = _small_allreduce(_pack_small(gsmall))
    sshapes = {n: (w[n].shape if n != "conv_w" else (1, 4, 1536)) for n, _ in SMALL if n != "loss"}
    sshapes["loss"] = ()
    gs = _unpack_small(tot, sshapes)
    loss = gs.pop("loss")
    gs["conv_w"] = lax.dynamic_slice_in_dim(gs["conv_w"], me * 192, 192, axis=2)
    names = [n for n, _ in SMALL if n != "loss"]
    flat = lambda d: jnp.concatenate([d[n].reshape(-1) for n in names])
    nsm = flat(gs).shape[0]
    rows = -(-nsm // (8 * 128)) * 8
    padr = lambda t: jnp.pad(t, (0, rows * 128 - nsm)).reshape(rows, 128)
    sg, sd, snm, snv = _adamw(padr(flat(gs)).reshape(1, rows, 128), padr(flat(w)), padr(flat(mo)), padr(flat(vo)),
                              tr=rows, name="adamw_small")

    def unflat(t):
        t, out, off = t.reshape(-1), {}, 0
        for n in names:
            sz = w[n].size
            out[n] = t[off:off + sz].reshape(w[n].shape)
            off += sz
        return out

    for dst, src in ((out_g, sg), (out_d, sd), (out_m, snm), (out_v, snv)):
        dst.update(unflat(src))
    return (loss, grad_x[None], *[out_g[n] for n in WEIGHTS], *[out_d[n] for n in WEIGHTS],
            *[out_m[n] for n in WEIGHTS], *[out_v[n] for n in WEIGHTS])
```

```python
import jax
import jax.numpy as jnp
from jax import lax
from jax.experimental import pallas as pl
from jax.experimental.pallas import tpu as pltpu

F32, BF16 = jnp.float32, jnp.bfloat16
MESH = pl.DeviceIdType.MESH

D = 1024
EPS = 1e-6
SSD_HEADS, SSD_P, SSD_N, SSD_Q = 16, 64, 128, 128
HG_HEADS, HG_K, HG_STEP, HG_SUB = 8, 128, 128, 64
XA_HEADS, XA_DH, MEM_LEN = 4, 256, 256
FFN = 2816
N_IN = 6672
N_DEV = 8
WA, WB, WC = 4096, 1536, 1536
ADAM_LR, ADAM_B1, ADAM_B2, ADAM_EPS, ADAM_WD, ADAM_STEP = 0.001, 0.9, 0.999, 1e-08, 0.01, 10
VMEM_MB = 2 ** 20

PACK = (("w_in", 834), ("w_out", 256), ("xa_wq", 128), ("xa_wkv", 256), ("xa_wo", 128),
        ("ffn_w_gate", 352), ("ffn_w_up", 352), ("ffn_w_down", 352))
ROW_TILE = 16
GROUPS = {"in": PACK[0:1], "mid": PACK[1:5], "ffn": PACK[5:8]}
GROUP_ROWS = {"in": 896, "mid": 768, "ffn": 1056}
ADAM_ROWS = {"in": 128, "mid": 128, "ffn": 176}
CONV_BITS_ROWS = ROW_TILE


def _rows_padded(r):
    return -(-r // ROW_TILE) * ROW_TILE
SMALL = (("norm_mix_w", 1024), ("conv_w", 6144), ("conv_b", 1536), ("dt_bias", 16), ("a_log", 16),
         ("d_skip", 16), ("ssd_norm_w", 1024), ("hg_lower_bounds", 2048), ("hg_norm_w", 128),
         ("norm_xa_w", 1024), ("norm_mem_w", 1024), ("norm_ffn_w", 1024), ("norm_final_w", 1024),
         ("loss", 1))


def _small_layout():
    off, lay = 0, {}
    for n, s in SMALL:
        lay[n] = (off, s)
        off += -(-s // 128) * 128
    return lay, off // 128


SMALL_LAY, SMALL_ROWS = _small_layout()
SMALL_ROWS_PAD = -(-SMALL_ROWS // 8) * 8


def _cparams(sem=None, vmem_mb=48):
    return pltpu.CompilerParams(dimension_semantics=sem, vmem_limit_bytes=vmem_mb * VMEM_MB)


def _dot(a, b):
    return jnp.dot(a.astype(BF16), b.astype(BF16), preferred_element_type=F32)


def _dot_nt(a, b):
    return lax.dot_general(a.astype(BF16), b.astype(BF16), (((1,), (1,)), ((), ())), preferred_element_type=F32)


def _dot_tn(a, b):
    return lax.dot_general(a.astype(BF16), b.astype(BF16), (((0,), (0,)), ((), ())), preferred_element_type=F32)


def _split3(a):
    a1 = a.astype(BF16)
    r1 = a - a1.astype(F32)
    a2 = r1.astype(BF16)
    a3 = (r1 - a2.astype(F32)).astype(BF16)
    return a1, a2, a3


def _dot_hi(a, b, general=0):
    if general == 0:
        return sum(jnp.dot(t, b.astype(BF16), preferred_element_type=F32) for t in _split3(a))
    return sum(jnp.dot(a.astype(BF16), t, preferred_element_type=F32) for t in _split3(b))


def _dot_nt_hi(a, b):
    return sum(_dot_nt(t, b) for t in _split3(a))


def _sigmoid(x):
    return 1.0 / (1.0 + jnp.exp(-x))


def _silu(x):
    return x * _sigmoid(x)


def _dsilu(x):
    s = _sigmoid(x)
    return s * (1.0 + x * (1.0 - s))


def _softplus(x):
    return jnp.maximum(x, 0.0) + jnp.log(1.0 + jnp.exp(-jnp.abs(x)))


def _rms_fwd(x, w):
    r = lax.rsqrt(jnp.mean(x * x, axis=1, keepdims=True) + EPS)
    return x * r * w


def _rms_bwd(dy, x, w):
    r = lax.rsqrt(jnp.mean(x * x, axis=1, keepdims=True) + EPS)
    xh = x * r
    g = dy * w
    dx = r * (g - xh * jnp.mean(g * xh, axis=1, keepdims=True))
    return dx, jnp.sum(dy * xh, axis=0, keepdims=True)


def _iota2(shape, dim):
    return lax.broadcasted_iota(jnp.int32, shape, dim)


def _tril(n):
    return (_iota2((n, n), 0) >= _iota2((n, n), 1)).astype(F32)


def _triu(n):
    return (_iota2((n, n), 0) <= _iota2((n, n), 1)).astype(F32)


def _norm_mm(x, nw, w, *, tm, tn, name, emit_h=False, out_dtype=F32, gather=None):
    L, K = x.shape
    N = w.shape[1]
    tm, tn = min(tm, L), min(tn, N)
    ni, nj = L // tm, N // tn

    def body(x_ref, nw_ref, w_ref, *rest):
        if emit_h:
            o_ref, h_ref, hs_ref = rest
        else:
            o_ref, hs_ref = rest

        @pl.when(pl.program_id(1) == 0)
        def _():
            h = _rms_fwd(x_ref[...], nw_ref[...]).astype(BF16)
            hs_ref[...] = h
            if emit_h:
                h_ref[...] = h

        o_ref[...] = jnp.dot(hs_ref[...], w_ref[...], preferred_element_type=F32).astype(out_dtype)

    out_shape = [jax.ShapeDtypeStruct((L, N), out_dtype)]
    out_specs = [pl.BlockSpec((tm, tn), lambda i, j: (i, j))]
    if emit_h:
        out_shape.append(jax.ShapeDtypeStruct((L, K), BF16))
        out_specs.append(pl.BlockSpec((tm, K), lambda i, j: (i, 0)))
    in_specs = [pl.BlockSpec((tm, K), lambda i, j: (i, 0)), pl.BlockSpec((1, K), lambda i, j: (0, 0)),
                pl.BlockSpec((K, tn), lambda i, j: (0, j))]
    at = lambda i, j: lambda: jnp.logical_and(pl.program_id(0) == i, pl.program_id(1) == j)
    payload = None if gather is None else (_Gather, gather)
    phases = [(at(0, 0), _Gather.start, True), (at(ni // 2, 0), _Gather.forward, True),
              (at(ni - 1, nj - 1), _Gather.finish, False)]
    n_out = len(out_shape)
    in_specs, out_specs, out_shape, scratch = _carried_specs(
        payload, in_specs, out_specs, out_shape, [pltpu.VMEM((tm, K), BF16)])
    res = pl.pallas_call(
        _carried(body, 3, n_out, payload, phases), name=name, grid=(ni, nj),
        in_specs=in_specs, out_specs=out_specs, out_shape=out_shape, scratch_shapes=scratch,
        compiler_params=_cparams(("parallel", "arbitrary") if gather is None else ("arbitrary", "arbitrary")),
    )(*((x, nw, w) if gather is None else (x, nw, w, gather)))
    return res if len(res) > 1 else res[0]


def _mm2_res(res, a1, a2, w, *, tm, name):
    L, N = res.shape
    K = a1.shape[1]
    tm = min(tm, L)

    def body(r_ref, a1_ref, a2_ref, w_ref, o_ref):
        acc = jnp.dot(a1_ref[...], w_ref[0:K, :], preferred_element_type=F32)
        acc += jnp.dot(a2_ref[...], w_ref[K:2 * K, :], preferred_element_type=F32)
        o_ref[...] = r_ref[...] + acc

    return pl.pallas_call(
        body, name=name, grid=(L // tm,),
        in_specs=[pl.BlockSpec((tm, N), lambda i: (i, 0)), pl.BlockSpec((tm, K), lambda i: (i, 0)),
                  pl.BlockSpec((tm, K), lambda i: (i, 0)), pl.BlockSpec((2 * K, N), lambda i: (0, 0))],
        out_specs=pl.BlockSpec((tm, N), lambda i: (i, 0)),
        out_shape=jax.ShapeDtypeStruct((L, N), F32),
        compiler_params=_cparams(("parallel",)),
    )(res, a1, a2, w)


def _mm_nt2(a, w, *, tm, name):
    L, N = a.shape
    K = w.shape[0] // 2
    tm = min(tm, L)

    def body(a_ref, w_ref, o1_ref, o2_ref):
        av = a_ref[...].astype(BF16)
        o1_ref[...] = _dot_nt(av, w_ref[0:K, :])
        o2_ref[...] = _dot_nt(av, w_ref[K:2 * K, :])

    return pl.pallas_call(
        body, name=name, grid=(L // tm,),
        in_specs=[pl.BlockSpec((tm, N), lambda i: (i, 0)), pl.BlockSpec((2 * K, N), lambda i: (0, 0))],
        out_specs=[pl.BlockSpec((tm, K), lambda i: (i, 0)), pl.BlockSpec((tm, K), lambda i: (i, 0))],
        out_shape=[jax.ShapeDtypeStruct((L, K), F32), jax.ShapeDtypeStruct((L, K), F32)],
        compiler_params=_cparams(("parallel",)),
    )(a, w)


def _dw(a, b, *, tM, tN, tl, name):
    L, M = a.shape
    N = b.shape[1]
    tM, tN, tl = min(tM, M), min(tN, N), min(tl, L)

    def body(a_ref, b_ref, o_ref):
        @pl.when(pl.program_id(2) == 0)
        def _():
            o_ref[...] = jnp.zeros_like(o_ref)

        o_ref[...] += _dot_tn(a_ref[...], b_ref[...])

    return pl.pallas_call(
        body, name=name, grid=(M // tM, N // tN, L // tl),
        in_specs=[pl.BlockSpec((tl, tM), lambda i, j, l: (l, i)), pl.BlockSpec((tl, tN), lambda i, j, l: (l, j))],
        out_specs=pl.BlockSpec((tM, tN), lambda i, j, l: (i, j)),
        out_shape=jax.ShapeDtypeStruct((M, N), F32),
        compiler_params=_cparams(("parallel", "parallel", "arbitrary")),
    )(a, b)


def _first_last(n):
    return [(lambda: pl.program_id(0) == 0, _Exchange.start, True),
            (lambda: pl.program_id(0) == n - 1, _Exchange.finish, False)]


def _mmnt_normbwd(a_list, w_list, x, nw, res, *, tm, name, exchange=None):
    L, Dm = x.shape
    tm = min(tm, L)
    n = len(a_list)
    has_res = res is not None

    def body(*refs):
        a_refs, w_refs = refs[:n], refs[n:2 * n]
        x_ref, nw_ref = refs[2 * n], refs[2 * n + 1]
        k = 2 * n + 2
        r_ref = refs[k] if has_res else None
        dx_ref, dnw_ref = refs[k + has_res], refs[k + has_res + 1]
        dh = _dot_nt(a_refs[0][...], w_refs[0][...])
        for a_ref, w_ref in zip(a_refs[1:], w_refs[1:]):
            dh += _dot_nt(a_ref[...], w_ref[...])
        dx, dnw = _rms_bwd(dh, x_ref[...], nw_ref[...])
        dx_ref[...] = dx + r_ref[...] if has_res else dx

        @pl.when(pl.program_id(0) == 0)
        def _():
            dnw_ref[...] = jnp.zeros_like(dnw_ref)

        dnw_ref[...] += dnw

    in_specs = [pl.BlockSpec((tm, a.shape[1]), lambda i: (i, 0)) for a in a_list]
    in_specs += [pl.BlockSpec(w.shape, lambda i: (0, 0)) for w in w_list]
    in_specs += [pl.BlockSpec((tm, Dm), lambda i: (i, 0)), pl.BlockSpec((1, Dm), lambda i: (0, 0))]
    args = [*a_list, *w_list, x, nw]
    if has_res:
        in_specs.append(pl.BlockSpec((tm, Dm), lambda i: (i, 0)))
        args.append(res)
    payload = None if exchange is None else (_Exchange, exchange)
    n_in = len(args)
    if exchange is not None:
        args.append(exchange)
    in_specs, out_specs, out_shape, scratch = _carried_specs(
        payload, in_specs, [pl.BlockSpec((tm, Dm), lambda i: (i, 0)), pl.BlockSpec((1, Dm), lambda i: (0, 0))],
        [jax.ShapeDtypeStruct((L, Dm), F32), jax.ShapeDtypeStruct((1, Dm), F32)], [])
    return pl.pallas_call(
        _carried(body, n_in, 2, payload, _first_last(L // tm)), name=name, grid=(L // tm,), in_specs=in_specs,
        out_specs=out_specs, out_shape=out_shape, scratch_shapes=scratch,
        compiler_params=_cparams(("arbitrary",), vmem_mb=56),
    )(*args)


CONV_TN = 512


def _conv_pre(cat, w_ref, b_ref, rows):
    shifted = [pltpu.roll(cat, 3 - k, 0)[8:8 + rows] for k in range(3)] + [cat[8:8 + rows]]
    pre = b_ref[...] + w_ref[3:4, :] * shifted[3]
    for k in range(3):
        pre += w_ref[k:k + 1, :] * shifted[k]
    return pre, shifted


def _conv_fwd(pc, cw, cb, *, tm):
    L, C = pc.shape
    tm = min(tm, L)
    tn = CONV_TN

    def body(u_ref, halo_ref, w_ref, b_ref, o_ref):
        halo = jnp.where(pl.program_id(1) > 0, halo_ref[...], 0.0)
        cat = jnp.concatenate([halo, u_ref[...]], axis=0)
        pre, _ = _conv_pre(cat, w_ref, b_ref, tm)
        o_ref[...] = _silu(pre)

    return pl.pallas_call(
        body, name="conv_fwd", grid=(C // tn, L // tm),
        in_specs=[pl.BlockSpec((tm, tn), lambda j, i: (i, j)),
                  pl.BlockSpec((8, tn), lambda j, i: (jnp.maximum(i * (tm // 8) - 1, 0), j)),
                  pl.BlockSpec((4, tn), lambda j, i: (0, j)), pl.BlockSpec((1, tn), lambda j, i: (0, j))],
        out_specs=pl.BlockSpec((tm, tn), lambda j, i: (i, j)),
        out_shape=jax.ShapeDtypeStruct((L, C), F32),
        compiler_params=_cparams(("parallel", "parallel")),
    )(pc, pc, cw, cb)


def _conv_bwd(pc, dact, cw, cb, *, tm):
    L, C = pc.shape
    tm = min(tm, L)
    tn = CONV_TN
    nt = L // tm

    def body(u_ref, halo_ref, unext_ref, da_ref, danext_ref, w_ref, b_ref, du_ref, dw_ref, db_ref):
        i = pl.program_id(1)
        halo = jnp.where(i > 0, halo_ref[...], 0.0)
        cat = jnp.concatenate([halo, u_ref[...], unext_ref[...]], axis=0)
        pre, shifted = _conv_pre(cat, w_ref, b_ref, tm + 8)
        da = jnp.concatenate([da_ref[...], jnp.where(i < nt - 1, danext_ref[...], 0.0)], axis=0)
        dpre = da * _dsilu(pre)
        du = w_ref[3:4, :] * dpre[0:tm]
        for k in range(3):
            du += w_ref[k:k + 1, :] * pltpu.roll(dpre, tm + 8 - (3 - k), 0)[0:tm]
        du_ref[...] = du.astype(BF16)

        @pl.when(i == 0)
        def _():
            dw_ref[...] = jnp.zeros_like(dw_ref)
            db_ref[...] = jnp.zeros_like(db_ref)

        dp = dpre[0:tm]
        dw_ref[...] += jnp.concatenate(
            [jnp.sum(dp * shifted[k][0:tm], axis=0, keepdims=True) for k in range(4)], axis=0)
        db_ref[...] += jnp.sum(dp, axis=0, keepdims=True)

    nb8 = L // 8
    return pl.pallas_call(
        body, name="conv_bwd", grid=(C // tn, nt),
        in_specs=[pl.BlockSpec((tm, tn), lambda j, i: (i, j)),
                  pl.BlockSpec((8, tn), lambda j, i: (jnp.maximum(i * (tm // 8) - 1, 0), j)),
                  pl.BlockSpec((8, tn), lambda j, i: (jnp.minimum((i + 1) * (tm // 8), nb8 - 1), j)),
                  pl.BlockSpec((tm, tn), lambda j, i: (i, j)),
                  pl.BlockSpec((8, tn), lambda j, i: (jnp.minimum((i + 1) * (tm // 8), nb8 - 1), j)),
                  pl.BlockSpec((4, tn), lambda j, i: (0, j)), pl.BlockSpec((1, tn), lambda j, i: (0, j))],
        out_specs=[pl.BlockSpec((tm, tn), lambda j, i: (i, j)), pl.BlockSpec((4, tn), lambda j, i: (0, j)),
                   pl.BlockSpec((1, tn), lambda j, i: (0, j))],
        out_shape=[jax.ShapeDtypeStruct((L, C), BF16), jax.ShapeDtypeStruct((4, C), F32),
                   jax.ShapeDtypeStruct((1, C), F32)],
        compiler_params=_cparams(("parallel", "arbitrary")),
    )(pc, pc, pc, dact, dact, cw, cb)


def _ssd_common(dtr_ref, dtb_ref, avec_ref, aexp_ref, e_ref, acx_ref, acol_ref, arow_ref):
    q = SSD_Q
    tril = _tril(q)
    dtpre = dtr_ref[...] + dtb_ref[...]
    dt = _softplus(dtpre)
    dtx = _dot_hi(dt, e_ref[...])
    acx_ref[...] = _dot_hi(tril, dtx * aexp_ref[...], 1)
    acol = _dot_hi(tril, dt * avec_ref[...], 1)
    acol_ref[...] = acol
    arow_ref[...] = acol.T
    return dtpre, dt, dtx


def _ssd_fwd(xbc, pb, dtb, avec, aexp, dexp, nw, emat):
    L = xbc.shape[0]
    q = SSD_Q
    nc = L // q

    def body(xbc_ref, z_ref, dtr_ref, dtb_ref, avec_ref, aexp_ref, dexp_ref, nw_ref, e_ref,
             ya_ref, ypre_ref, sin_ref, st_ref, acx_ref, acol_ref, arow_ref, xdt_ref, y_ref):
        @pl.when(pl.program_id(0) == 0)
        def _():
            st_ref[...] = jnp.zeros_like(st_ref)

        sin_ref[...] = st_ref[...]
        _, _, dtx = _ssd_common(dtr_ref, dtb_ref, avec_ref, aexp_ref, e_ref, acx_ref, acol_ref, arow_ref)
        xs = xbc_ref[:, 0:D]
        xdt = xs * dtx
        xdt_ref[...] = xdt
        acx = acx_ref[...]
        alast = acx_ref[q - 1:q, :]
        xdtd = xdt * jnp.exp(alast - acx)
        eac = jnp.exp(acx)
        ealast = jnp.exp(alast)
        causal = _iota2((q, q), 0) >= _iota2((q, q), 1)
        for g in range(2):
            gs = slice(512 * g, 512 * g + 512)
            bm = xbc_ref[:, D + 128 * g:D + 128 * g + 128]
            cm = xbc_ref[:, D + 256 + 128 * g:D + 256 + 128 * g + 128]
            stg = st_ref[:, gs]
            yoff = _dot(cm, stg) * eac[:, gs]
            gmat = _dot_nt(cm, bm)
            for e in range(8):
                h = 8 * g + e
                hs = slice(64 * h, 64 * h + 64)
                col = acol_ref[:, h:h + 1]
                row = arow_ref[h:h + 1, :]
                lm = jnp.exp(jnp.where(causal, col - row, -1e30))
                y_ref[:, hs] = _dot(gmat * lm, xdt_ref[:, hs])
            y_ref[:, gs] += yoff + dexp_ref[:, gs] * xs[:, gs]
            st_ref[:, gs] = stg * ealast[:, gs] + _dot_tn(bm, xdtd[:, gs])
        ypre_ref[...] = y_ref[...]
        for g in range(2):
            gs = slice(512 * g, 512 * g + 512)
            yz = y_ref[:, gs] * _silu(z_ref[:, gs])
            ya_ref[:, gs] = _rms_fwd(yz, nw_ref[:, gs]).astype(BF16)

    vec = lambda n: pl.BlockSpec((1, n), lambda c: (0, 0))
    return pl.pallas_call(
        body, name="ssd_fwd", grid=(nc,),
        in_specs=[pl.BlockSpec((q, 1536), lambda c: (c, 0)), pl.BlockSpec((q, D), lambda c: (c, 0)),
                  pl.BlockSpec((q, 128), lambda c: (c, 8)), vec(128), vec(128), vec(D), vec(D), vec(D),
                  pl.BlockSpec((128, D), lambda c: (0, 0))],
        out_specs=[pl.BlockSpec((q, D), lambda c: (c, 0)), pl.BlockSpec((q, D), lambda c: (c, 0)),
                   pl.BlockSpec((128, D), lambda c: (c, 0))],
        out_shape=[jax.ShapeDtypeStruct((L, D), BF16), jax.ShapeDtypeStruct((L, D), F32),
                   jax.ShapeDtypeStruct((nc * 128, D), F32)],
        scratch_shapes=[pltpu.VMEM((128, D), F32), pltpu.VMEM((q, D), F32), pltpu.VMEM((q, 128), F32),
                        pltpu.VMEM((128, q), F32), pltpu.VMEM((q, D), F32), pltpu.VMEM((q, D), F32)],
        compiler_params=_cparams(("arbitrary",)),
    )(xbc, pb, pb, dtb, avec, aexp, dexp, nw, emat)


def _ssd_bwd(xbc, pb, ypre, sin, dya, dtb, avec, aexp, dexp, nw, emat, exchange=None):
    L = xbc.shape[0]
    q = SSD_Q
    nc = L // q

    def body(xbc_ref, z_ref, dtr_ref, ypre_ref, sin_ref, dya_ref, dtb_ref, avec_ref, aexp_ref, dexp_ref, nw_ref,
             e_ref, dpb_ref, dxbc_ref, dnw_ref, ddtb_ref, da_ref, ddx_ref,
             dst_ref, acx_ref, acol_ref, arow_ref, xdt_ref, dxdt_ref, dy_ref, dacx_ref):
        @pl.when(pl.program_id(0) == 0)
        def _():
            dst_ref[...] = jnp.zeros_like(dst_ref)
            dnw_ref[...] = jnp.zeros_like(dnw_ref)
            ddtb_ref[...] = jnp.zeros_like(ddtb_ref)
            da_ref[...] = jnp.zeros_like(da_ref)
            ddx_ref[...] = jnp.zeros_like(ddx_ref)

        dtpre, dt, dtx = _ssd_common(dtr_ref, dtb_ref, avec_ref, aexp_ref, e_ref, acx_ref, acol_ref, arow_ref)
        xs = xbc_ref[:, 0:D]
        xdt = xs * dtx
        xdt_ref[...] = xdt
        acx = acx_ref[...]
        alast = acx_ref[q - 1:q, :]
        dec_end = jnp.exp(alast - acx)
        xdtd = xdt * dec_end
        eac = jnp.exp(acx)
        ealast = jnp.exp(alast)
        for g in range(2):
            gs = slice(512 * g, 512 * g + 512)
            y = ypre_ref[:, gs]
            z = z_ref[:, gs]
            sz = _silu(z)
            dyz, dnw = _rms_bwd(dya_ref[:, gs], y * sz, nw_ref[:, gs])
            dnw_ref[:, gs] += dnw
            dy_ref[:, gs] = dyz * sz
            dpb_ref[:, gs] = (dyz * y * _dsilu(z)).astype(BF16)
        dy = dy_ref[...]
        ddx_ref[...] += jnp.sum(dy * xs, axis=0, keepdims=True)
        ri = _iota2((q, q), 0)
        ci = _iota2((q, q), 1)
        causal = ri >= ci
        causal_t = ri <= ci
        dacol = jnp.zeros((q, 128), F32)
        dacol_t = jnp.zeros((128, q), F32)
        last_row = _iota2((q, 512), 0) == q - 1
        for g in range(2):
            gs = slice(512 * g, 512 * g + 512)
            bm = xbc_ref[:, D + 128 * g:D + 128 * g + 128]
            cm = xbc_ref[:, D + 256 + 128 * g:D + 256 + 128 * g + 128]
            stg = sin_ref[:, gs]
            dstg = dst_ref[:, gs]
            dyg = dy[:, gs]
            yoff = _dot(cm, stg) * eac[:, gs]
            dwm = dyg * eac[:, gs]
            dcm = _dot_nt(dwm, stg)
            dstin = _dot_tn(cm, dwm)
            dacx_g = dyg * yoff
            dxdtd = _dot(bm, dstg)
            dbm = _dot_nt(xdtd[:, gs], dstg)
            t = dxdtd * xdtd[:, gs]
            dacx_g -= t
            dalast = jnp.sum(t, axis=0, keepdims=True) + jnp.sum(dstg * stg, axis=0, keepdims=True) * ealast[:, gs]
            dst_ref[:, gs] = dstin + dstg * ealast[:, gs]
            dacx_ref[:, gs] = dacx_g + jnp.where(last_row, dalast, 0.0)
            gmat = _dot_nt(cm, bm)
            gmat_t = _dot_nt(bm, cm)
            dg = jnp.zeros((q, q), F32)
            for e in range(8):
                h = 8 * g + e
                hs = slice(64 * h, 64 * h + 64)
                col = acol_ref[:, h:h + 1]
                row = arow_ref[h:h + 1, :]
                lm = jnp.exp(jnp.where(causal, col - row, -1e30))
                lm_t = jnp.exp(jnp.where(causal_t, row - col, -1e30))
                dyh = dy_ref[:, hs]
                dm = _dot_nt(dyh, xdt_ref[:, hs])
                dxdt_ref[:, hs] = _dot(gmat_t * lm_t, dyh)
                dml = dm * lm
                dg += dml
                p = dml * gmat
                dacol += jnp.where(ci == h, jnp.sum(p, axis=1, keepdims=True), 0.0)
                dacol_t -= jnp.where(ri == h, jnp.sum(p, axis=0, keepdims=True), 0.0)
            dcm += _dot(dg, bm)
            dbm += _dot_tn(dg, cm)
            dxbc_ref[:, D + 128 * g:D + 128 * g + 128] = dbm
            dxbc_ref[:, D + 256 + 128 * g:D + 256 + 128 * g + 128] = dcm
            dxdt_ref[:, gs] += dxdtd * dec_end[:, gs]
        dxdt = dxdt_ref[...]
        dacum = dacol + dacol_t.T + _dot_nt_hi(dacx_ref[...], e_ref[...])
        da = _dot_hi(_triu(q), dacum, 1)
        ddt = da * avec_ref[...] + _dot_nt_hi(dxdt * xs, e_ref[...])
        da_ref[...] += jnp.sum(da * dt, axis=0, keepdims=True) * avec_ref[...]
        dxbc_ref[:, 0:D] = dexp_ref[...] * dy + dxdt * dtx
        ddtr = ddt * _sigmoid(dtpre)
        ddtb_ref[...] += jnp.sum(ddtr, axis=0, keepdims=True)
        dpb_ref[:, D:D + 128] = ddtr.astype(BF16)
        dpb_ref[:, D + 128:1536] = jnp.zeros((q, 384), BF16)

    rev = lambda c: nc - 1 - c
    vec = lambda n: pl.BlockSpec((1, n), lambda c: (0, 0))
    payload = None if exchange is None else (_Exchange, exchange)
    args = [xbc, pb, pb, ypre, sin, dya, dtb, avec, aexp, dexp, nw, emat]
    in_specs, out_specs, out_shape, scratch = _carried_specs(
        payload,
        [pl.BlockSpec((q, 1536), lambda c: (rev(c), 0)), pl.BlockSpec((q, D), lambda c: (rev(c), 0)),
         pl.BlockSpec((q, 128), lambda c: (rev(c), 8)), pl.BlockSpec((q, D), lambda c: (rev(c), 0)),
         pl.BlockSpec((128, D), lambda c: (rev(c), 0)), pl.BlockSpec((q, D), lambda c: (rev(c), 0)),
         vec(128), vec(128), vec(D), vec(D), vec(D), pl.BlockSpec((128, D), lambda c: (0, 0))],
        [pl.BlockSpec((q, 1536), lambda c: (rev(c), 0)), pl.BlockSpec((q, 1536), lambda c: (rev(c), 0)),
         vec(D), vec(128), vec(128), vec(D)],
        [jax.ShapeDtypeStruct((L, 1536), BF16), jax.ShapeDtypeStruct((L, 1536), F32),
         jax.ShapeDtypeStruct((1, D), F32), jax.ShapeDtypeStruct((1, 128), F32),
         jax.ShapeDtypeStruct((1, 128), F32), jax.ShapeDtypeStruct((1, D), F32)],
        [pltpu.VMEM((128, D), F32), pltpu.VMEM((q, D), F32), pltpu.VMEM((q, 128), F32),
         pltpu.VMEM((128, q), F32), pltpu.VMEM((q, D), F32), pltpu.VMEM((q, D), F32),
         pltpu.VMEM((q, D), F32), pltpu.VMEM((q, D), F32)])
    return pl.pallas_call(
        _carried(body, len(args), 6, payload, _first_last(nc)), name="ssd_bwd", grid=(nc,),
        in_specs=in_specs, out_specs=out_specs, out_shape=out_shape, scratch_shapes=scratch,
        compiler_params=_cparams(("arbitrary",)),
    )(*(args if exchange is None else args + [exchange]))


def _hg_gates(hq, hf, hgl_ref, b_ref):
    lb = 1.0 / (1.0 + jnp.exp(hgl_ref[1:2, :] - hgl_ref[0:1, :]))
    qf = _silu(hq)
    sg = _sigmoid(hf)
    f = lb + (1.0 - lb) * sg
    b_ref[...] = _dot_hi(_tril(HG_STEP), jnp.log(f), 1)
    return lb, qf, sg, f


def _hg_factors(qf, kf, b_ref):
    s, n = HG_SUB, HG_STEP
    b = b_ref[...]
    blast = b_ref[n - 1:n, :]
    m0, mb, m1 = b_ref[s // 2 - 1:s // 2, :], b_ref[s - 1:s, :], b_ref[s + s // 2 - 1:s + s // 2, :]
    b0, b1 = b[0:s], b[s:n]
    q0, q1, k0, k1 = qf[0:s], qf[s:n], kf[0:s], kf[s:n]
    fac = dict(
        eb=jnp.exp(blast), eq=jnp.exp(b), ek=jnp.exp(blast - b),
        eq0=jnp.exp(b0 - m0), ek0=jnp.exp(m0 - b0), eq1=jnp.exp(b1 - m1), ek1=jnp.exp(m1 - b1),
        eqb=jnp.exp(b1 - mb), ekb=jnp.exp(mb - b0))
    rd = lambda t: t.astype(BF16).astype(F32)
    val = dict(qe=qf * fac["eq"], ke=kf * fac["ek"], qm0=rd(q0 * fac["eq0"]), km0=rd(k0 * fac["ek0"]),
               qm1=rd(q1 * fac["eq1"]), km1=rd(k1 * fac["ek1"]), qb=rd(q1 * fac["eqb"]), kb=rd(k0 * fac["ekb"]))
    return fac, val


def _hgrn_fwd(pa, hgl, nwx):
    L = pa.shape[0]
    n, s = HG_STEP, HG_SUB
    nc = L // n

    def body(hq_ref, hf_ref, hi_ref, hg_ref, hgl_ref, nw_ref, ob_ref, opre_ref, sin_ref, st_ref, b_ref):
        @pl.when(pl.program_id(0) == 0)
        def _():
            st_ref[...] = jnp.zeros_like(st_ref)

        sin_ref[...] = st_ref[...]
        _, qf, _, f = _hg_gates(hq_ref[...], hf_ref[...], hgl_ref, b_ref)
        fac, val = _hg_factors(qf, 1.0 - f, b_ref)
        causal = _iota2((s, s), 0) >= _iota2((s, s), 1)
        for h in range(HG_HEADS):
            hs = slice(128 * h, 128 * h + 128)
            sth = st_ref[:, hs]
            v = hi_ref[:, hs]
            v0, v1 = v[0:s], v[s:n]
            a00 = jnp.where(causal, _dot_nt(val["qm0"][:, hs], val["km0"][:, hs]), 0.0)
            a11 = jnp.where(causal, _dot_nt(val["qm1"][:, hs], val["km1"][:, hs]), 0.0)
            a10 = _dot_nt(val["qb"][:, hs], val["kb"][:, hs])
            o = _dot_nt(val["qe"][:, hs], sth) + jnp.concatenate(
                [_dot(a00, v0), _dot(a10, v0) + _dot(a11, v1)], axis=0)
            st_ref[:, hs] = sth * fac["eb"][:, hs] + _dot_tn(v, val["ke"][:, hs])
            opre_ref[:, hs] = o
            ob_ref[:, hs] = (_rms_fwd(o, nw_ref[:, hs]) * _silu(hg_ref[:, hs])).astype(BF16)

    blk = lambda j: pl.BlockSpec((n, D), lambda c: (c, j))
    return pl.pallas_call(
        body, name="hgrn_fwd", grid=(nc,),
        in_specs=[blk(0), blk(1), blk(2), blk(3), pl.BlockSpec((2, D), lambda c: (0, 0)),
                  pl.BlockSpec((1, D), lambda c: (0, 0))],
        out_specs=[blk(0), blk(0), blk(0)],
        out_shape=[jax.ShapeDtypeStruct((L, D), BF16), jax.ShapeDtypeStruct((L, D), F32),
                   jax.ShapeDtypeStruct((nc * 128, D), F32)],
        scratch_shapes=[pltpu.VMEM((128, D), F32), pltpu.VMEM((n, D), F32)],
        compiler_params=_cparams(("arbitrary",)),
    )(pa, pa, pa, pa, hgl, nwx)


def _hgrn_bwd(pa, opre, sin, dob, hgl, nwx, exchange=None):
    L = pa.shape[0]
    n, s = HG_STEP, HG_SUB
    nc = L // n

    def body(hq_ref, hf_ref, hi_ref, hg_ref, opre_ref, sin_ref, dob_ref, hgl_ref, nw_ref,
             dpa_ref, dhgl_ref, dnw_ref, dst_ref, b_ref, dlb_ref, dq_ref, dk_ref, db_ref):
        i = pl.program_id(0)

        @pl.when(i == 0)
        def _():
            dst_ref[...] = jnp.zeros_like(dst_ref)
            dlb_ref[...] = jnp.zeros_like(dlb_ref)
            dnw_ref[...] = jnp.zeros_like(dnw_ref)

        hq = hq_ref[...]
        lb, qf, sg, f = _hg_gates(hq, hf_ref[...], hgl_ref, b_ref)
        kf = 1.0 - f
        fac, val = _hg_factors(qf, kf, b_ref)
        ri, ci = _iota2((s, s), 0), _iota2((s, s), 1)
        causal, causal_t = ri >= ci, ri <= ci
        last_row = _iota2((n, 128), 0) == n - 1
        for h in range(HG_HEADS):
            hs = slice(128 * h, 128 * h + 128)
            o = opre_ref[:, hs]
            gate = hg_ref[:, hs]
            dout = dob_ref[:, hs]
            sgate = _silu(gate)
            do, dnw = _rms_bwd(dout * sgate, o, nw_ref[:, hs])
            dnw_ref[:, hs] += dnw
            dpa_ref[:, 3 * D + 128 * h:3 * D + 128 * h + 128] = (
                dout * _rms_fwd(o, nw_ref[:, hs]) * _dsilu(gate)).astype(BF16)
            sth = sin_ref[:, hs]
            dsth = dst_ref[:, hs]
            v = hi_ref[:, hs]
            v0, v1 = v[0:s], v[s:n]
            do0, do1 = do[0:s], do[s:n]
            qe, ke = val["qe"][:, hs], val["ke"][:, hs]
            qm0, km0, qm1, km1 = val["qm0"][:, hs], val["km0"][:, hs], val["qm1"][:, hs], val["km1"][:, hs]
            qb, kb = val["qb"][:, hs], val["kb"][:, hs]
            dqe = _dot(do, sth)
            dstin = _dot_tn(do, qe)
            a00t = jnp.where(causal_t, _dot_nt(km0, qm0), 0.0)
            a11t = jnp.where(causal_t, _dot_nt(km1, qm1), 0.0)
            a10t = _dot_nt(kb, qb)
            dat00 = jnp.where(causal, _dot_nt(do0, v0), 0.0)
            dat11 = jnp.where(causal, _dot_nt(do1, v1), 0.0)
            dat10 = _dot_nt(do1, v0)
            dat00t = jnp.where(causal_t, _dot_nt(v0, do0), 0.0)
            dat11t = jnp.where(causal_t, _dot_nt(v1, do1), 0.0)
            dat10t = _dot_nt(v0, do1)
            dv = jnp.concatenate([_dot(a00t, do0) + _dot(a10t, do1), _dot(a11t, do1)], axis=0)
            dqm0, dkm0 = _dot(dat00, km0), _dot(dat00t, qm0)
            dqm1, dkm1 = _dot(dat11, km1), _dot(dat11t, qm1)
            dqb, dkb = _dot(dat10, kb), _dot(dat10t, qb)
            dke = _dot(v, dsth)
            dv += _dot_nt(ke, dsth)
            deb = jnp.sum(dsth * sth, axis=0, keepdims=True)
            dst_ref[:, hs] = dstin + dsth * fac["eb"][:, hs]
            dq = dqe * fac["eq"][:, hs] + jnp.concatenate(
                [dqm0 * fac["eq0"][:, hs], dqm1 * fac["eq1"][:, hs] + dqb * fac["eqb"][:, hs]], axis=0)
            dk = dke * fac["ek"][:, hs] + jnp.concatenate(
                [dkm0 * fac["ek0"][:, hs] + dkb * fac["ekb"][:, hs], dkm1 * fac["ek1"][:, hs]], axis=0)
            tke = dke * ke
            db = dqe * qe - tke + jnp.concatenate(
                [dqm0 * qm0 - dkm0 * km0 - dkb * kb, dqm1 * qm1 - dkm1 * km1 + dqb * qb], axis=0)
            dblast = jnp.sum(tke, axis=0, keepdims=True) + deb * fac["eb"][:, hs]
            db_ref[:, hs] = db + jnp.where(last_row, dblast, 0.0)
            dq_ref[:, hs] = dq
            dk_ref[:, hs] = dk
            dpa_ref[:, 2 * D + 128 * h:2 * D + 128 * h + 128] = dv.astype(BF16)
        dg = _dot_hi(_triu(n), db_ref[...], 1)
        df = dg / f - dk_ref[...]
        dpa_ref[:, D:2 * D] = (df * (1.0 - lb) * sg * (1.0 - sg)).astype(BF16)
        dpa_ref[:, 0:D] = (dq_ref[...] * _dsilu(hq)).astype(BF16)
        dlb_ref[...] += jnp.sum(df * (1.0 - sg), axis=0, keepdims=True)

        @pl.when(i == nc - 1)
        def _():
            d0 = dlb_ref[...] * lb * (1.0 - lb)
            dhgl_ref[...] = jnp.concatenate([d0, -d0], axis=0)

    rev = lambda c: nc - 1 - c
    blk = lambda j: pl.BlockSpec((n, D), lambda c: (rev(c), j))
    payload = None if exchange is None else (_Exchange, exchange)
    args = [pa, pa, pa, pa, opre, sin, dob, hgl, nwx]
    in_specs, out_specs, out_shape, scratch = _carried_specs(
        payload,
        [blk(0), blk(1), blk(2), blk(3), blk(0), blk(0), blk(0), pl.BlockSpec((2, D), lambda c: (0, 0)),
         pl.BlockSpec((1, D), lambda c: (0, 0))],
        [pl.BlockSpec((n, 4 * D), lambda c: (rev(c), 0)), pl.BlockSpec((2, D), lambda c: (0, 0)),
         pl.BlockSpec((1, D), lambda c: (0, 0))],
        [jax.ShapeDtypeStruct((L, 4 * D), BF16), jax.ShapeDtypeStruct((2, D), F32), jax.ShapeDtypeStruct((1, D), F32)],
        [pltpu.VMEM((128, D), F32), pltpu.VMEM((n, D), F32), pltpu.VMEM((1, D), F32),
         pltpu.VMEM((n, D), F32), pltpu.VMEM((n, D), F32), pltpu.VMEM((n, D), F32)])
    return pl.pallas_call(
        _carried(body, len(args), 3, payload, _first_last(nc)), name="hgrn_bwd", grid=(nc,),
        in_specs=in_specs, out_specs=out_specs, out_shape=out_shape, scratch_shapes=scratch,
        compiler_params=_cparams(("arbitrary",)),
    )(*(args if exchange is None else args + [exchange]))


XA_SCALE = XA_DH ** -0.5


def _xa_probs(qh, kmh):
    sc = _dot_nt(qh, kmh) * XA_SCALE
    p = jnp.exp(sc - jnp.max(sc, axis=1, keepdims=True))
    return p / jnp.sum(p, axis=1, keepdims=True)


def _xattn_fwd(x1, nw, wq, kv, wo, *, tm):
    L = x1.shape[0]
    tm = min(tm, L)

    def body(x_ref, nw_ref, wq_ref, kv_ref, wo_ref, o_ref, ox_ref):
        x = x_ref[...]
        q = _dot(_rms_fwd(x, nw_ref[...]), wq_ref[...])
        for h in range(XA_HEADS):
            hs = slice(XA_DH * h, XA_DH * h + XA_DH)
            p = _xa_probs(q[:, hs], kv_ref[:, hs])
            ox_ref[:, hs] = _dot(p, kv_ref[:, D + XA_DH * h:D + XA_DH * h + XA_DH])
        o_ref[...] = x + _dot(ox_ref[...], wo_ref[...])

    full = lambda a: pl.BlockSpec(a.shape, lambda i: (0, 0))
    return pl.pallas_call(
        body, name="xattn_fwd", grid=(L // tm,),
        in_specs=[pl.BlockSpec((tm, D), lambda i: (i, 0)), full(nw), full(wq), full(kv), full(wo)],
        out_specs=pl.BlockSpec((tm, D), lambda i: (i, 0)),
        out_shape=jax.ShapeDtypeStruct((L, D), F32),
        scratch_shapes=[pltpu.VMEM((tm, D), F32)],
        compiler_params=_cparams(("parallel",)),
    )(x1, nw, wq, kv, wo)


def _xattn_bwd(x1, dx2, nw, wq, kv, wo, *, tm):
    L = x1.shape[0]
    tm = min(tm, L)

    def body(x_ref, dx2_ref, nw_ref, wq_ref, kv_ref, wo_ref, dx1_ref, dx1b_ref, h_ref, dq_ref, ox_ref, dkv_ref,
             dnw_ref, dqs_ref):
        @pl.when(pl.program_id(0) == 0)
        def _():
            dkv_ref[...] = jnp.zeros_like(dkv_ref)
            dnw_ref[...] = jnp.zeros_like(dnw_ref)

        x = x_ref[...]
        dx2 = dx2_ref[...]
        hn = _rms_fwd(x, nw_ref[...]).astype(BF16)
        h_ref[...] = hn
        q = _dot(hn, wq_ref[...])
        dox = _dot_nt(dx2, wo_ref[...])
        for h in range(XA_HEADS):
            hs = slice(XA_DH * h, XA_DH * h + XA_DH)
            vs = slice(D + XA_DH * h, D + XA_DH * h + XA_DH)
            qh, kmh, vmh, doxh = q[:, hs], kv_ref[:, hs], kv_ref[:, vs], dox[:, hs]
            p = _xa_probs(qh, kmh)
            ox_ref[:, hs] = _dot(p, vmh).astype(BF16)
            dp = _dot_nt(doxh, vmh)
            dkv_ref[:, vs] += _dot_tn(p, doxh)
            ds = p * (dp - jnp.sum(dp * p, axis=1, keepdims=True)) * XA_SCALE
            dqs_ref[:, hs] = _dot(ds, kmh)
            dkv_ref[:, hs] += _dot_tn(ds, qh)
        dq = dqs_ref[...]
        dq_ref[...] = dq.astype(BF16)
        dx, dnw = _rms_bwd(_dot_nt(dq, wq_ref[...]), x, nw_ref[...])
        dx1 = dx2 + dx
        dx1_ref[...] = dx1
        dx1b_ref[...] = dx1.astype(BF16)
        dnw_ref[...] += dnw

    full = lambda a: pl.BlockSpec(a.shape, lambda i: (0, 0))
    row = pl.BlockSpec((tm, D), lambda i: (i, 0))
    return pl.pallas_call(
        body, name="xattn_bwd", grid=(L // tm,),
        in_specs=[row, row, full(nw), full(wq), full(kv), full(wo)],
        out_specs=[row, row, row, row, row, pl.BlockSpec((MEM_LEN, 2 * D), lambda i: (0, 0)),
                   pl.BlockSpec((1, D), lambda i: (0, 0))],
        out_shape=[jax.ShapeDtypeStruct((L, D), F32), jax.ShapeDtypeStruct((L, D), BF16),
                   jax.ShapeDtypeStruct((L, D), BF16), jax.ShapeDtypeStruct((L, D), BF16),
                   jax.ShapeDtypeStruct((L, D), BF16),
                   jax.ShapeDtypeStruct((MEM_LEN, 2 * D), F32), jax.ShapeDtypeStruct((1, D), F32)],
        scratch_shapes=[pltpu.VMEM((tm, D), F32)],
        compiler_params=_cparams(("arbitrary",)),
    )(x1, dx2, nw, wq, kv, wo)


FFN_TF = 256


def _ffn_fwd(x2, nw, wg, wu, wd, *, tm):
    L = x2.shape[0]
    tm = min(tm, L)
    tf = FFN_TF
    nf = FFN // tf

    def body(x_ref, nw_ref, wg_ref, wu_ref, wd_ref, o_ref, hn_ref, g_ref, u_ref, h_ref, acc_ref):
        j = pl.program_id(1)

        @pl.when(j == 0)
        def _():
            hn = _rms_fwd(x_ref[...], nw_ref[...]).astype(BF16)
            h_ref[...] = hn
            hn_ref[...] = hn
            acc_ref[...] = jnp.zeros_like(acc_ref)

        h = h_ref[...]
        g = _dot(h, wg_ref[...]).astype(BF16)
        u = _dot(h, wu_ref[...]).astype(BF16)
        g_ref[...] = g
        u_ref[...] = u
        acc_ref[...] += _dot(_silu(g.astype(F32)) * u.astype(F32), wd_ref[...])

        @pl.when(j == nf - 1)
        def _():
            o_ref[...] = x_ref[...] + acc_ref[...]

    row = pl.BlockSpec((tm, D), lambda i, j: (i, 0))
    colblk = pl.BlockSpec((tm, tf), lambda i, j: (i, j))
    return pl.pallas_call(
        body, name="ffn_fwd", grid=(L // tm, nf),
        in_specs=[row, pl.BlockSpec((1, D), lambda i, j: (0, 0)),
                  pl.BlockSpec((D, tf), lambda i, j: (0, j)), pl.BlockSpec((D, tf), lambda i, j: (0, j)),
                  pl.BlockSpec((tf, D), lambda i, j: (j, 0))],
        out_specs=[row, row, colblk, colblk],
        out_shape=[jax.ShapeDtypeStruct((L, D), F32), jax.ShapeDtypeStruct((L, D), BF16),
                   jax.ShapeDtypeStruct((L, FFN), BF16), jax.ShapeDtypeStruct((L, FFN), BF16)],
        scratch_shapes=[pltpu.VMEM((tm, D), BF16), pltpu.VMEM((tm, D), F32)],
        compiler_params=_cparams(("parallel", "arbitrary")),
    )(x2, nw, wg, wu, wd)


def _ffn_bwd(x2, dx3, dx3b, g, u, nw, wg, wu, wd, *, tm):
    L = x2.shape[0]
    tm = min(tm, L)
    tf = FFN_TF
    nf = FFN // tf

    def body(x_ref, dx3_ref, d3_ref, g_ref, u_ref, nw_ref, wg_ref, wu_ref, wd_ref,
             dx2_ref, dx2b_ref, a_ref, dg_ref, du_ref, dnw_ref, acc_ref):
        i, j = pl.program_id(0), pl.program_id(1)

        @pl.when(j == 0)
        def _():
            acc_ref[...] = jnp.zeros_like(acc_ref)

        g = g_ref[...].astype(F32)
        u = u_ref[...].astype(F32)
        s = _sigmoid(g)
        sg = g * s
        a_ref[...] = (sg * u).astype(BF16)
        da = _dot_nt(d3_ref[...], wd_ref[...])
        dg = (da * u * (s + sg * (1.0 - s))).astype(BF16)
        du = (da * sg).astype(BF16)
        dg_ref[...] = dg
        du_ref[...] = du
        acc_ref[...] += _dot_nt(dg, wg_ref[...]) + _dot_nt(du, wu_ref[...])

        @pl.when(jnp.logical_and(i == 0, j == 0))
        def _():
            dnw_ref[...] = jnp.zeros_like(dnw_ref)

        @pl.when(j == nf - 1)
        def _():
            dx, dnw = _rms_bwd(acc_ref[...], x_ref[...], nw_ref[...])
            dx2 = dx3_ref[...] + dx
            dx2_ref[...] = dx2
            dx2b_ref[...] = dx2.astype(BF16)
            dnw_ref[...] += dnw

    row = pl.BlockSpec((tm, D), lambda i, j: (i, 0))
    colblk = pl.BlockSpec((tm, tf), lambda i, j: (i, j))
    return pl.pallas_call(
        body, name="ffn_bwd", grid=(L // tm, nf),
        in_specs=[row, row, row, colblk, colblk, pl.BlockSpec((1, D), lambda i, j: (0, 0)),
                  pl.BlockSpec((D, tf), lambda i, j: (0, j)), pl.BlockSpec((D, tf), lambda i, j: (0, j)),
                  pl.BlockSpec((tf, D), lambda i, j: (j, 0))],
        out_specs=[row, row, colblk, colblk, colblk, pl.BlockSpec((1, D), lambda i, j: (0, 0))],
        out_shape=[jax.ShapeDtypeStruct((L, D), F32), jax.ShapeDtypeStruct((L, D), BF16),
                   jax.ShapeDtypeStruct((L, FFN), BF16), jax.ShapeDtypeStruct((L, FFN), BF16),
                   jax.ShapeDtypeStruct((L, FFN), BF16), jax.ShapeDtypeStruct((1, D), F32)],
        scratch_shapes=[pltpu.VMEM((tm, D), F32)],
        compiler_params=_cparams(("arbitrary", "arbitrary")),
    )(x2, dx3, dx3b, g, u, nw, wg, wu, wd)


def _final(x3, tgt, nw, *, tm):
    L = x3.shape[0]
    tm = min(tm, L)

    def body(x_ref, t_ref, nw_ref, dx_ref, dxb_ref, loss_ref, dnw_ref):
        @pl.when(pl.program_id(0) == 0)
        def _():
            loss_ref[...] = jnp.zeros_like(loss_ref)
            dnw_ref[...] = jnp.zeros_like(dnw_ref)

        x = x_ref[...]
        w = nw_ref[...]
        err = _rms_fwd(x, w) - t_ref[...]
        part = 0.5 * jnp.sum(jnp.sum(err * err, axis=1, keepdims=True), axis=0, keepdims=True) * (1.0 / D)
        loss_ref[...] += jnp.where(_iota2((1, 128), 1) == 0, part, 0.0)
        dx, dnw = _rms_bwd(err * (1.0 / D), x, w)
        dx_ref[...] = dx
        dxb_ref[...] = dx.astype(BF16)
        dnw_ref[...] += dnw

    row = pl.BlockSpec((tm, D), lambda i: (i, 0))
    return pl.pallas_call(
        body, name="final_loss", grid=(L // tm,),
        in_specs=[row, row, pl.BlockSpec((1, D), lambda i: (0, 0))],
        out_specs=[row, row, pl.BlockSpec((1, 128), lambda i: (0, 0)), pl.BlockSpec((1, D), lambda i: (0, 0))],
        out_shape=[jax.ShapeDtypeStruct((L, D), F32), jax.ShapeDtypeStruct((L, D), BF16),
                   jax.ShapeDtypeStruct((1, 128), F32), jax.ShapeDtypeStruct((1, D), F32)],
        compiler_params=_cparams(("arbitrary",)),
    )(x3, tgt, nw)


def _adamw(parts, w, m, v, *, tr, name):
    n_parts, R, C = parts.shape
    tr = min(tr, R)
    c1 = 1.0 / (1.0 - ADAM_B1 ** ADAM_STEP)
    c2 = 1.0 / (1.0 - ADAM_B2 ** ADAM_STEP)

    def body(p_ref, w_ref, m_ref, v_ref, g_ref, d_ref, nm_ref, nv_ref):
        g = p_ref[0]
        for k in range(1, n_parts):
            g = g + p_ref[k]
        nm = ADAM_B1 * m_ref[...] + (1.0 - ADAM_B1) * g
        nv = ADAM_B2 * v_ref[...] + (1.0 - ADAM_B2) * (g * g)
        g_ref[...] = g
        nm_ref[...] = nm
        nv_ref[...] = nv
        d_ref[...] = -ADAM_LR * ((nm * c1) / (jnp.sqrt(nv * c2) + ADAM_EPS) + ADAM_WD * w_ref[...])

    blk = pl.BlockSpec((tr, C), lambda i: (i, 0))
    sds = jax.ShapeDtypeStruct((R, C), F32)
    return pl.pallas_call(
        body, name=name, grid=(R // tr,),
        in_specs=[pl.BlockSpec((n_parts, tr, C), lambda i: (0, i, 0)), blk, blk, blk],
        out_specs=[blk, blk, blk, blk], out_shape=[sds, sds, sds, sds],
        compiler_params=_cparams(("parallel",)),
    )(parts, w, m, v)


def _position():
    return lax.axis_index("x"), lax.axis_index("y"), lax.axis_index("c")


def _comm_scratch():
    return [pltpu.SemaphoreType.DMA((7,)), pltpu.SemaphoreType.DMA((7,)), pltpu.SemaphoreType.DMA]


class _Gather:
    def __init__(self, x_ref, out_ref, send_sems, recv_sems, local_sem):
        x, y, c = _position()
        me, sibling = (x, y, c), (x, y, 1 - c)
        chips = [(1 - x, y), (x, 1 - y), (1 - x, 1 - y)]

        def rows(px, py, pc):
            return out_ref.at[4 * px + 2 * py + pc]

        def copy(k, block, to, src=None):
            return pltpu.make_async_remote_copy(
                src_ref=rows(*block) if src is None else src, dst_ref=rows(*block),
                send_sem=send_sems.at[k], recv_sem=recv_sems.at[k], device_id=to, device_id_type=MESH)

        self.mine = pltpu.make_async_copy(x_ref, rows(*me), local_sem)
        self.first = [copy(0, me, sibling, src=x_ref)]
        self.first += [copy(1 + j, me, (*chip, c), src=x_ref) for j, chip in enumerate(chips)]
        self.passed = [copy(4 + j, (*chip, c), sibling) for j, chip in enumerate(chips)]
        self.from_chips = [copy(1 + j, (*chip, c), me) for j, chip in enumerate(chips)]
        self.from_sibling = [copy(0, sibling, me)] + [copy(4 + j, (*chip, 1 - c), me) for j, chip in enumerate(chips)]

    def start(self):
        self.mine.start()
        for cp in self.first:
            cp.start()

    def forward(self):
        for got, cp in zip(self.from_chips, self.passed):
            got.wait_recv()
            cp.start()

    def finish(self):
        for got in self.from_sibling:
            got.wait_recv()
        for cp in self.first + self.passed:
            cp.wait_send()
        self.mine.wait()


class _Exchange:
    def __init__(self, g_ref, out_ref, send_sems, recv_sems, local_sem):
        x, y, c = _position()
        me = 4 * x + 2 * y + c
        self.mine = pltpu.make_async_copy(g_ref.at[me], out_ref.at[me], local_sem)
        self.copies = []
        for k in range(1, N_DEV):
            px = 1 - x if k & 4 else x
            py = 1 - y if k & 2 else y
            pc = 1 - c if k & 1 else c
            self.copies.append(pltpu.make_async_remote_copy(
                src_ref=g_ref.at[4 * px + 2 * py + pc], dst_ref=out_ref.at[me],
                send_sem=send_sems.at[k - 1], recv_sem=recv_sems.at[k - 1],
                device_id=(px, py, pc), device_id_type=MESH))

    def start(self):
        self.mine.start()
        for cp in self.copies:
            cp.start()

    def finish(self):
        for cp in self.copies:
            cp.wait()
        self.mine.wait()


def _allgather(xp):
    R, C = xp.shape

    def body(x_ref, out_ref, send_sems, recv_sems, local_sem):
        g = _Gather(x_ref, out_ref, send_sems, recv_sems, local_sem)
        g.start()
        g.forward()
        g.finish()

    return pl.pallas_call(
        body, name="allgather_w_in",
        out_shape=jax.ShapeDtypeStruct((N_DEV, R, C), xp.dtype),
        in_specs=[pl.BlockSpec(memory_space=pltpu.HBM)], out_specs=pl.BlockSpec(memory_space=pltpu.HBM),
        scratch_shapes=_comm_scratch(),
    )(xp)


def _carried(body, n_in, n_out, payload, phases):
    if payload is None:
        return body
    kind = payload[0]

    def new_body(*refs):
        ins, src_ref = refs[:n_in], refs[n_in]
        outs, dst_ref = refs[n_in + 1:n_in + 1 + n_out], refs[n_in + 1 + n_out]
        scratch, sems = refs[n_in + 2 + n_out:-3], refs[-3:]

        def run(before):
            for when, action, is_before in phases:
                if is_before == before:
                    @pl.when(when())
                    def _():
                        action(kind(src_ref, dst_ref, *sems))

        run(True)
        body(*ins, *outs, *scratch)
        run(False)

    return new_body


def _carried_specs(payload, in_specs, out_specs, out_shape, scratch):
    if payload is None:
        return in_specs, out_specs, out_shape, scratch
    kind, arr = payload
    landing = (N_DEV,) + arr.shape if kind is _Gather else arr.shape
    hbm = pl.BlockSpec(memory_space=pltpu.HBM)
    return (in_specs + [hbm], out_specs + [hbm], out_shape + [jax.ShapeDtypeStruct(landing, arr.dtype)],
            scratch + _comm_scratch())


def _small_allreduce(sp):
    R, C = sp.shape

    def body(s_ref, out_ref, buf_ref, send_sems, recv_sems):
        x, y, c = _position()
        me = 4 * x + 2 * y + c
        buf_ref[me] = s_ref[...]
        copies = []
        for k in range(1, N_DEV):
            px = 1 - x if k & 4 else x
            py = 1 - y if k & 2 else y
            pc = 1 - c if k & 1 else c
            cp = pltpu.make_async_remote_copy(
                src_ref=s_ref, dst_ref=buf_ref.at[me], send_sem=send_sems.at[k - 1], recv_sem=recv_sems.at[k - 1],
                device_id=(px, py, pc), device_id_type=MESH)
            cp.start()
            copies.append(cp)
        for cp in copies:
            cp.wait()
        tot = buf_ref[0]
        for k in range(1, N_DEV):
            tot = tot + buf_ref[k]
        out_ref[...] = tot

    return pl.pallas_call(
        body, name="allreduce_small",
        out_shape=jax.ShapeDtypeStruct((R, C), F32),
        in_specs=[pl.BlockSpec(memory_space=pltpu.VMEM)], out_specs=pl.BlockSpec(memory_space=pltpu.VMEM),
        scratch_shapes=[pltpu.VMEM((N_DEV, R, C), F32), pltpu.SemaphoreType.DMA((7,)), pltpu.SemaphoreType.DMA((7,))],
    )(sp)


def _col_shards(t, rows):
    n, cols = t.shape[0], t.shape[1] * t.shape[2] // rows
    return t.reshape(n, rows, cols).transpose(1, 0, 2).reshape(rows, n * cols)


def _to_col_shards(w, n):
    rows, tot = w.shape
    cols = tot // n
    return w.reshape(rows, n, cols).transpose(1, 0, 2).reshape(n, rows * cols // D, D)


def _local_step(x, mem, tgt, wt, small, dist=None):
    w_in = wt["w_in"]
    zpad = jnp.zeros((D, WB - D - SSD_HEADS), BF16)
    w_a = w_in[:, 2576:6672]
    w_b = jnp.concatenate([w_in[:, 0:D], w_in[:, 2560:2576], zpad], axis=1)
    w_c = w_in[:, D:2560]
    a_log, d_skip = small["a_log"], small["d_skip"]
    avec = jnp.pad(-jnp.exp(a_log), ((0, 0), (0, 128 - SSD_HEADS)))
    aexp = jnp.repeat(-jnp.exp(a_log), SSD_P, axis=1)
    dexp = jnp.repeat(d_skip, SSD_P, axis=1)
    dtb = jnp.pad(small["dt_bias"], ((0, 0), (0, 128 - SSD_HEADS)))
    emat = (lax.broadcasted_iota(jnp.int32, (128, D), 0) == lax.broadcasted_iota(jnp.int32, (128, D), 1) // SSD_P
            ).astype(F32)
    hg_nwx = jnp.tile(small["hg_norm_w"], (1, HG_HEADS))
    hgl = small["hg_lower_bounds"]
    nfw = small["norm_final_w"].reshape(1, D)

    received = {}
    pieces = (lambda group, grads: None) if dist is None else dist["pieces"]
    if dist is None:
        pa, hn_mix = _norm_mm(x, small["norm_mix_w"], w_a, tm=512, tn=1024, name="inproj_a", emit_h=True)
    else:
        pa, hn_mix, gath = _norm_mm(x, small["norm_mix_w"], w_a, tm=512, tn=1024, name="inproj_a", emit_h=True,
                                    gather=dist["rest_pack"])
        wt = {**wt, **dist["unpack_rest"](gath)}
    pb = _norm_mm(x, small["norm_mix_w"], w_b, tm=512, tn=512, name="inproj_b")
    pc = _norm_mm(x, small["norm_mix_w"], w_c, tm=512, tn=512, name="inproj_c")
    xbc = _conv_fwd(pc, small["conv_w"], small["conv_b"], tm=512)
    ya, ypre, ssd_sin = _ssd_fwd(xbc, pb, dtb, avec, aexp, dexp, small["ssd_norm_w"], emat)
    ob, opre, hg_sin = _hgrn_fwd(pa, hgl, hg_nwx)
    x1 = _mm2_res(x, ya, ob, wt["w_out"], tm=512, name="outproj")
    kvb, mn = _norm_mm(mem, small["norm_mem_w"], wt["xa_wkv"], tm=256, tn=1024, name="mem_kv", emit_h=True,
                       out_dtype=BF16)
    x2 = _xattn_fwd(x1, small["norm_xa_w"], wt["xa_wq"], kvb, wt["xa_wo"], tm=512)
    x3, hn_ffn, gate, up = _ffn_fwd(x2, small["norm_ffn_w"], wt["ffn_w_gate"], wt["ffn_w_up"], wt["ffn_w_down"],
                                    tm=1024)

    dx3, dx3b, loss, g_nf = _final(x3, tgt, nfw, tm=512)
    dx2, dx2b, act, dg, du, g_nffn = _ffn_bwd(x2, dx3, dx3b, gate, up, small["norm_ffn_w"], wt["ffn_w_gate"],
                                              wt["ffn_w_up"], wt["ffn_w_down"], tm=512)
    g_wg = _dw(hn_ffn, dg, tM=1024, tN=1408, tl=512, name="dw_gate")
    g_wu = _dw(hn_ffn, du, tM=1024, tN=1408, tl=512, name="dw_up")
    g_wd = _dw(act, dx3b, tM=1408, tN=1024, tl=512, name="dw_down")
    dx1, dx1b, hn_xa, dq, ox, dkv, g_nxa = _xattn_bwd(x1, dx2, small["norm_xa_w"], wt["xa_wq"], kvb, wt["xa_wo"],
                                                      tm=512)
    dkvb = dkv.astype(BF16)
    g_wq = _dw(hn_xa, dq, tM=1024, tN=1024, tl=512, name="dw_q")
    g_wo = _dw(ox, dx2b, tM=1024, tN=1024, tl=512, name="dw_o")
    g_wkv = _dw(mn, dkvb, tM=1024, tN=1024, tl=256, name="dw_kv")
    _, g_nmem = _mmnt_normbwd([dkvb], [wt["xa_wkv"]], mem, small["norm_mem_w"], None, tm=256, name="mem_bwd")
    dya, dob = _mm_nt2(dx1, wt["w_out"], tm=512, name="outproj_bwd")
    g_wout = jnp.concatenate([_dw(ya, dx1b, tM=1024, tN=1024, tl=512, name="dw_out_a"),
                              _dw(ob, dx1b, tM=1024, tN=1024, tl=512, name="dw_out_b")], axis=0)
    ffn_grads = {"ffn_w_gate": g_wg, "ffn_w_up": g_wu, "ffn_w_down": g_wd}
    mid_grads = {"w_out": g_wout, "xa_wq": g_wq, "xa_wkv": g_wkv, "xa_wo": g_wo}
    dpa, g_hgl, g_hgn_x, *got = _hgrn_bwd(pa, opre, hg_sin, dob, hgl, hg_nwx, exchange=pieces("ffn", ffn_grads))
    received["ffn"] = got[0] if got else None
    dpb, dxbc, g_ssdn, g_dtb, g_alog, g_dx, *got = _ssd_bwd(
        xbc, pb, ypre, ssd_sin, dya, dtb, avec, aexp, dexp, small["ssd_norm_w"], emat,
        exchange=pieces("mid", mid_grads))
    received["mid"] = got[0] if got else None
    dpc, g_cw, g_cb = _conv_bwd(pc, dxbc, small["conv_w"], small["conv_b"], tm=512)
    g_wa = _dw(hn_mix, dpa, tM=1024, tN=1024, tl=512, name="dw_in_a")
    g_wb = _dw(hn_mix, dpb, tM=1024, tN=512, tl=512, name="dw_in_b")
    g_wc = _dw(hn_mix, dpc, tM=1024, tN=512, tl=512, name="dw_in_c")
    g_win = jnp.concatenate([g_wb[:, 0:D], g_wc, g_wb[:, D:D + SSD_HEADS], g_wa], axis=1)
    grad_x, g_nmix, *got = _mmnt_normbwd([dpa, dpb, dpc], [w_a, w_b, w_c], x, small["norm_mix_w"], dx1, tm=256,
                                         name="inproj_bwd", exchange=pieces("in", {"w_in": g_win}))
    received["in"] = got[0] if got else None

    big = {"w_in": g_win, **mid_grads, **ffn_grads}
    smallg = {
        "norm_mix_w": g_nmix, "conv_w": g_cw, "conv_b": g_cb, "dt_bias": g_dtb[:, 0:SSD_HEADS],
        "a_log": g_alog[:, 0:SSD_HEADS], "d_skip": g_dx.reshape(SSD_HEADS, SSD_P).sum(axis=1).reshape(1, SSD_HEADS),
        "ssd_norm_w": g_ssdn, "hg_lower_bounds": g_hgl,
        "hg_norm_w": g_hgn_x.reshape(HG_HEADS, HG_K).sum(axis=0).reshape(1, HG_K),
        "norm_xa_w": g_nxa, "norm_mem_w": g_nmem, "norm_ffn_w": g_nffn, "norm_final_w": g_nf,
        "loss": loss[:, 0:1]}
    return grad_x, big, smallg, received


COL_SHARDED = ("w_in", "xa_wkv", "ffn_w_gate", "ffn_w_up")


def _pad_rows(t, rows):
    return jnp.pad(t, [(0, 0)] * (t.ndim - 2) + [(0, rows - t.shape[-2]), (0, 0)])


def _group_fill(parts, group, lead, dtype):
    used = sum(p.shape[-2] for p in parts)
    if used < GROUP_ROWS[group]:
        parts.append(jnp.zeros(lead + (GROUP_ROWS[group] - used, D), dtype))
    return parts


def _pack_group(shards, group, dtype, extra=None):
    parts = [_pad_rows(shards[n].astype(dtype).reshape(r, D), _rows_padded(r)) for n, r in GROUPS[group]]
    if extra is not None:
        parts.append(extra)
    return jnp.concatenate(_group_fill(parts, group, (), dtype), axis=0)


def _unpack_group(packed, group, shapes):
    out, off = {}, 0
    for n, r in GROUPS[group]:
        out[n] = packed[off:off + r].reshape(shapes[n])
        off += _rows_padded(r)
    return out


def _unpack_gathered(gath, groups):
    out, base = {}, 0
    for group in groups:
        off = base
        for n, r in GROUPS[group]:
            t = gath[:, off:off + r]
            out[n] = _col_shards(t, D) if n in COL_SHARDED else t.reshape(N_DEV * r, D)
            off += _rows_padded(r)
        base += GROUP_ROWS[group]
    return out


def _grad_pieces(group, grads):
    parts = [_pad_rows(_to_col_shards(grads[n], N_DEV) if n in COL_SHARDED else grads[n].reshape(N_DEV, r, D),
                       _rows_padded(r)) for n, r in GROUPS[group]]
    return jnp.concatenate(_group_fill(parts, group, (N_DEV,), F32), axis=1)


def _pack_small(vals):
    rows = []
    for n, s in SMALL:
        v = vals[n].reshape(-1)
        rows.append(jnp.pad(v, (0, -(-s // 128) * 128 - s)))
    flat = jnp.concatenate(rows)
    return jnp.pad(flat, (0, SMALL_ROWS_PAD * 128 - flat.shape[0])).reshape(SMALL_ROWS_PAD, 128)


def _unpack_small(packed, shapes):
    flat = packed.reshape(-1)
    return {n: flat[SMALL_LAY[n][0]:SMALL_LAY[n][0] + SMALL_LAY[n][1]].reshape(shapes[n]) for n, _ in SMALL}


WEIGHTS = ['norm_mix_w', 'w_in', 'conv_w', 'conv_b', 'dt_bias', 'a_log', 'd_skip', 'ssd_norm_w', 'hg_lower_bounds',
           'hg_norm_w', 'w_out', 'norm_xa_w', 'norm_mem_w', 'xa_wq', 'xa_wkv', 'xa_wo', 'norm_ffn_w', 'ffn_w_gate',
           'ffn_w_up', 'ffn_w_down', 'norm_final_w']
BIG = tuple(n for n, _ in PACK)


def kernel(x, mem, norm_mix_w, w_in, conv_w, conv_b, dt_bias, a_log, d_skip, ssd_norm_w, hg_lower_bounds, hg_norm_w, w_out, norm_xa_w, norm_mem_w, xa_wq, xa_wkv, xa_wo, norm_ffn_w, ffn_w_gate, ffn_w_up, ffn_w_down, norm_final_w, loss_target, m_norm_mix_w, m_w_in, m_conv_w, m_conv_b, m_dt_bias, m_a_log, m_d_skip, m_ssd_norm_w, m_hg_lower_bounds, m_hg_norm_w, m_w_out, m_norm_xa_w, m_norm_mem_w, m_xa_wq, m_xa_wkv, m_xa_wo, m_norm_ffn_w, m_ffn_w_gate, m_ffn_w_up, m_ffn_w_down, m_norm_final_w, v_norm_mix_w, v_w_in, v_conv_w, v_conv_b, v_dt_bias, v_a_log, v_d_skip, v_ssd_norm_w, v_hg_lower_bounds, v_hg_norm_w, v_w_out, v_norm_xa_w, v_norm_mem_w, v_xa_wq, v_xa_wkv, v_xa_wo, v_norm_ffn_w, v_ffn_w_gate, v_ffn_w_up, v_ffn_w_down, v_norm_final_w):
    args = dict(locals())
    w = {n: args[n] for n in WEIGHTS}
    mo = {n: args["m_" + n] for n in WEIGHTS}
    vo = {n: args["v_" + n] for n in WEIGHTS}
    me = 4 * lax.axis_index("x") + 2 * lax.axis_index("y") + lax.axis_index("c")

    big_sh = {n: w[n][0] for n in BIG}
    cw_bits = lax.bitcast_convert_type(conv_w[0], BF16).reshape(-1)
    cw_rows = jnp.pad(cw_bits, (0, CONV_BITS_ROWS * D - cw_bits.shape[0])).reshape(CONV_BITS_ROWS, D)
    gath = _allgather(_pack_group(big_sh, "in", BF16, extra=cw_rows))
    wt = _unpack_gathered(gath, ("in",))
    off = _rows_padded(GROUPS["in"][0][1])
    cw_all = lax.bitcast_convert_type(gath[:, off:off + 2].reshape(N_DEV, 2 * D)[:, 0:1536].reshape(N_DEV, 4, 192, 2),
                                      F32)
    conv_w_full = cw_all.transpose(1, 0, 2).reshape(4, 1536)

    small = {n: w[n][0] if w[n].ndim == 3 else w[n] for n in WEIGHTS if n not in BIG}
    small["conv_w"] = conv_w_full
    small["hg_lower_bounds"] = hg_lower_bounds
    dist = {"rest_pack": jnp.concatenate([_pack_group(big_sh, "mid", BF16), _pack_group(big_sh, "ffn", BF16)], axis=0),
            "unpack_rest": lambda g: _unpack_gathered(g, ("mid", "ffn")),
            "pieces": _grad_pieces}
    grad_x, _, gsmall, received = _local_step(x[0], mem[0], loss_target[0], wt, small, dist)

    shapes = {n: w[n].shape for n in BIG}
    out_g, out_d, out_m, out_v = {}, {}, {}, {}
    for group in GROUPS:
        wp = _pack_group(big_sh, group, F32)
        mp = _pack_group({n: mo[n][0] for n in BIG}, group, F32)
        vp = _pack_group({n: vo[n][0] for n in BIG}, group, F32)
        packed = _adamw(received[group], wp, mp, vp, tr=ADAM_ROWS[group], name="adamw_" + group)
        for dst, src in zip((out_g, out_d, out_m, out_v), packed):
            dst.update(_unpack_group(src, group, shapes))

    tot = _small_allreduce(_pack_small(gsmall))
    sshapes = {n: (w[n].shape if n != "conv_w" else (1, 4, 1536)) for n, _ in SMALL if n != "loss"}
    sshapes["loss"] = ()
    gs = _unpack_small(tot, sshapes)
    loss = gs.pop("loss")
    gs["conv_w"] = lax.dynamic_slice_in_dim(gs["conv_w"], me * 192, 192, axis=2)
    names = [n for n, _ in SMALL if n != "loss"]
    flat = lambda d: jnp.concatenate([d[n].reshape(-1) for n in names])
    nsm = flat(gs).shape[0]
    rows = -(-nsm // (8 * 128)) * 8
    padr = lambda t: jnp.pad(t, (0, rows * 128 - nsm)).reshape(rows, 128)
    sg, sd, snm, snv = _adamw(padr(flat(gs)).reshape(1, rows, 128), padr(flat(w)), padr(flat(mo)), padr(flat(vo)),
                              tr=rows, name="adamw_small")

    def unflat(t):
        t, out, off = t.reshape(-1), {}, 0
        for n in names:
            sz = w[n].size
            out[n] = t[off:off + sz].reshape(w[n].shape)
            off += sz
        return out

    for dst, src in ((out_g, sg), (out_d, sd), (out_m, snm), (out_v, snv)):
        dst.update(unflat(src))
    return (loss, grad_x[None], *[out_g[n] for n in WEIGHTS], *[out_d[n] for n in WEIGHTS],
            *[out_m[n] for n in WEIGHTS], *[out_v[n] for n in WEIGHTS])
```

```python
import jax
import jax.numpy as jnp
from jax import lax
from jax.experimental import pallas as pl
from jax.experimental.pallas import tpu as pltpu

F32, BF16 = jnp.float32, jnp.bfloat16
MESH = pl.DeviceIdType.MESH

D = 1024
EPS = 1e-6
SSD_HEADS, SSD_P, SSD_N, SSD_Q = 16, 64, 128, 128
HG_HEADS, HG_K, HG_STEP, HG_SUB = 8, 128, 128, 64
XA_HEADS, XA_DH, MEM_LEN = 4, 256, 256
FFN = 2816
N_IN = 6672
N_DEV = 8
WA, WB, WC = 4096, 1152, 1536
ADAM_LR, ADAM_B1, ADAM_B2, ADAM_EPS, ADAM_WD, ADAM_STEP = 0.001, 0.9, 0.999, 1e-08, 0.01, 10
VMEM_MB = 2 ** 20

PACK = (("w_in", 834), ("w_out", 256), ("xa_wq", 128), ("xa_wkv", 256), ("xa_wo", 128),
        ("ffn_w_gate", 352), ("ffn_w_up", 352), ("ffn_w_down", 352))
ROW_TILE = 16
GROUPS = {"in": PACK[0:1], "mid": PACK[1:5], "ffn": PACK[5:8]}
GROUP_ROWS = {"in": 896, "mid": 768, "ffn": 1056}
ADAM_ROWS = {"in": 128, "mid": 128, "ffn": 176}
CONV_BITS_ROWS = ROW_TILE


def _rows_padded(r):
    return -(-r // ROW_TILE) * ROW_TILE

SMALL = (("norm_mix_w", 1, 1024), ("conv_w", 4, 1536), ("conv_b", 1, 1536), ("dt_bias", 1, 16), ("a_log", 1, 16),
         ("d_skip", 1, 16), ("ssd_norm_w", 1, 1024), ("hg_lower_bounds", 2, 1024), ("hg_norm_w", 1, 128),
         ("norm_xa_w", 1, 1024), ("norm_mem_w", 1, 1024), ("norm_ffn_w", 1, 1024), ("norm_final_w", 1, 1024),
         ("loss", 1, 1))
SMALL_COLS = 1536
SMALL_SLAB = 8
SMALL_ROWS = SMALL_SLAB * len(SMALL)
SMALL_AT = {n: (SMALL_SLAB * i, r, c) for i, (n, r, c) in enumerate(SMALL)}


def _cparams(sem=None, vmem_mb=48):
    return pltpu.CompilerParams(dimension_semantics=sem, vmem_limit_bytes=vmem_mb * VMEM_MB)


def _dot(a, b):
    return jnp.dot(a.astype(BF16), b.astype(BF16), preferred_element_type=F32)


def _dot_nt(a, b):
    return lax.dot_general(a.astype(BF16), b.astype(BF16), (((1,), (1,)), ((), ())), preferred_element_type=F32)


def _dot_tn(a, b):
    return lax.dot_general(a.astype(BF16), b.astype(BF16), (((0,), (0,)), ((), ())), preferred_element_type=F32)


def _split3(a):
    a1 = a.astype(BF16)
    r1 = a - a1.astype(F32)
    a2 = r1.astype(BF16)
    a3 = (r1 - a2.astype(F32)).astype(BF16)
    return a1, a2, a3


def _dot_hi(a, b, general=0):
    if general == 0:
        return sum(jnp.dot(t, b.astype(BF16), preferred_element_type=F32) for t in _split3(a))
    return sum(jnp.dot(a.astype(BF16), t, preferred_element_type=F32) for t in _split3(b))


def _dot_nt_hi(a, b):
    return sum(_dot_nt(t, b) for t in _split3(a))


def _sigmoid(x):
    return 1.0 / (1.0 + jnp.exp(-x))


def _silu(x):
    return x * _sigmoid(x)


def _dsilu(x):
    s = _sigmoid(x)
    return s * (1.0 + x * (1.0 - s))


def _softplus(x):
    return jnp.maximum(x, 0.0) + jnp.log(1.0 + jnp.exp(-jnp.abs(x)))


def _rms_fwd(x, w):
    r = lax.rsqrt(jnp.mean(x * x, axis=1, keepdims=True) + EPS)
    return x * r * w


def _rms_bwd(dy, x, w):
    r = lax.rsqrt(jnp.mean(x * x, axis=1, keepdims=True) + EPS)
    xh = x * r
    g = dy * w
    dx = r * (g - xh * jnp.mean(g * xh, axis=1, keepdims=True))
    return dx, jnp.sum(dy * xh, axis=0, keepdims=True)


def _iota2(shape, dim):
    return lax.broadcasted_iota(jnp.int32, shape, dim)


def _tril(n):
    return (_iota2((n, n), 0) >= _iota2((n, n), 1)).astype(F32)


def _triu(n):
    return (_iota2((n, n), 0) <= _iota2((n, n), 1)).astype(F32)


def _norm_mm(x, nw, w, *, tm, tn, name, emit_h=False, out_dtype=F32):
    L, K = x.shape
    N = w.shape[1]
    tm, tn = min(tm, L), min(tn, N)
    ni, nj = L // tm, N // tn

    def body(x_ref, nw_ref, w_ref, *rest):
        if emit_h:
            o_ref, h_ref, hs_ref = rest
        else:
            o_ref, hs_ref = rest

        @pl.when(pl.program_id(1) == 0)
        def _():
            h = _rms_fwd(x_ref[...], nw_ref[...]).astype(BF16)
            hs_ref[...] = h
            if emit_h:
                h_ref[...] = h

        o_ref[...] = jnp.dot(hs_ref[...], w_ref[...], preferred_element_type=F32).astype(out_dtype)

    out_shape = [jax.ShapeDtypeStruct((L, N), out_dtype)]
    out_specs = [pl.BlockSpec((tm, tn), lambda i, j: (i, j))]
    if emit_h:
        out_shape.append(jax.ShapeDtypeStruct((L, K), BF16))
        out_specs.append(pl.BlockSpec((tm, K), lambda i, j: (i, 0)))
    res = pl.pallas_call(
        body, name=name, grid=(ni, nj),
        in_specs=[pl.BlockSpec((tm, K), lambda i, j: (i, 0)), pl.BlockSpec((1, K), lambda i, j: (0, 0)),
                  pl.BlockSpec((K, tn), lambda i, j: (0, j))],
        out_specs=out_specs, out_shape=out_shape, scratch_shapes=[pltpu.VMEM((tm, K), BF16)],
        compiler_params=_cparams(("parallel", "arbitrary")),
    )(x, nw, w)
    return res if len(res) > 1 else res[0]


def _inproj(x, nw, w_a, w_b, w_c, *, tm, gather=None):
    L, K = x.shape
    tm = min(tm, L)
    ni = L // tm

    def body(x_ref, nw_ref, wa_ref, wb_ref, wc_ref, pa_ref, hf_ref, z_ref, dt_ref, pc_ref, h_ref):
        h = _rms_fwd(x_ref[...], nw_ref[...]).astype(BF16)
        h_ref[...] = h
        pa_ref[...] = jnp.dot(h, wa_ref[:, 0:3 * D], preferred_element_type=F32).astype(BF16)
        hf_ref[...] = jnp.dot(h, wa_ref[:, 3 * D:4 * D], preferred_element_type=F32)
        pb = jnp.dot(h, wb_ref[...], preferred_element_type=F32)
        z_ref[...] = pb[:, 0:D].astype(BF16)
        dt_ref[...] = pb[:, D:D + 128]
        pc_ref[...] = jnp.dot(h, wc_ref[...], preferred_element_type=F32).astype(BF16)

    row = lambda n: pl.BlockSpec((tm, n), lambda i: (i, 0))
    full = lambda a: pl.BlockSpec(a.shape, lambda i: (0, 0))
    at = lambda i: lambda: pl.program_id(0) == i
    payload = None if gather is None else (_Gather, gather)
    phases = [(at(0), _Gather.start, True), (at(ni // 2), _Gather.forward, True), (at(ni - 1), _Gather.finish, False)]
    in_specs, out_specs, out_shape, scratch = _carried_specs(
        payload, [row(K), full(nw), full(w_a), full(w_b), full(w_c)],
        [row(3 * D), row(D), row(D), row(128), row(WC), row(K)],
        [jax.ShapeDtypeStruct((L, 3 * D), BF16), jax.ShapeDtypeStruct((L, D), F32),
         jax.ShapeDtypeStruct((L, D), BF16), jax.ShapeDtypeStruct((L, 128), F32),
         jax.ShapeDtypeStruct((L, WC), BF16), jax.ShapeDtypeStruct((L, K), BF16)], [])
    args = [x, nw, w_a, w_b, w_c]
    return pl.pallas_call(
        _carried(body, 5, 6, payload, phases), name="inproj", grid=(ni,),
        in_specs=in_specs, out_specs=out_specs, out_shape=out_shape, scratch_shapes=scratch,
        compiler_params=_cparams(("arbitrary",), vmem_mb=58),
    )(*(args if gather is None else args + [gather]))


def _mm2_res(res, a1, a2, w, *, tm, name):
    L, N = res.shape
    K = a1.shape[1]
    tm = min(tm, L)

    def body(r_ref, a1_ref, a2_ref, w_ref, o_ref):
        acc = jnp.dot(a1_ref[...], w_ref[0:K, :], preferred_element_type=F32)
        acc += jnp.dot(a2_ref[...], w_ref[K:2 * K, :], preferred_element_type=F32)
        o_ref[...] = r_ref[...] + acc

    return pl.pallas_call(
        body, name=name, grid=(L // tm,),
        in_specs=[pl.BlockSpec((tm, N), lambda i: (i, 0)), pl.BlockSpec((tm, K), lambda i: (i, 0)),
                  pl.BlockSpec((tm, K), lambda i: (i, 0)), pl.BlockSpec((2 * K, N), lambda i: (0, 0))],
        out_specs=pl.BlockSpec((tm, N), lambda i: (i, 0)),
        out_shape=jax.ShapeDtypeStruct((L, N), F32),
        compiler_params=_cparams(("parallel",)),
    )(res, a1, a2, w)


def _mm_nt2(a, w, *, tm, name):
    L, N = a.shape
    K = w.shape[0] // 2
    tm = min(tm, L)

    def body(a_ref, w_ref, o1_ref, o2_ref):
        av = a_ref[...].astype(BF16)
        o1_ref[...] = _dot_nt(av, w_ref[0:K, :]).astype(BF16)
        o2_ref[...] = _dot_nt(av, w_ref[K:2 * K, :]).astype(BF16)

    return pl.pallas_call(
        body, name=name, grid=(L // tm,),
        in_specs=[pl.BlockSpec((tm, N), lambda i: (i, 0)), pl.BlockSpec((2 * K, N), lambda i: (0, 0))],
        out_specs=[pl.BlockSpec((tm, K), lambda i: (i, 0)), pl.BlockSpec((tm, K), lambda i: (i, 0))],
        out_shape=[jax.ShapeDtypeStruct((L, K), BF16), jax.ShapeDtypeStruct((L, K), BF16)],
        compiler_params=_cparams(("parallel",)),
    )(a, w)


def _dw(a, b, *, tM, tN, tl, name):
    L, M = a.shape
    N = b.shape[1]
    tM, tN, tl = min(tM, M), min(tN, N), min(tl, L)

    def body(a_ref, b_ref, o_ref):
        @pl.when(pl.program_id(2) == 0)
        def _():
            o_ref[...] = jnp.zeros_like(o_ref)

        o_ref[...] += _dot_tn(a_ref[...], b_ref[...])

    return pl.pallas_call(
        body, name=name, grid=(M // tM, N // tN, L // tl),
        in_specs=[pl.BlockSpec((tl, tM), lambda i, j, l: (l, i)), pl.BlockSpec((tl, tN), lambda i, j, l: (l, j))],
        out_specs=pl.BlockSpec((tM, tN), lambda i, j, l: (i, j)),
        out_shape=jax.ShapeDtypeStruct((M, N), F32),
        compiler_params=_cparams(("parallel", "parallel", "arbitrary")),
    )(a, b)


def _first_last(n):
    return [(lambda: pl.program_id(0) == 0, _Exchange.start, True),
            (lambda: pl.program_id(0) == n - 1, _Exchange.finish, False)]


def _mmnt_normbwd(a_list, w_list, x, nw, res, *, tm, name, exchange=None):
    L, Dm = x.shape
    tm = min(tm, L)
    n = len(a_list)
    has_res = res is not None

    def body(*refs):
        a_refs, w_refs = refs[:n], refs[n:2 * n]
        x_ref, nw_ref = refs[2 * n], refs[2 * n + 1]
        k = 2 * n + 2
        r_ref = refs[k] if has_res else None
        dx_ref, dnw_ref = refs[k + has_res], refs[k + has_res + 1]
        dh = _dot_nt(a_refs[0][...], w_refs[0][...])
        for a_ref, w_ref in zip(a_refs[1:], w_refs[1:]):
            dh += _dot_nt(a_ref[...], w_ref[...])
        dx, dnw = _rms_bwd(dh, x_ref[...], nw_ref[...])
        dx_ref[...] = dx + r_ref[...] if has_res else dx

        @pl.when(pl.program_id(0) == 0)
        def _():
            dnw_ref[...] = jnp.zeros_like(dnw_ref)

        dnw_ref[...] += dnw

    in_specs = [pl.BlockSpec((tm, a.shape[1]), lambda i: (i, 0)) for a in a_list]
    in_specs += [pl.BlockSpec(w.shape, lambda i: (0, 0)) for w in w_list]
    in_specs += [pl.BlockSpec((tm, Dm), lambda i: (i, 0)), pl.BlockSpec((1, Dm), lambda i: (0, 0))]
    args = [*a_list, *w_list, x, nw]
    if has_res:
        in_specs.append(pl.BlockSpec((tm, Dm), lambda i: (i, 0)))
        args.append(res)
    payload = None if exchange is None else (_Exchange, exchange)
    n_in = len(args)
    if exchange is not None:
        args.append(exchange)
    in_specs, out_specs, out_shape, scratch = _carried_specs(
        payload, in_specs, [pl.BlockSpec((tm, Dm), lambda i: (i, 0)), pl.BlockSpec((1, Dm), lambda i: (0, 0))],
        [jax.ShapeDtypeStruct((L, Dm), F32), jax.ShapeDtypeStruct((1, Dm), F32)], [])
    return pl.pallas_call(
        _carried(body, n_in, 2, payload, _first_last(L // tm)), name=name, grid=(L // tm,), in_specs=in_specs,
        out_specs=out_specs, out_shape=out_shape, scratch_shapes=scratch,
        compiler_params=_cparams(("arbitrary",), vmem_mb=56),
    )(*args)


CONV_TN = 512


HALO = 16


def _conv_pre(cat, w_ref, b_ref, rows):
    shifted = [pltpu.roll(cat, 3 - k, 0)[HALO:HALO + rows] for k in range(3)] + [cat[HALO:HALO + rows]]
    pre = b_ref[...] + w_ref[3:4, :] * shifted[3]
    for k in range(3):
        pre += w_ref[k:k + 1, :] * shifted[k]
    return pre, shifted


def _conv_fwd(pc, cw, cb, *, tm):
    L, C = pc.shape
    tm = min(tm, L)
    tn = CONV_TN

    def body(u_ref, halo_ref, w_ref, b_ref, o_ref):
        halo = jnp.where(pl.program_id(1) > 0, halo_ref[...].astype(F32), 0.0)
        cat = jnp.concatenate([halo, u_ref[...].astype(F32)], axis=0)
        pre, _ = _conv_pre(cat, w_ref, b_ref, tm)
        o_ref[...] = _silu(pre).astype(BF16)

    return pl.pallas_call(
        body, name="conv_fwd", grid=(C // tn, L // tm),
        in_specs=[pl.BlockSpec((tm, tn), lambda j, i: (i, j)),
                  pl.BlockSpec((HALO, tn), lambda j, i: (jnp.maximum(i * (tm // HALO) - 1, 0), j)),
                  pl.BlockSpec((4, tn), lambda j, i: (0, j)), pl.BlockSpec((1, tn), lambda j, i: (0, j))],
        out_specs=pl.BlockSpec((tm, tn), lambda j, i: (i, j)),
        out_shape=jax.ShapeDtypeStruct((L, C), BF16),
        compiler_params=_cparams(("parallel", "parallel")),
    )(pc, pc, cw, cb)


def _conv_bwd(pc, dact, cw, cb, *, tm):
    L, C = pc.shape
    tm = min(tm, L)
    tn = CONV_TN
    nt = L // tm

    def body(u_ref, halo_ref, unext_ref, da_ref, danext_ref, w_ref, b_ref, du_ref, dw_ref, db_ref):
        i = pl.program_id(1)
        halo = jnp.where(i > 0, halo_ref[...].astype(F32), 0.0)
        cat = jnp.concatenate([halo, u_ref[...].astype(F32), unext_ref[...].astype(F32)], axis=0)
        pre, shifted = _conv_pre(cat, w_ref, b_ref, tm + HALO)
        da = jnp.concatenate([da_ref[...].astype(F32),
                              jnp.where(i < nt - 1, danext_ref[...].astype(F32), 0.0)], axis=0)
        dpre = da * _dsilu(pre)
        du = w_ref[3:4, :] * dpre[0:tm]
        for k in range(3):
            du += w_ref[k:k + 1, :] * pltpu.roll(dpre, tm + HALO - (3 - k), 0)[0:tm]
        du_ref[...] = du.astype(BF16)

        @pl.when(i == 0)
        def _():
            dw_ref[...] = jnp.zeros_like(dw_ref)
            db_ref[...] = jnp.zeros_like(db_ref)

        dp = dpre[0:tm]
        dw_ref[...] += jnp.concatenate(
            [jnp.sum(dp * shifted[k][0:tm], axis=0, keepdims=True) for k in range(4)], axis=0)
        db_ref[...] += jnp.sum(dp, axis=0, keepdims=True)

    nb = L // HALO
    return pl.pallas_call(
        body, name="conv_bwd", grid=(C // tn, nt),
        in_specs=[pl.BlockSpec((tm, tn), lambda j, i: (i, j)),
                  pl.BlockSpec((HALO, tn), lambda j, i: (jnp.maximum(i * (tm // HALO) - 1, 0), j)),
                  pl.BlockSpec((HALO, tn), lambda j, i: (jnp.minimum((i + 1) * (tm // HALO), nb - 1), j)),
                  pl.BlockSpec((tm, tn), lambda j, i: (i, j)),
                  pl.BlockSpec((HALO, tn), lambda j, i: (jnp.minimum((i + 1) * (tm // HALO), nb - 1), j)),
                  pl.BlockSpec((4, tn), lambda j, i: (0, j)), pl.BlockSpec((1, tn), lambda j, i: (0, j))],
        out_specs=[pl.BlockSpec((tm, tn), lambda j, i: (i, j)), pl.BlockSpec((4, tn), lambda j, i: (0, j)),
                   pl.BlockSpec((1, tn), lambda j, i: (0, j))],
        out_shape=[jax.ShapeDtypeStruct((L, C), BF16), jax.ShapeDtypeStruct((4, C), F32),
                   jax.ShapeDtypeStruct((1, C), F32)],
        compiler_params=_cparams(("parallel", "arbitrary")),
    )(pc, pc, pc, dact, dact, cw, cb)


def _ssd_common(dtr_ref, dtb_ref, avec_ref, aexp_ref, e_ref, acx_ref, acol_ref, arow_ref):
    q = SSD_Q
    tril = _tril(q)
    dtpre = dtr_ref[...] + dtb_ref[...]
    dt = _softplus(dtpre)
    dtx = _dot_hi(dt, e_ref[...])
    acx_ref[...] = _dot_hi(tril, dtx * aexp_ref[...], 1)
    acol = _dot_hi(tril, dt * avec_ref[...], 1)
    acol_ref[...] = acol
    arow_ref[...] = acol.T
    return dtpre, dt, dtx


def _ssd_fwd(xbc, pz, pdt, dtb, avec, aexp, dexp, nw, emat):
    L = xbc.shape[0]
    q = SSD_Q
    nc = L // q

    def body(xbc_ref, z_ref, dtr_ref, dtb_ref, avec_ref, aexp_ref, dexp_ref, nw_ref, e_ref,
             ya_ref, ypre_ref, sin_ref, st_ref, acx_ref, acol_ref, arow_ref, xdt_ref, y_ref):
        @pl.when(pl.program_id(0) == 0)
        def _():
            st_ref[...] = jnp.zeros_like(st_ref)

        sin_ref[...] = st_ref[...].astype(BF16)
        _, _, dtx = _ssd_common(dtr_ref, dtb_ref, avec_ref, aexp_ref, e_ref, acx_ref, acol_ref, arow_ref)
        xs = xbc_ref[:, 0:D].astype(F32)
        xdt = xs * dtx
        xdt_ref[...] = xdt
        acx = acx_ref[...]
        alast = acx_ref[q - 1:q, :]
        xdtd = xdt * jnp.exp(alast - acx)
        eac = jnp.exp(acx)
        ealast = jnp.exp(alast)
        causal = _iota2((q, q), 0) >= _iota2((q, q), 1)
        for g in range(2):
            gs = slice(512 * g, 512 * g + 512)
            bm = xbc_ref[:, D + 128 * g:D + 128 * g + 128]
            cm = xbc_ref[:, D + 256 + 128 * g:D + 256 + 128 * g + 128]
            stg = st_ref[:, gs]
            yoff = _dot(cm, stg) * eac[:, gs]
            gmat = _dot_nt(cm, bm)
            for e in range(8):
                h = 8 * g + e
                hs = slice(64 * h, 64 * h + 64)
                col = acol_ref[:, h:h + 1]
                row = arow_ref[h:h + 1, :]
                lm = jnp.exp(jnp.where(causal, col - row, -1e30))
                y_ref[:, hs] = _dot(gmat * lm, xdt_ref[:, hs])
            y_ref[:, gs] += yoff + dexp_ref[:, gs] * xs[:, gs]
            st_ref[:, gs] = stg * ealast[:, gs] + _dot_tn(bm, xdtd[:, gs])
        ypre_ref[...] = y_ref[...].astype(BF16)
        for g in range(2):
            gs = slice(512 * g, 512 * g + 512)
            yz = y_ref[:, gs] * _silu(z_ref[:, gs].astype(F32))
            ya_ref[:, gs] = _rms_fwd(yz, nw_ref[:, gs]).astype(BF16)

    vec = lambda n: pl.BlockSpec((1, n), lambda c: (0, 0))
    return pl.pallas_call(
        body, name="ssd_fwd", grid=(nc,),
        in_specs=[pl.BlockSpec((q, 1536), lambda c: (c, 0)), pl.BlockSpec((q, D), lambda c: (c, 0)),
                  pl.BlockSpec((q, 128), lambda c: (c, 0)), vec(128), vec(128), vec(D), vec(D), vec(D),
                  pl.BlockSpec((128, D), lambda c: (0, 0))],
        out_specs=[pl.BlockSpec((q, D), lambda c: (c, 0)), pl.BlockSpec((q, D), lambda c: (c, 0)),
                   pl.BlockSpec((128, D), lambda c: (c, 0))],
        out_shape=[jax.ShapeDtypeStruct((L, D), BF16), jax.ShapeDtypeStruct((L, D), BF16),
                   jax.ShapeDtypeStruct((nc * 128, D), BF16)],
        scratch_shapes=[pltpu.VMEM((128, D), F32), pltpu.VMEM((q, D), F32), pltpu.VMEM((q, 128), F32),
                        pltpu.VMEM((128, q), F32), pltpu.VMEM((q, D), F32), pltpu.VMEM((q, D), F32)],
        compiler_params=_cparams(("arbitrary",)),
    )(xbc, pz, pdt, dtb, avec, aexp, dexp, nw, emat)


def _ssd_bwd(xbc, pz, pdt, ypre, sin, dya, dtb, avec, aexp, dexp, nw, emat, exchange=None):
    L = xbc.shape[0]
    q = SSD_Q
    nc = L // q

    def body(xbc_ref, z_ref, dtr_ref, ypre_ref, sin_ref, dya_ref, dtb_ref, avec_ref, aexp_ref, dexp_ref, nw_ref,
             e_ref, dpb_ref, dxbc_ref, dnw_ref, ddtb_ref, da_ref, ddx_ref,
             dst_ref, acx_ref, acol_ref, arow_ref, xdt_ref, dxdt_ref, dy_ref, dacx_ref):
        @pl.when(pl.program_id(0) == 0)
        def _():
            dst_ref[...] = jnp.zeros_like(dst_ref)
            dnw_ref[...] = jnp.zeros_like(dnw_ref)
            ddtb_ref[...] = jnp.zeros_like(ddtb_ref)
            da_ref[...] = jnp.zeros_like(da_ref)
            ddx_ref[...] = jnp.zeros_like(ddx_ref)

        dtpre, dt, dtx = _ssd_common(dtr_ref, dtb_ref, avec_ref, aexp_ref, e_ref, acx_ref, acol_ref, arow_ref)
        xs = xbc_ref[:, 0:D].astype(F32)
        xdt = xs * dtx
        xdt_ref[...] = xdt
        acx = acx_ref[...]
        alast = acx_ref[q - 1:q, :]
        dec_end = jnp.exp(alast - acx)
        xdtd = xdt * dec_end
        eac = jnp.exp(acx)
        ealast = jnp.exp(alast)
        for g in range(2):
            gs = slice(512 * g, 512 * g + 512)
            y = ypre_ref[:, gs].astype(F32)
            z = z_ref[:, gs].astype(F32)
            sz = _silu(z)
            dyz, dnw = _rms_bwd(dya_ref[:, gs].astype(F32), y * sz, nw_ref[:, gs])
            dnw_ref[:, gs] += dnw
            dy_ref[:, gs] = dyz * sz
            dpb_ref[:, gs] = (dyz * y * _dsilu(z)).astype(BF16)
        dy = dy_ref[...]
        ddx_ref[...] += jnp.sum(dy * xs, axis=0, keepdims=True)
        ri = _iota2((q, q), 0)
        ci = _iota2((q, q), 1)
        causal = ri >= ci
        causal_t = ri <= ci
        dacol = jnp.zeros((q, 128), F32)
        dacol_t = jnp.zeros((128, q), F32)
        last_row = _iota2((q, 512), 0) == q - 1
        for g in range(2):
            gs = slice(512 * g, 512 * g + 512)
            bm = xbc_ref[:, D + 128 * g:D + 128 * g + 128]
            cm = xbc_ref[:, D + 256 + 128 * g:D + 256 + 128 * g + 128]
            stg = sin_ref[:, gs].astype(F32)
            dstg = dst_ref[:, gs]
            dyg = dy[:, gs]
            yoff = _dot(cm, stg) * eac[:, gs]
            dwm = dyg * eac[:, gs]
            dcm = _dot_nt(dwm, stg)
            dstin = _dot_tn(cm, dwm)
            dacx_g = dyg * yoff
            dxdtd = _dot(bm, dstg)
            dbm = _dot_nt(xdtd[:, gs], dstg)
            t = dxdtd * xdtd[:, gs]
            dacx_g -= t
            dalast = jnp.sum(t, axis=0, keepdims=True) + jnp.sum(dstg * stg, axis=0, keepdims=True) * ealast[:, gs]
            dst_ref[:, gs] = dstin + dstg * ealast[:, gs]
            dacx_ref[:, gs] = dacx_g + jnp.where(last_row, dalast, 0.0)
            gmat = _dot_nt(cm, bm)
            gmat_t = _dot_nt(bm, cm)
            dg = jnp.zeros((q, q), F32)
            for e in range(8):
                h = 8 * g + e
                hs = slice(64 * h, 64 * h + 64)
                col = acol_ref[:, h:h + 1]
                row = arow_ref[h:h + 1, :]
                lm = jnp.exp(jnp.where(causal, col - row, -1e30))
                lm_t = jnp.exp(jnp.where(causal_t, row - col, -1e30))
                dyh = dy_ref[:, hs]
                dm = _dot_nt(dyh, xdt_ref[:, hs])
                dxdt_ref[:, hs] = _dot(gmat_t * lm_t, dyh)
                dml = dm * lm
                dg += dml
                p = dml * gmat
                dacol += jnp.where(ci == h, jnp.sum(p, axis=1, keepdims=True), 0.0)
                dacol_t -= jnp.where(ri == h, jnp.sum(p, axis=0, keepdims=True), 0.0)
            dcm += _dot(dg, bm)
            dbm += _dot_tn(dg, cm)
            dxbc_ref[:, D + 128 * g:D + 128 * g + 128] = dbm.astype(BF16)
            dxbc_ref[:, D + 256 + 128 * g:D + 256 + 128 * g + 128] = dcm.astype(BF16)
            dxdt_ref[:, gs] += dxdtd * dec_end[:, gs]
        dxdt = dxdt_ref[...]
        dacum = dacol + dacol_t.T + _dot_nt_hi(dacx_ref[...], e_ref[...])
        da = _dot_hi(_triu(q), dacum, 1)
        ddt = da * avec_ref[...] + _dot_nt_hi(dxdt * xs, e_ref[...])
        da_ref[...] += jnp.sum(da * dt, axis=0, keepdims=True) * avec_ref[...]
        dxbc_ref[:, 0:D] = (dexp_ref[...] * dy + dxdt * dtx).astype(BF16)
        ddtr = ddt * _sigmoid(dtpre)
        ddtb_ref[...] += jnp.sum(ddtr, axis=0, keepdims=True)
        dpb_ref[:, D:D + 128] = ddtr.astype(BF16)

    rev = lambda c: nc - 1 - c
    vec = lambda n: pl.BlockSpec((1, n), lambda c: (0, 0))
    payload = None if exchange is None else (_Exchange, exchange)
    args = [xbc, pz, pdt, ypre, sin, dya, dtb, avec, aexp, dexp, nw, emat]
    in_specs, out_specs, out_shape, scratch = _carried_specs(
        payload,
        [pl.BlockSpec((q, 1536), lambda c: (rev(c), 0)), pl.BlockSpec((q, D), lambda c: (rev(c), 0)),
         pl.BlockSpec((q, 128), lambda c: (rev(c), 0)), pl.BlockSpec((q, D), lambda c: (rev(c), 0)),
         pl.BlockSpec((128, D), lambda c: (rev(c), 0)), pl.BlockSpec((q, D), lambda c: (rev(c), 0)),
         vec(128), vec(128), vec(D), vec(D), vec(D), pl.BlockSpec((128, D), lambda c: (0, 0))],
        [pl.BlockSpec((q, WB), lambda c: (rev(c), 0)), pl.BlockSpec((q, 1536), lambda c: (rev(c), 0)),
         vec(D), vec(128), vec(128), vec(D)],
        [jax.ShapeDtypeStruct((L, WB), BF16), jax.ShapeDtypeStruct((L, 1536), BF16),
         jax.ShapeDtypeStruct((1, D), F32), jax.ShapeDtypeStruct((1, 128), F32),
         jax.ShapeDtypeStruct((1, 128), F32), jax.ShapeDtypeStruct((1, D), F32)],
        [pltpu.VMEM((128, D), F32), pltpu.VMEM((q, D), F32), pltpu.VMEM((q, 128), F32),
         pltpu.VMEM((128, q), F32), pltpu.VMEM((q, D), F32), pltpu.VMEM((q, D), F32),
         pltpu.VMEM((q, D), F32), pltpu.VMEM((q, D), F32)])
    return pl.pallas_call(
        _carried(body, len(args), 6, payload, _first_last(nc)), name="ssd_bwd", grid=(nc,),
        in_specs=in_specs, out_specs=out_specs, out_shape=out_shape, scratch_shapes=scratch,
        compiler_params=_cparams(("arbitrary",)),
    )(*(args if exchange is None else args + [exchange]))


def _hg_gates(hq, hf, hgl_ref, b_ref):
    lb = 1.0 / (1.0 + jnp.exp(hgl_ref[1:2, :] - hgl_ref[0:1, :]))
    qf = _silu(hq)
    sg = _sigmoid(hf)
    f = lb + (1.0 - lb) * sg
    b_ref[...] = _dot_hi(_tril(HG_STEP), jnp.log(f), 1)
    return lb, qf, sg, f


def _hg_factors(qf, kf, b_ref):
    s, n = HG_SUB, HG_STEP
    b = b_ref[...]
    blast = b_ref[n - 1:n, :]
    m0, mb, m1 = b_ref[s // 2 - 1:s // 2, :], b_ref[s - 1:s, :], b_ref[s + s // 2 - 1:s + s // 2, :]
    b0, b1 = b[0:s], b[s:n]
    q0, q1, k0, k1 = qf[0:s], qf[s:n], kf[0:s], kf[s:n]
    fac = dict(
        eb=jnp.exp(blast), eq=jnp.exp(b), ek=jnp.exp(blast - b),
        eq0=jnp.exp(b0 - m0), ek0=jnp.exp(m0 - b0), eq1=jnp.exp(b1 - m1), ek1=jnp.exp(m1 - b1),
        eqb=jnp.exp(b1 - mb), ekb=jnp.exp(mb - b0))
    rd = lambda t: t.astype(BF16).astype(F32)
    val = dict(qe=qf * fac["eq"], ke=kf * fac["ek"], qm0=rd(q0 * fac["eq0"]), km0=rd(k0 * fac["ek0"]),
               qm1=rd(q1 * fac["eq1"]), km1=rd(k1 * fac["ek1"]), qb=rd(q1 * fac["eqb"]), kb=rd(k0 * fac["ekb"]))
    return fac, val


def _hgrn_fwd(pa, phf, hgl, nwx):
    L = pa.shape[0]
    n, s = HG_STEP, HG_SUB
    nc = L // n

    def body(hq_ref, hf_ref, hi_ref, hg_ref, hgl_ref, nw_ref, ob_ref, opre_ref, sin_ref, st_ref, b_ref):
        @pl.when(pl.program_id(0) == 0)
        def _():
            st_ref[...] = jnp.zeros_like(st_ref)

        sin_ref[...] = st_ref[...].astype(BF16)
        _, qf, _, f = _hg_gates(hq_ref[...].astype(F32), hf_ref[...], hgl_ref, b_ref)
        fac, val = _hg_factors(qf, 1.0 - f, b_ref)
        causal = _iota2((s, s), 0) >= _iota2((s, s), 1)
        for h in range(HG_HEADS):
            hs = slice(128 * h, 128 * h + 128)
            sth = st_ref[:, hs]
            v = hi_ref[:, hs]
            v0, v1 = v[0:s], v[s:n]
            a00 = jnp.where(causal, _dot_nt(val["qm0"][:, hs], val["km0"][:, hs]), 0.0)
            a11 = jnp.where(causal, _dot_nt(val["qm1"][:, hs], val["km1"][:, hs]), 0.0)
            a10 = _dot_nt(val["qb"][:, hs], val["kb"][:, hs])
            o = _dot_nt(val["qe"][:, hs], sth) + jnp.concatenate(
                [_dot(a00, v0), _dot(a10, v0) + _dot(a11, v1)], axis=0)
            st_ref[:, hs] = sth * fac["eb"][:, hs] + _dot_tn(v, val["ke"][:, hs])
            opre_ref[:, hs] = o.astype(BF16)
            ob_ref[:, hs] = (_rms_fwd(o, nw_ref[:, hs]) * _silu(hg_ref[:, hs].astype(F32))).astype(BF16)

    blk = lambda j: pl.BlockSpec((n, D), lambda c: (c, j))
    return pl.pallas_call(
        body, name="hgrn_fwd", grid=(nc,),
        in_specs=[blk(0), blk(0), blk(1), blk(2), pl.BlockSpec((2, D), lambda c: (0, 0)),
                  pl.BlockSpec((1, D), lambda c: (0, 0))],
        out_specs=[blk(0), blk(0), blk(0)],
        out_shape=[jax.ShapeDtypeStruct((L, D), BF16), jax.ShapeDtypeStruct((L, D), BF16),
                   jax.ShapeDtypeStruct((nc * 128, D), BF16)],
        scratch_shapes=[pltpu.VMEM((128, D), F32), pltpu.VMEM((n, D), F32)],
        compiler_params=_cparams(("arbitrary",)),
    )(pa, phf, pa, pa, hgl, nwx)


def _hgrn_bwd(pa, phf, opre, sin, dob, hgl, nwx, exchange=None):
    L = pa.shape[0]
    n, s = HG_STEP, HG_SUB
    nc = L // n

    def body(hq_ref, hf_ref, hi_ref, hg_ref, opre_ref, sin_ref, dob_ref, hgl_ref, nw_ref,
             dpa_ref, dhgl_ref, dnw_ref, dst_ref, b_ref, dlb_ref, dq_ref, dk_ref, db_ref):
        i = pl.program_id(0)

        @pl.when(i == 0)
        def _():
            dst_ref[...] = jnp.zeros_like(dst_ref)
            dlb_ref[...] = jnp.zeros_like(dlb_ref)
            dnw_ref[...] = jnp.zeros_like(dnw_ref)

        hq = hq_ref[...].astype(F32)
        lb, qf, sg, f = _hg_gates(hq, hf_ref[...], hgl_ref, b_ref)
        kf = 1.0 - f
        fac, val = _hg_factors(qf, kf, b_ref)
        ri, ci = _iota2((s, s), 0), _iota2((s, s), 1)
        causal, causal_t = ri >= ci, ri <= ci
        last_row = _iota2((n, 128), 0) == n - 1
        for h in range(HG_HEADS):
            hs = slice(128 * h, 128 * h + 128)
            o = opre_ref[:, hs].astype(F32)
            gate = hg_ref[:, hs].astype(F32)
            dout = dob_ref[:, hs].astype(F32)
            sgate = _silu(gate)
            do, dnw = _rms_bwd(dout * sgate, o, nw_ref[:, hs])
            dnw_ref[:, hs] += dnw
            dpa_ref[:, 2 * D + 128 * h:2 * D + 128 * h + 128] = (
                dout * _rms_fwd(o, nw_ref[:, hs]) * _dsilu(gate)).astype(BF16)
            sth = sin_ref[:, hs].astype(F32)
            dsth = dst_ref[:, hs]
            v = hi_ref[:, hs]
            v0, v1 = v[0:s], v[s:n]
            do0, do1 = do[0:s], do[s:n]
            qe, ke = val["qe"][:, hs], val["ke"][:, hs]
            qm0, km0, qm1, km1 = val["qm0"][:, hs], val["km0"][:, hs], val["qm1"][:, hs], val["km1"][:, hs]
            qb, kb = val["qb"][:, hs], val["kb"][:, hs]
            dqe = _dot(do, sth)
            dstin = _dot_tn(do, qe)
            a00t = jnp.where(causal_t, _dot_nt(km0, qm0), 0.0)
            a11t = jnp.where(causal_t, _dot_nt(km1, qm1), 0.0)
            a10t = _dot_nt(kb, qb)
            dat00 = jnp.where(causal, _dot_nt(do0, v0), 0.0)
            dat11 = jnp.where(causal, _dot_nt(do1, v1), 0.0)
            dat10 = _dot_nt(do1, v0)
            dat00t = jnp.where(causal_t, _dot_nt(v0, do0), 0.0)
            dat11t = jnp.where(causal_t, _dot_nt(v1, do1), 0.0)
            dat10t = _dot_nt(v0, do1)
            dv = jnp.concatenate([_dot(a00t, do0) + _dot(a10t, do1), _dot(a11t, do1)], axis=0)
            dqm0, dkm0 = _dot(dat00, km0), _dot(dat00t, qm0)
            dqm1, dkm1 = _dot(dat11, km1), _dot(dat11t, qm1)
            dqb, dkb = _dot(dat10, kb), _dot(dat10t, qb)
            dke = _dot(v, dsth)
            dv += _dot_nt(ke, dsth)
            deb = jnp.sum(dsth * sth, axis=0, keepdims=True)
            dst_ref[:, hs] = dstin + dsth * fac["eb"][:, hs]
            dq = dqe * fac["eq"][:, hs] + jnp.concatenate(
                [dqm0 * fac["eq0"][:, hs], dqm1 * fac["eq1"][:, hs] + dqb * fac["eqb"][:, hs]], axis=0)
            dk = dke * fac["ek"][:, hs] + jnp.concatenate(
                [dkm0 * fac["ek0"][:, hs] + dkb * fac["ekb"][:, hs], dkm1 * fac["ek1"][:, hs]], axis=0)
            tke = dke * ke
            db = dqe * qe - tke + jnp.concatenate(
                [dqm0 * qm0 - dkm0 * km0 - dkb * kb, dqm1 * qm1 - dkm1 * km1 + dqb * qb], axis=0)
            dblast = jnp.sum(tke, axis=0, keepdims=True) + deb * fac["eb"][:, hs]
            db_ref[:, hs] = db + jnp.where(last_row, dblast, 0.0)
            dq_ref[:, hs] = dq
            dk_ref[:, hs] = dk
            dpa_ref[:, D + 128 * h:D + 128 * h + 128] = dv.astype(BF16)
        dg = _dot_hi(_triu(n), db_ref[...], 1)
        df = dg / f - dk_ref[...]
        dpa_ref[:, 3 * D:4 * D] = (df * (1.0 - lb) * sg * (1.0 - sg)).astype(BF16)
        dpa_ref[:, 0:D] = (dq_ref[...] * _dsilu(hq)).astype(BF16)
        dlb_ref[...] += jnp.sum(df * (1.0 - sg), axis=0, keepdims=True)

        @pl.when(i == nc - 1)
        def _():
            d0 = dlb_ref[...] * lb * (1.0 - lb)
            dhgl_ref[...] = jnp.concatenate([d0, -d0], axis=0)

    rev = lambda c: nc - 1 - c
    blk = lambda j: pl.BlockSpec((n, D), lambda c: (rev(c), j))
    payload = None if exchange is None else (_Exchange, exchange)
    args = [pa, phf, pa, pa, opre, sin, dob, hgl, nwx]
    in_specs, out_specs, out_shape, scratch = _carried_specs(
        payload,
        [blk(0), blk(0), blk(1), blk(2), blk(0), blk(0), blk(0), pl.BlockSpec((2, D), lambda c: (0, 0)),
         pl.BlockSpec((1, D), lambda c: (0, 0))],
        [pl.BlockSpec((n, 4 * D), lambda c: (rev(c), 0)), pl.BlockSpec((2, D), lambda c: (0, 0)),
         pl.BlockSpec((1, D), lambda c: (0, 0))],
        [jax.ShapeDtypeStruct((L, 4 * D), BF16), jax.ShapeDtypeStruct((2, D), F32), jax.ShapeDtypeStruct((1, D), F32)],
        [pltpu.VMEM((128, D), F32), pltpu.VMEM((n, D), F32), pltpu.VMEM((1, D), F32),
         pltpu.VMEM((n, D), F32), pltpu.VMEM((n, D), F32), pltpu.VMEM((n, D), F32)])
    return pl.pallas_call(
        _carried(body, len(args), 3, payload, _first_last(nc)), name="hgrn_bwd", grid=(nc,),
        in_specs=in_specs, out_specs=out_specs, out_shape=out_shape, scratch_shapes=scratch,
        compiler_params=_cparams(("arbitrary",)),
    )(*(args if exchange is None else args + [exchange]))


XA_SCALE = XA_DH ** -0.5


def _xa_probs(qh, kmh):
    sc = _dot_nt(qh, kmh) * XA_SCALE
    p = jnp.exp(sc - jnp.max(sc, axis=1, keepdims=True))
    return p / jnp.sum(p, axis=1, keepdims=True)


def _xattn_fwd(x1, nw, wq, kv, wo, *, tm):
    L = x1.shape[0]
    tm = min(tm, L)

    def body(x_ref, nw_ref, wq_ref, kv_ref, wo_ref, o_ref, ox_ref):
        x = x_ref[...]
        q = _dot(_rms_fwd(x, nw_ref[...]), wq_ref[...])
        for h in range(XA_HEADS):
            hs = slice(XA_DH * h, XA_DH * h + XA_DH)
            p = _xa_probs(q[:, hs], kv_ref[:, hs])
            ox_ref[:, hs] = _dot(p, kv_ref[:, D + XA_DH * h:D + XA_DH * h + XA_DH])
        o_ref[...] = x + _dot(ox_ref[...], wo_ref[...])

    full = lambda a: pl.BlockSpec(a.shape, lambda i: (0, 0))
    return pl.pallas_call(
        body, name="xattn_fwd", grid=(L // tm,),
        in_specs=[pl.BlockSpec((tm, D), lambda i: (i, 0)), full(nw), full(wq), full(kv), full(wo)],
        out_specs=pl.BlockSpec((tm, D), lambda i: (i, 0)),
        out_shape=jax.ShapeDtypeStruct((L, D), F32),
        scratch_shapes=[pltpu.VMEM((tm, D), F32)],
        compiler_params=_cparams(("parallel",)),
    )(x1, nw, wq, kv, wo)


def _xattn_bwd(x1, dx2, nw, wq, kv, wo, *, tm):
    L = x1.shape[0]
    tm = min(tm, L)

    def body(x_ref, dx2_ref, nw_ref, wq_ref, kv_ref, wo_ref, dx1_ref, dx1b_ref, h_ref, dq_ref, ox_ref, dkv_ref,
             dnw_ref, dqs_ref):
        @pl.when(pl.program_id(0) == 0)
        def _():
            dkv_ref[...] = jnp.zeros_like(dkv_ref)
            dnw_ref[...] = jnp.zeros_like(dnw_ref)

        x = x_ref[...]
        dx2 = dx2_ref[...]
        hn = _rms_fwd(x, nw_ref[...]).astype(BF16)
        h_ref[...] = hn
        q = _dot(hn, wq_ref[...])
        dox = _dot_nt(dx2, wo_ref[...])
        for h in range(XA_HEADS):
            hs = slice(XA_DH * h, XA_DH * h + XA_DH)
            vs = slice(D + XA_DH * h, D + XA_DH * h + XA_DH)
            qh, kmh, vmh, doxh = q[:, hs], kv_ref[:, hs], kv_ref[:, vs], dox[:, hs]
            p = _xa_probs(qh, kmh)
            ox_ref[:, hs] = _dot(p, vmh).astype(BF16)
            dp = _dot_nt(doxh, vmh)
            dkv_ref[:, vs] += _dot_tn(p, doxh)
            ds = p * (dp - jnp.sum(dp * p, axis=1, keepdims=True)) * XA_SCALE
            dqs_ref[:, hs] = _dot(ds, kmh)
            dkv_ref[:, hs] += _dot_tn(ds, qh)
        dq = dqs_ref[...]
        dq_ref[...] = dq.astype(BF16)
        dx, dnw = _rms_bwd(_dot_nt(dq, wq_ref[...]), x, nw_ref[...])
        dx1 = dx2 + dx
        dx1_ref[...] = dx1
        dx1b_ref[...] = dx1.astype(BF16)
        dnw_ref[...] += dnw

    full = lambda a: pl.BlockSpec(a.shape, lambda i: (0, 0))
    row = pl.BlockSpec((tm, D), lambda i: (i, 0))
    return pl.pallas_call(
        body, name="xattn_bwd", grid=(L // tm,),
        in_specs=[row, row, full(nw), full(wq), full(kv), full(wo)],
        out_specs=[row, row, row, row, row, pl.BlockSpec((MEM_LEN, 2 * D), lambda i: (0, 0)),
                   pl.BlockSpec((1, D), lambda i: (0, 0))],
        out_shape=[jax.ShapeDtypeStruct((L, D), F32), jax.ShapeDtypeStruct((L, D), BF16),
                   jax.ShapeDtypeStruct((L, D), BF16), jax.ShapeDtypeStruct((L, D), BF16),
                   jax.ShapeDtypeStruct((L, D), BF16),
                   jax.ShapeDtypeStruct((MEM_LEN, 2 * D), F32), jax.ShapeDtypeStruct((1, D), F32)],
        scratch_shapes=[pltpu.VMEM((tm, D), F32)],
        compiler_params=_cparams(("arbitrary",)),
    )(x1, dx2, nw, wq, kv, wo)


FFN_TF = 256


def _ffn_fwd(x2, nw, wg, wu, wd, *, tm):
    L = x2.shape[0]
    tm = min(tm, L)
    tf = FFN_TF
    nf = FFN // tf

    def body(x_ref, nw_ref, wg_ref, wu_ref, wd_ref, o_ref, hn_ref, g_ref, u_ref, h_ref, acc_ref):
        j = pl.program_id(1)

        @pl.when(j == 0)
        def _():
            hn = _rms_fwd(x_ref[...], nw_ref[...]).astype(BF16)
            h_ref[...] = hn
            hn_ref[...] = hn
            acc_ref[...] = jnp.zeros_like(acc_ref)

        h = h_ref[...]
        g = _dot(h, wg_ref[...]).astype(BF16)
        u = _dot(h, wu_ref[...]).astype(BF16)
        g_ref[...] = g
        u_ref[...] = u
        acc_ref[...] += _dot(_silu(g.astype(F32)) * u.astype(F32), wd_ref[...])

        @pl.when(j == nf - 1)
        def _():
            o_ref[...] = x_ref[...] + acc_ref[...]

    row = pl.BlockSpec((tm, D), lambda i, j: (i, 0))
    colblk = pl.BlockSpec((tm, tf), lambda i, j: (i, j))
    return pl.pallas_call(
        body, name="ffn_fwd", grid=(L // tm, nf),
        in_specs=[row, pl.BlockSpec((1, D), lambda i, j: (0, 0)),
                  pl.BlockSpec((D, tf), lambda i, j: (0, j)), pl.BlockSpec((D, tf), lambda i, j: (0, j)),
                  pl.BlockSpec((tf, D), lambda i, j: (j, 0))],
        out_specs=[row, row, colblk, colblk],
        out_shape=[jax.ShapeDtypeStruct((L, D), F32), jax.ShapeDtypeStruct((L, D), BF16),
                   jax.ShapeDtypeStruct((L, FFN), BF16), jax.ShapeDtypeStruct((L, FFN), BF16)],
        scratch_shapes=[pltpu.VMEM((tm, D), BF16), pltpu.VMEM((tm, D), F32)],
        compiler_params=_cparams(("parallel", "arbitrary")),
    )(x2, nw, wg, wu, wd)


def _ffn_bwd(x2, dx3, g, u, nw, wg, wu, wd, *, tm):
    L = x2.shape[0]
    tm = min(tm, L)
    tf = FFN_TF
    nf = FFN // tf

    def body(x_ref, dx3_ref, g_ref, u_ref, nw_ref, wg_ref, wu_ref, wd_ref,
             dx2_ref, a_ref, dg_ref, du_ref, dnw_ref, d3_ref, acc_ref):
        i, j = pl.program_id(0), pl.program_id(1)

        @pl.when(j == 0)
        def _():
            d3_ref[...] = dx3_ref[...].astype(BF16)
            acc_ref[...] = jnp.zeros_like(acc_ref)

        g = g_ref[...].astype(F32)
        u = u_ref[...].astype(F32)
        s = _sigmoid(g)
        sg = g * s
        a_ref[...] = (sg * u).astype(BF16)
        da = _dot_nt(d3_ref[...], wd_ref[...])
        dg = (da * u * (s + sg * (1.0 - s))).astype(BF16)
        du = (da * sg).astype(BF16)
        dg_ref[...] = dg
        du_ref[...] = du
        acc_ref[...] += _dot_nt(dg, wg_ref[...]) + _dot_nt(du, wu_ref[...])

        @pl.when(jnp.logical_and(i == 0, j == 0))
        def _():
            dnw_ref[...] = jnp.zeros_like(dnw_ref)

        @pl.when(j == nf - 1)
        def _():
            dx, dnw = _rms_bwd(acc_ref[...], x_ref[...], nw_ref[...])
            dx2_ref[...] = dx3_ref[...] + dx
            dnw_ref[...] += dnw

    row = pl.BlockSpec((tm, D), lambda i, j: (i, 0))
    colblk = pl.BlockSpec((tm, tf), lambda i, j: (i, j))
    return pl.pallas_call(
        body, name="ffn_bwd", grid=(L // tm, nf),
        in_specs=[row, row, colblk, colblk, pl.BlockSpec((1, D), lambda i, j: (0, 0)),
                  pl.BlockSpec((D, tf), lambda i, j: (0, j)), pl.BlockSpec((D, tf), lambda i, j: (0, j)),
                  pl.BlockSpec((tf, D), lambda i, j: (j, 0))],
        out_specs=[row, colblk, colblk, colblk, pl.BlockSpec((1, D), lambda i, j: (0, 0))],
        out_shape=[jax.ShapeDtypeStruct((L, D), F32), jax.ShapeDtypeStruct((L, FFN), BF16),
                   jax.ShapeDtypeStruct((L, FFN), BF16), jax.ShapeDtypeStruct((L, FFN), BF16),
                   jax.ShapeDtypeStruct((1, D), F32)],
        scratch_shapes=[pltpu.VMEM((tm, D), BF16), pltpu.VMEM((tm, D), F32)],
        compiler_params=_cparams(("arbitrary", "arbitrary"), vmem_mb=56),
    )(x2, dx3, g, u, nw, wg, wu, wd)


def _final(x3, tgt, nw, *, tm):
    L = x3.shape[0]
    tm = min(tm, L)

    def body(x_ref, t_ref, nw_ref, dx_ref, dxb_ref, loss_ref, dnw_ref):
        @pl.when(pl.program_id(0) == 0)
        def _():
            loss_ref[...] = jnp.zeros_like(loss_ref)
            dnw_ref[...] = jnp.zeros_like(dnw_ref)

        x = x_ref[...]
        w = nw_ref[...]
        err = _rms_fwd(x, w) - t_ref[...]
        part = 0.5 * jnp.sum(jnp.sum(err * err, axis=1, keepdims=True), axis=0, keepdims=True) * (1.0 / D)
        loss_ref[...] += jnp.where(_iota2((1, 128), 1) == 0, part, 0.0)
        dx, dnw = _rms_bwd(err * (1.0 / D), x, w)
        dx_ref[...] = dx
        dxb_ref[...] = dx.astype(BF16)
        dnw_ref[...] += dnw

    row = pl.BlockSpec((tm, D), lambda i: (i, 0))
    return pl.pallas_call(
        body, name="final_loss", grid=(L // tm,),
        in_specs=[row, row, pl.BlockSpec((1, D), lambda i: (0, 0))],
        out_specs=[row, row, pl.BlockSpec((1, 128), lambda i: (0, 0)), pl.BlockSpec((1, D), lambda i: (0, 0))],
        out_shape=[jax.ShapeDtypeStruct((L, D), F32), jax.ShapeDtypeStruct((L, D), BF16),
                   jax.ShapeDtypeStruct((1, 128), F32), jax.ShapeDtypeStruct((1, D), F32)],
        compiler_params=_cparams(("arbitrary",)),
    )(x3, tgt, nw)


def _adam_update(g, w, m, v):
    c1 = 1.0 / (1.0 - ADAM_B1 ** ADAM_STEP)
    c2 = 1.0 / (1.0 - ADAM_B2 ** ADAM_STEP)
    nm = ADAM_B1 * m + (1.0 - ADAM_B1) * g
    nv = ADAM_B2 * v + (1.0 - ADAM_B2) * (g * g)
    return -ADAM_LR * ((nm * c1) / (jnp.sqrt(nv * c2) + ADAM_EPS) + ADAM_WD * w), nm, nv


def _adamw_small(tot, conv_w_grad, w, m, v):
    names = [n for n, _, _ in SMALL if n != "loss"]
    k = len(names)

    def body(tot_ref, cwg_ref, *refs):
        w_refs, m_refs, v_refs = refs[0:k], refs[k:2 * k], refs[2 * k:3 * k]
        g_refs, d_refs, nm_refs, nv_refs = (refs[(3 + j) * k:(4 + j) * k] for j in range(4))
        for i, n in enumerate(names):
            row, nr, nc = SMALL_AT[n]
            g = cwg_ref[...] if n == "conv_w" else tot_ref[row:row + nr, 0:nc]
            d, nm, nv = _adam_update(g, w_refs[i][...], m_refs[i][...], v_refs[i][...])
            g_refs[i][...] = g
            d_refs[i][...] = d
            nm_refs[i][...] = nm
            nv_refs[i][...] = nv

    sds = [jax.ShapeDtypeStruct(t.shape, F32) for t in w]
    res = pl.pallas_call(body, name="adamw_small", out_shape=sds * 4)(tot, conv_w_grad, *w, *m, *v)
    return res[0:k], res[k:2 * k], res[2 * k:3 * k], res[3 * k:4 * k]


def _adamw(parts, w, m, v, *, tr, name):
    n_parts, R, C = parts.shape
    tr = min(tr, R)

    def body(p_ref, w_ref, m_ref, v_ref, g_ref, d_ref, nm_ref, nv_ref):
        g = p_ref[0]
        for k in range(1, n_parts):
            g = g + p_ref[k]
        d, nm, nv = _adam_update(g, w_ref[...], m_ref[...], v_ref[...])
        g_ref[...] = g
        nm_ref[...] = nm
        nv_ref[...] = nv
        d_ref[...] = d

    blk = pl.BlockSpec((tr, C), lambda i: (i, 0))
    sds = jax.ShapeDtypeStruct((R, C), F32)
    return pl.pallas_call(
        body, name=name, grid=(R // tr,),
        in_specs=[pl.BlockSpec((n_parts, tr, C), lambda i: (0, i, 0)), blk, blk, blk],
        out_specs=[blk, blk, blk, blk], out_shape=[sds, sds, sds, sds],
        compiler_params=_cparams(("parallel",)),
    )(parts, w, m, v)


def _position():
    return lax.axis_index("x"), lax.axis_index("y"), lax.axis_index("c")


def _comm_scratch():
    return [pltpu.SemaphoreType.DMA((7,)), pltpu.SemaphoreType.DMA((7,)), pltpu.SemaphoreType.DMA]


class _Gather:
    def __init__(self, x_ref, out_ref, send_sems, recv_sems, local_sem):
        x, y, c = _position()
        me, sibling = (x, y, c), (x, y, 1 - c)
        chips = [(1 - x, y), (x, 1 - y), (1 - x, 1 - y)]

        def rows(px, py, pc):
            return out_ref.at[4 * px + 2 * py + pc]

        def copy(k, block, to, src=None):
            return pltpu.make_async_remote_copy(
                src_ref=rows(*block) if src is None else src, dst_ref=rows(*block),
                send_sem=send_sems.at[k], recv_sem=recv_sems.at[k], device_id=to, device_id_type=MESH)

        self.mine = pltpu.make_async_copy(x_ref, rows(*me), local_sem)
        self.first = [copy(0, me, sibling, src=x_ref)]
        self.first += [copy(1 + j, me, (*chip, c), src=x_ref) for j, chip in enumerate(chips)]
        self.passed = [copy(4 + j, (*chip, c), sibling) for j, chip in enumerate(chips)]
        self.from_chips = [copy(1 + j, (*chip, c), me) for j, chip in enumerate(chips)]
        self.from_sibling = [copy(0, sibling, me)] + [copy(4 + j, (*chip, 1 - c), me) for j, chip in enumerate(chips)]

    def start(self):
        self.mine.start()
        for cp in self.first:
            cp.start()

    def forward(self):
        for got, cp in zip(self.from_chips, self.passed):
            got.wait_recv()
            cp.start()

    def finish(self):
        for got in self.from_sibling:
            got.wait_recv()
        for cp in self.first + self.passed:
            cp.wait_send()
        self.mine.wait()


class _Exchange:
    def __init__(self, g_ref, out_ref, send_sems, recv_sems, local_sem):
        x, y, c = _position()
        me = 4 * x + 2 * y + c
        self.mine = pltpu.make_async_copy(g_ref.at[me], out_ref.at[me], local_sem)
        self.copies = []
        for k in range(1, N_DEV):
            px = 1 - x if k & 4 else x
            py = 1 - y if k & 2 else y
            pc = 1 - c if k & 1 else c
            self.copies.append(pltpu.make_async_remote_copy(
                src_ref=g_ref.at[4 * px + 2 * py + pc], dst_ref=out_ref.at[me],
                send_sem=send_sems.at[k - 1], recv_sem=recv_sems.at[k - 1],
                device_id=(px, py, pc), device_id_type=MESH))

    def start(self):
        self.mine.start()
        for cp in self.copies:
            cp.start()

    def finish(self):
        for cp in self.copies:
            cp.wait()
        self.mine.wait()


def _allgather(xp):
    R, C = xp.shape

    def body(x_ref, out_ref, send_sems, recv_sems, local_sem):
        g = _Gather(x_ref, out_ref, send_sems, recv_sems, local_sem)
        g.start()
        g.forward()
        g.finish()

    return pl.pallas_call(
        body, name="allgather_w_in",
        out_shape=jax.ShapeDtypeStruct((N_DEV, R, C), xp.dtype),
        in_specs=[pl.BlockSpec(memory_space=pltpu.HBM)], out_specs=pl.BlockSpec(memory_space=pltpu.HBM),
        scratch_shapes=_comm_scratch(),
    )(xp)


def _carried(body, n_in, n_out, payload, phases):
    if payload is None:
        return body
    kind = payload[0]

    def new_body(*refs):
        ins, src_ref = refs[:n_in], refs[n_in]
        outs, dst_ref = refs[n_in + 1:n_in + 1 + n_out], refs[n_in + 1 + n_out]
        scratch, sems = refs[n_in + 2 + n_out:-3], refs[-3:]

        def run(before):
            for when, action, is_before in phases:
                if is_before == before:
                    @pl.when(when())
                    def _():
                        action(kind(src_ref, dst_ref, *sems))

        run(True)
        body(*ins, *outs, *scratch)
        run(False)

    return new_body


def _carried_specs(payload, in_specs, out_specs, out_shape, scratch):
    if payload is None:
        return in_specs, out_specs, out_shape, scratch
    kind, arr = payload
    landing = (N_DEV,) + arr.shape if kind is _Gather else arr.shape
    hbm = pl.BlockSpec(memory_space=pltpu.HBM)
    return (in_specs + [hbm], out_specs + [hbm], out_shape + [jax.ShapeDtypeStruct(landing, arr.dtype)],
            scratch + _comm_scratch())


def _small_allreduce(sp):
    R, C = sp.shape

    def body(s_ref, out_ref, buf_ref, send_sems, recv_sems):
        x, y, c = _position()
        me = 4 * x + 2 * y + c
        buf_ref[me] = s_ref[...]
        copies = []
        for k in range(1, N_DEV):
            px = 1 - x if k & 4 else x
            py = 1 - y if k & 2 else y
            pc = 1 - c if k & 1 else c
            cp = pltpu.make_async_remote_copy(
                src_ref=s_ref, dst_ref=buf_ref.at[me], send_sem=send_sems.at[k - 1], recv_sem=recv_sems.at[k - 1],
                device_id=(px, py, pc), device_id_type=MESH)
            cp.start()
            copies.append(cp)
        for cp in copies:
            cp.wait()
        tot = buf_ref[0]
        for k in range(1, N_DEV):
            tot = tot + buf_ref[k]
        out_ref[...] = tot

    return pl.pallas_call(
        body, name="allreduce_small",
        out_shape=jax.ShapeDtypeStruct((R, C), F32),
        in_specs=[pl.BlockSpec(memory_space=pltpu.VMEM)], out_specs=pl.BlockSpec(memory_space=pltpu.VMEM),
        scratch_shapes=[pltpu.VMEM((N_DEV, R, C), F32), pltpu.SemaphoreType.DMA((7,)), pltpu.SemaphoreType.DMA((7,))],
    )(sp)


def _col_shards(t, rows):
    n, cols = t.shape[0], t.shape[1] * t.shape[2] // rows
    return t.reshape(n, rows, cols).transpose(1, 0, 2).reshape(rows, n * cols)


def _to_col_shards(w, n):
    rows, tot = w.shape
    cols = tot // n
    return w.reshape(rows, n, cols).transpose(1, 0, 2).reshape(n, rows * cols // D, D)


def _local_step(x, mem, tgt, wt, small, dist=None):
    w_in = wt["w_in"]
    zpad = jnp.zeros((D, WB - D - SSD_HEADS), BF16)
    w_a = jnp.concatenate([w_in[:, 2576:3600], w_in[:, 4624:6672], w_in[:, 3600:4624]], axis=1)
    w_b = jnp.concatenate([w_in[:, 0:D], w_in[:, 2560:2576], zpad], axis=1)
    w_c = w_in[:, D:2560]
    a_log, d_skip = small["a_log"], small["d_skip"]
    avec = jnp.pad(-jnp.exp(a_log), ((0, 0), (0, 128 - SSD_HEADS)))
    aexp = jnp.repeat(-jnp.exp(a_log), SSD_P, axis=1)
    dexp = jnp.repeat(d_skip, SSD_P, axis=1)
    dtb = jnp.pad(small["dt_bias"], ((0, 0), (0, 128 - SSD_HEADS)))
    emat = (lax.broadcasted_iota(jnp.int32, (128, D), 0) == lax.broadcasted_iota(jnp.int32, (128, D), 1) // SSD_P
            ).astype(F32)
    hg_nwx = jnp.tile(small["hg_norm_w"], (1, HG_HEADS))
    hgl = small["hg_lower_bounds"]
    nfw = small["norm_final_w"].reshape(1, D)

    received = {}
    pieces = (lambda group, grads: None) if dist is None else dist["pieces"]
    pa, phf, pz, pdt, pc, hn_mix, *got = _inproj(x, small["norm_mix_w"], w_a, w_b, w_c, tm=256,
                                                 gather=None if dist is None else dist["rest_pack"])
    if got:
        wt = {**wt, **dist["unpack_rest"](got[0])}
    xbc = _conv_fwd(pc, small["conv_w"], small["conv_b"], tm=512)
    ya, ypre, ssd_sin = _ssd_fwd(xbc, pz, pdt, dtb, avec, aexp, dexp, small["ssd_norm_w"], emat)
    ob, opre, hg_sin = _hgrn_fwd(pa, phf, hgl, hg_nwx)
    x1 = _mm2_res(x, ya, ob, wt["w_out"], tm=512, name="outproj")
    kvb, mn = _norm_mm(mem, small["norm_mem_w"], wt["xa_wkv"], tm=256, tn=1024, name="mem_kv", emit_h=True,
                       out_dtype=BF16)
    x2 = _xattn_fwd(x1, small["norm_xa_w"], wt["xa_wq"], kvb, wt["xa_wo"], tm=512)
    x3, hn_ffn, gate, up = _ffn_fwd(x2, small["norm_ffn_w"], wt["ffn_w_gate"], wt["ffn_w_up"], wt["ffn_w_down"],
                                    tm=1024)

    dx3, dx3b, loss, g_nf = _final(x3, tgt, nfw, tm=512)
    dx2, act, dg, du, g_nffn = _ffn_bwd(x2, dx3, gate, up, small["norm_ffn_w"], wt["ffn_w_gate"], wt["ffn_w_up"],
                                        wt["ffn_w_down"], tm=1024)
    g_wg = _dw(hn_ffn, dg, tM=1024, tN=1408, tl=512, name="dw_gate")
    g_wu = _dw(hn_ffn, du, tM=1024, tN=1408, tl=512, name="dw_up")
    g_wd = _dw(act, dx3b, tM=1408, tN=1024, tl=512, name="dw_down")
    dx1, dx1b, hn_xa, dq, ox, dkv, g_nxa = _xattn_bwd(x1, dx2, small["norm_xa_w"], wt["xa_wq"], kvb, wt["xa_wo"],
                                                      tm=512)
    dkvb = dkv.astype(BF16)
    g_wq = _dw(hn_xa, dq, tM=1024, tN=1024, tl=512, name="dw_q")
    g_wo = _dw(ox, dx2, tM=1024, tN=1024, tl=512, name="dw_o")
    g_wkv = _dw(mn, dkvb, tM=1024, tN=1024, tl=256, name="dw_kv")
    _, g_nmem = _mmnt_normbwd([dkvb], [wt["xa_wkv"]], mem, small["norm_mem_w"], None, tm=256, name="mem_bwd")
    dya, dob = _mm_nt2(dx1, wt["w_out"], tm=512, name="outproj_bwd")
    g_wout = jnp.concatenate([_dw(ya, dx1b, tM=1024, tN=1024, tl=512, name="dw_out_a"),
                              _dw(ob, dx1b, tM=1024, tN=1024, tl=512, name="dw_out_b")], axis=0)
    ffn_grads = {"ffn_w_gate": g_wg, "ffn_w_up": g_wu, "ffn_w_down": g_wd}
    mid_grads = {"w_out": g_wout, "xa_wq": g_wq, "xa_wkv": g_wkv, "xa_wo": g_wo}
    dpa, g_hgl, g_hgn_x, *got = _hgrn_bwd(pa, phf, opre, hg_sin, dob, hgl, hg_nwx,
                                          exchange=pieces("ffn", ffn_grads))
    received["ffn"] = got[0] if got else None
    dpb, dxbc, g_ssdn, g_dtb, g_alog, g_dx, *got = _ssd_bwd(
        xbc, pz, pdt, ypre, ssd_sin, dya, dtb, avec, aexp, dexp, small["ssd_norm_w"], emat,
        exchange=pieces("mid", mid_grads))
    received["mid"] = got[0] if got else None
    dpc, g_cw, g_cb = _conv_bwd(pc, dxbc, small["conv_w"], small["conv_b"], tm=512)
    g_wa = _dw(hn_mix, dpa, tM=1024, tN=1024, tl=512, name="dw_in_a")
    g_wb = _dw(hn_mix, dpb, tM=1024, tN=384, tl=512, name="dw_in_b")
    g_wc = _dw(hn_mix, dpc, tM=1024, tN=512, tl=512, name="dw_in_c")
    g_win = jnp.concatenate([g_wb[:, 0:D], g_wc, g_wb[:, D:D + SSD_HEADS], g_wa[:, 0:D], g_wa[:, 3 * D:4 * D],
                             g_wa[:, D:3 * D]], axis=1)
    grad_x, g_nmix, *got = _mmnt_normbwd([dpa, dpb, dpc], [w_a, w_b, w_c], x, small["norm_mix_w"], dx1, tm=256,
                                         name="inproj_bwd", exchange=pieces("in", {"w_in": g_win}))
    received["in"] = got[0] if got else None

    big = {"w_in": g_win, **mid_grads, **ffn_grads}
    smallg = {
        "norm_mix_w": g_nmix, "conv_w": g_cw, "conv_b": g_cb, "dt_bias": g_dtb[:, 0:SSD_HEADS],
        "a_log": g_alog[:, 0:SSD_HEADS], "d_skip": g_dx.reshape(SSD_HEADS, SSD_P).sum(axis=1).reshape(1, SSD_HEADS),
        "ssd_norm_w": g_ssdn, "hg_lower_bounds": g_hgl,
        "hg_norm_w": g_hgn_x.reshape(HG_HEADS, HG_K).sum(axis=0).reshape(1, HG_K),
        "norm_xa_w": g_nxa, "norm_mem_w": g_nmem, "norm_ffn_w": g_nffn, "norm_final_w": g_nf,
        "loss": loss[:, 0:1]}
    return grad_x, big, smallg, received


COL_SHARDED = ("w_in", "xa_wkv", "ffn_w_gate", "ffn_w_up")


def _pad_rows(t, rows):
    return jnp.pad(t, [(0, 0)] * (t.ndim - 2) + [(0, rows - t.shape[-2]), (0, 0)])


def _group_fill(parts, group, lead, dtype):
    used = sum(p.shape[-2] for p in parts)
    if used < GROUP_ROWS[group]:
        parts.append(jnp.zeros(lead + (GROUP_ROWS[group] - used, D), dtype))
    return parts


def _pack_group(shards, group, dtype, extra=None):
    parts = [_pad_rows(shards[n].astype(dtype).reshape(r, D), _rows_padded(r)) for n, r in GROUPS[group]]
    if extra is not None:
        parts.append(extra)
    return jnp.concatenate(_group_fill(parts, group, (), dtype), axis=0)


def _unpack_group(packed, group, shapes):
    out, off = {}, 0
    for n, r in GROUPS[group]:
        out[n] = packed[off:off + r].reshape(shapes[n])
        off += _rows_padded(r)
    return out


def _unpack_gathered(gath, groups):
    out, base = {}, 0
    for group in groups:
        off = base
        for n, r in GROUPS[group]:
            t = gath[:, off:off + r]
            out[n] = _col_shards(t, D) if n in COL_SHARDED else t.reshape(N_DEV * r, D)
            off += _rows_padded(r)
        base += GROUP_ROWS[group]
    return out


def _grad_pieces(group, grads):
    parts = [_pad_rows(_to_col_shards(grads[n], N_DEV) if n in COL_SHARDED else grads[n].reshape(N_DEV, r, D),
                       _rows_padded(r)) for n, r in GROUPS[group]]
    return jnp.concatenate(_group_fill(parts, group, (N_DEV,), F32), axis=1)


def _pack_small(vals):
    return jnp.concatenate(
        [jnp.pad(vals[n].reshape(r, c), ((0, SMALL_SLAB - r), (0, SMALL_COLS - c))) for n, r, c in SMALL], axis=0)


WEIGHTS = ['norm_mix_w', 'w_in', 'conv_w', 'conv_b', 'dt_bias', 'a_log', 'd_skip', 'ssd_norm_w', 'hg_lower_bounds',
           'hg_norm_w', 'w_out', 'norm_xa_w', 'norm_mem_w', 'xa_wq', 'xa_wkv', 'xa_wo', 'norm_ffn_w', 'ffn_w_gate',
           'ffn_w_up', 'ffn_w_down', 'norm_final_w']
BIG = tuple(n for n, _ in PACK)


def kernel(x, mem, norm_mix_w, w_in, conv_w, conv_b, dt_bias, a_log, d_skip, ssd_norm_w, hg_lower_bounds, hg_norm_w, w_out, norm_xa_w, norm_mem_w, xa_wq, xa_wkv, xa_wo, norm_ffn_w, ffn_w_gate, ffn_w_up, ffn_w_down, norm_final_w, loss_target, m_norm_mix_w, m_w_in, m_conv_w, m_conv_b, m_dt_bias, m_a_log, m_d_skip, m_ssd_norm_w, m_hg_lower_bounds, m_hg_norm_w, m_w_out, m_norm_xa_w, m_norm_mem_w, m_xa_wq, m_xa_wkv, m_xa_wo, m_norm_ffn_w, m_ffn_w_gate, m_ffn_w_up, m_ffn_w_down, m_norm_final_w, v_norm_mix_w, v_w_in, v_conv_w, v_conv_b, v_dt_bias, v_a_log, v_d_skip, v_ssd_norm_w, v_hg_lower_bounds, v_hg_norm_w, v_w_out, v_norm_xa_w, v_norm_mem_w, v_xa_wq, v_xa_wkv, v_xa_wo, v_norm_ffn_w, v_ffn_w_gate, v_ffn_w_up, v_ffn_w_down, v_norm_final_w):
    args = dict(locals())
    w = {n: args[n] for n in WEIGHTS}
    mo = {n: args["m_" + n] for n in WEIGHTS}
    vo = {n: args["v_" + n] for n in WEIGHTS}
    me = 4 * lax.axis_index("x") + 2 * lax.axis_index("y") + lax.axis_index("c")

    big_sh = {n: w[n][0] for n in BIG}
    cw_bits = lax.bitcast_convert_type(conv_w[0], BF16).reshape(-1)
    cw_rows = jnp.pad(cw_bits, (0, CONV_BITS_ROWS * D - cw_bits.shape[0])).reshape(CONV_BITS_ROWS, D)
    gath = _allgather(_pack_group(big_sh, "in", BF16, extra=cw_rows))
    wt = _unpack_gathered(gath, ("in",))
    off = _rows_padded(GROUPS["in"][0][1])
    cw_all = lax.bitcast_convert_type(gath[:, off:off + 2].reshape(N_DEV, 2 * D)[:, 0:1536].reshape(N_DEV, 4, 192, 2),
                                      F32)
    conv_w_full = cw_all.transpose(1, 0, 2).reshape(4, 1536)

    small = {n: w[n][0] if w[n].ndim == 3 else w[n] for n in WEIGHTS if n not in BIG}
    small["conv_w"] = conv_w_full
    small["hg_lower_bounds"] = hg_lower_bounds
    dist = {"rest_pack": jnp.concatenate([_pack_group(big_sh, "mid", BF16), _pack_group(big_sh, "ffn", BF16)], axis=0),
            "unpack_rest": lambda g: _unpack_gathered(g, ("mid", "ffn")),
            "pieces": _grad_pieces}
    grad_x, _, gsmall, received = _local_step(x[0], mem[0], loss_target[0], wt, small, dist)

    shapes = {n: w[n].shape for n in BIG}
    out_g, out_d, out_m, out_v = {}, {}, {}, {}
    for group in GROUPS:
        wp = _pack_group(big_sh, group, F32)
        mp = _pack_group({n: mo[n][0] for n in BIG}, group, F32)
        vp = _pack_group({n: vo[n][0] for n in BIG}, group, F32)
        packed = _adamw(received[group], wp, mp, vp, tr=ADAM_ROWS[group], name="adamw_" + group)
        for dst, src in zip((out_g, out_d, out_m, out_v), packed):
            dst.update(_unpack_group(src, group, shapes))

    tot = _small_allreduce(_pack_small(gsmall))
    loss = tot[SMALL_AT["loss"][0], 0]
    cw_row = SMALL_AT["conv_w"][0]
    conv_w_grad = lax.dynamic_slice(tot, (cw_row, me * 192), (4, 192))
    names = [n for n, _, _ in SMALL if n != "loss"]
    as2d = lambda t: t.reshape(t.shape[-2:] if t.ndim > 1 else (1, t.shape[0]))
    small_out = _adamw_small(tot, conv_w_grad, *[[as2d(d[n]) for n in names] for d in (w, mo, vo)])
    for dst, src in zip((out_g, out_d, out_m, out_v), small_out):
        dst.update({n: t.reshape(w[n].shape) for n, t in zip(names, src)})
    return (loss, grad_x[None], *[out_g[n] for n in WEIGHTS], *[out_d[n] for n in WEIGHTS],
            *[out_m[n] for n in WEIGHTS], *[out_v[n] for n in WEIGHTS])
```

```python
import jax
import jax.numpy as jnp
from jax import lax
from jax.experimental import pallas as pl
from jax.experimental.pallas import tpu as pltpu

F32, BF16 = jnp.float32, jnp.bfloat16
MESH = pl.DeviceIdType.MESH

D = 1024
EPS = 1e-6
SSD_HEADS, SSD_P, SSD_N, SSD_Q = 16, 64, 128, 128
HG_HEADS, HG_K, HG_STEP, HG_SUB = 8, 128, 128, 64
XA_HEADS, XA_DH, MEM_LEN = 4, 256, 256
FFN = 2816
N_IN = 6672
N_DEV = 8
WA, WB, WC = 4096, 1152, 1536
ADAM_LR, ADAM_B1, ADAM_B2, ADAM_EPS, ADAM_WD, ADAM_STEP = 0.001, 0.9, 0.999, 1e-08, 0.01, 10
VMEM_MB = 2 ** 20

PACK = (("w_in", 834), ("w_out", 256), ("xa_wq", 128), ("xa_wkv", 256), ("xa_wo", 128),
        ("ffn_w_gate", 352), ("ffn_w_up", 352), ("ffn_w_down", 352))
ROW_TILE = 16
GROUPS = {"in": PACK[0:1], "mid": PACK[1:5], "ffn": PACK[5:8]}
GROUP_ROWS = {"in": 896, "mid": 768, "ffn": 1056}
ADAM_ROWS = {"in": 128, "mid": 128, "ffn": 176}
CONV_BITS_ROWS = ROW_TILE


def _rows_padded(r):
    return -(-r // ROW_TILE) * ROW_TILE

SMALL = (("norm_mix_w", 1, 1024), ("conv_w", 4, 1536), ("conv_b", 1, 1536), ("dt_bias", 1, 16), ("a_log", 1, 16),
         ("d_skip", 1, 16), ("ssd_norm_w", 1, 1024), ("hg_lower_bounds", 2, 1024), ("hg_norm_w", 1, 128),
         ("norm_xa_w", 1, 1024), ("norm_mem_w", 1, 1024), ("norm_ffn_w", 1, 1024), ("norm_final_w", 1, 1024),
         ("loss", 1, 1))
SMALL_COLS = 1536
SMALL_ROWS = 24
SMALL_AT = {n: (sum(q for _, q, _ in SMALL[:i]), r, c) for i, (n, r, c) in enumerate(SMALL)}


def _cparams(sem=None, vmem_mb=48):
    return pltpu.CompilerParams(dimension_semantics=sem, vmem_limit_bytes=vmem_mb * VMEM_MB)


def _dot(a, b):
    return jnp.dot(a.astype(BF16), b.astype(BF16), preferred_element_type=F32)


def _dot_nt(a, b):
    return lax.dot_general(a.astype(BF16), b.astype(BF16), (((1,), (1,)), ((), ())), preferred_element_type=F32)


def _dot_tn(a, b):
    return lax.dot_general(a.astype(BF16), b.astype(BF16), (((0,), (0,)), ((), ())), preferred_element_type=F32)


def _split3(a):
    a1 = a.astype(BF16)
    r1 = a - a1.astype(F32)
    a2 = r1.astype(BF16)
    a3 = (r1 - a2.astype(F32)).astype(BF16)
    return a1, a2, a3


def _dot_hi(a, b, general=0):
    if general == 0:
        return sum(jnp.dot(t, b.astype(BF16), preferred_element_type=F32) for t in _split3(a))
    return sum(jnp.dot(a.astype(BF16), t, preferred_element_type=F32) for t in _split3(b))


def _dot_nt_hi(a, b):
    return sum(_dot_nt(t, b) for t in _split3(a))


def _sigmoid(x):
    return 1.0 / (1.0 + jnp.exp(-x))


def _silu(x):
    return x * _sigmoid(x)


def _dsilu(x):
    s = _sigmoid(x)
    return s * (1.0 + x * (1.0 - s))


def _softplus(x):
    return jnp.maximum(x, 0.0) + jnp.log(1.0 + jnp.exp(-jnp.abs(x)))


def _rms_fwd(x, w):
    r = lax.rsqrt(jnp.mean(x * x, axis=1, keepdims=True) + EPS)
    return x * r * w


def _rms_bwd(dy, x, w):
    r = lax.rsqrt(jnp.mean(x * x, axis=1, keepdims=True) + EPS)
    xh = x * r
    g = dy * w
    dx = r * (g - xh * jnp.mean(g * xh, axis=1, keepdims=True))
    return dx, jnp.sum(dy * xh, axis=0, keepdims=True)


def _iota2(shape, dim):
    return lax.broadcasted_iota(jnp.int32, shape, dim)


def _tril(n):
    return (_iota2((n, n), 0) >= _iota2((n, n), 1)).astype(F32)


def _triu(n):
    return (_iota2((n, n), 0) <= _iota2((n, n), 1)).astype(F32)


def _norm_mm(x, nw, w, *, tm, tn, name, emit_h=False, out_dtype=F32):
    L, K = x.shape
    N = w.shape[0]
    tm, tn = min(tm, L), min(tn, N)
    ni, nj = L // tm, N // tn

    def body(x_ref, nw_ref, w_ref, *rest):
        if emit_h:
            o_ref, h_ref, hs_ref = rest
        else:
            o_ref, hs_ref = rest

        @pl.when(pl.program_id(1) == 0)
        def _():
            h = _rms_fwd(x_ref[...], nw_ref[...]).astype(BF16)
            hs_ref[...] = h
            if emit_h:
                h_ref[...] = h

        o_ref[...] = _dot_nt(hs_ref[...], w_ref[...]).astype(out_dtype)

    out_shape = [jax.ShapeDtypeStruct((L, N), out_dtype)]
    out_specs = [pl.BlockSpec((tm, tn), lambda i, j: (i, j))]
    if emit_h:
        out_shape.append(jax.ShapeDtypeStruct((L, K), BF16))
        out_specs.append(pl.BlockSpec((tm, K), lambda i, j: (i, 0)))
    res = pl.pallas_call(
        body, name=name, grid=(ni, nj),
        in_specs=[pl.BlockSpec((tm, K), lambda i, j: (i, 0)), pl.BlockSpec((1, K), lambda i, j: (0, 0)),
                  pl.BlockSpec((tn, K), lambda i, j: (j, 0))],
        out_specs=out_specs, out_shape=out_shape, scratch_shapes=[pltpu.VMEM((tm, K), BF16)],
        compiler_params=_cparams(("parallel", "arbitrary")),
    )(x, nw, w)
    return res if len(res) > 1 else res[0]


def _inproj(x, nw, w_a, w_b, w_c, *, tm, gather=None):
    L, K = x.shape
    tm = min(tm, L)
    ni = L // tm

    def body(x_ref, nw_ref, wa_ref, wb_ref, wc_ref, pa_ref, hf_ref, z_ref, dt_ref, pc_ref, h_ref):
        h = _rms_fwd(x_ref[...], nw_ref[...]).astype(BF16)
        h_ref[...] = h
        pa_ref[...] = _dot_nt(h, wa_ref[0:3 * D, :]).astype(BF16)
        hf_ref[...] = _dot_nt(h, wa_ref[3 * D:4 * D, :])
        pb = _dot_nt(h, wb_ref[...])
        z_ref[...] = pb[:, 0:D].astype(BF16)
        dt_ref[...] = pb[:, D:D + 128]
        pc_ref[...] = _dot_nt(h, wc_ref[...]).astype(BF16)

    row = lambda n: pl.BlockSpec((tm, n), lambda i: (i, 0))
    full = lambda a: pl.BlockSpec(a.shape, lambda i: (0, 0))
    at = lambda i: lambda: pl.program_id(0) == i
    payload = None if gather is None else (_Gather, gather)
    phases = [(at(0), _Gather.start, True), (at(ni // 2), _Gather.forward, True), (at(ni - 1), _Gather.finish, False)]
    in_specs, out_specs, out_shape, scratch = _carried_specs(
        payload, [row(K), full(nw), full(w_a), full(w_b), full(w_c)],
        [row(3 * D), row(D), row(D), row(128), row(WC), row(K)],
        [jax.ShapeDtypeStruct((L, 3 * D), BF16), jax.ShapeDtypeStruct((L, D), F32),
         jax.ShapeDtypeStruct((L, D), BF16), jax.ShapeDtypeStruct((L, 128), F32),
         jax.ShapeDtypeStruct((L, WC), BF16), jax.ShapeDtypeStruct((L, K), BF16)], [])
    args = [x, nw, w_a, w_b, w_c]
    return pl.pallas_call(
        _carried(body, 5, 6, payload, phases), name="inproj", grid=(ni,),
        in_specs=in_specs, out_specs=out_specs, out_shape=out_shape, scratch_shapes=scratch,
        compiler_params=_cparams(("arbitrary",), vmem_mb=58),
    )(*(args if gather is None else args + [gather]))


def _mm2_res(res, a1, a2, w, *, tm, name):
    L, N = res.shape
    K = a1.shape[1]
    tm = min(tm, L)

    def body(r_ref, a1_ref, a2_ref, w_ref, o_ref):
        acc = jnp.dot(a1_ref[...], w_ref[0:K, :], preferred_element_type=F32)
        acc += jnp.dot(a2_ref[...], w_ref[K:2 * K, :], preferred_element_type=F32)
        o_ref[...] = r_ref[...] + acc

    return pl.pallas_call(
        body, name=name, grid=(L // tm,),
        in_specs=[pl.BlockSpec((tm, N), lambda i: (i, 0)), pl.BlockSpec((tm, K), lambda i: (i, 0)),
                  pl.BlockSpec((tm, K), lambda i: (i, 0)), pl.BlockSpec((2 * K, N), lambda i: (0, 0))],
        out_specs=pl.BlockSpec((tm, N), lambda i: (i, 0)),
        out_shape=jax.ShapeDtypeStruct((L, N), F32),
        compiler_params=_cparams(("parallel",)),
    )(res, a1, a2, w)


def _mm_nt2(a, w, *, tm, name):
    L, N = a.shape
    K = w.shape[0] // 2
    tm = min(tm, L)

    def body(a_ref, w_ref, o1_ref, o2_ref):
        av = a_ref[...].astype(BF16)
        o1_ref[...] = _dot_nt(av, w_ref[0:K, :]).astype(BF16)
        o2_ref[...] = _dot_nt(av, w_ref[K:2 * K, :]).astype(BF16)

    return pl.pallas_call(
        body, name=name, grid=(L // tm,),
        in_specs=[pl.BlockSpec((tm, N), lambda i: (i, 0)), pl.BlockSpec((2 * K, N), lambda i: (0, 0))],
        out_specs=[pl.BlockSpec((tm, K), lambda i: (i, 0)), pl.BlockSpec((tm, K), lambda i: (i, 0))],
        out_shape=[jax.ShapeDtypeStruct((L, K), BF16), jax.ShapeDtypeStruct((L, K), BF16)],
        compiler_params=_cparams(("parallel",)),
    )(a, w)


def _dw(a, b, *, tM, tN, tl, name):
    L, M = a.shape
    N = b.shape[1]
    tM, tN, tl = min(tM, M), min(tN, N), min(tl, L)

    def body(a_ref, b_ref, o_ref):
        @pl.when(pl.program_id(2) == 0)
        def _():
            o_ref[...] = jnp.zeros_like(o_ref)

        o_ref[...] += _dot_tn(a_ref[...], b_ref[...])

    return pl.pallas_call(
        body, name=name, grid=(M // tM, N // tN, L // tl),
        in_specs=[pl.BlockSpec((tl, tM), lambda i, j, l: (l, i)), pl.BlockSpec((tl, tN), lambda i, j, l: (l, j))],
        out_specs=pl.BlockSpec((tM, tN), lambda i, j, l: (i, j)),
        out_shape=jax.ShapeDtypeStruct((M, N), F32),
        compiler_params=_cparams(("parallel", "parallel", "arbitrary")),
    )(a, b)


def _first_last(n):
    return [(lambda: pl.program_id(0) == 0, _Exchange.start, True),
            (lambda: pl.program_id(0) == n - 1, _Exchange.finish, False)]


def _mm_normbwd(a_list, w_list, x, nw, res, *, tm, name, exchange=None):
    L, Dm = x.shape
    tm = min(tm, L)
    n = len(a_list)
    has_res = res is not None

    def body(*refs):
        a_refs, w_refs = refs[:n], refs[n:2 * n]
        x_ref, nw_ref = refs[2 * n], refs[2 * n + 1]
        k = 2 * n + 2
        r_ref = refs[k] if has_res else None
        dx_ref, dnw_ref = refs[k + has_res], refs[k + has_res + 1]
        dh = _dot(a_refs[0][...], w_refs[0][...])
        for a_ref, w_ref in zip(a_refs[1:], w_refs[1:]):
            dh += _dot(a_ref[...], w_ref[...])
        dx, dnw = _rms_bwd(dh, x_ref[...], nw_ref[...])
        dx_ref[...] = dx + r_ref[...] if has_res else dx

        @pl.when(pl.program_id(0) == 0)
        def _():
            dnw_ref[...] = jnp.zeros_like(dnw_ref)

        dnw_ref[...] += dnw

    in_specs = [pl.BlockSpec((tm, a.shape[1]), lambda i: (i, 0)) for a in a_list]
    in_specs += [pl.BlockSpec(w.shape, lambda i: (0, 0)) for w in w_list]
    in_specs += [pl.BlockSpec((tm, Dm), lambda i: (i, 0)), pl.BlockSpec((1, Dm), lambda i: (0, 0))]
    args = [*a_list, *w_list, x, nw]
    if has_res:
        in_specs.append(pl.BlockSpec((tm, Dm), lambda i: (i, 0)))
        args.append(res)
    payload = None if exchange is None else (_Exchange, exchange)
    n_in = len(args)
    if exchange is not None:
        args.append(exchange)
    in_specs, out_specs, out_shape, scratch = _carried_specs(
        payload, in_specs, [pl.BlockSpec((tm, Dm), lambda i: (i, 0)), pl.BlockSpec((1, Dm), lambda i: (0, 0))],
        [jax.ShapeDtypeStruct((L, Dm), F32), jax.ShapeDtypeStruct((1, Dm), F32)], [])
    return pl.pallas_call(
        _carried(body, n_in, 2, payload, _first_last(L // tm)), name=name, grid=(L // tm,), in_specs=in_specs,
        out_specs=out_specs, out_shape=out_shape, scratch_shapes=scratch,
        compiler_params=_cparams(("arbitrary",), vmem_mb=56),
    )(*args)


CONV_TN = 512


HALO = 16


def _conv_pre(cat, w_ref, b_ref, rows):
    shifted = [pltpu.roll(cat, 3 - k, 0)[HALO:HALO + rows] for k in range(3)] + [cat[HALO:HALO + rows]]
    pre = b_ref[...] + w_ref[3:4, :] * shifted[3]
    for k in range(3):
        pre += w_ref[k:k + 1, :] * shifted[k]
    return pre, shifted


def _conv_fwd(pc, cw, cb, *, tm):
    L, C = pc.shape
    tm = min(tm, L)
    tn = CONV_TN

    def body(u_ref, halo_ref, w_ref, b_ref, o_ref):
        halo = jnp.where(pl.program_id(1) > 0, halo_ref[...].astype(F32), 0.0)
        cat = jnp.concatenate([halo, u_ref[...].astype(F32)], axis=0)
        pre, _ = _conv_pre(cat, w_ref, b_ref, tm)
        o_ref[...] = _silu(pre).astype(BF16)

    return pl.pallas_call(
        body, name="conv_fwd", grid=(C // tn, L // tm),
        in_specs=[pl.BlockSpec((tm, tn), lambda j, i: (i, j)),
                  pl.BlockSpec((HALO, tn), lambda j, i: (jnp.maximum(i * (tm // HALO) - 1, 0), j)),
                  pl.BlockSpec((4, tn), lambda j, i: (0, j)), pl.BlockSpec((1, tn), lambda j, i: (0, j))],
        out_specs=pl.BlockSpec((tm, tn), lambda j, i: (i, j)),
        out_shape=jax.ShapeDtypeStruct((L, C), BF16),
        compiler_params=_cparams(("parallel", "parallel")),
    )(pc, pc, cw, cb)


def _conv_bwd(pc, dact, cw, cb, *, tm):
    L, C = pc.shape
    tm = min(tm, L)
    tn = CONV_TN
    nt = L // tm

    def body(u_ref, halo_ref, unext_ref, da_ref, danext_ref, w_ref, b_ref, du_ref, dw_ref, db_ref):
        i = pl.program_id(1)
        halo = jnp.where(i > 0, halo_ref[...].astype(F32), 0.0)
        cat = jnp.concatenate([halo, u_ref[...].astype(F32), unext_ref[...].astype(F32)], axis=0)
        pre, shifted = _conv_pre(cat, w_ref, b_ref, tm + HALO)
        da = jnp.concatenate([da_ref[...].astype(F32),
                              jnp.where(i < nt - 1, danext_ref[...].astype(F32), 0.0)], axis=0)
        dpre = da * _dsilu(pre)
        du = w_ref[3:4, :] * dpre[0:tm]
        for k in range(3):
            du += w_ref[k:k + 1, :] * pltpu.roll(dpre, tm + HALO - (3 - k), 0)[0:tm]
        du_ref[...] = du.astype(BF16)

        @pl.when(i == 0)
        def _():
            dw_ref[...] = jnp.zeros_like(dw_ref)
            db_ref[...] = jnp.zeros_like(db_ref)

        dp = dpre[0:tm]
        dw_ref[...] += jnp.concatenate(
            [jnp.sum(dp * shifted[k][0:tm], axis=0, keepdims=True) for k in range(4)], axis=0)
        db_ref[...] += jnp.sum(dp, axis=0, keepdims=True)

    nb = L // HALO
    return pl.pallas_call(
        body, name="conv_bwd", grid=(C // tn, nt),
        in_specs=[pl.BlockSpec((tm, tn), lambda j, i: (i, j)),
                  pl.BlockSpec((HALO, tn), lambda j, i: (jnp.maximum(i * (tm // HALO) - 1, 0), j)),
                  pl.BlockSpec((HALO, tn), lambda j, i: (jnp.minimum((i + 1) * (tm // HALO), nb - 1), j)),
                  pl.BlockSpec((tm, tn), lambda j, i: (i, j)),
                  pl.BlockSpec((HALO, tn), lambda j, i: (jnp.minimum((i + 1) * (tm // HALO), nb - 1), j)),
                  pl.BlockSpec((4, tn), lambda j, i: (0, j)), pl.BlockSpec((1, tn), lambda j, i: (0, j))],
        out_specs=[pl.BlockSpec((tm, tn), lambda j, i: (i, j)), pl.BlockSpec((4, tn), lambda j, i: (0, j)),
                   pl.BlockSpec((1, tn), lambda j, i: (0, j))],
        out_shape=[jax.ShapeDtypeStruct((L, C), BF16), jax.ShapeDtypeStruct((4, C), F32),
                   jax.ShapeDtypeStruct((1, C), F32)],
        compiler_params=_cparams(("parallel", "arbitrary")),
    )(pc, pc, pc, dact, dact, cw, cb)


def _ssd_common(dtr_ref, dtb_ref, avec_ref, aexp_ref, e_ref, acx_ref, acol_ref, arow_ref):
    q = SSD_Q
    tril = _tril(q)
    dtpre = dtr_ref[...] + dtb_ref[...]
    dt = _softplus(dtpre)
    dtx = _dot_hi(dt, e_ref[...])
    acx_ref[...] = _dot_hi(tril, dtx * aexp_ref[...], 1)
    acol = _dot_hi(tril, dt * avec_ref[...], 1)
    acol_ref[...] = acol
    arow_ref[...] = acol.T
    return dtpre, dt, dtx


def _ssd_fwd(xbc, pz, pdt, dtb, avec, aexp, dexp, nw, emat):
    L = xbc.shape[0]
    q = SSD_Q
    nc = L // q

    def body(xbc_ref, z_ref, dtr_ref, dtb_ref, avec_ref, aexp_ref, dexp_ref, nw_ref, e_ref,
             ya_ref, ypre_ref, sin_ref, st_ref, acx_ref, acol_ref, arow_ref, xdt_ref, y_ref):
        @pl.when(pl.program_id(0) == 0)
        def _():
            st_ref[...] = jnp.zeros_like(st_ref)

        sin_ref[...] = st_ref[...].astype(BF16)
        _, _, dtx = _ssd_common(dtr_ref, dtb_ref, avec_ref, aexp_ref, e_ref, acx_ref, acol_ref, arow_ref)
        xs = xbc_ref[:, 0:D].astype(F32)
        xdt = xs * dtx
        xdt_ref[...] = xdt
        acx = acx_ref[...]
        alast = acx_ref[q - 1:q, :]
        xdtd = xdt * jnp.exp(alast - acx)
        eac = jnp.exp(acx)
        ealast = jnp.exp(alast)
        causal = _iota2((q, q), 0) >= _iota2((q, q), 1)
        for g in range(2):
            gs = slice(512 * g, 512 * g + 512)
            bm = xbc_ref[:, D + 128 * g:D + 128 * g + 128]
            cm = xbc_ref[:, D + 256 + 128 * g:D + 256 + 128 * g + 128]
            stg = st_ref[:, gs]
            yoff = _dot(cm, stg) * eac[:, gs]
            gmat = _dot_nt(cm, bm)
            for e in range(8):
                h = 8 * g + e
                hs = slice(64 * h, 64 * h + 64)
                col = acol_ref[:, h:h + 1]
                row = arow_ref[h:h + 1, :]
                lm = jnp.exp(jnp.where(causal, col - row, -1e30))
                y_ref[:, hs] = _dot(gmat * lm, xdt_ref[:, hs])
            y_ref[:, gs] += yoff + dexp_ref[:, gs] * xs[:, gs]
            st_ref[:, gs] = stg * ealast[:, gs] + _dot_tn(bm, xdtd[:, gs])
        ypre_ref[...] = y_ref[...].astype(BF16)
        for g in range(2):
            gs = slice(512 * g, 512 * g + 512)
            yz = y_ref[:, gs] * _silu(z_ref[:, gs].astype(F32))
            ya_ref[:, gs] = _rms_fwd(yz, nw_ref[:, gs]).astype(BF16)

    vec = lambda n: pl.BlockSpec((1, n), lambda c: (0, 0))
    return pl.pallas_call(
        body, name="ssd_fwd", grid=(nc,),
        in_specs=[pl.BlockSpec((q, 1536), lambda c: (c, 0)), pl.BlockSpec((q, D), lambda c: (c, 0)),
                  pl.BlockSpec((q, 128), lambda c: (c, 0)), vec(128), vec(128), vec(D), vec(D), vec(D),
                  pl.BlockSpec((128, D), lambda c: (0, 0))],
        out_specs=[pl.BlockSpec((q, D), lambda c: (c, 0)), pl.BlockSpec((q, D), lambda c: (c, 0)),
                   pl.BlockSpec((128, D), lambda c: (c, 0))],
        out_shape=[jax.ShapeDtypeStruct((L, D), BF16), jax.ShapeDtypeStruct((L, D), BF16),
                   jax.ShapeDtypeStruct((nc * 128, D), BF16)],
        scratch_shapes=[pltpu.VMEM((128, D), F32), pltpu.VMEM((q, D), F32), pltpu.VMEM((q, 128), F32),
                        pltpu.VMEM((128, q), F32), pltpu.VMEM((q, D), F32), pltpu.VMEM((q, D), F32)],
        compiler_params=_cparams(("arbitrary",)),
    )(xbc, pz, pdt, dtb, avec, aexp, dexp, nw, emat)


def _ssd_bwd(xbc, pz, pdt, ypre, sin, dya, dtb, avec, aexp, dexp, nw, emat, exchange=None):
    L = xbc.shape[0]
    q = SSD_Q
    nc = L // q

    def body(xbc_ref, z_ref, dtr_ref, ypre_ref, sin_ref, dya_ref, dtb_ref, avec_ref, aexp_ref, dexp_ref, nw_ref,
             e_ref, dpb_ref, dxbc_ref, dnw_ref, ddtb_ref, da_ref, ddx_ref,
             dst_ref, acx_ref, acol_ref, arow_ref, xdt_ref, dxdt_ref, dy_ref, dacx_ref):
        @pl.when(pl.program_id(0) == 0)
        def _():
            dst_ref[...] = jnp.zeros_like(dst_ref)
            dnw_ref[...] = jnp.zeros_like(dnw_ref)
            ddtb_ref[...] = jnp.zeros_like(ddtb_ref)
            da_ref[...] = jnp.zeros_like(da_ref)
            ddx_ref[...] = jnp.zeros_like(ddx_ref)

        dtpre, dt, dtx = _ssd_common(dtr_ref, dtb_ref, avec_ref, aexp_ref, e_ref, acx_ref, acol_ref, arow_ref)
        xs = xbc_ref[:, 0:D].astype(F32)
        xdt = xs * dtx
        xdt_ref[...] = xdt
        acx = acx_ref[...]
        alast = acx_ref[q - 1:q, :]
        dec_end = jnp.exp(alast - acx)
        xdtd = xdt * dec_end
        eac = jnp.exp(acx)
        ealast = jnp.exp(alast)
        for g in range(2):
            gs = slice(512 * g, 512 * g + 512)
            y = ypre_ref[:, gs].astype(F32)
            z = z_ref[:, gs].astype(F32)
            sz = _silu(z)
            dyz, dnw = _rms_bwd(dya_ref[:, gs].astype(F32), y * sz, nw_ref[:, gs])
            dnw_ref[:, gs] += dnw
            dy_ref[:, gs] = dyz * sz
            dpb_ref[:, gs] = (dyz * y * _dsilu(z)).astype(BF16)
        dy = dy_ref[...]
        ddx_ref[...] += jnp.sum(dy * xs, axis=0, keepdims=True)
        ri = _iota2((q, q), 0)
        ci = _iota2((q, q), 1)
        causal = ri >= ci
        causal_t = ri <= ci
        dacol = jnp.zeros((q, 128), F32)
        dacol_t = jnp.zeros((128, q), F32)
        last_row = _iota2((q, 512), 0) == q - 1
        for g in range(2):
            gs = slice(512 * g, 512 * g + 512)
            bm = xbc_ref[:, D + 128 * g:D + 128 * g + 128]
            cm = xbc_ref[:, D + 256 + 128 * g:D + 256 + 128 * g + 128]
            stg = sin_ref[:, gs].astype(F32)
            dstg = dst_ref[:, gs]
            dyg = dy[:, gs]
            yoff = _dot(cm, stg) * eac[:, gs]
            dwm = dyg * eac[:, gs]
            dcm = _dot_nt(dwm, stg)
            dstin = _dot_tn(cm, dwm)
            dacx_g = dyg * yoff
            dxdtd = _dot(bm, dstg)
            dbm = _dot_nt(xdtd[:, gs], dstg)
            t = dxdtd * xdtd[:, gs]
            dacx_g -= t
            dalast = jnp.sum(t, axis=0, keepdims=True) + jnp.sum(dstg * stg, axis=0, keepdims=True) * ealast[:, gs]
            dst_ref[:, gs] = dstin + dstg * ealast[:, gs]
            dacx_ref[:, gs] = dacx_g + jnp.where(last_row, dalast, 0.0)
            gmat = _dot_nt(cm, bm)
            gmat_t = _dot_nt(bm, cm)
            dg = jnp.zeros((q, q), F32)
            for e in range(8):
                h = 8 * g + e
                hs = slice(64 * h, 64 * h + 64)
                col = acol_ref[:, h:h + 1]
                row = arow_ref[h:h + 1, :]
                lm = jnp.exp(jnp.where(causal, col - row, -1e30))
                lm_t = jnp.exp(jnp.where(causal_t, row - col, -1e30))
                dyh = dy_ref[:, hs]
                dm = _dot_nt(dyh, xdt_ref[:, hs])
                dxdt_ref[:, hs] = _dot(gmat_t * lm_t, dyh)
                dml = dm * lm
                dg += dml
                p = dml * gmat
                dacol += jnp.where(ci == h, jnp.sum(p, axis=1, keepdims=True), 0.0)
                dacol_t -= jnp.where(ri == h, jnp.sum(p, axis=0, keepdims=True), 0.0)
            dcm += _dot(dg, bm)
            dbm += _dot_tn(dg, cm)
            dxbc_ref[:, D + 128 * g:D + 128 * g + 128] = dbm.astype(BF16)
            dxbc_ref[:, D + 256 + 128 * g:D + 256 + 128 * g + 128] = dcm.astype(BF16)
            dxdt_ref[:, gs] += dxdtd * dec_end[:, gs]
        dxdt = dxdt_ref[...]
        dacum = dacol + dacol_t.T + _dot_nt_hi(dacx_ref[...], e_ref[...])
        da = _dot_hi(_triu(q), dacum, 1)
        ddt = da * avec_ref[...] + _dot_nt_hi(dxdt * xs, e_ref[...])
        da_ref[...] += jnp.sum(da * dt, axis=0, keepdims=True) * avec_ref[...]
        dxbc_ref[:, 0:D] = (dexp_ref[...] * dy + dxdt * dtx).astype(BF16)
        ddtr = ddt * _sigmoid(dtpre)
        ddtb_ref[...] += jnp.sum(ddtr, axis=0, keepdims=True)
        dpb_ref[:, D:D + 128] = ddtr.astype(BF16)

    rev = lambda c: nc - 1 - c
    vec = lambda n: pl.BlockSpec((1, n), lambda c: (0, 0))
    payload = None if exchange is None else (_Exchange, exchange)
    args = [xbc, pz, pdt, ypre, sin, dya, dtb, avec, aexp, dexp, nw, emat]
    in_specs, out_specs, out_shape, scratch = _carried_specs(
        payload,
        [pl.BlockSpec((q, 1536), lambda c: (rev(c), 0)), pl.BlockSpec((q, D), lambda c: (rev(c), 0)),
         pl.BlockSpec((q, 128), lambda c: (rev(c), 0)), pl.BlockSpec((q, D), lambda c: (rev(c), 0)),
         pl.BlockSpec((128, D), lambda c: (rev(c), 0)), pl.BlockSpec((q, D), lambda c: (rev(c), 0)),
         vec(128), vec(128), vec(D), vec(D), vec(D), pl.BlockSpec((128, D), lambda c: (0, 0))],
        [pl.BlockSpec((q, WB), lambda c: (rev(c), 0)), pl.BlockSpec((q, 1536), lambda c: (rev(c), 0)),
         vec(D), vec(128), vec(128), vec(D)],
        [jax.ShapeDtypeStruct((L, WB), BF16), jax.ShapeDtypeStruct((L, 1536), BF16),
         jax.ShapeDtypeStruct((1, D), F32), jax.ShapeDtypeStruct((1, 128), F32),
         jax.ShapeDtypeStruct((1, 128), F32), jax.ShapeDtypeStruct((1, D), F32)],
        [pltpu.VMEM((128, D), F32), pltpu.VMEM((q, D), F32), pltpu.VMEM((q, 128), F32),
         pltpu.VMEM((128, q), F32), pltpu.VMEM((q, D), F32), pltpu.VMEM((q, D), F32),
         pltpu.VMEM((q, D), F32), pltpu.VMEM((q, D), F32)])
    return pl.pallas_call(
        _carried(body, len(args), 6, payload, _first_last(nc)), name="ssd_bwd", grid=(nc,),
        in_specs=in_specs, out_specs=out_specs, out_shape=out_shape, scratch_shapes=scratch,
        compiler_params=_cparams(("arbitrary",)),
    )(*(args if exchange is None else args + [exchange]))


def _hg_gates(hq, hf, hgl_ref, b_ref):
    lb = 1.0 / (1.0 + jnp.exp(hgl_ref[1:2, :] - hgl_ref[0:1, :]))
    qf = _silu(hq)
    sg = _sigmoid(hf)
    f = lb + (1.0 - lb) * sg
    b_ref[...] = _dot_hi(_tril(HG_STEP), jnp.log(f), 1)
    return lb, qf, sg, f


def _hg_factors(qf, kf, b_ref):
    s, n = HG_SUB, HG_STEP
    b = b_ref[...]
    blast = b_ref[n - 1:n, :]
    m0, mb, m1 = b_ref[s // 2 - 1:s // 2, :], b_ref[s - 1:s, :], b_ref[s + s // 2 - 1:s + s // 2, :]
    b0, b1 = b[0:s], b[s:n]
    q0, q1, k0, k1 = qf[0:s], qf[s:n], kf[0:s], kf[s:n]
    fac = dict(
        eb=jnp.exp(blast), eq=jnp.exp(b), ek=jnp.exp(blast - b),
        eq0=jnp.exp(b0 - m0), ek0=jnp.exp(m0 - b0), eq1=jnp.exp(b1 - m1), ek1=jnp.exp(m1 - b1),
        eqb=jnp.exp(b1 - mb), ekb=jnp.exp(mb - b0))
    rd = lambda t: t.astype(BF16).astype(F32)
    val = dict(qe=qf * fac["eq"], ke=kf * fac["ek"], qm0=rd(q0 * fac["eq0"]), km0=rd(k0 * fac["ek0"]),
               qm1=rd(q1 * fac["eq1"]), km1=rd(k1 * fac["ek1"]), qb=rd(q1 * fac["eqb"]), kb=rd(k0 * fac["ekb"]))
    return fac, val


def _hgrn_fwd(pa, phf, hgl, nwx):
    L = pa.shape[0]
    n, s = HG_STEP, HG_SUB
    nc = L // n

    def body(hq_ref, hf_ref, hi_ref, hg_ref, hgl_ref, nw_ref, ob_ref, opre_ref, sin_ref, st_ref, b_ref):
        @pl.when(pl.program_id(0) == 0)
        def _():
            st_ref[...] = jnp.zeros_like(st_ref)

        sin_ref[...] = st_ref[...].astype(BF16)
        _, qf, _, f = _hg_gates(hq_ref[...].astype(F32), hf_ref[...], hgl_ref, b_ref)
        fac, val = _hg_factors(qf, 1.0 - f, b_ref)
        causal = _iota2((s, s), 0) >= _iota2((s, s), 1)
        for h in range(HG_HEADS):
            hs = slice(128 * h, 128 * h + 128)
            sth = st_ref[:, hs]
            v = hi_ref[:, hs]
            v0, v1 = v[0:s], v[s:n]
            a00 = jnp.where(causal, _dot_nt(val["qm0"][:, hs], val["km0"][:, hs]), 0.0)
            a11 = jnp.where(causal, _dot_nt(val["qm1"][:, hs], val["km1"][:, hs]), 0.0)
            a10 = _dot_nt(val["qb"][:, hs], val["kb"][:, hs])
            o = _dot_nt(val["qe"][:, hs], sth) + jnp.concatenate(
                [_dot(a00, v0), _dot(a10, v0) + _dot(a11, v1)], axis=0)
            st_ref[:, hs] = sth * fac["eb"][:, hs] + _dot_tn(v, val["ke"][:, hs])
            opre_ref[:, hs] = o.astype(BF16)
            ob_ref[:, hs] = (_rms_fwd(o, nw_ref[:, hs]) * _silu(hg_ref[:, hs].astype(F32))).astype(BF16)

    blk = lambda j: pl.BlockSpec((n, D), lambda c: (c, j))
    return pl.pallas_call(
        body, name="hgrn_fwd", grid=(nc,),
        in_specs=[blk(0), blk(0), blk(1), blk(2), pl.BlockSpec((2, D), lambda c: (0, 0)),
                  pl.BlockSpec((1, D), lambda c: (0, 0))],
        out_specs=[blk(0), blk(0), blk(0)],
        out_shape=[jax.ShapeDtypeStruct((L, D), BF16), jax.ShapeDtypeStruct((L, D), BF16),
                   jax.ShapeDtypeStruct((nc * 128, D), BF16)],
        scratch_shapes=[pltpu.VMEM((128, D), F32), pltpu.VMEM((n, D), F32)],
        compiler_params=_cparams(("arbitrary",)),
    )(pa, phf, pa, pa, hgl, nwx)


def _hgrn_bwd(pa, phf, opre, sin, dob, hgl, nwx, exchange=None):
    L = pa.shape[0]
    n, s = HG_STEP, HG_SUB
    nc = L // n

    def body(hq_ref, hf_ref, hi_ref, hg_ref, opre_ref, sin_ref, dob_ref, hgl_ref, nw_ref,
             dpa_ref, dhgl_ref, dnw_ref, dst_ref, b_ref, dlb_ref, dq_ref, dk_ref, db_ref):
        i = pl.program_id(0)

        @pl.when(i == 0)
        def _():
            dst_ref[...] = jnp.zeros_like(dst_ref)
            dlb_ref[...] = jnp.zeros_like(dlb_ref)
            dnw_ref[...] = jnp.zeros_like(dnw_ref)

        hq = hq_ref[...].astype(F32)
        lb, qf, sg, f = _hg_gates(hq, hf_ref[...], hgl_ref, b_ref)
        kf = 1.0 - f
        fac, val = _hg_factors(qf, kf, b_ref)
        ri, ci = _iota2((s, s), 0), _iota2((s, s), 1)
        causal, causal_t = ri >= ci, ri <= ci
        last_row = _iota2((n, 128), 0) == n - 1
        for h in range(HG_HEADS):
            hs = slice(128 * h, 128 * h + 128)
            o = opre_ref[:, hs].astype(F32)
            gate = hg_ref[:, hs].astype(F32)
            dout = dob_ref[:, hs].astype(F32)
            sgate = _silu(gate)
            do, dnw = _rms_bwd(dout * sgate, o, nw_ref[:, hs])
            dnw_ref[:, hs] += dnw
            dpa_ref[:, 2 * D + 128 * h:2 * D + 128 * h + 128] = (
                dout * _rms_fwd(o, nw_ref[:, hs]) * _dsilu(gate)).astype(BF16)
            sth = sin_ref[:, hs].astype(F32)
            dsth = dst_ref[:, hs]
            v = hi_ref[:, hs]
            v0, v1 = v[0:s], v[s:n]
            do0, do1 = do[0:s], do[s:n]
            qe, ke = val["qe"][:, hs], val["ke"][:, hs]
            qm0, km0, qm1, km1 = val["qm0"][:, hs], val["km0"][:, hs], val["qm1"][:, hs], val["km1"][:, hs]
            qb, kb = val["qb"][:, hs], val["kb"][:, hs]
            dqe = _dot(do, sth)
            dstin = _dot_tn(do, qe)
            a00t = jnp.where(causal_t, _dot_nt(km0, qm0), 0.0)
            a11t = jnp.where(causal_t, _dot_nt(km1, qm1), 0.0)
            a10t = _dot_nt(kb, qb)
            dat00 = jnp.where(causal, _dot_nt(do0, v0), 0.0)
            dat11 = jnp.where(causal, _dot_nt(do1, v1), 0.0)
            dat10 = _dot_nt(do1, v0)
            dat00t = jnp.where(causal_t, _dot_nt(v0, do0), 0.0)
            dat11t = jnp.where(causal_t, _dot_nt(v1, do1), 0.0)
            dat10t = _dot_nt(v0, do1)
            dv = jnp.concatenate([_dot(a00t, do0) + _dot(a10t, do1), _dot(a11t, do1)], axis=0)
            dqm0, dkm0 = _dot(dat00, km0), _dot(dat00t, qm0)
            dqm1, dkm1 = _dot(dat11, km1), _dot(dat11t, qm1)
            dqb, dkb = _dot(dat10, kb), _dot(dat10t, qb)
            dke = _dot(v, dsth)
            dv += _dot_nt(ke, dsth)
            deb = jnp.sum(dsth * sth, axis=0, keepdims=True)
            dst_ref[:, hs] = dstin + dsth * fac["eb"][:, hs]
            dq = dqe * fac["eq"][:, hs] + jnp.concatenate(
                [dqm0 * fac["eq0"][:, hs], dqm1 * fac["eq1"][:, hs] + dqb * fac["eqb"][:, hs]], axis=0)
            dk = dke * fac["ek"][:, hs] + jnp.concatenate(
                [dkm0 * fac["ek0"][:, hs] + dkb * fac["ekb"][:, hs], dkm1 * fac["ek1"][:, hs]], axis=0)
            tke = dke * ke
            db = dqe * qe - tke + jnp.concatenate(
                [dqm0 * qm0 - dkm0 * km0 - dkb * kb, dqm1 * qm1 - dkm1 * km1 + dqb * qb], axis=0)
            dblast = jnp.sum(tke, axis=0, keepdims=True) + deb * fac["eb"][:, hs]
            db_ref[:, hs] = db + jnp.where(last_row, dblast, 0.0)
            dq_ref[:, hs] = dq
            dk_ref[:, hs] = dk
            dpa_ref[:, D + 128 * h:D + 128 * h + 128] = dv.astype(BF16)
        dg = _dot_hi(_triu(n), db_ref[...], 1)
        df = dg / f - dk_ref[...]
        dpa_ref[:, 3 * D:4 * D] = (df * (1.0 - lb) * sg * (1.0 - sg)).astype(BF16)
        dpa_ref[:, 0:D] = (dq_ref[...] * _dsilu(hq)).astype(BF16)
        dlb_ref[...] += jnp.sum(df * (1.0 - sg), axis=0, keepdims=True)

        @pl.when(i == nc - 1)
        def _():
            d0 = dlb_ref[...] * lb * (1.0 - lb)
            dhgl_ref[...] = jnp.concatenate([d0, -d0], axis=0)

    rev = lambda c: nc - 1 - c
    blk = lambda j: pl.BlockSpec((n, D), lambda c: (rev(c), j))
    payload = None if exchange is None else (_Exchange, exchange)
    args = [pa, phf, pa, pa, opre, sin, dob, hgl, nwx]
    in_specs, out_specs, out_shape, scratch = _carried_specs(
        payload,
        [blk(0), blk(0), blk(1), blk(2), blk(0), blk(0), blk(0), pl.BlockSpec((2, D), lambda c: (0, 0)),
         pl.BlockSpec((1, D), lambda c: (0, 0))],
        [pl.BlockSpec((n, 4 * D), lambda c: (rev(c), 0)), pl.BlockSpec((2, D), lambda c: (0, 0)),
         pl.BlockSpec((1, D), lambda c: (0, 0))],
        [jax.ShapeDtypeStruct((L, 4 * D), BF16), jax.ShapeDtypeStruct((2, D), F32), jax.ShapeDtypeStruct((1, D), F32)],
        [pltpu.VMEM((128, D), F32), pltpu.VMEM((n, D), F32), pltpu.VMEM((1, D), F32),
         pltpu.VMEM((n, D), F32), pltpu.VMEM((n, D), F32), pltpu.VMEM((n, D), F32)])
    return pl.pallas_call(
        _carried(body, len(args), 3, payload, _first_last(nc)), name="hgrn_bwd", grid=(nc,),
        in_specs=in_specs, out_specs=out_specs, out_shape=out_shape, scratch_shapes=scratch,
        compiler_params=_cparams(("arbitrary",)),
    )(*(args if exchange is None else args + [exchange]))


XA_SCALE = XA_DH ** -0.5


def _xa_probs(qh, kmh):
    sc = _dot_nt(qh, kmh) * XA_SCALE
    p = jnp.exp(sc - jnp.max(sc, axis=1, keepdims=True))
    return p / jnp.sum(p, axis=1, keepdims=True)


def _xattn_fwd(x1, nw, wq, kv, wo, *, tm):
    L = x1.shape[0]
    tm = min(tm, L)

    def body(x_ref, nw_ref, wq_ref, kv_ref, wo_ref, o_ref, ox_ref):
        x = x_ref[...]
        q = _dot(_rms_fwd(x, nw_ref[...]), wq_ref[...])
        for h in range(XA_HEADS):
            hs = slice(XA_DH * h, XA_DH * h + XA_DH)
            p = _xa_probs(q[:, hs], kv_ref[:, hs])
            ox_ref[:, hs] = _dot(p, kv_ref[:, D + XA_DH * h:D + XA_DH * h + XA_DH])
        o_ref[...] = x + _dot(ox_ref[...], wo_ref[...])

    full = lambda a: pl.BlockSpec(a.shape, lambda i: (0, 0))
    return pl.pallas_call(
        body, name="xattn_fwd", grid=(L // tm,),
        in_specs=[pl.BlockSpec((tm, D), lambda i: (i, 0)), full(nw), full(wq), full(kv), full(wo)],
        out_specs=pl.BlockSpec((tm, D), lambda i: (i, 0)),
        out_shape=jax.ShapeDtypeStruct((L, D), F32),
        scratch_shapes=[pltpu.VMEM((tm, D), F32)],
        compiler_params=_cparams(("parallel",)),
    )(x1, nw, wq, kv, wo)


def _xattn_bwd(x1, dx2, nw, wq, kv, wo, *, tm):
    L = x1.shape[0]
    tm = min(tm, L)

    def body(x_ref, dx2_ref, nw_ref, wq_ref, kv_ref, wo_ref, dx1_ref, dx1b_ref, h_ref, dq_ref, ox_ref, dkv_ref,
             dnw_ref, dqs_ref):
        @pl.when(pl.program_id(0) == 0)
        def _():
            dkv_ref[...] = jnp.zeros_like(dkv_ref)
            dnw_ref[...] = jnp.zeros_like(dnw_ref)

        x = x_ref[...]
        dx2 = dx2_ref[...]
        hn = _rms_fwd(x, nw_ref[...]).astype(BF16)
        h_ref[...] = hn
        q = _dot(hn, wq_ref[...])
        dox = _dot_nt(dx2, wo_ref[...])
        for h in range(XA_HEADS):
            hs = slice(XA_DH * h, XA_DH * h + XA_DH)
            vs = slice(D + XA_DH * h, D + XA_DH * h + XA_DH)
            qh, kmh, vmh, doxh = q[:, hs], kv_ref[:, hs], kv_ref[:, vs], dox[:, hs]
            p = _xa_probs(qh, kmh)
            ox_ref[:, hs] = _dot(p, vmh).astype(BF16)
            dp = _dot_nt(doxh, vmh)
            dkv_ref[:, vs] += _dot_tn(p, doxh)
            ds = p * (dp - jnp.sum(dp * p, axis=1, keepdims=True)) * XA_SCALE
            dqs_ref[:, hs] = _dot(ds, kmh)
            dkv_ref[:, hs] += _dot_tn(ds, qh)
        dq = dqs_ref[...]
        dq_ref[...] = dq.astype(BF16)
        dx, dnw = _rms_bwd(_dot_nt(dq, wq_ref[...]), x, nw_ref[...])
        dx1 = dx2 + dx
        dx1_ref[...] = dx1
        dx1b_ref[...] = dx1.astype(BF16)
        dnw_ref[...] += dnw

    full = lambda a: pl.BlockSpec(a.shape, lambda i: (0, 0))
    row = pl.BlockSpec((tm, D), lambda i: (i, 0))
    return pl.pallas_call(
        body, name="xattn_bwd", grid=(L // tm,),
        in_specs=[row, row, full(nw), full(wq), full(kv), full(wo)],
        out_specs=[row, row, row, row, row, pl.BlockSpec((MEM_LEN, 2 * D), lambda i: (0, 0)),
                   pl.BlockSpec((1, D), lambda i: (0, 0))],
        out_shape=[jax.ShapeDtypeStruct((L, D), F32), jax.ShapeDtypeStruct((L, D), BF16),
                   jax.ShapeDtypeStruct((L, D), BF16), jax.ShapeDtypeStruct((L, D), BF16),
                   jax.ShapeDtypeStruct((L, D), BF16),
                   jax.ShapeDtypeStruct((MEM_LEN, 2 * D), F32), jax.ShapeDtypeStruct((1, D), F32)],
        scratch_shapes=[pltpu.VMEM((tm, D), F32)],
        compiler_params=_cparams(("arbitrary",)),
    )(x1, dx2, nw, wq, kv, wo)


FFN_TF = 256


def _ffn_fwd(x2, nw, wg, wu, wd, *, tm):
    L = x2.shape[0]
    tm = min(tm, L)
    tf = FFN_TF
    nf = FFN // tf

    def body(x_ref, nw_ref, wg_ref, wu_ref, wd_ref, o_ref, hn_ref, g_ref, u_ref, h_ref, acc_ref):
        j = pl.program_id(1)

        @pl.when(j == 0)
        def _():
            hn = _rms_fwd(x_ref[...], nw_ref[...]).astype(BF16)
            h_ref[...] = hn
            hn_ref[...] = hn
            acc_ref[...] = jnp.zeros_like(acc_ref)

        h = h_ref[...]
        g = _dot_nt(h, wg_ref[...]).astype(BF16)
        u = _dot_nt(h, wu_ref[...]).astype(BF16)
        g_ref[...] = g
        u_ref[...] = u
        acc_ref[...] += _dot(_silu(g.astype(F32)) * u.astype(F32), wd_ref[...])

        @pl.when(j == nf - 1)
        def _():
            o_ref[...] = x_ref[...] + acc_ref[...]

    row = pl.BlockSpec((tm, D), lambda i, j: (i, 0))
    colblk = pl.BlockSpec((tm, tf), lambda i, j: (i, j))
    return pl.pallas_call(
        body, name="ffn_fwd", grid=(L // tm, nf),
        in_specs=[row, pl.BlockSpec((1, D), lambda i, j: (0, 0)),
                  pl.BlockSpec((tf, D), lambda i, j: (j, 0)), pl.BlockSpec((tf, D), lambda i, j: (j, 0)),
                  pl.BlockSpec((tf, D), lambda i, j: (j, 0))],
        out_specs=[row, row, colblk, colblk],
        out_shape=[jax.ShapeDtypeStruct((L, D), F32), jax.ShapeDtypeStruct((L, D), BF16),
                   jax.ShapeDtypeStruct((L, FFN), BF16), jax.ShapeDtypeStruct((L, FFN), BF16)],
        scratch_shapes=[pltpu.VMEM((tm, D), BF16), pltpu.VMEM((tm, D), F32)],
        compiler_params=_cparams(("parallel", "arbitrary")),
    )(x2, nw, wg, wu, wd)


def _ffn_bwd(x2, dx3, g, u, nw, wg, wu, wd, *, tm):
    L = x2.shape[0]
    tm = min(tm, L)
    tf = FFN_TF
    nf = FFN // tf

    def body(x_ref, dx3_ref, g_ref, u_ref, nw_ref, wg_ref, wu_ref, wd_ref,
             dx2_ref, a_ref, dg_ref, du_ref, dnw_ref, d3_ref, acc_ref):
        i, j = pl.program_id(0), pl.program_id(1)

        @pl.when(j == 0)
        def _():
            d3_ref[...] = dx3_ref[...].astype(BF16)
            acc_ref[...] = jnp.zeros_like(acc_ref)

        g = g_ref[...].astype(F32)
        u = u_ref[...].astype(F32)
        s = _sigmoid(g)
        sg = g * s
        a_ref[...] = (sg * u).astype(BF16)
        da = _dot_nt(d3_ref[...], wd_ref[...])
        dg = (da * u * (s + sg * (1.0 - s))).astype(BF16)
        du = (da * sg).astype(BF16)
        dg_ref[...] = dg
        du_ref[...] = du
        acc_ref[...] += _dot(dg, wg_ref[...]) + _dot(du, wu_ref[...])

        @pl.when(jnp.logical_and(i == 0, j == 0))
        def _():
            dnw_ref[...] = jnp.zeros_like(dnw_ref)

        @pl.when(j == nf - 1)
        def _():
            dx, dnw = _rms_bwd(acc_ref[...], x_ref[...], nw_ref[...])
            dx2_ref[...] = dx3_ref[...] + dx
            dnw_ref[...] += dnw

    row = pl.BlockSpec((tm, D), lambda i, j: (i, 0))
    colblk = pl.BlockSpec((tm, tf), lambda i, j: (i, j))
    return pl.pallas_call(
        body, name="ffn_bwd", grid=(L // tm, nf),
        in_specs=[row, row, colblk, colblk, pl.BlockSpec((1, D), lambda i, j: (0, 0)),
                  pl.BlockSpec((tf, D), lambda i, j: (j, 0)), pl.BlockSpec((tf, D), lambda i, j: (j, 0)),
                  pl.BlockSpec((tf, D), lambda i, j: (j, 0))],
        out_specs=[row, colblk, colblk, colblk, pl.BlockSpec((1, D), lambda i, j: (0, 0))],
        out_shape=[jax.ShapeDtypeStruct((L, D), F32), jax.ShapeDtypeStruct((L, FFN), BF16),
                   jax.ShapeDtypeStruct((L, FFN), BF16), jax.ShapeDtypeStruct((L, FFN), BF16),
                   jax.ShapeDtypeStruct((1, D), F32)],
        scratch_shapes=[pltpu.VMEM((tm, D), BF16), pltpu.VMEM((tm, D), F32)],
        compiler_params=_cparams(("arbitrary", "arbitrary"), vmem_mb=56),
    )(x2, dx3, g, u, nw, wg, wu, wd)


def _final(x3, tgt, nw, *, tm):
    L = x3.shape[0]
    tm = min(tm, L)

    def body(x_ref, t_ref, nw_ref, dx_ref, dxb_ref, loss_ref, dnw_ref):
        @pl.when(pl.program_id(0) == 0)
        def _():
            loss_ref[...] = jnp.zeros_like(loss_ref)
            dnw_ref[...] = jnp.zeros_like(dnw_ref)

        x = x_ref[...]
        w = nw_ref[...]
        err = _rms_fwd(x, w) - t_ref[...]
        part = 0.5 * jnp.sum(jnp.sum(err * err, axis=1, keepdims=True), axis=0, keepdims=True) * (1.0 / D)
        loss_ref[...] += jnp.where(_iota2((1, 128), 1) == 0, part, 0.0)
        dx, dnw = _rms_bwd(err * (1.0 / D), x, w)
        dx_ref[...] = dx
        dxb_ref[...] = dx.astype(BF16)
        dnw_ref[...] += dnw

    row = pl.BlockSpec((tm, D), lambda i: (i, 0))
    return pl.pallas_call(
        body, name="final_loss", grid=(L // tm,),
        in_specs=[row, row, pl.BlockSpec((1, D), lambda i: (0, 0))],
        out_specs=[row, row, pl.BlockSpec((1, 128), lambda i: (0, 0)), pl.BlockSpec((1, D), lambda i: (0, 0))],
        out_shape=[jax.ShapeDtypeStruct((L, D), F32), jax.ShapeDtypeStruct((L, D), BF16),
                   jax.ShapeDtypeStruct((1, 128), F32), jax.ShapeDtypeStruct((1, D), F32)],
        compiler_params=_cparams(("arbitrary",)),
    )(x3, tgt, nw)


def _adam_update(g, w, m, v):
    c1 = 1.0 / (1.0 - ADAM_B1 ** ADAM_STEP)
    c2 = 1.0 / (1.0 - ADAM_B2 ** ADAM_STEP)
    nm = ADAM_B1 * m + (1.0 - ADAM_B1) * g
    nv = ADAM_B2 * v + (1.0 - ADAM_B2) * (g * g)
    return -ADAM_LR * ((nm * c1) / (jnp.sqrt(nv * c2) + ADAM_EPS) + ADAM_WD * w), nm, nv


def _adamw_small(tot, conv_w_grad, w, m, v):
    names = [n for n, _, _ in SMALL if n != "loss"]
    k = len(names)

    def body(tot_ref, cwg_ref, *refs):
        w_refs, m_refs, v_refs = refs[0:k], refs[k:2 * k], refs[2 * k:3 * k]
        g_refs, d_refs, nm_refs, nv_refs = (refs[(3 + j) * k:(4 + j) * k] for j in range(4))
        for i, n in enumerate(names):
            row, nr, nc = SMALL_AT[n]
            g = cwg_ref[...] if n == "conv_w" else tot_ref[row:row + nr, 0:nc]
            d, nm, nv = _adam_update(g, w_refs[i][...], m_refs[i][...], v_refs[i][...])
            g_refs[i][...] = g
            d_refs[i][...] = d
            nm_refs[i][...] = nm
            nv_refs[i][...] = nv

    sds = [jax.ShapeDtypeStruct(t.shape, F32) for t in w]
    res = pl.pallas_call(body, name="adamw_small", out_shape=sds * 4)(tot, conv_w_grad, *w, *m, *v)
    return res[0:k], res[k:2 * k], res[2 * k:3 * k], res[3 * k:4 * k]


def _adamw(parts, w, m, v, *, tr, name):
    n_parts, R, C = parts.shape
    tr = min(tr, R)

    def body(p_ref, w_ref, m_ref, v_ref, g_ref, d_ref, nm_ref, nv_ref):
        g = p_ref[0]
        for k in range(1, n_parts):
            g = g + p_ref[k]
        d, nm, nv = _adam_update(g, w_ref[...], m_ref[...], v_ref[...])
        g_ref[...] = g
        nm_ref[...] = nm
        nv_ref[...] = nv
        d_ref[...] = d

    blk = pl.BlockSpec((tr, C), lambda i: (i, 0))
    sds = jax.ShapeDtypeStruct((R, C), F32)
    return pl.pallas_call(
        body, name=name, grid=(R // tr,),
        in_specs=[pl.BlockSpec((n_parts, tr, C), lambda i: (0, i, 0)), blk, blk, blk],
        out_specs=[blk, blk, blk, blk], out_shape=[sds, sds, sds, sds],
        compiler_params=_cparams(("parallel",)),
    )(parts, w, m, v)


def _position():
    return lax.axis_index("x"), lax.axis_index("y"), lax.axis_index("c")


def _comm_scratch():
    return [pltpu.SemaphoreType.DMA((7,)), pltpu.SemaphoreType.DMA((7,)), pltpu.SemaphoreType.DMA]


class _Gather:
    def __init__(self, x_ref, out_ref, send_sems, recv_sems, local_sem):
        x, y, c = _position()
        me, sibling = (x, y, c), (x, y, 1 - c)
        chips = [(1 - x, y), (x, 1 - y), (1 - x, 1 - y)]

        def rows(px, py, pc):
            return out_ref.at[4 * px + 2 * py + pc]

        def copy(k, block, to, src=None):
            return pltpu.make_async_remote_copy(
                src_ref=rows(*block) if src is None else src, dst_ref=rows(*block),
                send_sem=send_sems.at[k], recv_sem=recv_sems.at[k], device_id=to, device_id_type=MESH)

        self.mine = pltpu.make_async_copy(x_ref, rows(*me), local_sem)
        self.first = [copy(0, me, sibling, src=x_ref)]
        self.first += [copy(1 + j, me, (*chip, c), src=x_ref) for j, chip in enumerate(chips)]
        self.passed = [copy(4 + j, (*chip, c), sibling) for j, chip in enumerate(chips)]
        self.from_chips = [copy(1 + j, (*chip, c), me) for j, chip in enumerate(chips)]
        self.from_sibling = [copy(0, sibling, me)] + [copy(4 + j, (*chip, 1 - c), me) for j, chip in enumerate(chips)]

    def start(self):
        self.mine.start()
        for cp in self.first:
            cp.start()

    def forward(self):
        for got, cp in zip(self.from_chips, self.passed):
            got.wait_recv()
            cp.start()

    def finish(self):
        for got in self.from_sibling:
            got.wait_recv()
        for cp in self.first + self.passed:
            cp.wait_send()
        self.mine.wait()


class _Exchange:
    def __init__(self, g_ref, out_ref, send_sems, recv_sems, local_sem):
        x, y, c = _position()
        me = 4 * x + 2 * y + c
        self.mine = pltpu.make_async_copy(g_ref.at[me], out_ref.at[me], local_sem)
        self.copies = []
        for k in range(1, N_DEV):
            px = 1 - x if k & 4 else x
            py = 1 - y if k & 2 else y
            pc = 1 - c if k & 1 else c
            self.copies.append(pltpu.make_async_remote_copy(
                src_ref=g_ref.at[4 * px + 2 * py + pc], dst_ref=out_ref.at[me],
                send_sem=send_sems.at[k - 1], recv_sem=recv_sems.at[k - 1],
                device_id=(px, py, pc), device_id_type=MESH))

    def start(self):
        self.mine.start()
        for cp in self.copies:
            cp.start()

    def finish(self):
        for cp in self.copies:
            cp.wait()
        self.mine.wait()


def _allgather(xp):
    R, C = xp.shape

    def body(x_ref, out_ref, send_sems, recv_sems, local_sem):
        g = _Gather(x_ref, out_ref, send_sems, recv_sems, local_sem)
        g.start()
        g.forward()
        g.finish()

    return pl.pallas_call(
        body, name="allgather_w_in",
        out_shape=jax.ShapeDtypeStruct((N_DEV, R, C), xp.dtype),
        in_specs=[pl.BlockSpec(memory_space=pltpu.HBM)], out_specs=pl.BlockSpec(memory_space=pltpu.HBM),
        scratch_shapes=_comm_scratch(),
    )(xp)


def _carried(body, n_in, n_out, payload, phases):
    if payload is None:
        return body
    kind = payload[0]

    def new_body(*refs):
        ins, src_ref = refs[:n_in], refs[n_in]
        outs, dst_ref = refs[n_in + 1:n_in + 1 + n_out], refs[n_in + 1 + n_out]
        scratch, sems = refs[n_in + 2 + n_out:-3], refs[-3:]

        def run(before):
            for when, action, is_before in phases:
                if is_before == before:
                    @pl.when(when())
                    def _():
                        action(kind(src_ref, dst_ref, *sems))

        run(True)
        body(*ins, *outs, *scratch)
        run(False)

    return new_body


def _carried_specs(payload, in_specs, out_specs, out_shape, scratch):
    if payload is None:
        return in_specs, out_specs, out_shape, scratch
    kind, arr = payload
    landing = (N_DEV,) + arr.shape if kind is _Gather else arr.shape
    hbm = pl.BlockSpec(memory_space=pltpu.HBM)
    return (in_specs + [hbm], out_specs + [hbm], out_shape + [jax.ShapeDtypeStruct(landing, arr.dtype)],
            scratch + _comm_scratch())


def _small_allreduce(sp):
    R, C = sp.shape

    def body(s_ref, out_ref, buf_ref, send_sems, recv_sems):
        x, y, c = _position()
        me = 4 * x + 2 * y + c
        buf_ref[me] = s_ref[...]
        copies = []
        for k in range(1, N_DEV):
            px = 1 - x if k & 4 else x
            py = 1 - y if k & 2 else y
            pc = 1 - c if k & 1 else c
            cp = pltpu.make_async_remote_copy(
                src_ref=s_ref, dst_ref=buf_ref.at[me], send_sem=send_sems.at[k - 1], recv_sem=recv_sems.at[k - 1],
                device_id=(px, py, pc), device_id_type=MESH)
            cp.start()
            copies.append(cp)
        for cp in copies:
            cp.wait()
        tot = buf_ref[0]
        for k in range(1, N_DEV):
            tot = tot + buf_ref[k]
        out_ref[...] = tot

    return pl.pallas_call(
        body, name="allreduce_small",
        out_shape=jax.ShapeDtypeStruct((R, C), F32),
        in_specs=[pl.BlockSpec(memory_space=pltpu.VMEM)], out_specs=pl.BlockSpec(memory_space=pltpu.VMEM),
        scratch_shapes=[pltpu.VMEM((N_DEV, R, C), F32), pltpu.SemaphoreType.DMA((7,)), pltpu.SemaphoreType.DMA((7,))],
    )(sp)


def _local_step(x, mem, tgt, wt, small, dist=None):
    w_in = wt["w_in"]
    zpad = jnp.zeros((WB - D - SSD_HEADS, D), BF16)
    w_a = jnp.concatenate([w_in[2576:3600], w_in[4624:6672], w_in[3600:4624]], axis=0)
    w_b = jnp.concatenate([w_in[0:D], w_in[2560:2576], zpad], axis=0)
    w_c = w_in[D:2560]
    a_log, d_skip = small["a_log"], small["d_skip"]
    avec = jnp.pad(-jnp.exp(a_log), ((0, 0), (0, 128 - SSD_HEADS)))
    aexp = jnp.repeat(-jnp.exp(a_log), SSD_P, axis=1)
    dexp = jnp.repeat(d_skip, SSD_P, axis=1)
    dtb = jnp.pad(small["dt_bias"], ((0, 0), (0, 128 - SSD_HEADS)))
    emat = (lax.broadcasted_iota(jnp.int32, (128, D), 0) == lax.broadcasted_iota(jnp.int32, (128, D), 1) // SSD_P
            ).astype(F32)
    hg_nwx = jnp.tile(small["hg_norm_w"], (1, HG_HEADS))
    hgl = small["hg_lower_bounds"]
    nfw = small["norm_final_w"].reshape(1, D)

    received = {}
    pieces = (lambda group, grads: None) if dist is None else dist["pieces"]
    pa, phf, pz, pdt, pc, hn_mix, *got = _inproj(x, small["norm_mix_w"], w_a, w_b, w_c, tm=256,
                                                 gather=None if dist is None else dist["rest_pack"])
    if got:
        wt = {**wt, **dist["unpack_rest"](got[0])}
    xbc = _conv_fwd(pc, small["conv_w"], small["conv_b"], tm=512)
    ya, ypre, ssd_sin = _ssd_fwd(xbc, pz, pdt, dtb, avec, aexp, dexp, small["ssd_norm_w"], emat)
    ob, opre, hg_sin = _hgrn_fwd(pa, phf, hgl, hg_nwx)
    x1 = _mm2_res(x, ya, ob, wt["w_out"], tm=512, name="outproj")
    kvb, mn = _norm_mm(mem, small["norm_mem_w"], wt["xa_wkv"], tm=256, tn=1024, name="mem_kv", emit_h=True,
                       out_dtype=BF16)
    x2 = _xattn_fwd(x1, small["norm_xa_w"], wt["xa_wq"], kvb, wt["xa_wo"], tm=512)
    x3, hn_ffn, gate, up = _ffn_fwd(x2, small["norm_ffn_w"], wt["ffn_w_gate"], wt["ffn_w_up"], wt["ffn_w_down"],
                                    tm=1024)

    dx3, dx3b, loss, g_nf = _final(x3, tgt, nfw, tm=512)
    dx2, act, dg, du, g_nffn = _ffn_bwd(x2, dx3, gate, up, small["norm_ffn_w"], wt["ffn_w_gate"], wt["ffn_w_up"],
                                        wt["ffn_w_down"], tm=1024)
    g_wg = _dw(dg, hn_ffn, tM=1408, tN=1024, tl=512, name="dw_gate")
    g_wu = _dw(du, hn_ffn, tM=1408, tN=1024, tl=512, name="dw_up")
    g_wd = _dw(act, dx3b, tM=1408, tN=1024, tl=512, name="dw_down")
    dx1, dx1b, hn_xa, dq, ox, dkv, g_nxa = _xattn_bwd(x1, dx2, small["norm_xa_w"], wt["xa_wq"], kvb, wt["xa_wo"],
                                                      tm=512)
    dkvb = dkv.astype(BF16)
    g_wq = _dw(hn_xa, dq, tM=1024, tN=1024, tl=512, name="dw_q")
    g_wo = _dw(ox, dx2, tM=1024, tN=1024, tl=512, name="dw_o")
    g_wkv = _dw(dkvb, mn, tM=1024, tN=1024, tl=256, name="dw_kv")
    _, g_nmem = _mm_normbwd([dkvb], [wt["xa_wkv"]], mem, small["norm_mem_w"], None, tm=256, name="mem_bwd")
    dya, dob = _mm_nt2(dx1, wt["w_out"], tm=512, name="outproj_bwd")
    g_wout = jnp.concatenate([_dw(ya, dx1b, tM=1024, tN=1024, tl=512, name="dw_out_a"),
                              _dw(ob, dx1b, tM=1024, tN=1024, tl=512, name="dw_out_b")], axis=0)
    ffn_grads = {"ffn_w_gate": g_wg, "ffn_w_up": g_wu, "ffn_w_down": g_wd}
    mid_grads = {"w_out": g_wout, "xa_wq": g_wq, "xa_wkv": g_wkv, "xa_wo": g_wo}
    dpa, g_hgl, g_hgn_x, *got = _hgrn_bwd(pa, phf, opre, hg_sin, dob, hgl, hg_nwx,
                                          exchange=pieces("ffn", ffn_grads))
    received["ffn"] = got[0] if got else None
    dpb, dxbc, g_ssdn, g_dtb, g_alog, g_dx, *got = _ssd_bwd(
        xbc, pz, pdt, ypre, ssd_sin, dya, dtb, avec, aexp, dexp, small["ssd_norm_w"], emat,
        exchange=pieces("mid", mid_grads))
    received["mid"] = got[0] if got else None
    dpc, g_cw, g_cb = _conv_bwd(pc, dxbc, small["conv_w"], small["conv_b"], tm=512)
    g_wa = _dw(dpa, hn_mix, tM=1024, tN=1024, tl=512, name="dw_in_a")
    g_wb = _dw(dpb, hn_mix, tM=384, tN=1024, tl=512, name="dw_in_b")
    g_wc = _dw(dpc, hn_mix, tM=512, tN=1024, tl=512, name="dw_in_c")
    g_win = jnp.concatenate([g_wb[0:D], g_wc, g_wb[D:D + SSD_HEADS], g_wa[0:D], g_wa[3 * D:4 * D], g_wa[D:3 * D]],
                            axis=0)
    grad_x, g_nmix, *got = _mm_normbwd([dpa, dpb, dpc], [w_a, w_b, w_c], x, small["norm_mix_w"], dx1, tm=256,
                                         name="inproj_bwd", exchange=pieces("in", {"w_in": g_win}))
    received["in"] = got[0] if got else None

    big = {"w_in": g_win, **mid_grads, **ffn_grads}
    smallg = {
        "norm_mix_w": g_nmix, "conv_w": g_cw, "conv_b": g_cb, "dt_bias": g_dtb[:, 0:SSD_HEADS],
        "a_log": g_alog[:, 0:SSD_HEADS], "d_skip": g_dx.reshape(SSD_HEADS, SSD_P).sum(axis=1).reshape(1, SSD_HEADS),
        "ssd_norm_w": g_ssdn, "hg_lower_bounds": g_hgl,
        "hg_norm_w": g_hgn_x.reshape(HG_HEADS, HG_K).sum(axis=0).reshape(1, HG_K),
        "norm_xa_w": g_nxa, "norm_mem_w": g_nmem, "norm_ffn_w": g_nffn, "norm_final_w": g_nf,
        "loss": loss[:, 0:1]}
    return grad_x, big, smallg, received


COL_SHARDED = ("w_in", "xa_wkv", "ffn_w_gate", "ffn_w_up")


def _pad_rows(t, rows):
    return jnp.pad(t, [(0, 0)] * (t.ndim - 2) + [(0, rows - t.shape[-2]), (0, 0)])


def _group_fill(parts, group, lead, dtype):
    used = sum(p.shape[-2] for p in parts)
    if used < GROUP_ROWS[group]:
        parts.append(jnp.zeros(lead + (GROUP_ROWS[group] - used, D), dtype))
    return parts


def _pack_group(shards, group, dtype, extra=None):
    parts = [_pad_rows(shards[n].astype(dtype).reshape(r, D), _rows_padded(r)) for n, r in GROUPS[group]]
    if extra is not None:
        parts.append(extra)
    return jnp.concatenate(_group_fill(parts, group, (), dtype), axis=0)


def _unpack_group(packed, group, shapes):
    out, off = {}, 0
    for n, r in GROUPS[group]:
        t = packed[off:off + r]
        out[n] = (t.T if n in COL_SHARDED else t).reshape(shapes[n])
        off += _rows_padded(r)
    return out


def _row_shards(d):
    return {n: d[n][0].T if n in COL_SHARDED else d[n][0] for n in BIG}


def _unpack_gathered(gath, groups):
    out, base = {}, 0
    for group in groups:
        off = base
        for n, r in GROUPS[group]:
            out[n] = gath[:, off:off + r].reshape(N_DEV * r, D)
            off += _rows_padded(r)
        base += GROUP_ROWS[group]
    return out


def _grad_pieces(group, grads):
    parts = [_pad_rows(grads[n].reshape(N_DEV, r, D), _rows_padded(r)) for n, r in GROUPS[group]]
    return jnp.concatenate(_group_fill(parts, group, (N_DEV,), F32), axis=1)


def _pack_small(vals):
    tot = None
    for n, r, c in SMALL:
        row = SMALL_AT[n][0]
        part = jnp.pad(vals[n].reshape(r, c), ((row, SMALL_ROWS - row - r), (0, SMALL_COLS - c)))
        tot = part if tot is None else tot + part
    return tot


WEIGHTS = ['norm_mix_w', 'w_in', 'conv_w', 'conv_b', 'dt_bias', 'a_log', 'd_skip', 'ssd_norm_w', 'hg_lower_bounds',
           'hg_norm_w', 'w_out', 'norm_xa_w', 'norm_mem_w', 'xa_wq', 'xa_wkv', 'xa_wo', 'norm_ffn_w', 'ffn_w_gate',
           'ffn_w_up', 'ffn_w_down', 'norm_final_w']
BIG = tuple(n for n, _ in PACK)


def kernel(x, mem, norm_mix_w, w_in, conv_w, conv_b, dt_bias, a_log, d_skip, ssd_norm_w, hg_lower_bounds, hg_norm_w, w_out, norm_xa_w, norm_mem_w, xa_wq, xa_wkv, xa_wo, norm_ffn_w, ffn_w_gate, ffn_w_up, ffn_w_down, norm_final_w, loss_target, m_norm_mix_w, m_w_in, m_conv_w, m_conv_b, m_dt_bias, m_a_log, m_d_skip, m_ssd_norm_w, m_hg_lower_bounds, m_hg_norm_w, m_w_out, m_norm_xa_w, m_norm_mem_w, m_xa_wq, m_xa_wkv, m_xa_wo, m_norm_ffn_w, m_ffn_w_gate, m_ffn_w_up, m_ffn_w_down, m_norm_final_w, v_norm_mix_w, v_w_in, v_conv_w, v_conv_b, v_dt_bias, v_a_log, v_d_skip, v_ssd_norm_w, v_hg_lower_bounds, v_hg_norm_w, v_w_out, v_norm_xa_w, v_norm_mem_w, v_xa_wq, v_xa_wkv, v_xa_wo, v_norm_ffn_w, v_ffn_w_gate, v_ffn_w_up, v_ffn_w_down, v_norm_final_w):
    args = dict(locals())
    w = {n: args[n] for n in WEIGHTS}
    mo = {n: args["m_" + n] for n in WEIGHTS}
    vo = {n: args["v_" + n] for n in WEIGHTS}
    me = 4 * lax.axis_index("x") + 2 * lax.axis_index("y") + lax.axis_index("c")

    big_sh = _row_shards(w)
    cw_bits = lax.bitcast_convert_type(conv_w[0], BF16).reshape(-1)
    cw_rows = jnp.pad(cw_bits, (0, CONV_BITS_ROWS * D - cw_bits.shape[0])).reshape(CONV_BITS_ROWS, D)
    gath = _allgather(_pack_group(big_sh, "in", BF16, extra=cw_rows))
    wt = _unpack_gathered(gath, ("in",))
    off = _rows_padded(GROUPS["in"][0][1])
    cw_all = lax.bitcast_convert_type(gath[:, off:off + 2].reshape(N_DEV, 2 * D)[:, 0:1536].reshape(N_DEV, 4, 192, 2),
                                      F32)
    conv_w_full = cw_all.transpose(1, 0, 2).reshape(4, 1536)

    small = {n: w[n][0] if w[n].ndim == 3 else w[n] for n in WEIGHTS if n not in BIG}
    small["conv_w"] = conv_w_full
    small["hg_lower_bounds"] = hg_lower_bounds
    dist = {"rest_pack": jnp.concatenate([_pack_group(big_sh, "mid", BF16), _pack_group(big_sh, "ffn", BF16)], axis=0),
            "unpack_rest": lambda g: _unpack_gathered(g, ("mid", "ffn")),
            "pieces": _grad_pieces}
    grad_x, _, gsmall, received = _local_step(x[0], mem[0], loss_target[0], wt, small, dist)

    shapes = {n: w[n].shape for n in BIG}
    m_sh, v_sh = _row_shards(mo), _row_shards(vo)
    out_g, out_d, out_m, out_v = {}, {}, {}, {}
    for group in GROUPS:
        wp = _pack_group(big_sh, group, F32)
        mp = _pack_group(m_sh, group, F32)
        vp = _pack_group(v_sh, group, F32)
        packed = _adamw(received[group], wp, mp, vp, tr=ADAM_ROWS[group], name="adamw_" + group)
        for dst, src in zip((out_g, out_d, out_m, out_v), packed):
            dst.update(_unpack_group(src, group, shapes))

    tot = _small_allreduce(_pack_small(gsmall))
    loss = tot[SMALL_AT["loss"][0], 0]
    cw_row = SMALL_AT["conv_w"][0]
    conv_w_grad = lax.dynamic_slice(tot, (cw_row, me * 192), (4, 192))
    names = [n for n, _, _ in SMALL if n != "loss"]
    as2d = lambda t: t.reshape(t.shape[-2:] if t.ndim > 1 else (1, t.shape[0]))
    small_out = _adamw_small(tot, conv_w_grad, *[[as2d(d[n]) for n in names] for d in (w, mo, vo)])
    for dst, src in zip((out_g, out_d, out_m, out_v), small_out):
        dst.update({n: t.reshape(w[n].shape) for n, t in zip(names, src)})
    return (loss, grad_x[None], *[out_g[n] for n in WEIGHTS], *[out_d[n] for n in WEIGHTS],
            *[out_m[n] for n in WEIGHTS], *[out_v[n] for n in WEIGHTS])
```

```python
import jax
import jax.numpy as jnp
from jax import lax
from jax.experimental import pallas as pl
from jax.experimental.pallas import tpu as pltpu

F32, BF16 = jnp.float32, jnp.bfloat16
MESH = pl.DeviceIdType.MESH

D = 1024
EPS = 1e-6
SSD_HEADS, SSD_P, SSD_N, SSD_Q = 16, 64, 128, 128
HG_HEADS, HG_K, HG_STEP, HG_SUB = 8, 128, 128, 64
XA_HEADS, XA_DH, MEM_LEN = 4, 256, 256
FFN = 2816
N_IN = 6672
N_DEV = 8
WA, WB, WC = 4096, 1152, 1536
ADAM_LR, ADAM_B1, ADAM_B2, ADAM_EPS, ADAM_WD, ADAM_STEP = 0.001, 0.9, 0.999, 1e-08, 0.01, 10
VMEM_MB = 2 ** 20

PACK = (("w_in", 834), ("w_out", 256), ("xa_wq", 128), ("xa_wkv", 256), ("xa_wo", 128),
        ("ffn_w_gate", 352), ("ffn_w_up", 352), ("ffn_w_down", 352))
ROW_TILE = 16
GROUPS = {"in": PACK[0:1], "mid": PACK[1:5], "ffn": PACK[5:8]}
GROUP_ROWS = {"in": 896, "mid": 768, "ffn": 1056}
ADAM_ROWS = {"in": 128, "mid": 128, "ffn": 176}
CONV_BITS_ROWS = ROW_TILE


def _rows_padded(r):
    return -(-r // ROW_TILE) * ROW_TILE

SMALL = (("norm_mix_w", 1, 1024), ("conv_w", 4, 1536), ("conv_b", 1, 1536), ("dt_bias", 1, 16), ("a_log", 1, 16),
         ("d_skip", 1, 16), ("ssd_norm_w", 1, 1024), ("hg_lower_bounds", 2, 1024), ("hg_norm_w", 1, 128),
         ("norm_xa_w", 1, 1024), ("norm_mem_w", 1, 1024), ("norm_ffn_w", 1, 1024), ("norm_final_w", 1, 1024),
         ("loss", 1, 1))
SMALL_COLS = 1536
SMALL_ROWS = 24
SMALL_AT = {n: (sum(q for _, q, _ in SMALL[:i]), r, c) for i, (n, r, c) in enumerate(SMALL)}


def _cparams(sem=None, vmem_mb=48):
    return pltpu.CompilerParams(dimension_semantics=sem, vmem_limit_bytes=vmem_mb * VMEM_MB)


def _dot(a, b):
    return jnp.dot(a.astype(BF16), b.astype(BF16), preferred_element_type=F32)


def _dot_nt(a, b):
    return lax.dot_general(a.astype(BF16), b.astype(BF16), (((1,), (1,)), ((), ())), preferred_element_type=F32)


def _dot_tn(a, b):
    return lax.dot_general(a.astype(BF16), b.astype(BF16), (((0,), (0,)), ((), ())), preferred_element_type=F32)


def _split3(a):
    a1 = a.astype(BF16)
    r1 = a - a1.astype(F32)
    a2 = r1.astype(BF16)
    a3 = (r1 - a2.astype(F32)).astype(BF16)
    return a1, a2, a3


def _dot_hi(a, b, general=0):
    if general == 0:
        return sum(jnp.dot(t, b.astype(BF16), preferred_element_type=F32) for t in _split3(a))
    return sum(jnp.dot(a.astype(BF16), t, preferred_element_type=F32) for t in _split3(b))


def _dot_nt_hi(a, b):
    return sum(_dot_nt(t, b) for t in _split3(a))


def _sigmoid(x):
    return 1.0 / (1.0 + jnp.exp(-x))


def _sigmoid_gate(x):
    return pl.reciprocal(1.0 + jnp.exp(-x), approx=True)


def _silu(x):
    return x * _sigmoid_gate(x)


def _dsilu(x):
    s = _sigmoid_gate(x)
    return s * (1.0 + x * (1.0 - s))


def _softplus(x):
    return jnp.maximum(x, 0.0) + jnp.log(1.0 + jnp.exp(-jnp.abs(x)))


def _rms_fwd(x, w):
    r = lax.rsqrt(jnp.mean(x * x, axis=1, keepdims=True) + EPS)
    return x * r * w


def _rms_bwd(dy, x, w):
    r = lax.rsqrt(jnp.mean(x * x, axis=1, keepdims=True) + EPS)
    xh = x * r
    g = dy * w
    dx = r * (g - xh * jnp.mean(g * xh, axis=1, keepdims=True))
    return dx, jnp.sum(dy * xh, axis=0, keepdims=True)


def _iota2(shape, dim):
    return lax.broadcasted_iota(jnp.int32, shape, dim)


def _tril(n):
    return (_iota2((n, n), 0) >= _iota2((n, n), 1)).astype(F32)


def _triu(n):
    return (_iota2((n, n), 0) <= _iota2((n, n), 1)).astype(F32)


def _norm_mm(x, nw, w, *, tm, tn, name, emit_h=False, out_dtype=F32):
    L, K = x.shape
    N = w.shape[0]
    tm, tn = min(tm, L), min(tn, N)
    ni, nj = L // tm, N // tn

    def body(x_ref, nw_ref, w_ref, *rest):
        if emit_h:
            o_ref, h_ref, hs_ref = rest
        else:
            o_ref, hs_ref = rest

        @pl.when(pl.program_id(1) == 0)
        def _():
            h = _rms_fwd(x_ref[...], nw_ref[...]).astype(BF16)
            hs_ref[...] = h
            if emit_h:
                h_ref[...] = h

        o_ref[...] = _dot_nt(hs_ref[...], w_ref[...]).astype(out_dtype)

    out_shape = [jax.ShapeDtypeStruct((L, N), out_dtype)]
    out_specs = [pl.BlockSpec((tm, tn), lambda i, j: (i, j))]
    if emit_h:
        out_shape.append(jax.ShapeDtypeStruct((L, K), BF16))
        out_specs.append(pl.BlockSpec((tm, K), lambda i, j: (i, 0)))
    res = pl.pallas_call(
        body, name=name, grid=(ni, nj),
        in_specs=[pl.BlockSpec((tm, K), lambda i, j: (i, 0)), pl.BlockSpec((1, K), lambda i, j: (0, 0)),
                  pl.BlockSpec((tn, K), lambda i, j: (j, 0))],
        out_specs=out_specs, out_shape=out_shape, scratch_shapes=[pltpu.VMEM((tm, K), BF16)],
        compiler_params=_cparams(("parallel", "arbitrary")),
    )(x, nw, w)
    return res if len(res) > 1 else res[0]


def _inproj(x, nw, w_a, w_b, w_c, *, tm, gather=None):
    L, K = x.shape
    tm = min(tm, L)
    ni = L // tm

    def body(x_ref, nw_ref, wa_ref, wb_ref, wc_ref, pa_ref, hf_ref, z_ref, dt_ref, pc_ref, h_ref):
        h = _rms_fwd(x_ref[...], nw_ref[...]).astype(BF16)
        h_ref[...] = h
        pa_ref[...] = _dot_nt(h, wa_ref[0:3 * D, :]).astype(BF16)
        hf_ref[...] = _dot_nt(h, wa_ref[3 * D:4 * D, :])
        pb = _dot_nt(h, wb_ref[...])
        z_ref[...] = pb[:, 0:D].astype(BF16)
        dt_ref[...] = pb[:, D:D + 128]
        pc_ref[...] = _dot_nt(h, wc_ref[...]).astype(BF16)

    row = lambda n: pl.BlockSpec((tm, n), lambda i: (i, 0))
    full = lambda a: pl.BlockSpec(a.shape, lambda i: (0, 0))
    at = lambda i: lambda: pl.program_id(0) == i
    payload = None if gather is None else (_Gather, gather)
    phases = [(at(0), _Gather.start, True), (at(ni // 2), _Gather.forward, True), (at(ni - 1), _Gather.finish, False)]
    in_specs, out_specs, out_shape, scratch = _carried_specs(
        payload, [row(K), full(nw), full(w_a), full(w_b), full(w_c)],
        [row(3 * D), row(D), row(D), row(128), row(WC), row(K)],
        [jax.ShapeDtypeStruct((L, 3 * D), BF16), jax.ShapeDtypeStruct((L, D), F32),
         jax.ShapeDtypeStruct((L, D), BF16), jax.ShapeDtypeStruct((L, 128), F32),
         jax.ShapeDtypeStruct((L, WC), BF16), jax.ShapeDtypeStruct((L, K), BF16)], [])
    args = [x, nw, w_a, w_b, w_c]
    return pl.pallas_call(
        _carried(body, 5, 6, payload, phases), name="inproj", grid=(ni,),
        in_specs=in_specs, out_specs=out_specs, out_shape=out_shape, scratch_shapes=scratch,
        compiler_params=_cparams(("arbitrary",), vmem_mb=58),
    )(*(args if gather is None else args + [gather]))


def _mm2_res(res, a1, a2, w, *, tm, name):
    L, N = res.shape
    K = a1.shape[1]
    tm = min(tm, L)

    def body(r_ref, a1_ref, a2_ref, w_ref, o_ref):
        acc = jnp.dot(a1_ref[...], w_ref[0:K, :], preferred_element_type=F32)
        acc += jnp.dot(a2_ref[...], w_ref[K:2 * K, :], preferred_element_type=F32)
        o_ref[...] = r_ref[...] + acc

    return pl.pallas_call(
        body, name=name, grid=(L // tm,),
        in_specs=[pl.BlockSpec((tm, N), lambda i: (i, 0)), pl.BlockSpec((tm, K), lambda i: (i, 0)),
                  pl.BlockSpec((tm, K), lambda i: (i, 0)), pl.BlockSpec((2 * K, N), lambda i: (0, 0))],
        out_specs=pl.BlockSpec((tm, N), lambda i: (i, 0)),
        out_shape=jax.ShapeDtypeStruct((L, N), F32),
        compiler_params=_cparams(("parallel",)),
    )(res, a1, a2, w)


def _mm_nt2(a, w, *, tm, name):
    L, N = a.shape
    K = w.shape[0] // 2
    tm = min(tm, L)

    def body(a_ref, w_ref, o1_ref, o2_ref):
        av = a_ref[...].astype(BF16)
        o1_ref[...] = _dot_nt(av, w_ref[0:K, :]).astype(BF16)
        o2_ref[...] = _dot_nt(av, w_ref[K:2 * K, :]).astype(BF16)

    return pl.pallas_call(
        body, name=name, grid=(L // tm,),
        in_specs=[pl.BlockSpec((tm, N), lambda i: (i, 0)), pl.BlockSpec((2 * K, N), lambda i: (0, 0))],
        out_specs=[pl.BlockSpec((tm, K), lambda i: (i, 0)), pl.BlockSpec((tm, K), lambda i: (i, 0))],
        out_shape=[jax.ShapeDtypeStruct((L, K), BF16), jax.ShapeDtypeStruct((L, K), BF16)],
        compiler_params=_cparams(("parallel",)),
    )(a, w)


def _dw(a, b, *, tM, tN, tl, name):
    L, M = a.shape
    N = b.shape[1]
    tM, tN, tl = min(tM, M), min(tN, N), min(tl, L)

    def body(a_ref, b_ref, o_ref):
        @pl.when(pl.program_id(2) == 0)
        def _():
            o_ref[...] = jnp.zeros_like(o_ref)

        o_ref[...] += _dot_tn(a_ref[...], b_ref[...])

    return pl.pallas_call(
        body, name=name, grid=(M // tM, N // tN, L // tl),
        in_specs=[pl.BlockSpec((tl, tM), lambda i, j, l: (l, i)), pl.BlockSpec((tl, tN), lambda i, j, l: (l, j))],
        out_specs=pl.BlockSpec((tM, tN), lambda i, j, l: (i, j)),
        out_shape=jax.ShapeDtypeStruct((M, N), F32),
        compiler_params=_cparams(("parallel", "parallel", "arbitrary")),
    )(a, b)


def _first_last(n):
    return [(lambda: pl.program_id(0) == 0, _Exchange.start, True),
            (lambda: pl.program_id(0) == n - 1, _Exchange.finish, False)]


def _mm_normbwd(a_list, w_list, x, nw, res, *, tm, name, exchange=None):
    L, Dm = x.shape
    tm = min(tm, L)
    n = len(a_list)
    has_res = res is not None

    def body(*refs):
        a_refs, w_refs = refs[:n], refs[n:2 * n]
        x_ref, nw_ref = refs[2 * n], refs[2 * n + 1]
        k = 2 * n + 2
        r_ref = refs[k] if has_res else None
        dx_ref, dnw_ref = refs[k + has_res], refs[k + has_res + 1]
        dh = _dot(a_refs[0][...], w_refs[0][...])
        for a_ref, w_ref in zip(a_refs[1:], w_refs[1:]):
            dh += _dot(a_ref[...], w_ref[...])
        dx, dnw = _rms_bwd(dh, x_ref[...], nw_ref[...])
        dx_ref[...] = dx + r_ref[...] if has_res else dx

        @pl.when(pl.program_id(0) == 0)
        def _():
            dnw_ref[...] = jnp.zeros_like(dnw_ref)

        dnw_ref[...] += dnw

    in_specs = [pl.BlockSpec((tm, a.shape[1]), lambda i: (i, 0)) for a in a_list]
    in_specs += [pl.BlockSpec(w.shape, lambda i: (0, 0)) for w in w_list]
    in_specs += [pl.BlockSpec((tm, Dm), lambda i: (i, 0)), pl.BlockSpec((1, Dm), lambda i: (0, 0))]
    args = [*a_list, *w_list, x, nw]
    if has_res:
        in_specs.append(pl.BlockSpec((tm, Dm), lambda i: (i, 0)))
        args.append(res)
    payload = None if exchange is None else (_Exchange, exchange)
    n_in = len(args)
    if exchange is not None:
        args.append(exchange)
    in_specs, out_specs, out_shape, scratch = _carried_specs(
        payload, in_specs, [pl.BlockSpec((tm, Dm), lambda i: (i, 0)), pl.BlockSpec((1, Dm), lambda i: (0, 0))],
        [jax.ShapeDtypeStruct((L, Dm), F32), jax.ShapeDtypeStruct((1, Dm), F32)], [])
    return pl.pallas_call(
        _carried(body, n_in, 2, payload, _first_last(L // tm)), name=name, grid=(L // tm,), in_specs=in_specs,
        out_specs=out_specs, out_shape=out_shape, scratch_shapes=scratch,
        compiler_params=_cparams(("arbitrary",), vmem_mb=56),
    )(*args)


CONV_TN = 512


HALO = 16


def _conv_pre(cat, w_ref, b_ref, rows):
    shifted = [pltpu.roll(cat, 3 - k, 0)[HALO:HALO + rows] for k in range(3)] + [cat[HALO:HALO + rows]]
    pre = b_ref[...] + w_ref[3:4, :] * shifted[3]
    for k in range(3):
        pre += w_ref[k:k + 1, :] * shifted[k]
    return pre, shifted


def _conv_fwd(pc, cw, cb, *, tm):
    L, C = pc.shape
    tm = min(tm, L)
    tn = CONV_TN

    def body(u_ref, halo_ref, w_ref, b_ref, o_ref):
        halo = jnp.where(pl.program_id(1) > 0, halo_ref[...].astype(F32), 0.0)
        cat = jnp.concatenate([halo, u_ref[...].astype(F32)], axis=0)
        pre, _ = _conv_pre(cat, w_ref, b_ref, tm)
        o_ref[...] = _silu(pre).astype(BF16)

    return pl.pallas_call(
        body, name="conv_fwd", grid=(C // tn, L // tm),
        in_specs=[pl.BlockSpec((tm, tn), lambda j, i: (i, j)),
                  pl.BlockSpec((HALO, tn), lambda j, i: (jnp.maximum(i * (tm // HALO) - 1, 0), j)),
                  pl.BlockSpec((4, tn), lambda j, i: (0, j)), pl.BlockSpec((1, tn), lambda j, i: (0, j))],
        out_specs=pl.BlockSpec((tm, tn), lambda j, i: (i, j)),
        out_shape=jax.ShapeDtypeStruct((L, C), BF16),
        compiler_params=_cparams(("parallel", "parallel")),
    )(pc, pc, cw, cb)


def _conv_bwd(pc, dact, cw, cb, *, tm):
    L, C = pc.shape
    tm = min(tm, L)
    tn = CONV_TN
    nt = L // tm

    def body(u_ref, halo_ref, unext_ref, da_ref, danext_ref, w_ref, b_ref, du_ref, dw_ref, db_ref):
        i = pl.program_id(1)
        halo = jnp.where(i > 0, halo_ref[...].astype(F32), 0.0)
        cat = jnp.concatenate([halo, u_ref[...].astype(F32), unext_ref[...].astype(F32)], axis=0)
        pre, shifted = _conv_pre(cat, w_ref, b_ref, tm + HALO)
        da = jnp.concatenate([da_ref[...].astype(F32),
                              jnp.where(i < nt - 1, danext_ref[...].astype(F32), 0.0)], axis=0)
        dpre = da * _dsilu(pre)
        du = w_ref[3:4, :] * dpre[0:tm]
        for k in range(3):
            du += w_ref[k:k + 1, :] * pltpu.roll(dpre, tm + HALO - (3 - k), 0)[0:tm]
        du_ref[...] = du.astype(BF16)

        @pl.when(i == 0)
        def _():
            dw_ref[...] = jnp.zeros_like(dw_ref)
            db_ref[...] = jnp.zeros_like(db_ref)

        dp = dpre[0:tm]
        dw_ref[...] += jnp.concatenate(
            [jnp.sum(dp * shifted[k][0:tm], axis=0, keepdims=True) for k in range(4)], axis=0)
        db_ref[...] += jnp.sum(dp, axis=0, keepdims=True)

    nb = L // HALO
    return pl.pallas_call(
        body, name="conv_bwd", grid=(C // tn, nt),
        in_specs=[pl.BlockSpec((tm, tn), lambda j, i: (i, j)),
                  pl.BlockSpec((HALO, tn), lambda j, i: (jnp.maximum(i * (tm // HALO) - 1, 0), j)),
                  pl.BlockSpec((HALO, tn), lambda j, i: (jnp.minimum((i + 1) * (tm // HALO), nb - 1), j)),
                  pl.BlockSpec((tm, tn), lambda j, i: (i, j)),
                  pl.BlockSpec((HALO, tn), lambda j, i: (jnp.minimum((i + 1) * (tm // HALO), nb - 1), j)),
                  pl.BlockSpec((4, tn), lambda j, i: (0, j)), pl.BlockSpec((1, tn), lambda j, i: (0, j))],
        out_specs=[pl.BlockSpec((tm, tn), lambda j, i: (i, j)), pl.BlockSpec((4, tn), lambda j, i: (0, j)),
                   pl.BlockSpec((1, tn), lambda j, i: (0, j))],
        out_shape=[jax.ShapeDtypeStruct((L, C), BF16), jax.ShapeDtypeStruct((4, C), F32),
                   jax.ShapeDtypeStruct((1, C), F32)],
        compiler_params=_cparams(("parallel", "arbitrary")),
    )(pc, pc, pc, dact, dact, cw, cb)


def _ssd_common(dtr_ref, dtb_ref, avec_ref, aexp_ref, e_ref, acx_ref, acol_ref, arow_ref):
    q = SSD_Q
    tril = _tril(q)
    dtpre = dtr_ref[...] + dtb_ref[...]
    dt = _softplus(dtpre)
    dtx = _dot_hi(dt, e_ref[...])
    acx_ref[...] = _dot_hi(tril, dtx * aexp_ref[...], 1)
    acol = _dot_hi(tril, dt * avec_ref[...], 1)
    acol_ref[...] = acol
    arow_ref[...] = acol.T
    return dtpre, dt, dtx


def _ssd_fwd(xbc, pz, pdt, dtb, avec, aexp, dexp, nw, emat):
    L = xbc.shape[0]
    q = SSD_Q
    nc = L // q

    def body(xbc_ref, z_ref, dtr_ref, dtb_ref, avec_ref, aexp_ref, dexp_ref, nw_ref, e_ref,
             ya_ref, ypre_ref, sin_ref, st_ref, acx_ref, acol_ref, arow_ref, xdt_ref, y_ref):
        @pl.when(pl.program_id(0) == 0)
        def _():
            st_ref[...] = jnp.zeros_like(st_ref)

        sin_ref[...] = st_ref[...].astype(BF16)
        _, _, dtx = _ssd_common(dtr_ref, dtb_ref, avec_ref, aexp_ref, e_ref, acx_ref, acol_ref, arow_ref)
        xs = xbc_ref[:, 0:D].astype(F32)
        xdt = xs * dtx
        xdt_ref[...] = xdt
        acx = acx_ref[...]
        alast = acx_ref[q - 1:q, :]
        xdtd = xdt * jnp.exp(alast - acx)
        eac = jnp.exp(acx)
        ealast = jnp.exp(alast)
        causal = _iota2((q, q), 0) >= _iota2((q, q), 1)
        for g in range(2):
            gs = slice(512 * g, 512 * g + 512)
            bm = xbc_ref[:, D + 128 * g:D + 128 * g + 128]
            cm = xbc_ref[:, D + 256 + 128 * g:D + 256 + 128 * g + 128]
            stg = st_ref[:, gs]
            yoff = _dot(cm, stg) * eac[:, gs]
            gmat = _dot_nt(cm, bm)
            for e in range(8):
                h = 8 * g + e
                hs = slice(64 * h, 64 * h + 64)
                col = acol_ref[:, h:h + 1]
                row = arow_ref[h:h + 1, :]
                lm = jnp.exp(jnp.where(causal, col - row, -1e30))
                y_ref[:, hs] = _dot(gmat * lm, xdt_ref[:, hs])
            y_ref[:, gs] += yoff + dexp_ref[:, gs] * xs[:, gs]
            st_ref[:, gs] = stg * ealast[:, gs] + _dot_tn(bm, xdtd[:, gs])
        ypre_ref[...] = y_ref[...].astype(BF16)
        for g in range(2):
            gs = slice(512 * g, 512 * g + 512)
            yz = y_ref[:, gs] * _silu(z_ref[:, gs].astype(F32))
            ya_ref[:, gs] = _rms_fwd(yz, nw_ref[:, gs]).astype(BF16)

    vec = lambda n: pl.BlockSpec((1, n), lambda c: (0, 0))
    return pl.pallas_call(
        body, name="ssd_fwd", grid=(nc,),
        in_specs=[pl.BlockSpec((q, 1536), lambda c: (c, 0)), pl.BlockSpec((q, D), lambda c: (c, 0)),
                  pl.BlockSpec((q, 128), lambda c: (c, 0)), vec(128), vec(128), vec(D), vec(D), vec(D),
                  pl.BlockSpec((128, D), lambda c: (0, 0))],
        out_specs=[pl.BlockSpec((q, D), lambda c: (c, 0)), pl.BlockSpec((q, D), lambda c: (c, 0)),
                   pl.BlockSpec((128, D), lambda c: (c, 0))],
        out_shape=[jax.ShapeDtypeStruct((L, D), BF16), jax.ShapeDtypeStruct((L, D), BF16),
                   jax.ShapeDtypeStruct((nc * 128, D), BF16)],
        scratch_shapes=[pltpu.VMEM((128, D), F32), pltpu.VMEM((q, D), F32), pltpu.VMEM((q, 128), F32),
                        pltpu.VMEM((128, q), F32), pltpu.VMEM((q, D), F32), pltpu.VMEM((q, D), F32)],
        compiler_params=_cparams(("arbitrary",)),
    )(xbc, pz, pdt, dtb, avec, aexp, dexp, nw, emat)


def _ssd_bwd(xbc, pz, pdt, ypre, sin, dya, dtb, avec, aexp, dexp, nw, emat, exchange=None):
    L = xbc.shape[0]
    q = SSD_Q
    nc = L // q

    def body(xbc_ref, z_ref, dtr_ref, ypre_ref, sin_ref, dya_ref, dtb_ref, avec_ref, aexp_ref, dexp_ref, nw_ref,
             e_ref, dpb_ref, dxbc_ref, dnw_ref, ddtb_ref, da_ref, ddx_ref,
             dst_ref, acx_ref, acol_ref, arow_ref, xdt_ref, dxdt_ref, dy_ref, dacx_ref):
        @pl.when(pl.program_id(0) == 0)
        def _():
            dst_ref[...] = jnp.zeros_like(dst_ref)
            dnw_ref[...] = jnp.zeros_like(dnw_ref)
            ddtb_ref[...] = jnp.zeros_like(ddtb_ref)
            da_ref[...] = jnp.zeros_like(da_ref)
            ddx_ref[...] = jnp.zeros_like(ddx_ref)

        dtpre, dt, dtx = _ssd_common(dtr_ref, dtb_ref, avec_ref, aexp_ref, e_ref, acx_ref, acol_ref, arow_ref)
        xs = xbc_ref[:, 0:D].astype(F32)
        xdt = xs * dtx
        xdt_ref[...] = xdt
        acx = acx_ref[...]
        alast = acx_ref[q - 1:q, :]
        dec_end = jnp.exp(alast - acx)
        xdtd = xdt * dec_end
        eac = jnp.exp(acx)
        ealast = jnp.exp(alast)
        for g in range(2):
            gs = slice(512 * g, 512 * g + 512)
            y = ypre_ref[:, gs].astype(F32)
            z = z_ref[:, gs].astype(F32)
            sz = _silu(z)
            dyz, dnw = _rms_bwd(dya_ref[:, gs].astype(F32), y * sz, nw_ref[:, gs])
            dnw_ref[:, gs] += dnw
            dy_ref[:, gs] = dyz * sz
            dpb_ref[:, gs] = (dyz * y * _dsilu(z)).astype(BF16)
        dy = dy_ref[...]
        ddx_ref[...] += jnp.sum(dy * xs, axis=0, keepdims=True)
        ri = _iota2((q, q), 0)
        ci = _iota2((q, q), 1)
        causal = ri >= ci
        causal_t = ri <= ci
        dacol = jnp.zeros((q, 128), F32)
        dacol_t = jnp.zeros((128, q), F32)
        last_row = _iota2((q, 512), 0) == q - 1
        for g in range(2):
            gs = slice(512 * g, 512 * g + 512)
            bm = xbc_ref[:, D + 128 * g:D + 128 * g + 128]
            cm = xbc_ref[:, D + 256 + 128 * g:D + 256 + 128 * g + 128]
            stg = sin_ref[:, gs].astype(F32)
            dstg = dst_ref[:, gs]
            dyg = dy[:, gs]
            yoff = _dot(cm, stg) * eac[:, gs]
            dwm = dyg * eac[:, gs]
            dcm = _dot_nt(dwm, stg)
            dstin = _dot_tn(cm, dwm)
            dacx_g = dyg * yoff
            dxdtd = _dot(bm, dstg)
            dbm = _dot_nt(xdtd[:, gs], dstg)
            t = dxdtd * xdtd[:, gs]
            dacx_g -= t
            dalast = jnp.sum(t, axis=0, keepdims=True) + jnp.sum(dstg * stg, axis=0, keepdims=True) * ealast[:, gs]
            dst_ref[:, gs] = dstin + dstg * ealast[:, gs]
            dacx_ref[:, gs] = dacx_g + jnp.where(last_row, dalast, 0.0)
            gmat = _dot_nt(cm, bm)
            gmat_t = _dot_nt(bm, cm)
            dg = jnp.zeros((q, q), F32)
            for e in range(8):
                h = 8 * g + e
                hs = slice(64 * h, 64 * h + 64)
                col = acol_ref[:, h:h + 1]
                row = arow_ref[h:h + 1, :]
                lm = jnp.exp(jnp.where(causal, col - row, -1e30))
                lm_t = jnp.exp(jnp.where(causal_t, row - col, -1e30))
                dyh = dy_ref[:, hs]
                dm = _dot_nt(dyh, xdt_ref[:, hs])
                dxdt_ref[:, hs] = _dot(gmat_t * lm_t, dyh)
                dml = dm * lm
                dg += dml
                p = dml * gmat
                dacol += jnp.where(ci == h, jnp.sum(p, axis=1, keepdims=True), 0.0)
                dacol_t -= jnp.where(ri == h, jnp.sum(p, axis=0, keepdims=True), 0.0)
            dcm += _dot(dg, bm)
            dbm += _dot_tn(dg, cm)
            dxbc_ref[:, D + 128 * g:D + 128 * g + 128] = dbm.astype(BF16)
            dxbc_ref[:, D + 256 + 128 * g:D + 256 + 128 * g + 128] = dcm.astype(BF16)
            dxdt_ref[:, gs] += dxdtd * dec_end[:, gs]
        dxdt = dxdt_ref[...]
        dacum = dacol + dacol_t.T + _dot_nt_hi(dacx_ref[...], e_ref[...])
        da = _dot_hi(_triu(q), dacum, 1)
        ddt = da * avec_ref[...] + _dot_nt_hi(dxdt * xs, e_ref[...])
        da_ref[...] += jnp.sum(da * dt, axis=0, keepdims=True) * avec_ref[...]
        dxbc_ref[:, 0:D] = (dexp_ref[...] * dy + dxdt * dtx).astype(BF16)
        ddtr = ddt * _sigmoid(dtpre)
        ddtb_ref[...] += jnp.sum(ddtr, axis=0, keepdims=True)
        dpb_ref[:, D:D + 128] = ddtr.astype(BF16)

    rev = lambda c: nc - 1 - c
    vec = lambda n: pl.BlockSpec((1, n), lambda c: (0, 0))
    payload = None if exchange is None else (_Exchange, exchange)
    args = [xbc, pz, pdt, ypre, sin, dya, dtb, avec, aexp, dexp, nw, emat]
    in_specs, out_specs, out_shape, scratch = _carried_specs(
        payload,
        [pl.BlockSpec((q, 1536), lambda c: (rev(c), 0)), pl.BlockSpec((q, D), lambda c: (rev(c), 0)),
         pl.BlockSpec((q, 128), lambda c: (rev(c), 0)), pl.BlockSpec((q, D), lambda c: (rev(c), 0)),
         pl.BlockSpec((128, D), lambda c: (rev(c), 0)), pl.BlockSpec((q, D), lambda c: (rev(c), 0)),
         vec(128), vec(128), vec(D), vec(D), vec(D), pl.BlockSpec((128, D), lambda c: (0, 0))],
        [pl.BlockSpec((q, WB), lambda c: (rev(c), 0)), pl.BlockSpec((q, 1536), lambda c: (rev(c), 0)),
         vec(D), vec(128), vec(128), vec(D)],
        [jax.ShapeDtypeStruct((L, WB), BF16), jax.ShapeDtypeStruct((L, 1536), BF16),
         jax.ShapeDtypeStruct((1, D), F32), jax.ShapeDtypeStruct((1, 128), F32),
         jax.ShapeDtypeStruct((1, 128), F32), jax.ShapeDtypeStruct((1, D), F32)],
        [pltpu.VMEM((128, D), F32), pltpu.VMEM((q, D), F32), pltpu.VMEM((q, 128), F32),
         pltpu.VMEM((128, q), F32), pltpu.VMEM((q, D), F32), pltpu.VMEM((q, D), F32),
         pltpu.VMEM((q, D), F32), pltpu.VMEM((q, D), F32)])
    return pl.pallas_call(
        _carried(body, len(args), 6, payload, _first_last(nc)), name="ssd_bwd", grid=(nc,),
        in_specs=in_specs, out_specs=out_specs, out_shape=out_shape, scratch_shapes=scratch,
        compiler_params=_cparams(("arbitrary",)),
    )(*(args if exchange is None else args + [exchange]))


def _hg_gates(hq, hf, hgl_ref, b_ref):
    lb = 1.0 / (1.0 + jnp.exp(hgl_ref[1:2, :] - hgl_ref[0:1, :]))
    qf = _silu(hq)
    sg = _sigmoid(hf)
    f = lb + (1.0 - lb) * sg
    b_ref[...] = _dot_hi(_tril(HG_STEP), jnp.log(f), 1)
    return lb, qf, sg, f


def _hg_factors(qf, kf, b_ref):
    s, n = HG_SUB, HG_STEP
    b = b_ref[...]
    blast = b_ref[n - 1:n, :]
    m0, mb, m1 = b_ref[s // 2 - 1:s // 2, :], b_ref[s - 1:s, :], b_ref[s + s // 2 - 1:s + s // 2, :]
    b0, b1 = b[0:s], b[s:n]
    q0, q1, k0, k1 = qf[0:s], qf[s:n], kf[0:s], kf[s:n]
    fac = dict(
        eb=jnp.exp(blast), eq=jnp.exp(b), ek=jnp.exp(blast - b),
        eq0=jnp.exp(b0 - m0), ek0=jnp.exp(m0 - b0), eq1=jnp.exp(b1 - m1), ek1=jnp.exp(m1 - b1),
        eqb=jnp.exp(b1 - mb), ekb=jnp.exp(mb - b0))
    rd = lambda t: t.astype(BF16).astype(F32)
    val = dict(qe=qf * fac["eq"], ke=kf * fac["ek"], qm0=rd(q0 * fac["eq0"]), km0=rd(k0 * fac["ek0"]),
               qm1=rd(q1 * fac["eq1"]), km1=rd(k1 * fac["ek1"]), qb=rd(q1 * fac["eqb"]), kb=rd(k0 * fac["ekb"]))
    return fac, val


def _hgrn_fwd(pa, phf, hgl, nwx):
    L = pa.shape[0]
    n, s = HG_STEP, HG_SUB
    nc = L // n

    def body(hq_ref, hf_ref, hi_ref, hg_ref, hgl_ref, nw_ref, ob_ref, opre_ref, sin_ref, st_ref, b_ref):
        @pl.when(pl.program_id(0) == 0)
        def _():
            st_ref[...] = jnp.zeros_like(st_ref)

        sin_ref[...] = st_ref[...].astype(BF16)
        _, qf, _, f = _hg_gates(hq_ref[...].astype(F32), hf_ref[...], hgl_ref, b_ref)
        fac, val = _hg_factors(qf, 1.0 - f, b_ref)
        causal = _iota2((s, s), 0) >= _iota2((s, s), 1)
        for h in range(HG_HEADS):
            hs = slice(128 * h, 128 * h + 128)
            sth = st_ref[:, hs]
            v = hi_ref[:, hs]
            v0, v1 = v[0:s], v[s:n]
            a00 = jnp.where(causal, _dot_nt(val["qm0"][:, hs], val["km0"][:, hs]), 0.0)
            a11 = jnp.where(causal, _dot_nt(val["qm1"][:, hs], val["km1"][:, hs]), 0.0)
            a10 = _dot_nt(val["qb"][:, hs], val["kb"][:, hs])
            o = _dot_nt(val["qe"][:, hs], sth) + jnp.concatenate(
                [_dot(a00, v0), _dot(a10, v0) + _dot(a11, v1)], axis=0)
            st_ref[:, hs] = sth * fac["eb"][:, hs] + _dot_tn(v, val["ke"][:, hs])
            opre_ref[:, hs] = o.astype(BF16)
            ob_ref[:, hs] = (_rms_fwd(o, nw_ref[:, hs]) * _silu(hg_ref[:, hs].astype(F32))).astype(BF16)

    blk = lambda j: pl.BlockSpec((n, D), lambda c: (c, j))
    return pl.pallas_call(
        body, name="hgrn_fwd", grid=(nc,),
        in_specs=[blk(0), blk(0), blk(1), blk(2), pl.BlockSpec((2, D), lambda c: (0, 0)),
                  pl.BlockSpec((1, D), lambda c: (0, 0))],
        out_specs=[blk(0), blk(0), blk(0)],
        out_shape=[jax.ShapeDtypeStruct((L, D), BF16), jax.ShapeDtypeStruct((L, D), BF16),
                   jax.ShapeDtypeStruct((nc * 128, D), BF16)],
        scratch_shapes=[pltpu.VMEM((128, D), F32), pltpu.VMEM((n, D), F32)],
        compiler_params=_cparams(("arbitrary",)),
    )(pa, phf, pa, pa, hgl, nwx)


def _hgrn_bwd(pa, phf, opre, sin, dob, hgl, nwx, exchange=None):
    L = pa.shape[0]
    n, s = HG_STEP, HG_SUB
    nc = L // n

    def body(hq_ref, hf_ref, hi_ref, hg_ref, opre_ref, sin_ref, dob_ref, hgl_ref, nw_ref,
             dpa_ref, dhgl_ref, dnw_ref, dst_ref, b_ref, dlb_ref, dq_ref, dk_ref, db_ref):
        i = pl.program_id(0)

        @pl.when(i == 0)
        def _():
            dst_ref[...] = jnp.zeros_like(dst_ref)
            dlb_ref[...] = jnp.zeros_like(dlb_ref)
            dnw_ref[...] = jnp.zeros_like(dnw_ref)

        hq = hq_ref[...].astype(F32)
        lb, qf, sg, f = _hg_gates(hq, hf_ref[...], hgl_ref, b_ref)
        kf = 1.0 - f
        fac, val = _hg_factors(qf, kf, b_ref)
        ri, ci = _iota2((s, s), 0), _iota2((s, s), 1)
        causal, causal_t = ri >= ci, ri <= ci
        last_row = _iota2((n, 128), 0) == n - 1
        for h in range(HG_HEADS):
            hs = slice(128 * h, 128 * h + 128)
            o = opre_ref[:, hs].astype(F32)
            gate = hg_ref[:, hs].astype(F32)
            dout = dob_ref[:, hs].astype(F32)
            sgate = _silu(gate)
            do, dnw = _rms_bwd(dout * sgate, o, nw_ref[:, hs])
            dnw_ref[:, hs] += dnw
            dpa_ref[:, 2 * D + 128 * h:2 * D + 128 * h + 128] = (
                dout * _rms_fwd(o, nw_ref[:, hs]) * _dsilu(gate)).astype(BF16)
            sth = sin_ref[:, hs].astype(F32)
            dsth = dst_ref[:, hs]
            v = hi_ref[:, hs]
            v0, v1 = v[0:s], v[s:n]
            do0, do1 = do[0:s], do[s:n]
            qe, ke = val["qe"][:, hs], val["ke"][:, hs]
            qm0, km0, qm1, km1 = val["qm0"][:, hs], val["km0"][:, hs], val["qm1"][:, hs], val["km1"][:, hs]
            qb, kb = val["qb"][:, hs], val["kb"][:, hs]
            dqe = _dot(do, sth)
            dstin = _dot_tn(do, qe)
            a00t = jnp.where(causal_t, _dot_nt(km0, qm0), 0.0)
            a11t = jnp.where(causal_t, _dot_nt(km1, qm1), 0.0)
            a10t = _dot_nt(kb, qb)
            dat00 = jnp.where(causal, _dot_nt(do0, v0), 0.0)
            dat11 = jnp.where(causal, _dot_nt(do1, v1), 0.0)
            dat10 = _dot_nt(do1, v0)
            dat00t = jnp.where(causal_t, _dot_nt(v0, do0), 0.0)
            dat11t = jnp.where(causal_t, _dot_nt(v1, do1), 0.0)
            dat10t = _dot_nt(v0, do1)
            dv = jnp.concatenate([_dot(a00t, do0) + _dot(a10t, do1), _dot(a11t, do1)], axis=0)
            dqm0, dkm0 = _dot(dat00, km0), _dot(dat00t, qm0)
            dqm1, dkm1 = _dot(dat11, km1), _dot(dat11t, qm1)
            dqb, dkb = _dot(dat10, kb), _dot(dat10t, qb)
            dke = _dot(v, dsth)
            dv += _dot_nt(ke, dsth)
            deb = jnp.sum(dsth * sth, axis=0, keepdims=True)
            dst_ref[:, hs] = dstin + dsth * fac["eb"][:, hs]
            dq = dqe * fac["eq"][:, hs] + jnp.concatenate(
                [dqm0 * fac["eq0"][:, hs], dqm1 * fac["eq1"][:, hs] + dqb * fac["eqb"][:, hs]], axis=0)
            dk = dke * fac["ek"][:, hs] + jnp.concatenate(
                [dkm0 * fac["ek0"][:, hs] + dkb * fac["ekb"][:, hs], dkm1 * fac["ek1"][:, hs]], axis=0)
            tke = dke * ke
            db = dqe * qe - tke + jnp.concatenate(
                [dqm0 * qm0 - dkm0 * km0 - dkb * kb, dqm1 * qm1 - dkm1 * km1 + dqb * qb], axis=0)
            dblast = jnp.sum(tke, axis=0, keepdims=True) + deb * fac["eb"][:, hs]
            db_ref[:, hs] = db + jnp.where(last_row, dblast, 0.0)
            dq_ref[:, hs] = dq
            dk_ref[:, hs] = dk
            dpa_ref[:, D + 128 * h:D + 128 * h + 128] = dv.astype(BF16)
        dg = _dot_hi(_triu(n), db_ref[...], 1)
        df = dg / f - dk_ref[...]
        dpa_ref[:, 3 * D:4 * D] = (df * (1.0 - lb) * sg * (1.0 - sg)).astype(BF16)
        dpa_ref[:, 0:D] = (dq_ref[...] * _dsilu(hq)).astype(BF16)
        dlb_ref[...] += jnp.sum(df * (1.0 - sg), axis=0, keepdims=True)

        @pl.when(i == nc - 1)
        def _():
            d0 = dlb_ref[...] * lb * (1.0 - lb)
            dhgl_ref[...] = jnp.concatenate([d0, -d0], axis=0)

    rev = lambda c: nc - 1 - c
    blk = lambda j: pl.BlockSpec((n, D), lambda c: (rev(c), j))
    payload = None if exchange is None else (_Exchange, exchange)
    args = [pa, phf, pa, pa, opre, sin, dob, hgl, nwx]
    in_specs, out_specs, out_shape, scratch = _carried_specs(
        payload,
        [blk(0), blk(0), blk(1), blk(2), blk(0), blk(0), blk(0), pl.BlockSpec((2, D), lambda c: (0, 0)),
         pl.BlockSpec((1, D), lambda c: (0, 0))],
        [pl.BlockSpec((n, 4 * D), lambda c: (rev(c), 0)), pl.BlockSpec((2, D), lambda c: (0, 0)),
         pl.BlockSpec((1, D), lambda c: (0, 0))],
        [jax.ShapeDtypeStruct((L, 4 * D), BF16), jax.ShapeDtypeStruct((2, D), F32), jax.ShapeDtypeStruct((1, D), F32)],
        [pltpu.VMEM((128, D), F32), pltpu.VMEM((n, D), F32), pltpu.VMEM((1, D), F32),
         pltpu.VMEM((n, D), F32), pltpu.VMEM((n, D), F32), pltpu.VMEM((n, D), F32)])
    return pl.pallas_call(
        _carried(body, len(args), 3, payload, _first_last(nc)), name="hgrn_bwd", grid=(nc,),
        in_specs=in_specs, out_specs=out_specs, out_shape=out_shape, scratch_shapes=scratch,
        compiler_params=_cparams(("arbitrary",)),
    )(*(args if exchange is None else args + [exchange]))


XA_SCALE = XA_DH ** -0.5


def _xa_probs(qh, kmh):
    sc = _dot_nt(qh, kmh) * XA_SCALE
    p = jnp.exp(sc - jnp.max(sc, axis=1, keepdims=True))
    return p * (1.0 / jnp.sum(p, axis=1, keepdims=True))


def _xattn_fwd(x1, nw, wq, kv, wo, *, tm):
    L = x1.shape[0]
    tm = min(tm, L)

    def body(x_ref, nw_ref, wq_ref, kv_ref, wo_ref, o_ref, ox_ref):
        x = x_ref[...]
        q = _dot(_rms_fwd(x, nw_ref[...]), wq_ref[...])
        for h in range(XA_HEADS):
            hs = slice(XA_DH * h, XA_DH * h + XA_DH)
            p = _xa_probs(q[:, hs], kv_ref[:, hs])
            ox_ref[:, hs] = _dot(p, kv_ref[:, D + XA_DH * h:D + XA_DH * h + XA_DH])
        o_ref[...] = x + _dot(ox_ref[...], wo_ref[...])

    full = lambda a: pl.BlockSpec(a.shape, lambda i: (0, 0))
    return pl.pallas_call(
        body, name="xattn_fwd", grid=(L // tm,),
        in_specs=[pl.BlockSpec((tm, D), lambda i: (i, 0)), full(nw), full(wq), full(kv), full(wo)],
        out_specs=pl.BlockSpec((tm, D), lambda i: (i, 0)),
        out_shape=jax.ShapeDtypeStruct((L, D), F32),
        scratch_shapes=[pltpu.VMEM((tm, D), F32)],
        compiler_params=_cparams(("parallel",)),
    )(x1, nw, wq, kv, wo)


def _xattn_bwd(x1, dx2, nw, wq, kv, wo, *, tm):
    L = x1.shape[0]
    tm = min(tm, L)

    def body(x_ref, dx2_ref, nw_ref, wq_ref, kv_ref, wo_ref, dx1_ref, dx1b_ref, h_ref, dq_ref, ox_ref, dkv_ref,
             dnw_ref, dqs_ref):
        @pl.when(pl.program_id(0) == 0)
        def _():
            dkv_ref[...] = jnp.zeros_like(dkv_ref)
            dnw_ref[...] = jnp.zeros_like(dnw_ref)

        x = x_ref[...]
        dx2 = dx2_ref[...]
        hn = _rms_fwd(x, nw_ref[...]).astype(BF16)
        h_ref[...] = hn
        q = _dot(hn, wq_ref[...])
        dox = _dot_nt(dx2, wo_ref[...])
        for h in range(XA_HEADS):
            hs = slice(XA_DH * h, XA_DH * h + XA_DH)
            vs = slice(D + XA_DH * h, D + XA_DH * h + XA_DH)
            qh, kmh, vmh, doxh = q[:, hs], kv_ref[:, hs], kv_ref[:, vs], dox[:, hs]
            p = _xa_probs(qh, kmh)
            ox_ref[:, hs] = _dot(p, vmh).astype(BF16)
            dp = _dot_nt(doxh, vmh)
            dkv_ref[:, vs] += _dot_tn(p, doxh)
            ds = p * (dp - jnp.sum(dp * p, axis=1, keepdims=True)) * XA_SCALE
            dqs_ref[:, hs] = _dot(ds, kmh)
            dkv_ref[:, hs] += _dot_tn(ds, qh)
        dq = dqs_ref[...]
        dq_ref[...] = dq.astype(BF16)
        dx, dnw = _rms_bwd(_dot_nt(dq, wq_ref[...]), x, nw_ref[...])
        dx1 = dx2 + dx
        dx1_ref[...] = dx1
        dx1b_ref[...] = dx1.astype(BF16)
        dnw_ref[...] += dnw

    full = lambda a: pl.BlockSpec(a.shape, lambda i: (0, 0))
    row = pl.BlockSpec((tm, D), lambda i: (i, 0))
    return pl.pallas_call(
        body, name="xattn_bwd", grid=(L // tm,),
        in_specs=[row, row, full(nw), full(wq), full(kv), full(wo)],
        out_specs=[row, row, row, row, row, pl.BlockSpec((MEM_LEN, 2 * D), lambda i: (0, 0)),
                   pl.BlockSpec((1, D), lambda i: (0, 0))],
        out_shape=[jax.ShapeDtypeStruct((L, D), F32), jax.ShapeDtypeStruct((L, D), BF16),
                   jax.ShapeDtypeStruct((L, D), BF16), jax.ShapeDtypeStruct((L, D), BF16),
                   jax.ShapeDtypeStruct((L, D), BF16),
                   jax.ShapeDtypeStruct((MEM_LEN, 2 * D), F32), jax.ShapeDtypeStruct((1, D), F32)],
        scratch_shapes=[pltpu.VMEM((tm, D), F32)],
        compiler_params=_cparams(("arbitrary",)),
    )(x1, dx2, nw, wq, kv, wo)


def _ffn_blk(tm, tf):
    return pl.BlockSpec((tm, tf), lambda i, j: (i, j))


def _interleave(wg, wu, tf):
    return jnp.stack([wg.reshape(FFN // tf, tf, D), wu.reshape(FFN // tf, tf, D)], axis=1).reshape(2 * FFN, D)


def _ffn_fwd(x2, nw, wgu, wd, *, tm, tf):
    L = x2.shape[0]
    tm = min(tm, L)
    nf = FFN // tf

    def body(x_ref, nw_ref, wgu_ref, wd_ref, o_ref, hn_ref, g_ref, u_ref, h_ref, acc_ref):
        j = pl.program_id(1)

        @pl.when(j == 0)
        def _():
            hn = _rms_fwd(x_ref[...], nw_ref[...]).astype(BF16)
            h_ref[...] = hn
            hn_ref[...] = hn
            acc_ref[...] = jnp.zeros_like(acc_ref)

        gu = _dot_nt(h_ref[...], wgu_ref[...]).astype(BF16)
        g, u = gu[:, 0:tf], gu[:, tf:2 * tf]
        g_ref[...] = g
        u_ref[...] = u
        acc_ref[...] += _dot(_silu(g.astype(F32)) * u.astype(F32), wd_ref[...])

        @pl.when(j == nf - 1)
        def _():
            o_ref[...] = x_ref[...] + acc_ref[...]

    row = pl.BlockSpec((tm, D), lambda i, j: (i, 0))
    wide = jax.ShapeDtypeStruct((L, FFN), BF16)
    return pl.pallas_call(
        body, name="ffn_fwd", grid=(L // tm, nf),
        in_specs=[row, pl.BlockSpec((1, D), lambda i, j: (0, 0)),
                  pl.BlockSpec((2 * tf, D), lambda i, j: (j, 0)), pl.BlockSpec((tf, D), lambda i, j: (j, 0))],
        out_specs=[row, row, _ffn_blk(tm, tf), _ffn_blk(tm, tf)],
        out_shape=[jax.ShapeDtypeStruct((L, D), F32), jax.ShapeDtypeStruct((L, D), BF16), wide, wide],
        scratch_shapes=[pltpu.VMEM((tm, D), BF16), pltpu.VMEM((tm, D), F32)],
        compiler_params=_cparams(("parallel", "arbitrary"), vmem_mb=58),
    )(x2, nw, wgu, wd)


def _ffn_bwd(x2, dx3, g, u, nw, wgu, wd, *, tm, tf):
    L = x2.shape[0]
    tm = min(tm, L)
    nf = FFN // tf

    def body(x_ref, dx3_ref, g_ref, u_ref, nw_ref, wgu_ref, wd_ref,
             dx2_ref, a_ref, dg_ref, du_ref, dnw_ref, d3_ref, acc_ref):
        i, j = pl.program_id(0), pl.program_id(1)

        @pl.when(j == 0)
        def _():
            d3_ref[...] = dx3_ref[...].astype(BF16)
            acc_ref[...] = jnp.zeros_like(acc_ref)

        g = g_ref[...].astype(F32)
        u = u_ref[...].astype(F32)
        s = _sigmoid_gate(g)
        sg = g * s
        a_ref[...] = (sg * u).astype(BF16)
        da = _dot_nt(d3_ref[...], wd_ref[...])
        dg = (da * u * (s + sg * (1.0 - s))).astype(BF16)
        du = (da * sg).astype(BF16)
        dg_ref[...] = dg
        du_ref[...] = du
        acc_ref[...] += _dot(jnp.concatenate([dg, du], axis=1), wgu_ref[...])

        @pl.when(jnp.logical_and(i == 0, j == 0))
        def _():
            dnw_ref[...] = jnp.zeros_like(dnw_ref)

        @pl.when(j == nf - 1)
        def _():
            dx, dnw = _rms_bwd(acc_ref[...], x_ref[...], nw_ref[...])
            dx2_ref[...] = dx3_ref[...] + dx
            dnw_ref[...] += dnw

    row = pl.BlockSpec((tm, D), lambda i, j: (i, 0))
    blk = _ffn_blk(tm, tf)
    wide = jax.ShapeDtypeStruct((L, FFN), BF16)
    return pl.pallas_call(
        body, name="ffn_bwd", grid=(L // tm, nf),
        in_specs=[row, row, blk, blk, pl.BlockSpec((1, D), lambda i, j: (0, 0)),
                  pl.BlockSpec((2 * tf, D), lambda i, j: (j, 0)), pl.BlockSpec((tf, D), lambda i, j: (j, 0))],
        out_specs=[row, blk, blk, blk, pl.BlockSpec((1, D), lambda i, j: (0, 0))],
        out_shape=[jax.ShapeDtypeStruct((L, D), F32), wide, wide, wide, jax.ShapeDtypeStruct((1, D), F32)],
        scratch_shapes=[pltpu.VMEM((tm, D), BF16), pltpu.VMEM((tm, D), F32)],
        compiler_params=_cparams(("arbitrary", "arbitrary"), vmem_mb=56),
    )(x2, dx3, g, u, nw, wgu, wd)


def _final(x3, tgt, nw, *, tm):
    L = x3.shape[0]
    tm = min(tm, L)

    def body(x_ref, t_ref, nw_ref, dx_ref, dxb_ref, loss_ref, dnw_ref):
        @pl.when(pl.program_id(0) == 0)
        def _():
            loss_ref[...] = jnp.zeros_like(loss_ref)
            dnw_ref[...] = jnp.zeros_like(dnw_ref)

        x = x_ref[...]
        w = nw_ref[...]
        err = _rms_fwd(x, w) - t_ref[...]
        part = 0.5 * jnp.sum(jnp.sum(err * err, axis=1, keepdims=True), axis=0, keepdims=True) * (1.0 / D)
        loss_ref[...] += jnp.where(_iota2((1, 128), 1) == 0, part, 0.0)
        dx, dnw = _rms_bwd(err * (1.0 / D), x, w)
        dx_ref[...] = dx
        dxb_ref[...] = dx.astype(BF16)
        dnw_ref[...] += dnw

    row = pl.BlockSpec((tm, D), lambda i: (i, 0))
    return pl.pallas_call(
        body, name="final_loss", grid=(L // tm,),
        in_specs=[row, row, pl.BlockSpec((1, D), lambda i: (0, 0))],
        out_specs=[row, row, pl.BlockSpec((1, 128), lambda i: (0, 0)), pl.BlockSpec((1, D), lambda i: (0, 0))],
        out_shape=[jax.ShapeDtypeStruct((L, D), F32), jax.ShapeDtypeStruct((L, D), BF16),
                   jax.ShapeDtypeStruct((1, 128), F32), jax.ShapeDtypeStruct((1, D), F32)],
        compiler_params=_cparams(("arbitrary",)),
    )(x3, tgt, nw)


def _adam_update(g, w, m, v):
    c1 = 1.0 / (1.0 - ADAM_B1 ** ADAM_STEP)
    c2 = 1.0 / (1.0 - ADAM_B2 ** ADAM_STEP)
    nm = ADAM_B1 * m + (1.0 - ADAM_B1) * g
    nv = ADAM_B2 * v + (1.0 - ADAM_B2) * (g * g)
    return -ADAM_LR * ((nm * c1) / (jnp.sqrt(nv * c2) + ADAM_EPS) + ADAM_WD * w), nm, nv


def _adamw_small(tot, conv_w_grad, w, m, v):
    names = [n for n, _, _ in SMALL if n != "loss"]
    k = len(names)

    def body(tot_ref, cwg_ref, *refs):
        w_refs, m_refs, v_refs = refs[0:k], refs[k:2 * k], refs[2 * k:3 * k]
        g_refs, d_refs, nm_refs, nv_refs = (refs[(3 + j) * k:(4 + j) * k] for j in range(4))
        for i, n in enumerate(names):
            row, nr, nc = SMALL_AT[n]
            g = cwg_ref[...] if n == "conv_w" else tot_ref[row:row + nr, 0:nc]
            d, nm, nv = _adam_update(g, w_refs[i][...], m_refs[i][...], v_refs[i][...])
            g_refs[i][...] = g
            d_refs[i][...] = d
            nm_refs[i][...] = nm
            nv_refs[i][...] = nv

    sds = [jax.ShapeDtypeStruct(t.shape, F32) for t in w]
    res = pl.pallas_call(body, name="adamw_small", out_shape=sds * 4)(tot, conv_w_grad, *w, *m, *v)
    return res[0:k], res[k:2 * k], res[2 * k:3 * k], res[3 * k:4 * k]


def _adamw(parts, w, m, v, *, tr, name):
    n_parts, R, C = parts.shape
    tr = min(tr, R)

    def body(p_ref, w_ref, m_ref, v_ref, g_ref, d_ref, nm_ref, nv_ref):
        g = p_ref[0]
        for k in range(1, n_parts):
            g = g + p_ref[k]
        d, nm, nv = _adam_update(g, w_ref[...], m_ref[...], v_ref[...])
        g_ref[...] = g
        nm_ref[...] = nm
        nv_ref[...] = nv
        d_ref[...] = d

    blk = pl.BlockSpec((tr, C), lambda i: (i, 0))
    sds = jax.ShapeDtypeStruct((R, C), F32)
    return pl.pallas_call(
        body, name=name, grid=(R // tr,),
        in_specs=[pl.BlockSpec((n_parts, tr, C), lambda i: (0, i, 0)), blk, blk, blk],
        out_specs=[blk, blk, blk, blk], out_shape=[sds, sds, sds, sds],
        compiler_params=_cparams(("parallel",)),
    )(parts, w, m, v)


def _position():
    return lax.axis_index("x"), lax.axis_index("y"), lax.axis_index("c")


def _comm_scratch():
    return [pltpu.SemaphoreType.DMA((7,)), pltpu.SemaphoreType.DMA((7,)), pltpu.SemaphoreType.DMA]


class _Gather:
    def __init__(self, x_ref, out_ref, send_sems, recv_sems, local_sem):
        x, y, c = _position()
        me, sibling = (x, y, c), (x, y, 1 - c)
        chips = [(1 - x, y), (x, 1 - y), (1 - x, 1 - y)]

        def rows(px, py, pc):
            return out_ref.at[4 * px + 2 * py + pc]

        def copy(k, block, to, src=None):
            return pltpu.make_async_remote_copy(
                src_ref=rows(*block) if src is None else src, dst_ref=rows(*block),
                send_sem=send_sems.at[k], recv_sem=recv_sems.at[k], device_id=to, device_id_type=MESH)

        self.mine = pltpu.make_async_copy(x_ref, rows(*me), local_sem)
        self.first = [copy(0, me, sibling, src=x_ref)]
        self.first += [copy(1 + j, me, (*chip, c), src=x_ref) for j, chip in enumerate(chips)]
        self.passed = [copy(4 + j, (*chip, c), sibling) for j, chip in enumerate(chips)]
        self.from_chips = [copy(1 + j, (*chip, c), me) for j, chip in enumerate(chips)]
        self.from_sibling = [copy(0, sibling, me)] + [copy(4 + j, (*chip, 1 - c), me) for j, chip in enumerate(chips)]

    def start(self):
        self.mine.start()
        for cp in self.first:
            cp.start()

    def forward(self):
        for got, cp in zip(self.from_chips, self.passed):
            got.wait_recv()
            cp.start()

    def finish(self):
        for got in self.from_sibling:
            got.wait_recv()
        for cp in self.first + self.passed:
            cp.wait_send()
        self.mine.wait()


class _Exchange:
    def __init__(self, g_ref, out_ref, send_sems, recv_sems, local_sem):
        x, y, c = _position()
        me = 4 * x + 2 * y + c
        self.mine = pltpu.make_async_copy(g_ref.at[me], out_ref.at[me], local_sem)
        self.copies = []
        for k in range(1, N_DEV):
            px = 1 - x if k & 4 else x
            py = 1 - y if k & 2 else y
            pc = 1 - c if k & 1 else c
            self.copies.append(pltpu.make_async_remote_copy(
                src_ref=g_ref.at[4 * px + 2 * py + pc], dst_ref=out_ref.at[me],
                send_sem=send_sems.at[k - 1], recv_sem=recv_sems.at[k - 1],
                device_id=(px, py, pc), device_id_type=MESH))

    def start(self):
        self.mine.start()
        for cp in self.copies:
            cp.start()

    def finish(self):
        for cp in self.copies:
            cp.wait()
        self.mine.wait()


def _allgather(xp):
    R, C = xp.shape

    def body(x_ref, out_ref, send_sems, recv_sems, local_sem):
        g = _Gather(x_ref, out_ref, send_sems, recv_sems, local_sem)
        g.start()
        g.forward()
        g.finish()

    return pl.pallas_call(
        body, name="allgather_w_in",
        out_shape=jax.ShapeDtypeStruct((N_DEV, R, C), xp.dtype),
        in_specs=[pl.BlockSpec(memory_space=pltpu.HBM)], out_specs=pl.BlockSpec(memory_space=pltpu.HBM),
        scratch_shapes=_comm_scratch(),
    )(xp)


def _carried(body, n_in, n_out, payload, phases):
    if payload is None:
        return body
    kind = payload[0]

    def new_body(*refs):
        ins, src_ref = refs[:n_in], refs[n_in]
        outs, dst_ref = refs[n_in + 1:n_in + 1 + n_out], refs[n_in + 1 + n_out]
        scratch, sems = refs[n_in + 2 + n_out:-3], refs[-3:]

        def run(before):
            for when, action, is_before in phases:
                if is_before == before:
                    @pl.when(when())
                    def _():
                        action(kind(src_ref, dst_ref, *sems))

        run(True)
        body(*ins, *outs, *scratch)
        run(False)

    return new_body


def _carried_specs(payload, in_specs, out_specs, out_shape, scratch):
    if payload is None:
        return in_specs, out_specs, out_shape, scratch
    kind, arr = payload
    landing = (N_DEV,) + arr.shape if kind is _Gather else arr.shape
    hbm = pl.BlockSpec(memory_space=pltpu.HBM)
    return (in_specs + [hbm], out_specs + [hbm], out_shape + [jax.ShapeDtypeStruct(landing, arr.dtype)],
            scratch + _comm_scratch())


def _small_allreduce(sp):
    R, C = sp.shape

    def body(s_ref, out_ref, buf_ref, send_sems, recv_sems):
        x, y, c = _position()
        me = 4 * x + 2 * y + c
        buf_ref[me] = s_ref[...]
        copies = []
        for k in range(1, N_DEV):
            px = 1 - x if k & 4 else x
            py = 1 - y if k & 2 else y
            pc = 1 - c if k & 1 else c
            cp = pltpu.make_async_remote_copy(
                src_ref=s_ref, dst_ref=buf_ref.at[me], send_sem=send_sems.at[k - 1], recv_sem=recv_sems.at[k - 1],
                device_id=(px, py, pc), device_id_type=MESH)
            cp.start()
            copies.append(cp)
        for cp in copies:
            cp.wait()
        tot = buf_ref[0]
        for k in range(1, N_DEV):
            tot = tot + buf_ref[k]
        out_ref[...] = tot

    return pl.pallas_call(
        body, name="allreduce_small",
        out_shape=jax.ShapeDtypeStruct((R, C), F32),
        in_specs=[pl.BlockSpec(memory_space=pltpu.VMEM)], out_specs=pl.BlockSpec(memory_space=pltpu.VMEM),
        scratch_shapes=[pltpu.VMEM((N_DEV, R, C), F32), pltpu.SemaphoreType.DMA((7,)), pltpu.SemaphoreType.DMA((7,))],
    )(sp)


def _local_step(x, mem, tgt, wt, small, dist=None):
    w_in = wt["w_in"]
    zpad = jnp.zeros((WB - D - SSD_HEADS, D), BF16)
    w_a = jnp.concatenate([w_in[2576:3600], w_in[4624:6672], w_in[3600:4624]], axis=0)
    w_b = jnp.concatenate([w_in[0:D], w_in[2560:2576], zpad], axis=0)
    w_c = w_in[D:2560]
    a_log, d_skip = small["a_log"], small["d_skip"]
    avec = jnp.pad(-jnp.exp(a_log), ((0, 0), (0, 128 - SSD_HEADS)))
    aexp = jnp.repeat(-jnp.exp(a_log), SSD_P, axis=1)
    dexp = jnp.repeat(d_skip, SSD_P, axis=1)
    dtb = jnp.pad(small["dt_bias"], ((0, 0), (0, 128 - SSD_HEADS)))
    emat = (lax.broadcasted_iota(jnp.int32, (128, D), 0) == lax.broadcasted_iota(jnp.int32, (128, D), 1) // SSD_P
            ).astype(F32)
    hg_nwx = jnp.tile(small["hg_norm_w"], (1, HG_HEADS))
    hgl = small["hg_lower_bounds"]
    nfw = small["norm_final_w"].reshape(1, D)

    received = {}
    pieces = (lambda group, grads: None) if dist is None else dist["pieces"]
    pa, phf, pz, pdt, pc, hn_mix, *got = _inproj(x, small["norm_mix_w"], w_a, w_b, w_c, tm=256,
                                                 gather=None if dist is None else dist["rest_pack"])
    if got:
        wt = {**wt, **dist["unpack_rest"](got[0])}
    xbc = _conv_fwd(pc, small["conv_w"], small["conv_b"], tm=512)
    ya, ypre, ssd_sin = _ssd_fwd(xbc, pz, pdt, dtb, avec, aexp, dexp, small["ssd_norm_w"], emat)
    ob, opre, hg_sin = _hgrn_fwd(pa, phf, hgl, hg_nwx)
    x1 = _mm2_res(x, ya, ob, wt["w_out"], tm=512, name="outproj")
    kvb, mn = _norm_mm(mem, small["norm_mem_w"], wt["xa_wkv"], tm=256, tn=1024, name="mem_kv", emit_h=True,
                       out_dtype=BF16)
    x2 = _xattn_fwd(x1, small["norm_xa_w"], wt["xa_wq"], kvb, wt["xa_wo"], tm=512)
    x3, hn_ffn, gate, up = _ffn_fwd(x2, small["norm_ffn_w"], _interleave(wt["ffn_w_gate"], wt["ffn_w_up"], 1408),
                                    wt["ffn_w_down"], tm=512, tf=1408)

    dx3, dx3b, loss, g_nf = _final(x3, tgt, nfw, tm=512)
    dx2, act, dg, du, g_nffn = _ffn_bwd(x2, dx3, gate, up, small["norm_ffn_w"],
                                        _interleave(wt["ffn_w_gate"], wt["ffn_w_up"], 256), wt["ffn_w_down"],
                                        tm=1024, tf=256)
    g_wg = _dw(dg, hn_ffn, tM=1408, tN=1024, tl=2048, name="dw_gate")
    g_wu = _dw(du, hn_ffn, tM=1408, tN=1024, tl=2048, name="dw_up")
    g_wd = _dw(act, dx3b, tM=1408, tN=1024, tl=2048, name="dw_down")
    dx1, dx1b, hn_xa, dq, ox, dkv, g_nxa = _xattn_bwd(x1, dx2, small["norm_xa_w"], wt["xa_wq"], kvb, wt["xa_wo"],
                                                      tm=512)
    dkvb = dkv.astype(BF16)
    g_wq = _dw(hn_xa, dq, tM=1024, tN=1024, tl=2048, name="dw_q")
    g_wo = _dw(ox, dx2, tM=1024, tN=1024, tl=2048, name="dw_o")
    g_wkv = _dw(dkvb, mn, tM=1024, tN=1024, tl=256, name="dw_kv")
    _, g_nmem = _mm_normbwd([dkvb], [wt["xa_wkv"]], mem, small["norm_mem_w"], None, tm=256, name="mem_bwd")
    dya, dob = _mm_nt2(dx1, wt["w_out"], tm=512, name="outproj_bwd")
    g_wout = jnp.concatenate([_dw(ya, dx1b, tM=1024, tN=1024, tl=2048, name="dw_out_a"),
                              _dw(ob, dx1b, tM=1024, tN=1024, tl=2048, name="dw_out_b")], axis=0)
    ffn_grads = {"ffn_w_gate": g_wg, "ffn_w_up": g_wu, "ffn_w_down": g_wd}
    mid_grads = {"w_out": g_wout, "xa_wq": g_wq, "xa_wkv": g_wkv, "xa_wo": g_wo}
    dpa, g_hgl, g_hgn_x, *got = _hgrn_bwd(pa, phf, opre, hg_sin, dob, hgl, hg_nwx,
                                          exchange=pieces("ffn", ffn_grads))
    received["ffn"] = got[0] if got else None
    dpb, dxbc, g_ssdn, g_dtb, g_alog, g_dx, *got = _ssd_bwd(
        xbc, pz, pdt, ypre, ssd_sin, dya, dtb, avec, aexp, dexp, small["ssd_norm_w"], emat,
        exchange=pieces("mid", mid_grads))
    received["mid"] = got[0] if got else None
    dpc, g_cw, g_cb = _conv_bwd(pc, dxbc, small["conv_w"], small["conv_b"], tm=512)
    g_wa = _dw(dpa, hn_mix, tM=1024, tN=1024, tl=2048, name="dw_in_a")
    g_wb = _dw(dpb, hn_mix, tM=384, tN=1024, tl=2048, name="dw_in_b")
    g_wc = _dw(dpc, hn_mix, tM=512, tN=1024, tl=2048, name="dw_in_c")
    g_win = jnp.concatenate([g_wb[0:D], g_wc, g_wb[D:D + SSD_HEADS], g_wa[0:D], g_wa[3 * D:4 * D], g_wa[D:3 * D]],
                            axis=0)
    grad_x, g_nmix, *got = _mm_normbwd([dpa, dpb, dpc], [w_a, w_b, w_c], x, small["norm_mix_w"], dx1, tm=256,
                                         name="inproj_bwd", exchange=pieces("in", {"w_in": g_win}))
    received["in"] = got[0] if got else None

    big = {"w_in": g_win, **mid_grads, **ffn_grads}
    smallg = {
        "norm_mix_w": g_nmix, "conv_w": g_cw, "conv_b": g_cb, "dt_bias": g_dtb[:, 0:SSD_HEADS],
        "a_log": g_alog[:, 0:SSD_HEADS], "d_skip": g_dx.reshape(SSD_HEADS, SSD_P).sum(axis=1).reshape(1, SSD_HEADS),
        "ssd_norm_w": g_ssdn, "hg_lower_bounds": g_hgl,
        "hg_norm_w": g_hgn_x.reshape(HG_HEADS, HG_K).sum(axis=0).reshape(1, HG_K),
        "norm_xa_w": g_nxa, "norm_mem_w": g_nmem, "norm_ffn_w": g_nffn, "norm_final_w": g_nf,
        "loss": loss[:, 0:1]}
    return grad_x, big, smallg, received


COL_SHARDED = ("w_in", "xa_wkv", "ffn_w_gate", "ffn_w_up")


def _pad_rows(t, rows):
    return jnp.pad(t, [(0, 0)] * (t.ndim - 2) + [(0, rows - t.shape[-2]), (0, 0)])


def _group_fill(parts, group, lead, dtype):
    used = sum(p.shape[-2] for p in parts)
    if used < GROUP_ROWS[group]:
        parts.append(jnp.zeros(lead + (GROUP_ROWS[group] - used, D), dtype))
    return parts


def _pack_group(shards, group, dtype, extra=None):
    parts = [_pad_rows(shards[n].astype(dtype).reshape(r, D), _rows_padded(r)) for n, r in GROUPS[group]]
    if extra is not None:
        parts.append(extra)
    return jnp.concatenate(_group_fill(parts, group, (), dtype), axis=0)


def _unpack_group(packed, group, shapes):
    out, off = {}, 0
    for n, r in GROUPS[group]:
        t = packed[off:off + r]
        out[n] = (t.T if n in COL_SHARDED else t).reshape(shapes[n])
        off += _rows_padded(r)
    return out


def _row_shards(d):
    return {n: d[n][0].T if n in COL_SHARDED else d[n][0] for n in BIG}


def _unpack_gathered(gath, groups):
    out, base = {}, 0
    for group in groups:
        off = base
        for n, r in GROUPS[group]:
            out[n] = gath[:, off:off + r].reshape(N_DEV * r, D)
            off += _rows_padded(r)
        base += GROUP_ROWS[group]
    return out


def _grad_pieces(group, grads):
    parts = [_pad_rows(grads[n].reshape(N_DEV, r, D), _rows_padded(r)) for n, r in GROUPS[group]]
    return jnp.concatenate(_group_fill(parts, group, (N_DEV,), F32), axis=1)


def _pack_small(vals):
    tot = None
    for n, r, c in SMALL:
        row = SMALL_AT[n][0]
        part = jnp.pad(vals[n].reshape(r, c), ((row, SMALL_ROWS - row - r), (0, SMALL_COLS - c)))
        tot = part if tot is None else tot + part
    return tot


WEIGHTS = ['norm_mix_w', 'w_in', 'conv_w', 'conv_b', 'dt_bias', 'a_log', 'd_skip', 'ssd_norm_w', 'hg_lower_bounds',
           'hg_norm_w', 'w_out', 'norm_xa_w', 'norm_mem_w', 'xa_wq', 'xa_wkv', 'xa_wo', 'norm_ffn_w', 'ffn_w_gate',
           'ffn_w_up', 'ffn_w_down', 'norm_final_w']
BIG = tuple(n for n, _ in PACK)


def kernel(x, mem, norm_mix_w, w_in, conv_w, conv_b, dt_bias, a_log, d_skip, ssd_norm_w, hg_lower_bounds, hg_norm_w, w_out, norm_xa_w, norm_mem_w, xa_wq, xa_wkv, xa_wo, norm_ffn_w, ffn_w_gate, ffn_w_up, ffn_w_down, norm_final_w, loss_target, m_norm_mix_w, m_w_in, m_conv_w, m_conv_b, m_dt_bias, m_a_log, m_d_skip, m_ssd_norm_w, m_hg_lower_bounds, m_hg_norm_w, m_w_out, m_norm_xa_w, m_norm_mem_w, m_xa_wq, m_xa_wkv, m_xa_wo, m_norm_ffn_w, m_ffn_w_gate, m_ffn_w_up, m_ffn_w_down, m_norm_final_w, v_norm_mix_w, v_w_in, v_conv_w, v_conv_b, v_dt_bias, v_a_log, v_d_skip, v_ssd_norm_w, v_hg_lower_bounds, v_hg_norm_w, v_w_out, v_norm_xa_w, v_norm_mem_w, v_xa_wq, v_xa_wkv, v_xa_wo, v_norm_ffn_w, v_ffn_w_gate, v_ffn_w_up, v_ffn_w_down, v_norm_final_w):
    args = dict(locals())
    w = {n: args[n] for n in WEIGHTS}
    mo = {n: args["m_" + n] for n in WEIGHTS}
    vo = {n: args["v_" + n] for n in WEIGHTS}
    me = 4 * lax.axis_index("x") + 2 * lax.axis_index("y") + lax.axis_index("c")

    big_sh = _row_shards(w)
    cw_bits = lax.bitcast_convert_type(conv_w[0], BF16).reshape(-1)
    cw_rows = jnp.pad(cw_bits, (0, CONV_BITS_ROWS * D - cw_bits.shape[0])).reshape(CONV_BITS_ROWS, D)
    gath = _allgather(_pack_group(big_sh, "in", BF16, extra=cw_rows))
    wt = _unpack_gathered(gath, ("in",))
    off = _rows_padded(GROUPS["in"][0][1])
    cw_all = lax.bitcast_convert_type(gath[:, off:off + 2].reshape(N_DEV, 2 * D)[:, 0:1536].reshape(N_DEV, 4, 192, 2),
                                      F32)
    conv_w_full = cw_all.transpose(1, 0, 2).reshape(4, 1536)

    small = {n: w[n][0] if w[n].ndim == 3 else w[n] for n in WEIGHTS if n not in BIG}
    small["conv_w"] = conv_w_full
    small["hg_lower_bounds"] = hg_lower_bounds
    dist = {"rest_pack": jnp.concatenate([_pack_group(big_sh, "mid", BF16), _pack_group(big_sh, "ffn", BF16)], axis=0),
            "unpack_rest": lambda g: _unpack_gathered(g, ("mid", "ffn")),
            "pieces": _grad_pieces}
    grad_x, _, gsmall, received = _local_step(x[0], mem[0], loss_target[0], wt, small, dist)

    shapes = {n: w[n].shape for n in BIG}
    m_sh, v_sh = _row_shards(mo), _row_shards(vo)
    out_g, out_d, out_m, out_v = {}, {}, {}, {}
    for group in GROUPS:
        wp = _pack_group(big_sh, group, F32)
        mp = _pack_group(m_sh, group, F32)
        vp = _pack_group(v_sh, group, F32)
        packed = _adamw(received[group], wp, mp, vp, tr=ADAM_ROWS[group], name="adamw_" + group)
        for dst, src in zip((out_g, out_d, out_m, out_v), packed):
            dst.update(_unpack_group(src, group, shapes))

    tot = _small_allreduce(_pack_small(gsmall))
    loss = tot[SMALL_AT["loss"][0], 0]
    cw_row = SMALL_AT["conv_w"][0]
    conv_w_grad = lax.dynamic_slice(tot, (cw_row, me * 192), (4, 192))
    names = [n for n, _, _ in SMALL if n != "loss"]
    as2d = lambda t: t.reshape(t.shape[-2:] if t.ndim > 1 else (1, t.shape[0]))
    small_out = _adamw_small(tot, conv_w_grad, *[[as2d(d[n]) for n in names] for d in (w, mo, vo)])
    for dst, src in zip((out_g, out_d, out_m, out_v), small_out):
        dst.update({n: t.reshape(w[n].shape) for n, t in zip(names, src)})
    return (loss, grad_x[None], *[out_g[n] for n in WEIGHTS], *[out_d[n] for n in WEIGHTS],
            *[out_m[n] for n in WEIGHTS], *[out_v[n] for n in WEIGHTS])
```

```python
import jax
import jax.numpy as jnp
from jax import lax
from jax.experimental import pallas as pl
from jax.experimental.pallas import tpu as pltpu

F32, BF16 = jnp.float32, jnp.bfloat16
MESH = pl.DeviceIdType.MESH

D = 1024
EPS = 1e-6
SSD_HEADS, SSD_P, SSD_N, SSD_Q = 16, 64, 128, 128
HG_HEADS, HG_K, HG_STEP, HG_SUB = 8, 128, 128, 64
XA_HEADS, XA_DH, MEM_LEN = 4, 256, 256
FFN = 2816
N_IN = 6672
N_DEV = 8
WA, WB, WC = 4096, 1152, 1536
ADAM_LR, ADAM_B1, ADAM_B2, ADAM_EPS, ADAM_WD, ADAM_STEP = 0.001, 0.9, 0.999, 1e-08, 0.01, 10
VMEM_MB = 2 ** 20

PACK = (("w_in", 834), ("w_out", 256), ("xa_wq", 128), ("xa_wkv", 256), ("xa_wo", 128),
        ("ffn_w_gate", 352), ("ffn_w_up", 352), ("ffn_w_down", 352))
ROW_TILE = 16
GROUPS = {"in": PACK[0:1], "mid": PACK[1:5], "ffn": PACK[5:8]}
GROUP_ROWS = {"in": 896, "mid": 768, "ffn": 1056}
ADAM_ROWS = {"in": 128, "mid": 128, "ffn": 176}
CONV_BITS_ROWS = ROW_TILE


def _rows_padded(r):
    return -(-r // ROW_TILE) * ROW_TILE

SMALL = (("norm_mix_w", 1, 1024), ("conv_w", 4, 1536), ("conv_b", 1, 1536), ("dt_bias", 1, 16), ("a_log", 1, 16),
         ("d_skip", 1, 16), ("ssd_norm_w", 1, 1024), ("hg_lower_bounds", 2, 1024), ("hg_norm_w", 1, 128),
         ("norm_xa_w", 1, 1024), ("norm_mem_w", 1, 1024), ("norm_ffn_w", 1, 1024), ("norm_final_w", 1, 1024),
         ("loss", 1, 1))
SMALL_COLS = 1536
SMALL_ROWS = 24
SMALL_AT = {n: (sum(q for _, q, _ in SMALL[:i]), r, c) for i, (n, r, c) in enumerate(SMALL)}


def _cparams(sem=None, vmem_mb=48):
    return pltpu.CompilerParams(dimension_semantics=sem, vmem_limit_bytes=vmem_mb * VMEM_MB)


def _dot(a, b):
    return jnp.dot(a.astype(BF16), b.astype(BF16), preferred_element_type=F32)


def _dot_nt(a, b):
    return lax.dot_general(a.astype(BF16), b.astype(BF16), (((1,), (1,)), ((), ())), preferred_element_type=F32)


def _dot_tn(a, b):
    return lax.dot_general(a.astype(BF16), b.astype(BF16), (((0,), (0,)), ((), ())), preferred_element_type=F32)


def _split3(a):
    a1 = a.astype(BF16)
    r1 = a - a1.astype(F32)
    a2 = r1.astype(BF16)
    a3 = (r1 - a2.astype(F32)).astype(BF16)
    return a1, a2, a3


def _dot_hi(a, b, general=0):
    if general == 0:
        return sum(jnp.dot(t, b.astype(BF16), preferred_element_type=F32) for t in _split3(a))
    return sum(jnp.dot(a.astype(BF16), t, preferred_element_type=F32) for t in _split3(b))


def _dot_nt_hi(a, b):
    return sum(_dot_nt(t, b) for t in _split3(a))


def _sigmoid(x):
    return 1.0 / (1.0 + jnp.exp(-x))


def _sigmoid_gate(x):
    return pl.reciprocal(1.0 + jnp.exp(-x), approx=True)


def _silu(x):
    return x * _sigmoid_gate(x)


def _dsilu(x):
    s = _sigmoid_gate(x)
    return s * (1.0 + x * (1.0 - s))


def _softplus(x):
    return jnp.maximum(x, 0.0) + jnp.log(1.0 + jnp.exp(-jnp.abs(x)))


def _rms_fwd(x, w):
    r = lax.rsqrt(jnp.mean(x * x, axis=1, keepdims=True) + EPS)
    return x * r * w


def _rms_bwd(dy, x, w):
    r = lax.rsqrt(jnp.mean(x * x, axis=1, keepdims=True) + EPS)
    xh = x * r
    g = dy * w
    dx = r * (g - xh * jnp.mean(g * xh, axis=1, keepdims=True))
    return dx, jnp.sum(dy * xh, axis=0, keepdims=True)


def _iota2(shape, dim):
    return lax.broadcasted_iota(jnp.int32, shape, dim)


def _tril(n):
    return (_iota2((n, n), 0) >= _iota2((n, n), 1)).astype(F32)


def _triu(n):
    return (_iota2((n, n), 0) <= _iota2((n, n), 1)).astype(F32)


def _norm_mm(x, nw, w, *, tm, tn, name, emit_h=False, out_dtype=F32):
    L, K = x.shape
    N = w.shape[0]
    tm, tn = min(tm, L), min(tn, N)
    ni, nj = L // tm, N // tn

    def body(x_ref, nw_ref, w_ref, *rest):
        if emit_h:
            o_ref, h_ref, hs_ref = rest
        else:
            o_ref, hs_ref = rest

        @pl.when(pl.program_id(1) == 0)
        def _():
            h = _rms_fwd(x_ref[...], nw_ref[...]).astype(BF16)
            hs_ref[...] = h
            if emit_h:
                h_ref[...] = h

        o_ref[...] = _dot_nt(hs_ref[...], w_ref[...]).astype(out_dtype)

    out_shape = [jax.ShapeDtypeStruct((L, N), out_dtype)]
    out_specs = [pl.BlockSpec((tm, tn), lambda i, j: (i, j))]
    if emit_h:
        out_shape.append(jax.ShapeDtypeStruct((L, K), BF16))
        out_specs.append(pl.BlockSpec((tm, K), lambda i, j: (i, 0)))
    res = pl.pallas_call(
        body, name=name, grid=(ni, nj),
        in_specs=[pl.BlockSpec((tm, K), lambda i, j: (i, 0)), pl.BlockSpec((1, K), lambda i, j: (0, 0)),
                  pl.BlockSpec((tn, K), lambda i, j: (j, 0))],
        out_specs=out_specs, out_shape=out_shape, scratch_shapes=[pltpu.VMEM((tm, K), BF16)],
        compiler_params=_cparams(("parallel", "arbitrary")),
    )(x, nw, w)
    return res if len(res) > 1 else res[0]


def _inproj(x, nw, w_a, w_b, w_c, *, tm, gather=None):
    L, K = x.shape
    tm = min(tm, L)
    ni = L // tm

    def body(x_ref, nw_ref, wa_ref, wb_ref, wc_ref, pa_ref, hf_ref, z_ref, dt_ref, pc_ref, h_ref):
        h = _rms_fwd(x_ref[...], nw_ref[...]).astype(BF16)
        h_ref[...] = h
        pa_ref[...] = _dot_nt(h, wa_ref[0:3 * D, :]).astype(BF16)
        hf_ref[...] = _dot_nt(h, wa_ref[3 * D:4 * D, :])
        pb = _dot_nt(h, wb_ref[...])
        z_ref[...] = pb[:, 0:D].astype(BF16)
        dt_ref[...] = pb[:, D:D + 128]
        pc_ref[...] = _dot_nt(h, wc_ref[...]).astype(BF16)

    row = lambda n: pl.BlockSpec((tm, n), lambda i: (i, 0))
    full = lambda a: pl.BlockSpec(a.shape, lambda i: (0, 0))
    at = lambda i: lambda: pl.program_id(0) == i
    payload = None if gather is None else (_Gather, gather)
    phases = [(at(0), _Gather.start, True), (at(ni // 2), _Gather.forward, True), (at(ni - 1), _Gather.finish, False)]
    in_specs, out_specs, out_shape, scratch = _carried_specs(
        payload, [row(K), full(nw), full(w_a), full(w_b), full(w_c)],
        [row(3 * D), row(D), row(D), row(128), row(WC), row(K)],
        [jax.ShapeDtypeStruct((L, 3 * D), BF16), jax.ShapeDtypeStruct((L, D), F32),
         jax.ShapeDtypeStruct((L, D), BF16), jax.ShapeDtypeStruct((L, 128), F32),
         jax.ShapeDtypeStruct((L, WC), BF16), jax.ShapeDtypeStruct((L, K), BF16)], [])
    args = [x, nw, w_a, w_b, w_c]
    return pl.pallas_call(
        _carried(body, 5, 6, payload, phases), name="inproj", grid=(ni,),
        in_specs=in_specs, out_specs=out_specs, out_shape=out_shape, scratch_shapes=scratch,
        compiler_params=_cparams(("arbitrary",), vmem_mb=58),
    )(*(args if gather is None else args + [gather]))


def _mm2_res(res, a1, a2, w, *, tm, name):
    L, N = res.shape
    K = a1.shape[1]
    tm = min(tm, L)

    def body(r_ref, a1_ref, a2_ref, w_ref, o_ref):
        acc = jnp.dot(a1_ref[...], w_ref[0:K, :], preferred_element_type=F32)
        acc += jnp.dot(a2_ref[...], w_ref[K:2 * K, :], preferred_element_type=F32)
        o_ref[...] = r_ref[...] + acc

    return pl.pallas_call(
        body, name=name, grid=(L // tm,),
        in_specs=[pl.BlockSpec((tm, N), lambda i: (i, 0)), pl.BlockSpec((tm, K), lambda i: (i, 0)),
                  pl.BlockSpec((tm, K), lambda i: (i, 0)), pl.BlockSpec((2 * K, N), lambda i: (0, 0))],
        out_specs=pl.BlockSpec((tm, N), lambda i: (i, 0)),
        out_shape=jax.ShapeDtypeStruct((L, N), F32),
        compiler_params=_cparams(("parallel",)),
    )(res, a1, a2, w)


def _mm_nt2(a, w, *, tm, name):
    L, N = a.shape
    K = w.shape[0] // 2
    tm = min(tm, L)

    def body(a_ref, w_ref, o1_ref, o2_ref):
        av = a_ref[...].astype(BF16)
        o1_ref[...] = _dot_nt(av, w_ref[0:K, :]).astype(BF16)
        o2_ref[...] = _dot_nt(av, w_ref[K:2 * K, :]).astype(BF16)

    return pl.pallas_call(
        body, name=name, grid=(L // tm,),
        in_specs=[pl.BlockSpec((tm, N), lambda i: (i, 0)), pl.BlockSpec((2 * K, N), lambda i: (0, 0))],
        out_specs=[pl.BlockSpec((tm, K), lambda i: (i, 0)), pl.BlockSpec((tm, K), lambda i: (i, 0))],
        out_shape=[jax.ShapeDtypeStruct((L, K), BF16), jax.ShapeDtypeStruct((L, K), BF16)],
        compiler_params=_cparams(("parallel",)),
    )(a, w)


def _dw(a, b, *, tM, tN, tl, name):
    L, M = a.shape
    N = b.shape[1]
    tM, tN, tl = min(tM, M), min(tN, N), min(tl, L)

    def body(a_ref, b_ref, o_ref):
        @pl.when(pl.program_id(2) == 0)
        def _():
            o_ref[...] = jnp.zeros_like(o_ref)

        o_ref[...] += _dot_tn(a_ref[...], b_ref[...])

    return pl.pallas_call(
        body, name=name, grid=(M // tM, N // tN, L // tl),
        in_specs=[pl.BlockSpec((tl, tM), lambda i, j, l: (l, i)), pl.BlockSpec((tl, tN), lambda i, j, l: (l, j))],
        out_specs=pl.BlockSpec((tM, tN), lambda i, j, l: (i, j)),
        out_shape=jax.ShapeDtypeStruct((M, N), F32),
        compiler_params=_cparams(("parallel", "parallel", "arbitrary")),
    )(a, b)


def _first_last(n):
    return [(lambda: pl.program_id(0) == 0, _Exchange.start, True),
            (lambda: pl.program_id(0) == n - 1, _Exchange.finish, False)]


def _mm_normbwd(a_list, w_list, x, nw, res, *, tm, name, exchange=None):
    L, Dm = x.shape
    tm = min(tm, L)
    n = len(a_list)
    has_res = res is not None

    def body(*refs):
        a_refs, w_refs = refs[:n], refs[n:2 * n]
        x_ref, nw_ref = refs[2 * n], refs[2 * n + 1]
        k = 2 * n + 2
        r_ref = refs[k] if has_res else None
        dx_ref, dnw_ref = refs[k + has_res], refs[k + has_res + 1]
        dh = _dot(a_refs[0][...], w_refs[0][...])
        for a_ref, w_ref in zip(a_refs[1:], w_refs[1:]):
            dh += _dot(a_ref[...], w_ref[...])
        dx, dnw = _rms_bwd(dh, x_ref[...], nw_ref[...])
        dx_ref[...] = dx + r_ref[...] if has_res else dx

        @pl.when(pl.program_id(0) == 0)
        def _():
            dnw_ref[...] = jnp.zeros_like(dnw_ref)

        dnw_ref[...] += dnw

    in_specs = [pl.BlockSpec((tm, a.shape[1]), lambda i: (i, 0)) for a in a_list]
    in_specs += [pl.BlockSpec(w.shape, lambda i: (0, 0)) for w in w_list]
    in_specs += [pl.BlockSpec((tm, Dm), lambda i: (i, 0)), pl.BlockSpec((1, Dm), lambda i: (0, 0))]
    args = [*a_list, *w_list, x, nw]
    if has_res:
        in_specs.append(pl.BlockSpec((tm, Dm), lambda i: (i, 0)))
        args.append(res)
    payload = None if exchange is None else (_Exchange, exchange)
    n_in = len(args)
    if exchange is not None:
        args.append(exchange)
    in_specs, out_specs, out_shape, scratch = _carried_specs(
        payload, in_specs, [pl.BlockSpec((tm, Dm), lambda i: (i, 0)), pl.BlockSpec((1, Dm), lambda i: (0, 0))],
        [jax.ShapeDtypeStruct((L, Dm), F32), jax.ShapeDtypeStruct((1, Dm), F32)], [])
    return pl.pallas_call(
        _carried(body, n_in, 2, payload, _first_last(L // tm)), name=name, grid=(L // tm,), in_specs=in_specs,
        out_specs=out_specs, out_shape=out_shape, scratch_shapes=scratch,
        compiler_params=_cparams(("arbitrary",), vmem_mb=56),
    )(*args)


CONV_TN = 512


HALO = 16


def _conv_pre(cat, w_ref, b_ref, rows):
    shifted = [pltpu.roll(cat, 3 - k, 0)[HALO:HALO + rows] for k in range(3)] + [cat[HALO:HALO + rows]]
    pre = b_ref[...] + w_ref[3:4, :] * shifted[3]
    for k in range(3):
        pre += w_ref[k:k + 1, :] * shifted[k]
    return pre, shifted


def _conv_fwd(pc, cw, cb, *, tm):
    L, C = pc.shape
    tm = min(tm, L)
    tn = CONV_TN

    def body(u_ref, halo_ref, w_ref, b_ref, o_ref):
        halo = jnp.where(pl.program_id(1) > 0, halo_ref[...].astype(F32), 0.0)
        cat = jnp.concatenate([halo, u_ref[...].astype(F32)], axis=0)
        pre, _ = _conv_pre(cat, w_ref, b_ref, tm)
        o_ref[...] = _silu(pre).astype(BF16)

    return pl.pallas_call(
        body, name="conv_fwd", grid=(C // tn, L // tm),
        in_specs=[pl.BlockSpec((tm, tn), lambda j, i: (i, j)),
                  pl.BlockSpec((HALO, tn), lambda j, i: (jnp.maximum(i * (tm // HALO) - 1, 0), j)),
                  pl.BlockSpec((4, tn), lambda j, i: (0, j)), pl.BlockSpec((1, tn), lambda j, i: (0, j))],
        out_specs=pl.BlockSpec((tm, tn), lambda j, i: (i, j)),
        out_shape=jax.ShapeDtypeStruct((L, C), BF16),
        compiler_params=_cparams(("parallel", "parallel")),
    )(pc, pc, cw, cb)


def _conv_bwd(pc, dact, cw, cb, *, tm):
    L, C = pc.shape
    tm = min(tm, L)
    tn = CONV_TN
    nt = L // tm

    def body(u_ref, halo_ref, unext_ref, da_ref, danext_ref, w_ref, b_ref, du_ref, dw_ref, db_ref):
        i = pl.program_id(1)
        halo = jnp.where(i > 0, halo_ref[...].astype(F32), 0.0)
        cat = jnp.concatenate([halo, u_ref[...].astype(F32), unext_ref[...].astype(F32)], axis=0)
        pre, shifted = _conv_pre(cat, w_ref, b_ref, tm + HALO)
        da = jnp.concatenate([da_ref[...].astype(F32),
                              jnp.where(i < nt - 1, danext_ref[...].astype(F32), 0.0)], axis=0)
        dpre = da * _dsilu(pre)
        du = w_ref[3:4, :] * dpre[0:tm]
        for k in range(3):
            du += w_ref[k:k + 1, :] * pltpu.roll(dpre, tm + HALO - (3 - k), 0)[0:tm]
        du_ref[...] = du.astype(BF16)

        @pl.when(i == 0)
        def _():
            dw_ref[...] = jnp.zeros_like(dw_ref)
            db_ref[...] = jnp.zeros_like(db_ref)

        dp = dpre[0:tm]
        dw_ref[...] += jnp.concatenate(
            [jnp.sum(dp * shifted[k][0:tm], axis=0, keepdims=True) for k in range(4)], axis=0)
        db_ref[...] += jnp.sum(dp, axis=0, keepdims=True)

    nb = L // HALO
    return pl.pallas_call(
        body, name="conv_bwd", grid=(C // tn, nt),
        in_specs=[pl.BlockSpec((tm, tn), lambda j, i: (i, j)),
                  pl.BlockSpec((HALO, tn), lambda j, i: (jnp.maximum(i * (tm // HALO) - 1, 0), j)),
                  pl.BlockSpec((HALO, tn), lambda j, i: (jnp.minimum((i + 1) * (tm // HALO), nb - 1), j)),
                  pl.BlockSpec((tm, tn), lambda j, i: (i, j)),
                  pl.BlockSpec((HALO, tn), lambda j, i: (jnp.minimum((i + 1) * (tm // HALO), nb - 1), j)),
                  pl.BlockSpec((4, tn), lambda j, i: (0, j)), pl.BlockSpec((1, tn), lambda j, i: (0, j))],
        out_specs=[pl.BlockSpec((tm, tn), lambda j, i: (i, j)), pl.BlockSpec((4, tn), lambda j, i: (0, j)),
                   pl.BlockSpec((1, tn), lambda j, i: (0, j))],
        out_shape=[jax.ShapeDtypeStruct((L, C), BF16), jax.ShapeDtypeStruct((4, C), F32),
                   jax.ShapeDtypeStruct((1, C), F32)],
        compiler_params=_cparams(("parallel", "arbitrary")),
    )(pc, pc, pc, dact, dact, cw, cb)


def _ssd_common(dtr_ref, dtb_ref, avec_ref, aexp_ref, e_ref, acx_ref, acol_ref, arow_ref):
    q = SSD_Q
    tril = _tril(q)
    dtpre = dtr_ref[...] + dtb_ref[...]
    dt = _softplus(dtpre)
    dtx = _dot_hi(dt, e_ref[...])
    acx_ref[...] = _dot_hi(tril, dtx * aexp_ref[...], 1)
    acol = _dot_hi(tril, dt * avec_ref[...], 1)
    acol_ref[...] = acol
    arow_ref[...] = acol.T
    return dtpre, dt, dtx


def _ssd_fwd(xbc, pz, pdt, dtb, avec, aexp, dexp, nw, emat):
    L = xbc.shape[0]
    q = SSD_Q
    nc = L // q

    def body(xbc_ref, z_ref, dtr_ref, dtb_ref, avec_ref, aexp_ref, dexp_ref, nw_ref, e_ref,
             ya_ref, ypre_ref, sin_ref, st_ref, acx_ref, acol_ref, arow_ref, xdt_ref, y_ref):
        @pl.when(pl.program_id(0) == 0)
        def _():
            st_ref[...] = jnp.zeros_like(st_ref)

        sin_ref[...] = st_ref[...].astype(BF16)
        _, _, dtx = _ssd_common(dtr_ref, dtb_ref, avec_ref, aexp_ref, e_ref, acx_ref, acol_ref, arow_ref)
        xs = xbc_ref[:, 0:D].astype(F32)
        xdt = xs * dtx
        xdt_ref[...] = xdt
        acx = acx_ref[...]
        alast = acx_ref[q - 1:q, :]
        xdtd = xdt * jnp.exp(alast - acx)
        eac = jnp.exp(acx)
        ealast = jnp.exp(alast)
        causal = _iota2((q, q), 0) >= _iota2((q, q), 1)
        for g in range(2):
            gs = slice(512 * g, 512 * g + 512)
            bm = xbc_ref[:, D + 128 * g:D + 128 * g + 128]
            cm = xbc_ref[:, D + 256 + 128 * g:D + 256 + 128 * g + 128]
            stg = st_ref[:, gs]
            yoff = _dot(cm, stg) * eac[:, gs]
            gmat = _dot_nt(cm, bm)
            for e in range(8):
                h = 8 * g + e
                hs = slice(64 * h, 64 * h + 64)
                col = acol_ref[:, h:h + 1]
                row = arow_ref[h:h + 1, :]
                lm = jnp.exp(jnp.where(causal, col - row, -1e30))
                y_ref[:, hs] = _dot(gmat * lm, xdt_ref[:, hs])
            y_ref[:, gs] += yoff + dexp_ref[:, gs] * xs[:, gs]
            st_ref[:, gs] = stg * ealast[:, gs] + _dot_tn(bm, xdtd[:, gs])
        ypre_ref[...] = y_ref[...].astype(BF16)
        for g in range(2):
            gs = slice(512 * g, 512 * g + 512)
            yz = y_ref[:, gs] * _silu(z_ref[:, gs].astype(F32))
            ya_ref[:, gs] = _rms_fwd(yz, nw_ref[:, gs]).astype(BF16)

    vec = lambda n: pl.BlockSpec((1, n), lambda c: (0, 0))
    return pl.pallas_call(
        body, name="ssd_fwd", grid=(nc,),
        in_specs=[pl.BlockSpec((q, 1536), lambda c: (c, 0)), pl.BlockSpec((q, D), lambda c: (c, 0)),
                  pl.BlockSpec((q, 128), lambda c: (c, 0)), vec(128), vec(128), vec(D), vec(D), vec(D),
                  pl.BlockSpec((128, D), lambda c: (0, 0))],
        out_specs=[pl.BlockSpec((q, D), lambda c: (c, 0)), pl.BlockSpec((q, D), lambda c: (c, 0)),
                   pl.BlockSpec((128, D), lambda c: (c, 0))],
        out_shape=[jax.ShapeDtypeStruct((L, D), BF16), jax.ShapeDtypeStruct((L, D), BF16),
                   jax.ShapeDtypeStruct((nc * 128, D), BF16)],
        scratch_shapes=[pltpu.VMEM((128, D), F32), pltpu.VMEM((q, D), F32), pltpu.VMEM((q, 128), F32),
                        pltpu.VMEM((128, q), F32), pltpu.VMEM((q, D), F32), pltpu.VMEM((q, D), F32)],
        compiler_params=_cparams(("arbitrary",)),
    )(xbc, pz, pdt, dtb, avec, aexp, dexp, nw, emat)


def _ssd_bwd(xbc, pz, pdt, ypre, sin, dya, dtb, avec, aexp, dexp, nw, emat, exchange=None):
    L = xbc.shape[0]
    q = SSD_Q
    nc = L // q

    def body(xbc_ref, z_ref, dtr_ref, ypre_ref, sin_ref, dya_ref, dtb_ref, avec_ref, aexp_ref, dexp_ref, nw_ref,
             e_ref, dpb_ref, dxbc_ref, dnw_ref, ddtb_ref, da_ref, ddx_ref,
             dst_ref, acx_ref, acol_ref, arow_ref, xdt_ref, dxdt_ref, dy_ref, dacx_ref):
        @pl.when(pl.program_id(0) == 0)
        def _():
            dst_ref[...] = jnp.zeros_like(dst_ref)
            dnw_ref[...] = jnp.zeros_like(dnw_ref)
            ddtb_ref[...] = jnp.zeros_like(ddtb_ref)
            da_ref[...] = jnp.zeros_like(da_ref)
            ddx_ref[...] = jnp.zeros_like(ddx_ref)

        dtpre, dt, dtx = _ssd_common(dtr_ref, dtb_ref, avec_ref, aexp_ref, e_ref, acx_ref, acol_ref, arow_ref)
        xs = xbc_ref[:, 0:D].astype(F32)
        xdt = xs * dtx
        xdt_ref[...] = xdt
        acx = acx_ref[...]
        alast = acx_ref[q - 1:q, :]
        dec_end = jnp.exp(alast - acx)
        xdtd = xdt * dec_end
        eac = jnp.exp(acx)
        ealast = jnp.exp(alast)
        for g in range(2):
            gs = slice(512 * g, 512 * g + 512)
            y = ypre_ref[:, gs].astype(F32)
            z = z_ref[:, gs].astype(F32)
            sz = _silu(z)
            dyz, dnw = _rms_bwd(dya_ref[:, gs].astype(F32), y * sz, nw_ref[:, gs])
            dnw_ref[:, gs] += dnw
            dy_ref[:, gs] = dyz * sz
            dpb_ref[:, gs] = (dyz * y * _dsilu(z)).astype(BF16)
        dy = dy_ref[...]
        ddx_ref[...] += jnp.sum(dy * xs, axis=0, keepdims=True)
        ri = _iota2((q, q), 0)
        ci = _iota2((q, q), 1)
        causal = ri >= ci
        causal_t = ri <= ci
        dacol = jnp.zeros((q, 128), F32)
        dacol_t = jnp.zeros((128, q), F32)
        last_row = _iota2((q, 512), 0) == q - 1
        for g in range(2):
            gs = slice(512 * g, 512 * g + 512)
            bm = xbc_ref[:, D + 128 * g:D + 128 * g + 128]
            cm = xbc_ref[:, D + 256 + 128 * g:D + 256 + 128 * g + 128]
            stg = sin_ref[:, gs].astype(F32)
            dstg = dst_ref[:, gs]
            dyg = dy[:, gs]
            yoff = _dot(cm, stg) * eac[:, gs]
            dwm = dyg * eac[:, gs]
            dcm = _dot_nt(dwm, stg)
            dstin = _dot_tn(cm, dwm)
            dacx_g = dyg * yoff
            dxdtd = _dot(bm, dstg)
            dbm = _dot_nt(xdtd[:, gs], dstg)
            t = dxdtd * xdtd[:, gs]
            dacx_g -= t
            dalast = jnp.sum(t, axis=0, keepdims=True) + jnp.sum(dstg * stg, axis=0, keepdims=True) * ealast[:, gs]
            dst_ref[:, gs] = dstin + dstg * ealast[:, gs]
            dacx_ref[:, gs] = dacx_g + jnp.where(last_row, dalast, 0.0)
            gmat = _dot_nt(cm, bm)
            gmat_t = _dot_nt(bm, cm)
            dg = jnp.zeros((q, q), F32)
            for e in range(8):
                h = 8 * g + e
                hs = slice(64 * h, 64 * h + 64)
                col = acol_ref[:, h:h + 1]
                row = arow_ref[h:h + 1, :]
                lm = jnp.exp(jnp.where(causal, col - row, -1e30))
                lm_t = jnp.exp(jnp.where(causal_t, row - col, -1e30))
                dyh = dy_ref[:, hs]
                dm = _dot_nt(dyh, xdt_ref[:, hs])
                dxdt_ref[:, hs] = _dot(gmat_t * lm_t, dyh)
                dml = dm * lm
                dg += dml
                p = dml * gmat
                dacol += jnp.where(ci == h, jnp.sum(p, axis=1, keepdims=True), 0.0)
                dacol_t -= jnp.where(ri == h, jnp.sum(p, axis=0, keepdims=True), 0.0)
            dcm += _dot(dg, bm)
            dbm += _dot_tn(dg, cm)
            dxbc_ref[:, D + 128 * g:D + 128 * g + 128] = dbm.astype(BF16)
            dxbc_ref[:, D + 256 + 128 * g:D + 256 + 128 * g + 128] = dcm.astype(BF16)
            dxdt_ref[:, gs] += dxdtd * dec_end[:, gs]
        dxdt = dxdt_ref[...]
        dacum = dacol + dacol_t.T + _dot_nt_hi(dacx_ref[...], e_ref[...])
        da = _dot_hi(_triu(q), dacum, 1)
        ddt = da * avec_ref[...] + _dot_nt_hi(dxdt * xs, e_ref[...])
        da_ref[...] += jnp.sum(da * dt, axis=0, keepdims=True) * avec_ref[...]
        dxbc_ref[:, 0:D] = (dexp_ref[...] * dy + dxdt * dtx).astype(BF16)
        ddtr = ddt * _sigmoid(dtpre)
        ddtb_ref[...] += jnp.sum(ddtr, axis=0, keepdims=True)
        dpb_ref[:, D:D + 128] = ddtr.astype(BF16)

    rev = lambda c: nc - 1 - c
    vec = lambda n: pl.BlockSpec((1, n), lambda c: (0, 0))
    payload = None if exchange is None else (_Exchange, exchange)
    args = [xbc, pz, pdt, ypre, sin, dya, dtb, avec, aexp, dexp, nw, emat]
    in_specs, out_specs, out_shape, scratch = _carried_specs(
        payload,
        [pl.BlockSpec((q, 1536), lambda c: (rev(c), 0)), pl.BlockSpec((q, D), lambda c: (rev(c), 0)),
         pl.BlockSpec((q, 128), lambda c: (rev(c), 0)), pl.BlockSpec((q, D), lambda c: (rev(c), 0)),
         pl.BlockSpec((128, D), lambda c: (rev(c), 0)), pl.BlockSpec((q, D), lambda c: (rev(c), 0)),
         vec(128), vec(128), vec(D), vec(D), vec(D), pl.BlockSpec((128, D), lambda c: (0, 0))],
        [pl.BlockSpec((q, WB), lambda c: (rev(c), 0)), pl.BlockSpec((q, 1536), lambda c: (rev(c), 0)),
         vec(D), vec(128), vec(128), vec(D)],
        [jax.ShapeDtypeStruct((L, WB), BF16), jax.ShapeDtypeStruct((L, 1536), BF16),
         jax.ShapeDtypeStruct((1, D), F32), jax.ShapeDtypeStruct((1, 128), F32),
         jax.ShapeDtypeStruct((1, 128), F32), jax.ShapeDtypeStruct((1, D), F32)],
        [pltpu.VMEM((128, D), F32), pltpu.VMEM((q, D), F32), pltpu.VMEM((q, 128), F32),
         pltpu.VMEM((128, q), F32), pltpu.VMEM((q, D), F32), pltpu.VMEM((q, D), F32),
         pltpu.VMEM((q, D), F32), pltpu.VMEM((q, D), F32)])
    return pl.pallas_call(
        _carried(body, len(args), 6, payload, _first_last(nc)), name="ssd_bwd", grid=(nc,),
        in_specs=in_specs, out_specs=out_specs, out_shape=out_shape, scratch_shapes=scratch,
        compiler_params=_cparams(("arbitrary",)),
    )(*(args if exchange is None else args + [exchange]))


def _hg_gates(hq, hf, hgl_ref, b_ref):
    lb = 1.0 / (1.0 + jnp.exp(hgl_ref[1:2, :] - hgl_ref[0:1, :]))
    qf = _silu(hq)
    sg = _sigmoid(hf)
    f = lb + (1.0 - lb) * sg
    b_ref[...] = _dot_hi(_tril(HG_STEP), jnp.log(f), 1)
    return lb, qf, sg, f


def _hg_factors(qf, kf, b_ref):
    s, n = HG_SUB, HG_STEP
    b = b_ref[...]
    blast = b_ref[n - 1:n, :]
    m0, mb, m1 = b_ref[s // 2 - 1:s // 2, :], b_ref[s - 1:s, :], b_ref[s + s // 2 - 1:s + s // 2, :]
    b0, b1 = b[0:s], b[s:n]
    q0, q1, k0, k1 = qf[0:s], qf[s:n], kf[0:s], kf[s:n]
    fac = dict(
        eb=jnp.exp(blast), eq=jnp.exp(b), ek=jnp.exp(blast - b),
        eq0=jnp.exp(b0 - m0), ek0=jnp.exp(m0 - b0), eq1=jnp.exp(b1 - m1), ek1=jnp.exp(m1 - b1),
        eqb=jnp.exp(b1 - mb), ekb=jnp.exp(mb - b0))
    rd = lambda t: t.astype(BF16).astype(F32)
    val = dict(qe=qf * fac["eq"], ke=kf * fac["ek"], qm0=rd(q0 * fac["eq0"]), km0=rd(k0 * fac["ek0"]),
               qm1=rd(q1 * fac["eq1"]), km1=rd(k1 * fac["ek1"]), qb=rd(q1 * fac["eqb"]), kb=rd(k0 * fac["ekb"]))
    return fac, val


def _hgrn_fwd(pa, phf, hgl, nwx):
    L = pa.shape[0]
    n, s = HG_STEP, HG_SUB
    nc = L // n

    def body(hq_ref, hf_ref, hi_ref, hg_ref, hgl_ref, nw_ref, ob_ref, opre_ref, sin_ref, st_ref, b_ref):
        @pl.when(pl.program_id(0) == 0)
        def _():
            st_ref[...] = jnp.zeros_like(st_ref)

        sin_ref[...] = st_ref[...].astype(BF16)
        _, qf, _, f = _hg_gates(hq_ref[...].astype(F32), hf_ref[...], hgl_ref, b_ref)
        fac, val = _hg_factors(qf, 1.0 - f, b_ref)
        causal = _iota2((s, s), 0) >= _iota2((s, s), 1)
        for h in range(HG_HEADS):
            hs = slice(128 * h, 128 * h + 128)
            sth = st_ref[:, hs]
            v = hi_ref[:, hs]
            v0, v1 = v[0:s], v[s:n]
            a00 = jnp.where(causal, _dot_nt(val["qm0"][:, hs], val["km0"][:, hs]), 0.0)
            a11 = jnp.where(causal, _dot_nt(val["qm1"][:, hs], val["km1"][:, hs]), 0.0)
            a10 = _dot_nt(val["qb"][:, hs], val["kb"][:, hs])
            o = _dot_nt(val["qe"][:, hs], sth) + jnp.concatenate(
                [_dot(a00, v0), _dot(a10, v0) + _dot(a11, v1)], axis=0)
            st_ref[:, hs] = sth * fac["eb"][:, hs] + _dot_tn(v, val["ke"][:, hs])
            opre_ref[:, hs] = o.astype(BF16)
            ob_ref[:, hs] = (_rms_fwd(o, nw_ref[:, hs]) * _silu(hg_ref[:, hs].astype(F32))).astype(BF16)

    blk = lambda j: pl.BlockSpec((n, D), lambda c: (c, j))
    return pl.pallas_call(
        body, name="hgrn_fwd", grid=(nc,),
        in_specs=[blk(0), blk(0), blk(1), blk(2), pl.BlockSpec((2, D), lambda c: (0, 0)),
                  pl.BlockSpec((1, D), lambda c: (0, 0))],
        out_specs=[blk(0), blk(0), blk(0)],
        out_shape=[jax.ShapeDtypeStruct((L, D), BF16), jax.ShapeDtypeStruct((L, D), BF16),
                   jax.ShapeDtypeStruct((nc * 128, D), BF16)],
        scratch_shapes=[pltpu.VMEM((128, D), F32), pltpu.VMEM((n, D), F32)],
        compiler_params=_cparams(("arbitrary",)),
    )(pa, phf, pa, pa, hgl, nwx)


def _hgrn_bwd(pa, phf, opre, sin, dob, hgl, nwx, exchange=None):
    L = pa.shape[0]
    n, s = HG_STEP, HG_SUB
    nc = L // n

    def body(hq_ref, hf_ref, hi_ref, hg_ref, opre_ref, sin_ref, dob_ref, hgl_ref, nw_ref,
             dpa_ref, dhgl_ref, dnw_ref, dst_ref, b_ref, dlb_ref, dq_ref, dk_ref, db_ref):
        i = pl.program_id(0)

        @pl.when(i == 0)
        def _():
            dst_ref[...] = jnp.zeros_like(dst_ref)
            dlb_ref[...] = jnp.zeros_like(dlb_ref)
            dnw_ref[...] = jnp.zeros_like(dnw_ref)

        hq = hq_ref[...].astype(F32)
        lb, qf, sg, f = _hg_gates(hq, hf_ref[...], hgl_ref, b_ref)
        kf = 1.0 - f
        fac, val = _hg_factors(qf, kf, b_ref)
        ri, ci = _iota2((s, s), 0), _iota2((s, s), 1)
        causal, causal_t = ri >= ci, ri <= ci
        last_row = _iota2((n, 128), 0) == n - 1
        for h in range(HG_HEADS):
            hs = slice(128 * h, 128 * h + 128)
            o = opre_ref[:, hs].astype(F32)
            gate = hg_ref[:, hs].astype(F32)
            dout = dob_ref[:, hs].astype(F32)
            sgate = _silu(gate)
            do, dnw = _rms_bwd(dout * sgate, o, nw_ref[:, hs])
            dnw_ref[:, hs] += dnw
            dpa_ref[:, 2 * D + 128 * h:2 * D + 128 * h + 128] = (
                dout * _rms_fwd(o, nw_ref[:, hs]) * _dsilu(gate)).astype(BF16)
            sth = sin_ref[:, hs].astype(F32)
            dsth = dst_ref[:, hs]
            v = hi_ref[:, hs]
            v0, v1 = v[0:s], v[s:n]
            do0, do1 = do[0:s], do[s:n]
            qe, ke = val["qe"][:, hs], val["ke"][:, hs]
            qm0, km0, qm1, km1 = val["qm0"][:, hs], val["km0"][:, hs], val["qm1"][:, hs], val["km1"][:, hs]
            qb, kb = val["qb"][:, hs], val["kb"][:, hs]
            dqe = _dot(do, sth)
            dstin = _dot_tn(do, qe)
            a00t = jnp.where(causal_t, _dot_nt(km0, qm0), 0.0)
            a11t = jnp.where(causal_t, _dot_nt(km1, qm1), 0.0)
            a10t = _dot_nt(kb, qb)
            dat00 = jnp.where(causal, _dot_nt(do0, v0), 0.0)
            dat11 = jnp.where(causal, _dot_nt(do1, v1), 0.0)
            dat10 = _dot_nt(do1, v0)
            dat00t = jnp.where(causal_t, _dot_nt(v0, do0), 0.0)
            dat11t = jnp.where(causal_t, _dot_nt(v1, do1), 0.0)
            dat10t = _dot_nt(v0, do1)
            dv = jnp.concatenate([_dot(a00t, do0) + _dot(a10t, do1), _dot(a11t, do1)], axis=0)
            dqm0, dkm0 = _dot(dat00, km0), _dot(dat00t, qm0)
            dqm1, dkm1 = _dot(dat11, km1), _dot(dat11t, qm1)
            dqb, dkb = _dot(dat10, kb), _dot(dat10t, qb)
            dke = _dot(v, dsth)
            dv += _dot_nt(ke, dsth)
            deb = jnp.sum(dsth * sth, axis=0, keepdims=True)
            dst_ref[:, hs] = dstin + dsth * fac["eb"][:, hs]
            dq = dqe * fac["eq"][:, hs] + jnp.concatenate(
                [dqm0 * fac["eq0"][:, hs], dqm1 * fac["eq1"][:, hs] + dqb * fac["eqb"][:, hs]], axis=0)
            dk = dke * fac["ek"][:, hs] + jnp.concatenate(
                [dkm0 * fac["ek0"][:, hs] + dkb * fac["ekb"][:, hs], dkm1 * fac["ek1"][:, hs]], axis=0)
            tke = dke * ke
            db = dqe * qe - tke + jnp.concatenate(
                [dqm0 * qm0 - dkm0 * km0 - dkb * kb, dqm1 * qm1 - dkm1 * km1 + dqb * qb], axis=0)
            dblast = jnp.sum(tke, axis=0, keepdims=True) + deb * fac["eb"][:, hs]
            db_ref[:, hs] = db + jnp.where(last_row, dblast, 0.0)
            dq_ref[:, hs] = dq
            dk_ref[:, hs] = dk
            dpa_ref[:, D + 128 * h:D + 128 * h + 128] = dv.astype(BF16)
        dg = _dot_hi(_triu(n), db_ref[...], 1)
        df = dg / f - dk_ref[...]
        dpa_ref[:, 3 * D:4 * D] = (df * (1.0 - lb) * sg * (1.0 - sg)).astype(BF16)
        dpa_ref[:, 0:D] = (dq_ref[...] * _dsilu(hq)).astype(BF16)
        dlb_ref[...] += jnp.sum(df * (1.0 - sg), axis=0, keepdims=True)

        @pl.when(i == nc - 1)
        def _():
            d0 = dlb_ref[...] * lb * (1.0 - lb)
            dhgl_ref[...] = jnp.concatenate([d0, -d0], axis=0)

    rev = lambda c: nc - 1 - c
    blk = lambda j: pl.BlockSpec((n, D), lambda c: (rev(c), j))
    payload = None if exchange is None else (_Exchange, exchange)
    args = [pa, phf, pa, pa, opre, sin, dob, hgl, nwx]
    in_specs, out_specs, out_shape, scratch = _carried_specs(
        payload,
        [blk(0), blk(0), blk(1), blk(2), blk(0), blk(0), blk(0), pl.BlockSpec((2, D), lambda c: (0, 0)),
         pl.BlockSpec((1, D), lambda c: (0, 0))],
        [pl.BlockSpec((n, 4 * D), lambda c: (rev(c), 0)), pl.BlockSpec((2, D), lambda c: (0, 0)),
         pl.BlockSpec((1, D), lambda c: (0, 0))],
        [jax.ShapeDtypeStruct((L, 4 * D), BF16), jax.ShapeDtypeStruct((2, D), F32), jax.ShapeDtypeStruct((1, D), F32)],
        [pltpu.VMEM((128, D), F32), pltpu.VMEM((n, D), F32), pltpu.VMEM((1, D), F32),
         pltpu.VMEM((n, D), F32), pltpu.VMEM((n, D), F32), pltpu.VMEM((n, D), F32)])
    return pl.pallas_call(
        _carried(body, len(args), 3, payload, _first_last(nc)), name="hgrn_bwd", grid=(nc,),
        in_specs=in_specs, out_specs=out_specs, out_shape=out_shape, scratch_shapes=scratch,
        compiler_params=_cparams(("arbitrary",)),
    )(*(args if exchange is None else args + [exchange]))


XA_SCALE = XA_DH ** -0.5


def _xa_probs(qh, kmh):
    sc = _dot_nt(qh, kmh) * XA_SCALE
    p = jnp.exp(sc - jnp.max(sc, axis=1, keepdims=True))
    return p * (1.0 / jnp.sum(p, axis=1, keepdims=True))


def _xattn_fwd(x1, nw, wq, kv, wo, *, tm):
    L = x1.shape[0]
    tm = min(tm, L)

    def body(x_ref, nw_ref, wq_ref, kv_ref, wo_ref, o_ref, ox_ref):
        x = x_ref[...]
        q = _dot(_rms_fwd(x, nw_ref[...]), wq_ref[...])
        for h in range(XA_HEADS):
            hs = slice(XA_DH * h, XA_DH * h + XA_DH)
            p = _xa_probs(q[:, hs], kv_ref[:, hs])
            ox_ref[:, hs] = _dot(p, kv_ref[:, D + XA_DH * h:D + XA_DH * h + XA_DH])
        o_ref[...] = x + _dot(ox_ref[...], wo_ref[...])

    full = lambda a: pl.BlockSpec(a.shape, lambda i: (0, 0))
    return pl.pallas_call(
        body, name="xattn_fwd", grid=(L // tm,),
        in_specs=[pl.BlockSpec((tm, D), lambda i: (i, 0)), full(nw), full(wq), full(kv), full(wo)],
        out_specs=pl.BlockSpec((tm, D), lambda i: (i, 0)),
        out_shape=jax.ShapeDtypeStruct((L, D), F32),
        scratch_shapes=[pltpu.VMEM((tm, D), F32)],
        compiler_params=_cparams(("parallel",)),
    )(x1, nw, wq, kv, wo)


def _xattn_bwd(x1, dx2, nw, wq, kv, wo, *, tm):
    L = x1.shape[0]
    tm = min(tm, L)

    def body(x_ref, dx2_ref, nw_ref, wq_ref, kv_ref, wo_ref, dx1_ref, dx1b_ref, h_ref, dq_ref, ox_ref, dkv_ref,
             dnw_ref, dqs_ref):
        @pl.when(pl.program_id(0) == 0)
        def _():
            dkv_ref[...] = jnp.zeros_like(dkv_ref)
            dnw_ref[...] = jnp.zeros_like(dnw_ref)

        x = x_ref[...]
        dx2 = dx2_ref[...]
        hn = _rms_fwd(x, nw_ref[...]).astype(BF16)
        h_ref[...] = hn
        q = _dot(hn, wq_ref[...])
        dox = _dot_nt(dx2, wo_ref[...])
        for h in range(XA_HEADS):
            hs = slice(XA_DH * h, XA_DH * h + XA_DH)
            vs = slice(D + XA_DH * h, D + XA_DH * h + XA_DH)
            qh, kmh, vmh, doxh = q[:, hs], kv_ref[:, hs], kv_ref[:, vs], dox[:, hs]
            p = _xa_probs(qh, kmh)
            ox_ref[:, hs] = _dot(p, vmh).astype(BF16)
            dp = _dot_nt(doxh, vmh)
            dkv_ref[:, vs] += _dot_tn(p, doxh)
            ds = p * (dp - jnp.sum(dp * p, axis=1, keepdims=True)) * XA_SCALE
            dqs_ref[:, hs] = _dot(ds, kmh)
            dkv_ref[:, hs] += _dot_tn(ds, qh)
        dq = dqs_ref[...]
        dq_ref[...] = dq.astype(BF16)
        dx, dnw = _rms_bwd(_dot_nt(dq, wq_ref[...]), x, nw_ref[...])
        dx1 = dx2 + dx
        dx1_ref[...] = dx1
        dx1b_ref[...] = dx1.astype(BF16)
        dnw_ref[...] += dnw

    full = lambda a: pl.BlockSpec(a.shape, lambda i: (0, 0))
    row = pl.BlockSpec((tm, D), lambda i: (i, 0))
    return pl.pallas_call(
        body, name="xattn_bwd", grid=(L // tm,),
        in_specs=[row, row, full(nw), full(wq), full(kv), full(wo)],
        out_specs=[row, row, row, row, row, pl.BlockSpec((MEM_LEN, 2 * D), lambda i: (0, 0)),
                   pl.BlockSpec((1, D), lambda i: (0, 0))],
        out_shape=[jax.ShapeDtypeStruct((L, D), F32), jax.ShapeDtypeStruct((L, D), BF16),
                   jax.ShapeDtypeStruct((L, D), BF16), jax.ShapeDtypeStruct((L, D), BF16),
                   jax.ShapeDtypeStruct((L, D), BF16),
                   jax.ShapeDtypeStruct((MEM_LEN, 2 * D), F32), jax.ShapeDtypeStruct((1, D), F32)],
        scratch_shapes=[pltpu.VMEM((tm, D), F32)],
        compiler_params=_cparams(("arbitrary",)),
    )(x1, dx2, nw, wq, kv, wo)


def _ffn_blk(tm, tf):
    return pl.BlockSpec((tm, tf), lambda i, j: (i, j))


def _interleave(wg, wu, tf):
    return jnp.stack([wg.reshape(FFN // tf, tf, D), wu.reshape(FFN // tf, tf, D)], axis=1).reshape(2 * FFN, D)


def _ffn_fwd(x2, nw, wgu, wd, *, tm, tf):
    L = x2.shape[0]
    tm = min(tm, L)
    nf = FFN // tf

    def body(x_ref, nw_ref, wgu_ref, wd_ref, o_ref, hn_ref, g_ref, u_ref, h_ref, acc_ref):
        j = pl.program_id(1)

        @pl.when(j == 0)
        def _():
            hn = _rms_fwd(x_ref[...], nw_ref[...]).astype(BF16)
            h_ref[...] = hn
            hn_ref[...] = hn
            acc_ref[...] = jnp.zeros_like(acc_ref)

        gu = _dot_nt(h_ref[...], wgu_ref[...]).astype(BF16)
        g, u = gu[:, 0:tf], gu[:, tf:2 * tf]
        g_ref[...] = g
        u_ref[...] = u
        acc_ref[...] += _dot(_silu(g.astype(F32)) * u.astype(F32), wd_ref[...])

        @pl.when(j == nf - 1)
        def _():
            o_ref[...] = x_ref[...] + acc_ref[...]

    row = pl.BlockSpec((tm, D), lambda i, j: (i, 0))
    wide = jax.ShapeDtypeStruct((L, FFN), BF16)
    return pl.pallas_call(
        body, name="ffn_fwd", grid=(L // tm, nf),
        in_specs=[row, pl.BlockSpec((1, D), lambda i, j: (0, 0)),
                  pl.BlockSpec((2 * tf, D), lambda i, j: (j, 0)), pl.BlockSpec((tf, D), lambda i, j: (j, 0))],
        out_specs=[row, row, _ffn_blk(tm, tf), _ffn_blk(tm, tf)],
        out_shape=[jax.ShapeDtypeStruct((L, D), F32), jax.ShapeDtypeStruct((L, D), BF16), wide, wide],
        scratch_shapes=[pltpu.VMEM((tm, D), BF16), pltpu.VMEM((tm, D), F32)],
        compiler_params=_cparams(("parallel", "arbitrary"), vmem_mb=58),
    )(x2, nw, wgu, wd)


def _ffn_bwd(x2, dx3, g, u, nw, wgu, wd, *, tm, tf):
    L = x2.shape[0]
    tm = min(tm, L)
    nf = FFN // tf

    def body(x_ref, dx3_ref, g_ref, u_ref, nw_ref, wgu_ref, wd_ref,
             dx2_ref, a_ref, dg_ref, du_ref, dnw_ref, d3_ref, acc_ref, da_ref, dgu_ref):
        i, j = pl.program_id(0), pl.program_id(1)

        @pl.when(j == 0)
        def _():
            d3_ref[...] = dx3_ref[...].astype(BF16)
            acc_ref[...] = jnp.zeros_like(acc_ref)
            da_ref[...] = jnp.zeros_like(da_ref)
            dgu_ref[...] = jnp.zeros_like(dgu_ref)

        @pl.when(jnp.logical_and(i == 0, j == 0))
        def _():
            dnw_ref[...] = jnp.zeros_like(dnw_ref)

        acc_ref[...] += _dot(dgu_ref[...], wgu_ref[...])
        g = g_ref[...].astype(F32)
        u = u_ref[...].astype(F32)
        s = _sigmoid_gate(g)
        sg = g * s
        da = da_ref[...]
        a_ref[...] = (sg * u).astype(BF16)
        dg = (da * u * (s + sg * (1.0 - s))).astype(BF16)
        du = (da * sg).astype(BF16)
        dg_ref[...] = dg
        du_ref[...] = du
        dgu_ref[...] = jnp.concatenate([dg, du], axis=1)
        da_ref[...] = _dot_nt(d3_ref[...], wd_ref[...])

        @pl.when(j == nf + 1)
        def _():
            dx, dnw = _rms_bwd(acc_ref[...], x_ref[...], nw_ref[...])
            dx2_ref[...] = dx3_ref[...] + dx
            dnw_ref[...] += dnw

    clamp = lambda j, d: jnp.clip(j - d, 0, nf - 1)
    row = pl.BlockSpec((tm, D), lambda i, j: (i, 0))
    blk = pl.BlockSpec((tm, tf), lambda i, j: (i, clamp(j, 1)))
    wide = jax.ShapeDtypeStruct((L, FFN), BF16)
    return pl.pallas_call(
        body, name="ffn_bwd", grid=(L // tm, nf + 2),
        in_specs=[row, row, blk, blk, pl.BlockSpec((1, D), lambda i, j: (0, 0)),
                  pl.BlockSpec((2 * tf, D), lambda i, j: (clamp(j, 2), 0)),
                  pl.BlockSpec((tf, D), lambda i, j: (clamp(j, 0), 0))],
        out_specs=[row, blk, blk, blk, pl.BlockSpec((1, D), lambda i, j: (0, 0))],
        out_shape=[jax.ShapeDtypeStruct((L, D), F32), wide, wide, wide, jax.ShapeDtypeStruct((1, D), F32)],
        scratch_shapes=[pltpu.VMEM((tm, D), BF16), pltpu.VMEM((tm, D), F32), pltpu.VMEM((tm, tf), F32),
                        pltpu.VMEM((tm, 2 * tf), BF16)],
        compiler_params=_cparams(("arbitrary", "arbitrary"), vmem_mb=56),
    )(x2, dx3, g, u, nw, wgu, wd)


def _final(x3, tgt, nw, *, tm):
    L = x3.shape[0]
    tm = min(tm, L)

    def body(x_ref, t_ref, nw_ref, dx_ref, dxb_ref, loss_ref, dnw_ref):
        @pl.when(pl.program_id(0) == 0)
        def _():
            loss_ref[...] = jnp.zeros_like(loss_ref)
            dnw_ref[...] = jnp.zeros_like(dnw_ref)

        x = x_ref[...]
        w = nw_ref[...]
        err = _rms_fwd(x, w) - t_ref[...]
        part = 0.5 * jnp.sum(jnp.sum(err * err, axis=1, keepdims=True), axis=0, keepdims=True) * (1.0 / D)
        loss_ref[...] += jnp.where(_iota2((1, 128), 1) == 0, part, 0.0)
        dx, dnw = _rms_bwd(err * (1.0 / D), x, w)
        dx_ref[...] = dx
        dxb_ref[...] = dx.astype(BF16)
        dnw_ref[...] += dnw

    row = pl.BlockSpec((tm, D), lambda i: (i, 0))
    return pl.pallas_call(
        body, name="final_loss", grid=(L // tm,),
        in_specs=[row, row, pl.BlockSpec((1, D), lambda i: (0, 0))],
        out_specs=[row, row, pl.BlockSpec((1, 128), lambda i: (0, 0)), pl.BlockSpec((1, D), lambda i: (0, 0))],
        out_shape=[jax.ShapeDtypeStruct((L, D), F32), jax.ShapeDtypeStruct((L, D), BF16),
                   jax.ShapeDtypeStruct((1, 128), F32), jax.ShapeDtypeStruct((1, D), F32)],
        compiler_params=_cparams(("arbitrary",)),
    )(x3, tgt, nw)


def _adam_update(g, w, m, v):
    c1 = 1.0 / (1.0 - ADAM_B1 ** ADAM_STEP)
    c2 = 1.0 / (1.0 - ADAM_B2 ** ADAM_STEP)
    nm = ADAM_B1 * m + (1.0 - ADAM_B1) * g
    nv = ADAM_B2 * v + (1.0 - ADAM_B2) * (g * g)
    return -ADAM_LR * ((nm * c1) / (jnp.sqrt(nv * c2) + ADAM_EPS) + ADAM_WD * w), nm, nv


def _adamw_small(tot, conv_w_grad, w, m, v):
    names = [n for n, _, _ in SMALL if n != "loss"]
    k = len(names)

    def body(tot_ref, cwg_ref, *refs):
        w_refs, m_refs, v_refs = refs[0:k], refs[k:2 * k], refs[2 * k:3 * k]
        g_refs, d_refs, nm_refs, nv_refs = (refs[(3 + j) * k:(4 + j) * k] for j in range(4))
        for i, n in enumerate(names):
            row, nr, nc = SMALL_AT[n]
            g = cwg_ref[...] if n == "conv_w" else tot_ref[row:row + nr, 0:nc]
            d, nm, nv = _adam_update(g, w_refs[i][...], m_refs[i][...], v_refs[i][...])
            g_refs[i][...] = g
            d_refs[i][...] = d
            nm_refs[i][...] = nm
            nv_refs[i][...] = nv

    sds = [jax.ShapeDtypeStruct(t.shape, F32) for t in w]
    res = pl.pallas_call(body, name="adamw_small", out_shape=sds * 4)(tot, conv_w_grad, *w, *m, *v)
    return res[0:k], res[k:2 * k], res[2 * k:3 * k], res[3 * k:4 * k]


def _adamw(parts, w, m, v, *, tr, name):
    n_parts, R, C = parts.shape
    tr = min(tr, R)

    def body(p_ref, w_ref, m_ref, v_ref, g_ref, d_ref, nm_ref, nv_ref):
        g = p_ref[0]
        for k in range(1, n_parts):
            g = g + p_ref[k]
        d, nm, nv = _adam_update(g, w_ref[...], m_ref[...], v_ref[...])
        g_ref[...] = g
        nm_ref[...] = nm
        nv_ref[...] = nv
        d_ref[...] = d

    blk = pl.BlockSpec((tr, C), lambda i: (i, 0))
    sds = jax.ShapeDtypeStruct((R, C), F32)
    return pl.pallas_call(
        body, name=name, grid=(R // tr,),
        in_specs=[pl.BlockSpec((n_parts, tr, C), lambda i: (0, i, 0)), blk, blk, blk],
        out_specs=[blk, blk, blk, blk], out_shape=[sds, sds, sds, sds],
        compiler_params=_cparams(("parallel",)),
    )(parts, w, m, v)


def _position():
    return lax.axis_index("x"), lax.axis_index("y"), lax.axis_index("c")


def _comm_scratch():
    return [pltpu.SemaphoreType.DMA((7,)), pltpu.SemaphoreType.DMA((7,)), pltpu.SemaphoreType.DMA]


class _Gather:
    def __init__(self, x_ref, out_ref, send_sems, recv_sems, local_sem):
        x, y, c = _position()
        me, sibling = (x, y, c), (x, y, 1 - c)
        chips = [(1 - x, y), (x, 1 - y), (1 - x, 1 - y)]

        def rows(px, py, pc):
            return out_ref.at[4 * px + 2 * py + pc]

        def copy(k, block, to, src=None):
            return pltpu.make_async_remote_copy(
                src_ref=rows(*block) if src is None else src, dst_ref=rows(*block),
                send_sem=send_sems.at[k], recv_sem=recv_sems.at[k], device_id=to, device_id_type=MESH)

        self.mine = pltpu.make_async_copy(x_ref, rows(*me), local_sem)
        self.first = [copy(0, me, sibling, src=x_ref)]
        self.first += [copy(1 + j, me, (*chip, c), src=x_ref) for j, chip in enumerate(chips)]
        self.passed = [copy(4 + j, (*chip, c), sibling) for j, chip in enumerate(chips)]
        self.from_chips = [copy(1 + j, (*chip, c), me) for j, chip in enumerate(chips)]
        self.from_sibling = [copy(0, sibling, me)] + [copy(4 + j, (*chip, 1 - c), me) for j, chip in enumerate(chips)]

    def start(self):
        self.mine.start()
        for cp in self.first:
            cp.start()

    def forward(self):
        for got, cp in zip(self.from_chips, self.passed):
            got.wait_recv()
            cp.start()

    def finish(self):
        for got in self.from_sibling:
            got.wait_recv()
        for cp in self.first + self.passed:
            cp.wait_send()
        self.mine.wait()


class _Exchange:
    def __init__(self, g_ref, out_ref, send_sems, recv_sems, local_sem):
        x, y, c = _position()
        me = 4 * x + 2 * y + c
        self.mine = pltpu.make_async_copy(g_ref.at[me], out_ref.at[me], local_sem)
        self.copies = []
        for k in range(1, N_DEV):
            px = 1 - x if k & 4 else x
            py = 1 - y if k & 2 else y
            pc = 1 - c if k & 1 else c
            self.copies.append(pltpu.make_async_remote_copy(
                src_ref=g_ref.at[4 * px + 2 * py + pc], dst_ref=out_ref.at[me],
                send_sem=send_sems.at[k - 1], recv_sem=recv_sems.at[k - 1],
                device_id=(px, py, pc), device_id_type=MESH))

    def start(self):
        self.mine.start()
        for cp in self.copies:
            cp.start()

    def finish(self):
        for cp in self.copies:
            cp.wait()
        self.mine.wait()


def _allgather(xp):
    R, C = xp.shape

    def body(x_ref, out_ref, send_sems, recv_sems, local_sem):
        g = _Gather(x_ref, out_ref, send_sems, recv_sems, local_sem)
        g.start()
        g.forward()
        g.finish()

    return pl.pallas_call(
        body, name="allgather_w_in",
        out_shape=jax.ShapeDtypeStruct((N_DEV, R, C), xp.dtype),
        in_specs=[pl.BlockSpec(memory_space=pltpu.HBM)], out_specs=pl.BlockSpec(memory_space=pltpu.HBM),
        scratch_shapes=_comm_scratch(),
    )(xp)


def _carried(body, n_in, n_out, payload, phases):
    if payload is None:
        return body
    kind = payload[0]

    def new_body(*refs):
        ins, src_ref = refs[:n_in], refs[n_in]
        outs, dst_ref = refs[n_in + 1:n_in + 1 + n_out], refs[n_in + 1 + n_out]
        scratch, sems = refs[n_in + 2 + n_out:-3], refs[-3:]

        def run(before):
            for when, action, is_before in phases:
                if is_before == before:
                    @pl.when(when())
                    def _():
                        action(kind(src_ref, dst_ref, *sems))

        run(True)
        body(*ins, *outs, *scratch)
        run(False)

    return new_body


def _carried_specs(payload, in_specs, out_specs, out_shape, scratch):
    if payload is None:
        return in_specs, out_specs, out_shape, scratch
    kind, arr = payload
    landing = (N_DEV,) + arr.shape if kind is _Gather else arr.shape
    hbm = pl.BlockSpec(memory_space=pltpu.HBM)
    return (in_specs + [hbm], out_specs + [hbm], out_shape + [jax.ShapeDtypeStruct(landing, arr.dtype)],
            scratch + _comm_scratch())


def _small_allreduce(sp):
    R, C = sp.shape

    def body(s_ref, out_ref, buf_ref, send_sems, recv_sems):
        x, y, c = _position()
        me = 4 * x + 2 * y + c
        buf_ref[me] = s_ref[...]
        copies = []
        for k in range(1, N_DEV):
            px = 1 - x if k & 4 else x
            py = 1 - y if k & 2 else y
            pc = 1 - c if k & 1 else c
            cp = pltpu.make_async_remote_copy(
                src_ref=s_ref, dst_ref=buf_ref.at[me], send_sem=send_sems.at[k - 1], recv_sem=recv_sems.at[k - 1],
                device_id=(px, py, pc), device_id_type=MESH)
            cp.start()
            copies.append(cp)
        for cp in copies:
            cp.wait()
        tot = buf_ref[0]
        for k in range(1, N_DEV):
            tot = tot + buf_ref[k]
        out_ref[...] = tot

    return pl.pallas_call(
        body, name="allreduce_small",
        out_shape=jax.ShapeDtypeStruct((R, C), F32),
        in_specs=[pl.BlockSpec(memory_space=pltpu.VMEM)], out_specs=pl.BlockSpec(memory_space=pltpu.VMEM),
        scratch_shapes=[pltpu.VMEM((N_DEV, R, C), F32), pltpu.SemaphoreType.DMA((7,)), pltpu.SemaphoreType.DMA((7,))],
    )(sp)


def _local_step(x, mem, tgt, wt, small, dist=None):
    w_in = wt["w_in"]
    zpad = jnp.zeros((WB - D - SSD_HEADS, D), BF16)
    w_a = jnp.concatenate([w_in[2576:3600], w_in[4624:6672], w_in[3600:4624]], axis=0)
    w_b = jnp.concatenate([w_in[0:D], w_in[2560:2576], zpad], axis=0)
    w_c = w_in[D:2560]
    a_log, d_skip = small["a_log"], small["d_skip"]
    avec = jnp.pad(-jnp.exp(a_log), ((0, 0), (0, 128 - SSD_HEADS)))
    aexp = jnp.repeat(-jnp.exp(a_log), SSD_P, axis=1)
    dexp = jnp.repeat(d_skip, SSD_P, axis=1)
    dtb = jnp.pad(small["dt_bias"], ((0, 0), (0, 128 - SSD_HEADS)))
    emat = (lax.broadcasted_iota(jnp.int32, (128, D), 0) == lax.broadcasted_iota(jnp.int32, (128, D), 1) // SSD_P
            ).astype(F32)
    hg_nwx = jnp.tile(small["hg_norm_w"], (1, HG_HEADS))
    hgl = small["hg_lower_bounds"]
    nfw = small["norm_final_w"].reshape(1, D)

    received = {}
    pieces = (lambda group, grads: None) if dist is None else dist["pieces"]
    pa, phf, pz, pdt, pc, hn_mix, *got = _inproj(x, small["norm_mix_w"], w_a, w_b, w_c, tm=256,
                                                 gather=None if dist is None else dist["rest_pack"])
    if got:
        wt = {**wt, **dist["unpack_rest"](got[0])}
    xbc = _conv_fwd(pc, small["conv_w"], small["conv_b"], tm=512)
    ya, ypre, ssd_sin = _ssd_fwd(xbc, pz, pdt, dtb, avec, aexp, dexp, small["ssd_norm_w"], emat)
    ob, opre, hg_sin = _hgrn_fwd(pa, phf, hgl, hg_nwx)
    x1 = _mm2_res(x, ya, ob, wt["w_out"], tm=512, name="outproj")
    kvb, mn = _norm_mm(mem, small["norm_mem_w"], wt["xa_wkv"], tm=256, tn=1024, name="mem_kv", emit_h=True,
                       out_dtype=BF16)
    x2 = _xattn_fwd(x1, small["norm_xa_w"], wt["xa_wq"], kvb, wt["xa_wo"], tm=512)
    x3, hn_ffn, gate, up = _ffn_fwd(x2, small["norm_ffn_w"], _interleave(wt["ffn_w_gate"], wt["ffn_w_up"], 1408),
                                    wt["ffn_w_down"], tm=512, tf=1408)

    dx3, dx3b, loss, g_nf = _final(x3, tgt, nfw, tm=512)
    dx2, act, dg, du, g_nffn = _ffn_bwd(x2, dx3, gate, up, small["norm_ffn_w"],
                                        _interleave(wt["ffn_w_gate"], wt["ffn_w_up"], 256), wt["ffn_w_down"],
                                        tm=1024, tf=256)
    g_wg = _dw(dg, hn_ffn, tM=1408, tN=1024, tl=2048, name="dw_gate")
    g_wu = _dw(du, hn_ffn, tM=1408, tN=1024, tl=2048, name="dw_up")
    g_wd = _dw(act, dx3b, tM=1408, tN=1024, tl=2048, name="dw_down")
    dx1, dx1b, hn_xa, dq, ox, dkv, g_nxa = _xattn_bwd(x1, dx2, small["norm_xa_w"], wt["xa_wq"], kvb, wt["xa_wo"],
                                                      tm=512)
    dkvb = dkv.astype(BF16)
    g_wq = _dw(hn_xa, dq, tM=1024, tN=1024, tl=2048, name="dw_q")
    g_wo = _dw(ox, dx2, tM=1024, tN=1024, tl=2048, name="dw_o")
    g_wkv = _dw(dkvb, mn, tM=1024, tN=1024, tl=256, name="dw_kv")
    _, g_nmem = _mm_normbwd([dkvb], [wt["xa_wkv"]], mem, small["norm_mem_w"], None, tm=256, name="mem_bwd")
    dya, dob = _mm_nt2(dx1, wt["w_out"], tm=512, name="outproj_bwd")
    g_wout = jnp.concatenate([_dw(ya, dx1b, tM=1024, tN=1024, tl=2048, name="dw_out_a"),
                              _dw(ob, dx1b, tM=1024, tN=1024, tl=2048, name="dw_out_b")], axis=0)
    ffn_grads = {"ffn_w_gate": g_wg, "ffn_w_up": g_wu, "ffn_w_down": g_wd}
    mid_grads = {"w_out": g_wout, "xa_wq": g_wq, "xa_wkv": g_wkv, "xa_wo": g_wo}
    dpa, g_hgl, g_hgn_x, *got = _hgrn_bwd(pa, phf, opre, hg_sin, dob, hgl, hg_nwx,
                                          exchange=pieces("ffn", ffn_grads))
    received["ffn"] = got[0] if got else None
    dpb, dxbc, g_ssdn, g_dtb, g_alog, g_dx, *got = _ssd_bwd(
        xbc, pz, pdt, ypre, ssd_sin, dya, dtb, avec, aexp, dexp, small["ssd_norm_w"], emat,
        exchange=pieces("mid", mid_grads))
    received["mid"] = got[0] if got else None
    dpc, g_cw, g_cb = _conv_bwd(pc, dxbc, small["conv_w"], small["conv_b"], tm=512)
    g_wa = _dw(dpa, hn_mix, tM=1024, tN=1024, tl=2048, name="dw_in_a")
    g_wb = _dw(dpb, hn_mix, tM=384, tN=1024, tl=2048, name="dw_in_b")
    g_wc = _dw(dpc, hn_mix, tM=512, tN=1024, tl=2048, name="dw_in_c")
    g_win = jnp.concatenate([g_wb[0:D], g_wc, g_wb[D:D + SSD_HEADS], g_wa[0:D], g_wa[3 * D:4 * D], g_wa[D:3 * D]],
                            axis=0)
    grad_x, g_nmix, *got = _mm_normbwd([dpa, dpb, dpc], [w_a, w_b, w_c], x, small["norm_mix_w"], dx1, tm=256,
                                         name="inproj_bwd", exchange=pieces("in", {"w_in": g_win}))
    received["in"] = got[0] if got else None

    big = {"w_in": g_win, **mid_grads, **ffn_grads}
    smallg = {
        "norm_mix_w": g_nmix, "conv_w": g_cw, "conv_b": g_cb, "dt_bias": g_dtb[:, 0:SSD_HEADS],
        "a_log": g_alog[:, 0:SSD_HEADS], "d_skip": g_dx.reshape(SSD_HEADS, SSD_P).sum(axis=1).reshape(1, SSD_HEADS),
        "ssd_norm_w": g_ssdn, "hg_lower_bounds": g_hgl,
        "hg_norm_w": g_hgn_x.reshape(HG_HEADS, HG_K).sum(axis=0).reshape(1, HG_K),
        "norm_xa_w": g_nxa, "norm_mem_w": g_nmem, "norm_ffn_w": g_nffn, "norm_final_w": g_nf,
        "loss": loss[:, 0:1]}
    return grad_x, big, smallg, received


COL_SHARDED = ("w_in", "xa_wkv", "ffn_w_gate", "ffn_w_up")


def _pad_rows(t, rows):
    return jnp.pad(t, [(0, 0)] * (t.ndim - 2) + [(0, rows - t.shape[-2]), (0, 0)])


def _group_fill(parts, group, lead, dtype):
    used = sum(p.shape[-2] for p in parts)
    if used < GROUP_ROWS[group]:
        parts.append(jnp.zeros(lead + (GROUP_ROWS[group] - used, D), dtype))
    return parts


def _pack_group(shards, group, dtype, extra=None):
    parts = [_pad_rows(shards[n].astype(dtype).reshape(r, D), _rows_padded(r)) for n, r in GROUPS[group]]
    if extra is not None:
        parts.append(extra)
    return jnp.concatenate(_group_fill(parts, group, (), dtype), axis=0)


def _unpack_group(packed, group, shapes):
    out, off = {}, 0
    for n, r in GROUPS[group]:
        t = packed[off:off + r]
        out[n] = (t.T if n in COL_SHARDED else t).reshape(shapes[n])
        off += _rows_padded(r)
    return out


def _row_shards(d):
    return {n: d[n][0].T if n in COL_SHARDED else d[n][0] for n in BIG}


def _unpack_gathered(gath, groups):
    out, base = {}, 0
    for group in groups:
        off = base
        for n, r in GROUPS[group]:
            out[n] = gath[:, off:off + r].reshape(N_DEV * r, D)
            off += _rows_padded(r)
        base += GROUP_ROWS[group]
    return out


def _grad_pieces(group, grads):
    parts = [_pad_rows(grads[n].reshape(N_DEV, r, D), _rows_padded(r)) for n, r in GROUPS[group]]
    return jnp.concatenate(_group_fill(parts, group, (N_DEV,), F32), axis=1)


def _pack_small(vals):
    tot = None
    for n, r, c in SMALL:
        row = SMALL_AT[n][0]
        part = jnp.pad(vals[n].reshape(r, c), ((row, SMALL_ROWS - row - r), (0, SMALL_COLS - c)))
        tot = part if tot is None else tot + part
    return tot


WEIGHTS = ['norm_mix_w', 'w_in', 'conv_w', 'conv_b', 'dt_bias', 'a_log', 'd_skip', 'ssd_norm_w', 'hg_lower_bounds',
           'hg_norm_w', 'w_out', 'norm_xa_w', 'norm_mem_w', 'xa_wq', 'xa_wkv', 'xa_wo', 'norm_ffn_w', 'ffn_w_gate',
           'ffn_w_up', 'ffn_w_down', 'norm_final_w']
BIG = tuple(n for n, _ in PACK)


def kernel(x, mem, norm_mix_w, w_in, conv_w, conv_b, dt_bias, a_log, d_skip, ssd_norm_w, hg_lower_bounds, hg_norm_w, w_out, norm_xa_w, norm_mem_w, xa_wq, xa_wkv, xa_wo, norm_ffn_w, ffn_w_gate, ffn_w_up, ffn_w_down, norm_final_w, loss_target, m_norm_mix_w, m_w_in, m_conv_w, m_conv_b, m_dt_bias, m_a_log, m_d_skip, m_ssd_norm_w, m_hg_lower_bounds, m_hg_norm_w, m_w_out, m_norm_xa_w, m_norm_mem_w, m_xa_wq, m_xa_wkv, m_xa_wo, m_norm_ffn_w, m_ffn_w_gate, m_ffn_w_up, m_ffn_w_down, m_norm_final_w, v_norm_mix_w, v_w_in, v_conv_w, v_conv_b, v_dt_bias, v_a_log, v_d_skip, v_ssd_norm_w, v_hg_lower_bounds, v_hg_norm_w, v_w_out, v_norm_xa_w, v_norm_mem_w, v_xa_wq, v_xa_wkv, v_xa_wo, v_norm_ffn_w, v_ffn_w_gate, v_ffn_w_up, v_ffn_w_down, v_norm_final_w):
    args = dict(locals())
    w = {n: args[n] for n in WEIGHTS}
    mo = {n: args["m_" + n] for n in WEIGHTS}
    vo = {n: args["v_" + n] for n in WEIGHTS}
    me = 4 * lax.axis_index("x") + 2 * lax.axis_index("y") + lax.axis_index("c")

    big_sh = _row_shards(w)
    cw_bits = lax.bitcast_convert_type(conv_w[0], BF16).reshape(-1)
    cw_rows = jnp.pad(cw_bits, (0, CONV_BITS_ROWS * D - cw_bits.shape[0])).reshape(CONV_BITS_ROWS, D)
    gath = _allgather(_pack_group(big_sh, "in", BF16, extra=cw_rows))
    wt = _unpack_gathered(gath, ("in",))
    off = _rows_padded(GROUPS["in"][0][1])
    cw_all = lax.bitcast_convert_type(gath[:, off:off + 2].reshape(N_DEV, 2 * D)[:, 0:1536].reshape(N_DEV, 4, 192, 2),
                                      F32)
    conv_w_full = cw_all.transpose(1, 0, 2).reshape(4, 1536)

    small = {n: w[n][0] if w[n].ndim == 3 else w[n] for n in WEIGHTS if n not in BIG}
    small["conv_w"] = conv_w_full
    small["hg_lower_bounds"] = hg_lower_bounds
    dist = {"rest_pack": jnp.concatenate([_pack_group(big_sh, "mid", BF16), _pack_group(big_sh, "ffn", BF16)], axis=0),
            "unpack_rest": lambda g: _unpack_gathered(g, ("mid", "ffn")),
            "pieces": _grad_pieces}
    grad_x, _, gsmall, received = _local_step(x[0], mem[0], loss_target[0], wt, small, dist)

    shapes = {n: w[n].shape for n in BIG}
    m_sh, v_sh = _row_shards(mo), _row_shards(vo)
    out_g, out_d, out_m, out_v = {}, {}, {}, {}
    for group in GROUPS:
        wp = _pack_group(big_sh, group, F32)
        mp = _pack_group(m_sh, group, F32)
        vp = _pack_group(v_sh, group, F32)
        packed = _adamw(received[group], wp, mp, vp, tr=ADAM_ROWS[group], name="adamw_" + group)
        for dst, src in zip((out_g, out_d, out_m, out_v), packed):
            dst.update(_unpack_group(src, group, shapes))

    tot = _small_allreduce(_pack_small(gsmall))
    loss = tot[SMALL_AT["loss"][0], 0]
    cw_row = SMALL_AT["conv_w"][0]
    conv_w_grad = lax.dynamic_slice(tot, (cw_row, me * 192), (4, 192))
    names = [n for n, _, _ in SMALL if n != "loss"]
    as2d = lambda t: t.reshape(t.shape[-2:] if t.ndim > 1 else (1, t.shape[0]))
    small_out = _adamw_small(tot, conv_w_grad, *[[as2d(d[n]) for n in names] for d in (w, mo, vo)])
    for dst, src in zip((out_g, out_d, out_m, out_v), small_out):
        dst.update({n: t.reshape(w[n].shape) for n, t in zip(names, src)})
    return (loss, grad_x[None], *[out_g[n] for n in WEIGHTS], *[out_d[n] for n in WEIGHTS],
            *[out_m[n] for n in WEIGHTS], *[out_v[n] for n in WEIGHTS])
```

```python
import jax
import jax.numpy as jnp
from jax import lax
from jax.experimental import pallas as pl
from jax.experimental.pallas import tpu as pltpu

F32, BF16 = jnp.float32, jnp.bfloat16
MESH = pl.DeviceIdType.MESH

D = 1024
EPS = 1e-6
SSD_HEADS, SSD_P, SSD_N, SSD_Q = 16, 64, 128, 128
HG_HEADS, HG_K, HG_STEP, HG_SUB = 8, 128, 128, 64
XA_HEADS, XA_DH, MEM_LEN = 4, 256, 256
FFN = 2816
N_IN = 6672
N_DEV = 8
WA, WB, WC = 4096, 1152, 1536
ADAM_LR, ADAM_B1, ADAM_B2, ADAM_EPS, ADAM_WD, ADAM_STEP = 0.001, 0.9, 0.999, 1e-08, 0.01, 10
VMEM_MB = 2 ** 20

PACK = (("w_in", 834), ("w_out", 256), ("xa_wq", 128), ("xa_wkv", 256), ("xa_wo", 128),
        ("ffn_w_gate", 352), ("ffn_w_up", 352), ("ffn_w_down", 352))
ROW_TILE = 16
GROUPS = {"in": PACK[0:1], "mid": PACK[1:5], "ffn": PACK[5:8]}
GROUP_ROWS = {"in": 896, "mid": 768, "ffn": 1056}
ADAM_ROWS = {"in": 128, "mid": 128, "ffn": 176}
PIECE_DTYPE = {"in": BF16, "mid": F32, "ffn": F32}
CONV_BITS_ROWS = ROW_TILE


def _rows_padded(r):
    return -(-r // ROW_TILE) * ROW_TILE

SMALL = (("norm_mix_w", 1, 1024), ("conv_w", 4, 1536), ("conv_b", 1, 1536), ("dt_bias", 1, 16), ("a_log", 1, 16),
         ("d_skip", 1, 16), ("ssd_norm_w", 1, 1024), ("hg_lower_bounds", 2, 1024), ("hg_norm_w", 1, 128),
         ("norm_xa_w", 1, 1024), ("norm_mem_w", 1, 1024), ("norm_ffn_w", 1, 1024), ("norm_final_w", 1, 1024),
         ("loss", 1, 1))
SMALL_COLS = 1536
SMALL_ROWS = 24
SMALL_AT = {n: (sum(q for _, q, _ in SMALL[:i]), r, c) for i, (n, r, c) in enumerate(SMALL)}


def _cparams(sem=None, vmem_mb=48):
    return pltpu.CompilerParams(dimension_semantics=sem, vmem_limit_bytes=vmem_mb * VMEM_MB)


def _dot(a, b):
    return jnp.dot(a.astype(BF16), b.astype(BF16), preferred_element_type=F32)


def _dot_nt(a, b):
    return lax.dot_general(a.astype(BF16), b.astype(BF16), (((1,), (1,)), ((), ())), preferred_element_type=F32)


def _dot_tn(a, b):
    return lax.dot_general(a.astype(BF16), b.astype(BF16), (((0,), (0,)), ((), ())), preferred_element_type=F32)


def _split(a, terms):
    out, r = [], a
    for k in range(terms):
        t = r.astype(BF16)
        out.append(t)
        if k + 1 < terms:
            r = r - t.astype(F32)
    return out


def _dot_hi(a, b, general=0, terms=3):
    if general == 0:
        return sum(jnp.dot(t, b.astype(BF16), preferred_element_type=F32) for t in _split(a, terms))
    return sum(jnp.dot(a.astype(BF16), t, preferred_element_type=F32) for t in _split(b, terms))


def _dot_nt_hi(a, b, terms=3):
    return sum(_dot_nt(t, b) for t in _split(a, terms))


def _sigmoid(x):
    return 1.0 / (1.0 + jnp.exp(-x))


def _sigmoid_gate(x):
    return pl.reciprocal(1.0 + jnp.exp(-x), approx=True)


def _silu(x):
    return x * _sigmoid_gate(x)


def _dsilu(x):
    s = _sigmoid_gate(x)
    return s * (1.0 + x * (1.0 - s))


def _softplus(x):
    return jnp.maximum(x, 0.0) + jnp.log(1.0 + jnp.exp(-jnp.abs(x)))


def _rms_fwd(x, w):
    r = lax.rsqrt(jnp.mean(x * x, axis=1, keepdims=True) + EPS)
    return x * r * w


def _rms_bwd(dy, x, w):
    r = lax.rsqrt(jnp.mean(x * x, axis=1, keepdims=True) + EPS)
    xh = x * r
    g = dy * w
    dx = r * (g - xh * jnp.mean(g * xh, axis=1, keepdims=True))
    return dx, jnp.sum(dy * xh, axis=0, keepdims=True)


def _iota2(shape, dim):
    return lax.broadcasted_iota(jnp.int32, shape, dim)


def _tril(n):
    return (_iota2((n, n), 0) >= _iota2((n, n), 1)).astype(F32)


def _triu(n):
    return (_iota2((n, n), 0) <= _iota2((n, n), 1)).astype(F32)


def _norm_mm(x, nw, w, *, tm, tn, name, emit_h=False, out_dtype=F32):
    L, K = x.shape
    N = w.shape[0]
    tm, tn = min(tm, L), min(tn, N)
    ni, nj = L // tm, N // tn

    def body(x_ref, nw_ref, w_ref, *rest):
        if emit_h:
            o_ref, h_ref, hs_ref = rest
        else:
            o_ref, hs_ref = rest

        @pl.when(pl.program_id(1) == 0)
        def _():
            h = _rms_fwd(x_ref[...], nw_ref[...]).astype(BF16)
            hs_ref[...] = h
            if emit_h:
                h_ref[...] = h

        o_ref[...] = _dot_nt(hs_ref[...], w_ref[...]).astype(out_dtype)

    out_shape = [jax.ShapeDtypeStruct((L, N), out_dtype)]
    out_specs = [pl.BlockSpec((tm, tn), lambda i, j: (i, j))]
    if emit_h:
        out_shape.append(jax.ShapeDtypeStruct((L, K), BF16))
        out_specs.append(pl.BlockSpec((tm, K), lambda i, j: (i, 0)))
    res = pl.pallas_call(
        body, name=name, grid=(ni, nj),
        in_specs=[pl.BlockSpec((tm, K), lambda i, j: (i, 0)), pl.BlockSpec((1, K), lambda i, j: (0, 0)),
                  pl.BlockSpec((tn, K), lambda i, j: (j, 0))],
        out_specs=out_specs, out_shape=out_shape, scratch_shapes=[pltpu.VMEM((tm, K), BF16)],
        compiler_params=_cparams(("parallel", "arbitrary")),
    )(x, nw, w)
    return res if len(res) > 1 else res[0]


def _inproj(x, nw, w_a, w_b, w_c, *, tm, gather=None):
    L, K = x.shape
    tm = min(tm, L)
    ni = L // tm

    def body(x_ref, nw_ref, wa_ref, wb_ref, wc_ref, pa_ref, hf_ref, z_ref, dt_ref, pc_ref, h_ref):
        h = _rms_fwd(x_ref[...], nw_ref[...]).astype(BF16)
        h_ref[...] = h
        pa_ref[...] = _dot_nt(h, wa_ref[0:3 * D, :]).astype(BF16)
        hf_ref[...] = _dot_nt(h, wa_ref[3 * D:4 * D, :])
        pb = _dot_nt(h, wb_ref[...])
        z_ref[...] = pb[:, 0:D].astype(BF16)
        dt_ref[...] = pb[:, D:D + 128]
        pc_ref[...] = _dot_nt(h, wc_ref[...]).astype(BF16)

    row = lambda n: pl.BlockSpec((tm, n), lambda i: (i, 0))
    full = lambda a: pl.BlockSpec(a.shape, lambda i: (0, 0))
    at = lambda i: lambda: pl.program_id(0) == i
    payload = None if gather is None else (_Gather, gather)
    phases = [(at(0), _Gather.start, True), (at(ni // 2), _Gather.forward, True), (at(ni - 1), _Gather.finish, False)]
    in_specs, out_specs, out_shape, scratch = _carried_specs(
        payload, [row(K), full(nw), full(w_a), full(w_b), full(w_c)],
        [row(3 * D), row(D), row(D), row(128), row(WC), row(K)],
        [jax.ShapeDtypeStruct((L, 3 * D), BF16), jax.ShapeDtypeStruct((L, D), F32),
         jax.ShapeDtypeStruct((L, D), BF16), jax.ShapeDtypeStruct((L, 128), F32),
         jax.ShapeDtypeStruct((L, WC), BF16), jax.ShapeDtypeStruct((L, K), BF16)], [])
    args = [x, nw, w_a, w_b, w_c]
    return pl.pallas_call(
        _carried(body, 5, 6, payload, phases), name="inproj", grid=(ni,),
        in_specs=in_specs, out_specs=out_specs, out_shape=out_shape, scratch_shapes=scratch,
        compiler_params=_cparams(("arbitrary",), vmem_mb=58),
    )(*(args if gather is None else args + [gather]))


def _mm2_res(res, a1, a2, w, *, tm, name):
    L, N = res.shape
    K = a1.shape[1]
    tm = min(tm, L)

    def body(r_ref, a1_ref, a2_ref, w_ref, o_ref):
        acc = jnp.dot(a1_ref[...], w_ref[0:K, :], preferred_element_type=F32)
        acc += jnp.dot(a2_ref[...], w_ref[K:2 * K, :], preferred_element_type=F32)
        o_ref[...] = r_ref[...] + acc

    return pl.pallas_call(
        body, name=name, grid=(L // tm,),
        in_specs=[pl.BlockSpec((tm, N), lambda i: (i, 0)), pl.BlockSpec((tm, K), lambda i: (i, 0)),
                  pl.BlockSpec((tm, K), lambda i: (i, 0)), pl.BlockSpec((2 * K, N), lambda i: (0, 0))],
        out_specs=pl.BlockSpec((tm, N), lambda i: (i, 0)),
        out_shape=jax.ShapeDtypeStruct((L, N), F32),
        compiler_params=_cparams(("parallel",)),
    )(res, a1, a2, w)


def _dw(a, b, *, tM, tN, tl, name):
    L, M = a.shape
    N = b.shape[1]
    tM, tN, tl = min(tM, M), min(tN, N), min(tl, L)

    def body(a_ref, b_ref, o_ref):
        @pl.when(pl.program_id(2) == 0)
        def _():
            o_ref[...] = jnp.zeros_like(o_ref)

        o_ref[...] += _dot_tn(a_ref[...], b_ref[...])

    return pl.pallas_call(
        body, name=name, grid=(M // tM, N // tN, L // tl),
        in_specs=[pl.BlockSpec((tl, tM), lambda i, j, l: (l, i)), pl.BlockSpec((tl, tN), lambda i, j, l: (l, j))],
        out_specs=pl.BlockSpec((tM, tN), lambda i, j, l: (i, j)),
        out_shape=jax.ShapeDtypeStruct((M, N), F32),
        compiler_params=_cparams(("parallel", "parallel", "arbitrary")),
    )(a, b)


def _first_last(n):
    return [(lambda: pl.program_id(0) == 0, _Exchange.start, True),
            (lambda: pl.program_id(0) == n - 1, _Exchange.finish, False)]


def _mm_normbwd(a_list, w_list, x, nw, res, *, tm, name, exchange=None):
    L, Dm = x.shape
    tm = min(tm, L)
    n = len(a_list)
    has_res = res is not None
    nt = L // tm

    def body(*refs):
        a_refs, w_refs = refs[:n], refs[n:2 * n]
        x_ref, nw_ref = refs[2 * n], refs[2 * n + 1]
        k = 2 * n + 2
        r_ref = refs[k] if has_res else None
        dx_ref, dnw_ref, dh_ref = refs[k + has_res], refs[k + has_res + 1], refs[k + has_res + 2]

        @pl.when(pl.program_id(0) == 0)
        def _():
            dnw_ref[...] = jnp.zeros_like(dnw_ref)
            dh_ref[...] = jnp.zeros_like(dh_ref)

        dx, dnw = _rms_bwd(dh_ref[...], x_ref[...], nw_ref[...])
        dx_ref[...] = dx + r_ref[...] if has_res else dx
        dnw_ref[...] += dnw
        dh = _dot(a_refs[0][...], w_refs[0][...])
        for a_ref, w_ref in zip(a_refs[1:], w_refs[1:]):
            dh += _dot(a_ref[...], w_ref[...])
        dh_ref[...] = dh

    ahead = lambda width: pl.BlockSpec((tm, width), lambda i: (jnp.minimum(i, nt - 1), 0))
    behind = pl.BlockSpec((tm, Dm), lambda i: (jnp.maximum(i - 1, 0), 0))
    in_specs = [ahead(a.shape[1]) for a in a_list]
    in_specs += [pl.BlockSpec(w.shape, lambda i: (0, 0)) for w in w_list]
    in_specs += [behind, pl.BlockSpec((1, Dm), lambda i: (0, 0))]
    args = [*a_list, *w_list, x, nw]
    if has_res:
        in_specs.append(behind)
        args.append(res)
    payload = None if exchange is None else (_Exchange, exchange)
    n_in = len(args)
    if exchange is not None:
        args.append(exchange)
    in_specs, out_specs, out_shape, scratch = _carried_specs(
        payload, in_specs, [behind, pl.BlockSpec((1, Dm), lambda i: (0, 0))],
        [jax.ShapeDtypeStruct((L, Dm), F32), jax.ShapeDtypeStruct((1, Dm), F32)], [pltpu.VMEM((tm, Dm), F32)])
    return pl.pallas_call(
        _carried(body, n_in, 2, payload, _first_last(nt + 1)), name=name, grid=(nt + 1,), in_specs=in_specs,
        out_specs=out_specs, out_shape=out_shape, scratch_shapes=scratch,
        compiler_params=_cparams(("arbitrary",), vmem_mb=56),
    )(*args)


CONV_TN = 512


HALO = 16


def _conv_pre(cat, w_ref, b_ref, rows):
    shifted = [pltpu.roll(cat, 3 - k, 0)[HALO:HALO + rows] for k in range(3)] + [cat[HALO:HALO + rows]]
    pre = b_ref[...] + w_ref[3:4, :] * shifted[3]
    for k in range(3):
        pre += w_ref[k:k + 1, :] * shifted[k]
    return pre, shifted


def _conv_fwd(pc, cw, cb, *, tm):
    L, C = pc.shape
    tm = min(tm, L)
    tn = CONV_TN

    def body(u_ref, halo_ref, w_ref, b_ref, o_ref):
        halo = jnp.where(pl.program_id(1) > 0, halo_ref[...].astype(F32), 0.0)
        cat = jnp.concatenate([halo, u_ref[...].astype(F32)], axis=0)
        pre, _ = _conv_pre(cat, w_ref, b_ref, tm)
        o_ref[...] = _silu(pre).astype(BF16)

    return pl.pallas_call(
        body, name="conv_fwd", grid=(C // tn, L // tm),
        in_specs=[pl.BlockSpec((tm, tn), lambda j, i: (i, j)),
                  pl.BlockSpec((HALO, tn), lambda j, i: (jnp.maximum(i * (tm // HALO) - 1, 0), j)),
                  pl.BlockSpec((4, tn), lambda j, i: (0, j)), pl.BlockSpec((1, tn), lambda j, i: (0, j))],
        out_specs=pl.BlockSpec((tm, tn), lambda j, i: (i, j)),
        out_shape=jax.ShapeDtypeStruct((L, C), BF16),
        compiler_params=_cparams(("parallel", "parallel")),
    )(pc, pc, cw, cb)


def _conv_bwd(pc, dact, cw, cb, *, tm):
    L, C = pc.shape
    tm = min(tm, L)
    tn = CONV_TN
    nt = L // tm

    def body(u_ref, halo_ref, unext_ref, da_ref, danext_ref, w_ref, b_ref, du_ref, dw_ref, db_ref):
        i = pl.program_id(1)
        halo = jnp.where(i > 0, halo_ref[...].astype(F32), 0.0)
        cat = jnp.concatenate([halo, u_ref[...].astype(F32), unext_ref[...].astype(F32)], axis=0)
        pre, shifted = _conv_pre(cat, w_ref, b_ref, tm + HALO)
        da = jnp.concatenate([da_ref[...].astype(F32),
                              jnp.where(i < nt - 1, danext_ref[...].astype(F32), 0.0)], axis=0)
        dpre = da * _dsilu(pre)
        du = w_ref[3:4, :] * dpre[0:tm]
        for k in range(3):
            du += w_ref[k:k + 1, :] * pltpu.roll(dpre, tm + HALO - (3 - k), 0)[0:tm]
        du_ref[...] = du.astype(BF16)

        @pl.when(i == 0)
        def _():
            dw_ref[...] = jnp.zeros_like(dw_ref)
            db_ref[...] = jnp.zeros_like(db_ref)

        dp = dpre[0:tm]
        dw_ref[...] += jnp.concatenate(
            [jnp.sum(dp * shifted[k][0:tm], axis=0, keepdims=True) for k in range(4)], axis=0)
        db_ref[...] += jnp.sum(dp, axis=0, keepdims=True)

    nb = L // HALO
    return pl.pallas_call(
        body, name="conv_bwd", grid=(C // tn, nt),
        in_specs=[pl.BlockSpec((tm, tn), lambda j, i: (i, j)),
                  pl.BlockSpec((HALO, tn), lambda j, i: (jnp.maximum(i * (tm // HALO) - 1, 0), j)),
                  pl.BlockSpec((HALO, tn), lambda j, i: (jnp.minimum((i + 1) * (tm // HALO), nb - 1), j)),
                  pl.BlockSpec((tm, tn), lambda j, i: (i, j)),
                  pl.BlockSpec((HALO, tn), lambda j, i: (jnp.minimum((i + 1) * (tm // HALO), nb - 1), j)),
                  pl.BlockSpec((4, tn), lambda j, i: (0, j)), pl.BlockSpec((1, tn), lambda j, i: (0, j))],
        out_specs=[pl.BlockSpec((tm, tn), lambda j, i: (i, j)), pl.BlockSpec((4, tn), lambda j, i: (0, j)),
                   pl.BlockSpec((1, tn), lambda j, i: (0, j))],
        out_shape=[jax.ShapeDtypeStruct((L, C), BF16), jax.ShapeDtypeStruct((4, C), F32),
                   jax.ShapeDtypeStruct((1, C), F32)],
        compiler_params=_cparams(("parallel", "arbitrary")),
    )(pc, pc, pc, dact, dact, cw, cb)


def _ssd_common(dtr_ref, dtb_ref, avec_ref, aexp_ref, e_ref, acx_ref, acol_ref, arow_ref):
    q = SSD_Q
    tril = _tril(q)
    dtpre = dtr_ref[...] + dtb_ref[...]
    dt = _softplus(dtpre)
    dtx = _dot_hi(dt, e_ref[...])
    acx_ref[...] = _dot_hi(tril, dtx * aexp_ref[...], 1)
    acol = _dot_hi(tril, dt * avec_ref[...], 1)
    acol_ref[...] = acol
    arow_ref[...] = acol.T
    return dtpre, dt, dtx


def _ssd_fwd(xbc, pz, pdt, dtb, avec, aexp, dexp, nw, emat):
    L = xbc.shape[0]
    q = SSD_Q
    nc = L // q

    def body(xbc_ref, z_ref, dtr_ref, dtb_ref, avec_ref, aexp_ref, dexp_ref, nw_ref, e_ref,
             ya_ref, ypre_ref, sin_ref, st_ref, acx_ref, acol_ref, arow_ref, xdt_ref, y_ref):
        @pl.when(pl.program_id(0) == 0)
        def _():
            st_ref[...] = jnp.zeros_like(st_ref)

        sin_ref[...] = st_ref[...].astype(BF16)
        _, _, dtx = _ssd_common(dtr_ref, dtb_ref, avec_ref, aexp_ref, e_ref, acx_ref, acol_ref, arow_ref)
        xs = xbc_ref[:, 0:D].astype(F32)
        xdt = xs * dtx
        xdt_ref[...] = xdt
        acx = acx_ref[...]
        alast = acx_ref[q - 1:q, :]
        xdtd = xdt * jnp.exp(alast - acx)
        eac = jnp.exp(acx)
        ealast = jnp.exp(alast)
        causal = _iota2((q, q), 0) >= _iota2((q, q), 1)
        for g in range(2):
            gs = slice(512 * g, 512 * g + 512)
            bm = xbc_ref[:, D + 128 * g:D + 128 * g + 128]
            cm = xbc_ref[:, D + 256 + 128 * g:D + 256 + 128 * g + 128]
            stg = st_ref[:, gs]
            yoff = _dot(cm, stg) * eac[:, gs]
            gmat = _dot_nt(cm, bm)
            for e in range(8):
                h = 8 * g + e
                hs = slice(64 * h, 64 * h + 64)
                col = acol_ref[:, h:h + 1]
                row = arow_ref[h:h + 1, :]
                lm = jnp.exp(jnp.where(causal, col - row, -1e30))
                y_ref[:, hs] = _dot(gmat * lm, xdt_ref[:, hs])
            y_ref[:, gs] += yoff + dexp_ref[:, gs] * xs[:, gs]
            st_ref[:, gs] = stg * ealast[:, gs] + _dot_tn(bm, xdtd[:, gs])
        ypre_ref[...] = y_ref[...].astype(BF16)
        for g in range(2):
            gs = slice(512 * g, 512 * g + 512)
            yz = y_ref[:, gs] * _silu(z_ref[:, gs].astype(F32))
            ya_ref[:, gs] = _rms_fwd(yz, nw_ref[:, gs]).astype(BF16)

    vec = lambda n: pl.BlockSpec((1, n), lambda c: (0, 0))
    return pl.pallas_call(
        body, name="ssd_fwd", grid=(nc,),
        in_specs=[pl.BlockSpec((q, 1536), lambda c: (c, 0)), pl.BlockSpec((q, D), lambda c: (c, 0)),
                  pl.BlockSpec((q, 128), lambda c: (c, 0)), vec(128), vec(128), vec(D), vec(D), vec(D),
                  pl.BlockSpec((128, D), lambda c: (0, 0))],
        out_specs=[pl.BlockSpec((q, D), lambda c: (c, 0)), pl.BlockSpec((q, D), lambda c: (c, 0)),
                   pl.BlockSpec((128, D), lambda c: (c, 0))],
        out_shape=[jax.ShapeDtypeStruct((L, D), BF16), jax.ShapeDtypeStruct((L, D), BF16),
                   jax.ShapeDtypeStruct((nc * 128, D), BF16)],
        scratch_shapes=[pltpu.VMEM((128, D), F32), pltpu.VMEM((q, D), F32), pltpu.VMEM((q, 128), F32),
                        pltpu.VMEM((128, q), F32), pltpu.VMEM((q, D), F32), pltpu.VMEM((q, D), F32)],
        compiler_params=_cparams(("arbitrary",)),
    )(xbc, pz, pdt, dtb, avec, aexp, dexp, nw, emat)


def _ssd_bwd(xbc, pz, pdt, ypre, sin, dya, dtb, avec, aexp, dexp, nw, emat, exchange=None):
    L = xbc.shape[0]
    q = SSD_Q
    nc = L // q

    def body(xbc_ref, z_ref, dtr_ref, ypre_ref, sin_ref, dya_ref, dtb_ref, avec_ref, aexp_ref, dexp_ref, nw_ref,
             e_ref, dpb_ref, dxbc_ref, dnw_ref, ddtb_ref, da_ref, ddx_ref,
             dst_ref, acx_ref, acol_ref, arow_ref, xdt_ref, dxdt_ref, dy_ref, dacx_ref):
        @pl.when(pl.program_id(0) == 0)
        def _():
            dst_ref[...] = jnp.zeros_like(dst_ref)
            dnw_ref[...] = jnp.zeros_like(dnw_ref)
            ddtb_ref[...] = jnp.zeros_like(ddtb_ref)
            da_ref[...] = jnp.zeros_like(da_ref)
            ddx_ref[...] = jnp.zeros_like(ddx_ref)

        dtpre, dt, dtx = _ssd_common(dtr_ref, dtb_ref, avec_ref, aexp_ref, e_ref, acx_ref, acol_ref, arow_ref)
        xs = xbc_ref[:, 0:D].astype(F32)
        xdt = xs * dtx
        xdt_ref[...] = xdt
        acx = acx_ref[...]
        alast = acx_ref[q - 1:q, :]
        dec_end = jnp.exp(alast - acx)
        xdtd = xdt * dec_end
        eac = jnp.exp(acx)
        ealast = jnp.exp(alast)
        for g in range(2):
            gs = slice(512 * g, 512 * g + 512)
            y = ypre_ref[:, gs].astype(F32)
            z = z_ref[:, gs].astype(F32)
            sz = _silu(z)
            dyz, dnw = _rms_bwd(dya_ref[:, gs].astype(F32), y * sz, nw_ref[:, gs])
            dnw_ref[:, gs] += dnw
            dy_ref[:, gs] = dyz * sz
            dpb_ref[:, gs] = (dyz * y * _dsilu(z)).astype(BF16)
        dy = dy_ref[...]
        ddx_ref[...] += jnp.sum(dy * xs, axis=0, keepdims=True)
        ri = _iota2((q, q), 0)
        ci = _iota2((q, q), 1)
        causal = ri >= ci
        causal_t = ri <= ci
        dacol = jnp.zeros((q, 128), F32)
        dacol_t = jnp.zeros((128, q), F32)
        last_row = _iota2((q, 512), 0) == q - 1
        for g in range(2):
            gs = slice(512 * g, 512 * g + 512)
            bm = xbc_ref[:, D + 128 * g:D + 128 * g + 128]
            cm = xbc_ref[:, D + 256 + 128 * g:D + 256 + 128 * g + 128]
            stg = sin_ref[:, gs].astype(F32)
            dstg = dst_ref[:, gs]
            dyg = dy[:, gs]
            yoff = _dot(cm, stg) * eac[:, gs]
            dwm = dyg * eac[:, gs]
            dcm = _dot_nt(dwm, stg)
            dstin = _dot_tn(cm, dwm)
            dacx_g = dyg * yoff
            dxdtd = _dot(bm, dstg)
            dbm = _dot_nt(xdtd[:, gs], dstg)
            t = dxdtd * xdtd[:, gs]
            dacx_g -= t
            dalast = jnp.sum(t, axis=0, keepdims=True) + jnp.sum(dstg * stg, axis=0, keepdims=True) * ealast[:, gs]
            dst_ref[:, gs] = dstin + dstg * ealast[:, gs]
            dacx_ref[:, gs] = dacx_g + jnp.where(last_row, dalast, 0.0)
            gmat = _dot_nt(cm, bm)
            gmat_t = _dot_nt(bm, cm)
            dg = jnp.zeros((q, q), F32)
            for e in range(8):
                h = 8 * g + e
                hs = slice(64 * h, 64 * h + 64)
                col = acol_ref[:, h:h + 1]
                row = arow_ref[h:h + 1, :]
                lm = jnp.exp(jnp.where(causal, col - row, -1e30))
                lm_t = jnp.exp(jnp.where(causal_t, row - col, -1e30))
                dyh = dy_ref[:, hs]
                dm = _dot_nt(dyh, xdt_ref[:, hs])
                dxdt_ref[:, hs] = _dot(gmat_t * lm_t, dyh)
                dml = dm * lm
                dg += dml
                p = dml * gmat
                dacol += jnp.where(ci == h, jnp.sum(p, axis=1, keepdims=True), 0.0)
                dacol_t -= jnp.where(ri == h, jnp.sum(p, axis=0, keepdims=True), 0.0)
            dcm += _dot(dg, bm)
            dbm += _dot_tn(dg, cm)
            dxbc_ref[:, D + 128 * g:D + 128 * g + 128] = dbm.astype(BF16)
            dxbc_ref[:, D + 256 + 128 * g:D + 256 + 128 * g + 128] = dcm.astype(BF16)
            dxdt_ref[:, gs] += dxdtd * dec_end[:, gs]
        dxdt = dxdt_ref[...]
        dacum = dacol + dacol_t.T + _dot_nt_hi(dacx_ref[...], e_ref[...])
        da = _dot_hi(_triu(q), dacum, 1)
        ddt = da * avec_ref[...] + _dot_nt_hi(dxdt * xs, e_ref[...])
        da_ref[...] += jnp.sum(da * dt, axis=0, keepdims=True) * avec_ref[...]
        dxbc_ref[:, 0:D] = (dexp_ref[...] * dy + dxdt * dtx).astype(BF16)
        ddtr = ddt * _sigmoid(dtpre)
        ddtb_ref[...] += jnp.sum(ddtr, axis=0, keepdims=True)
        dpb_ref[:, D:D + 128] = ddtr.astype(BF16)

    rev = lambda c: nc - 1 - c
    vec = lambda n: pl.BlockSpec((1, n), lambda c: (0, 0))
    payload = None if exchange is None else (_Exchange, exchange)
    args = [xbc, pz, pdt, ypre, sin, dya, dtb, avec, aexp, dexp, nw, emat]
    in_specs, out_specs, out_shape, scratch = _carried_specs(
        payload,
        [pl.BlockSpec((q, 1536), lambda c: (rev(c), 0)), pl.BlockSpec((q, D), lambda c: (rev(c), 0)),
         pl.BlockSpec((q, 128), lambda c: (rev(c), 0)), pl.BlockSpec((q, D), lambda c: (rev(c), 0)),
         pl.BlockSpec((128, D), lambda c: (rev(c), 0)), pl.BlockSpec((q, D), lambda c: (rev(c), 0)),
         vec(128), vec(128), vec(D), vec(D), vec(D), pl.BlockSpec((128, D), lambda c: (0, 0))],
        [pl.BlockSpec((q, WB), lambda c: (rev(c), 0)), pl.BlockSpec((q, 1536), lambda c: (rev(c), 0)),
         vec(D), vec(128), vec(128), vec(D)],
        [jax.ShapeDtypeStruct((L, WB), BF16), jax.ShapeDtypeStruct((L, 1536), BF16),
         jax.ShapeDtypeStruct((1, D), F32), jax.ShapeDtypeStruct((1, 128), F32),
         jax.ShapeDtypeStruct((1, 128), F32), jax.ShapeDtypeStruct((1, D), F32)],
        [pltpu.VMEM((128, D), F32), pltpu.VMEM((q, D), F32), pltpu.VMEM((q, 128), F32),
         pltpu.VMEM((128, q), F32), pltpu.VMEM((q, D), F32), pltpu.VMEM((q, D), F32),
         pltpu.VMEM((q, D), F32), pltpu.VMEM((q, D), F32)])
    return pl.pallas_call(
        _carried(body, len(args), 6, payload, _first_last(nc)), name="ssd_bwd", grid=(nc,),
        in_specs=in_specs, out_specs=out_specs, out_shape=out_shape, scratch_shapes=scratch,
        compiler_params=_cparams(("arbitrary",)),
    )(*(args if exchange is None else args + [exchange]))


def _hg_gates(hq, hf, hgl_ref, b_ref):
    lb = 1.0 / (1.0 + jnp.exp(hgl_ref[1:2, :] - hgl_ref[0:1, :]))
    qf = _silu(hq)
    sg = _sigmoid(hf)
    f = lb + (1.0 - lb) * sg
    b_ref[...] = _dot_hi(_tril(HG_STEP), jnp.log(f), 1)
    return lb, qf, sg, f


def _hg_factors(qf, kf, b_ref):
    s, n = HG_SUB, HG_STEP
    b = b_ref[...]
    blast = b_ref[n - 1:n, :]
    m0, mb, m1 = b_ref[s // 2 - 1:s // 2, :], b_ref[s - 1:s, :], b_ref[s + s // 2 - 1:s + s // 2, :]
    b0, b1 = b[0:s], b[s:n]
    q0, q1, k0, k1 = qf[0:s], qf[s:n], kf[0:s], kf[s:n]
    fac = dict(
        eb=jnp.exp(blast), eq=jnp.exp(b), ek=jnp.exp(blast - b),
        eq0=jnp.exp(b0 - m0), ek0=jnp.exp(m0 - b0), eq1=jnp.exp(b1 - m1), ek1=jnp.exp(m1 - b1),
        eqb=jnp.exp(b1 - mb), ekb=jnp.exp(mb - b0))
    rd = lambda t: t.astype(BF16).astype(F32)
    val = dict(qe=qf * fac["eq"], ke=kf * fac["ek"], qm0=rd(q0 * fac["eq0"]), km0=rd(k0 * fac["ek0"]),
               qm1=rd(q1 * fac["eq1"]), km1=rd(k1 * fac["ek1"]), qb=rd(q1 * fac["eqb"]), kb=rd(k0 * fac["ekb"]))
    return fac, val


def _hgrn_fwd(pa, phf, hgl, nwx):
    L = pa.shape[0]
    n, s = HG_STEP, HG_SUB
    nc = L // n

    def body(hq_ref, hf_ref, hi_ref, hg_ref, hgl_ref, nw_ref, ob_ref, opre_ref, sin_ref, st_ref, b_ref):
        @pl.when(pl.program_id(0) == 0)
        def _():
            st_ref[...] = jnp.zeros_like(st_ref)

        sin_ref[...] = st_ref[...].astype(BF16)
        _, qf, _, f = _hg_gates(hq_ref[...].astype(F32), hf_ref[...], hgl_ref, b_ref)
        fac, val = _hg_factors(qf, 1.0 - f, b_ref)
        causal = _iota2((s, s), 0) >= _iota2((s, s), 1)
        for h in range(HG_HEADS):
            hs = slice(128 * h, 128 * h + 128)
            sth = st_ref[:, hs]
            v = hi_ref[:, hs]
            v0, v1 = v[0:s], v[s:n]
            a00 = jnp.where(causal, _dot_nt(val["qm0"][:, hs], val["km0"][:, hs]), 0.0)
            a11 = jnp.where(causal, _dot_nt(val["qm1"][:, hs], val["km1"][:, hs]), 0.0)
            a10 = _dot_nt(val["qb"][:, hs], val["kb"][:, hs])
            o = _dot_nt(val["qe"][:, hs], sth) + jnp.concatenate(
                [_dot(a00, v0), _dot(a10, v0) + _dot(a11, v1)], axis=0)
            st_ref[:, hs] = sth * fac["eb"][:, hs] + _dot_tn(v, val["ke"][:, hs])
            opre_ref[:, hs] = o.astype(BF16)
            ob_ref[:, hs] = (_rms_fwd(o, nw_ref[:, hs]) * _silu(hg_ref[:, hs].astype(F32))).astype(BF16)

    blk = lambda j: pl.BlockSpec((n, D), lambda c: (c, j))
    return pl.pallas_call(
        body, name="hgrn_fwd", grid=(nc,),
        in_specs=[blk(0), blk(0), blk(1), blk(2), pl.BlockSpec((2, D), lambda c: (0, 0)),
                  pl.BlockSpec((1, D), lambda c: (0, 0))],
        out_specs=[blk(0), blk(0), blk(0)],
        out_shape=[jax.ShapeDtypeStruct((L, D), BF16), jax.ShapeDtypeStruct((L, D), BF16),
                   jax.ShapeDtypeStruct((nc * 128, D), BF16)],
        scratch_shapes=[pltpu.VMEM((128, D), F32), pltpu.VMEM((n, D), F32)],
        compiler_params=_cparams(("arbitrary",)),
    )(pa, phf, pa, pa, hgl, nwx)


def _hgrn_bwd(pa, phf, opre, sin, dob, hgl, nwx, exchange=None):
    L = pa.shape[0]
    n, s = HG_STEP, HG_SUB
    nc = L // n

    def body(hq_ref, hf_ref, hi_ref, hg_ref, opre_ref, sin_ref, dob_ref, hgl_ref, nw_ref,
             dpa_ref, dhgl_ref, dnw_ref, dst_ref, b_ref, dlb_ref, dq_ref, dk_ref, db_ref):
        i = pl.program_id(0)

        @pl.when(i == 0)
        def _():
            dst_ref[...] = jnp.zeros_like(dst_ref)
            dlb_ref[...] = jnp.zeros_like(dlb_ref)
            dnw_ref[...] = jnp.zeros_like(dnw_ref)

        hq = hq_ref[...].astype(F32)
        lb, qf, sg, f = _hg_gates(hq, hf_ref[...], hgl_ref, b_ref)
        kf = 1.0 - f
        fac, val = _hg_factors(qf, kf, b_ref)
        ri, ci = _iota2((s, s), 0), _iota2((s, s), 1)
        causal, causal_t = ri >= ci, ri <= ci
        last_row = _iota2((n, 128), 0) == n - 1
        for h in range(HG_HEADS):
            hs = slice(128 * h, 128 * h + 128)
            o = opre_ref[:, hs].astype(F32)
            gate = hg_ref[:, hs].astype(F32)
            dout = dob_ref[:, hs].astype(F32)
            sgate = _silu(gate)
            do, dnw = _rms_bwd(dout * sgate, o, nw_ref[:, hs])
            dnw_ref[:, hs] += dnw
            dpa_ref[:, 2 * D + 128 * h:2 * D + 128 * h + 128] = (
                dout * _rms_fwd(o, nw_ref[:, hs]) * _dsilu(gate)).astype(BF16)
            sth = sin_ref[:, hs].astype(F32)
            dsth = dst_ref[:, hs]
            v = hi_ref[:, hs]
            v0, v1 = v[0:s], v[s:n]
            do0, do1 = do[0:s], do[s:n]
            qe, ke = val["qe"][:, hs], val["ke"][:, hs]
            qm0, km0, qm1, km1 = val["qm0"][:, hs], val["km0"][:, hs], val["qm1"][:, hs], val["km1"][:, hs]
            qb, kb = val["qb"][:, hs], val["kb"][:, hs]
            dqe = _dot(do, sth)
            dstin = _dot_tn(do, qe)
            a00t = jnp.where(causal_t, _dot_nt(km0, qm0), 0.0)
            a11t = jnp.where(causal_t, _dot_nt(km1, qm1), 0.0)
            a10t = _dot_nt(kb, qb)
            dat00 = jnp.where(causal, _dot_nt(do0, v0), 0.0)
            dat11 = jnp.where(causal, _dot_nt(do1, v1), 0.0)
            dat10 = _dot_nt(do1, v0)
            dat00t = jnp.where(causal_t, _dot_nt(v0, do0), 0.0)
            dat11t = jnp.where(causal_t, _dot_nt(v1, do1), 0.0)
            dat10t = _dot_nt(v0, do1)
            dv = jnp.concatenate([_dot(a00t, do0) + _dot(a10t, do1), _dot(a11t, do1)], axis=0)
            dqm0, dkm0 = _dot(dat00, km0), _dot(dat00t, qm0)
            dqm1, dkm1 = _dot(dat11, km1), _dot(dat11t, qm1)
            dqb, dkb = _dot(dat10, kb), _dot(dat10t, qb)
            dke = _dot(v, dsth)
            dv += _dot_nt(ke, dsth)
            deb = jnp.sum(dsth * sth, axis=0, keepdims=True)
            dst_ref[:, hs] = dstin + dsth * fac["eb"][:, hs]
            dq = dqe * fac["eq"][:, hs] + jnp.concatenate(
                [dqm0 * fac["eq0"][:, hs], dqm1 * fac["eq1"][:, hs] + dqb * fac["eqb"][:, hs]], axis=0)
            dk = dke * fac["ek"][:, hs] + jnp.concatenate(
                [dkm0 * fac["ek0"][:, hs] + dkb * fac["ekb"][:, hs], dkm1 * fac["ek1"][:, hs]], axis=0)
            tke = dke * ke
            db = dqe * qe - tke + jnp.concatenate(
                [dqm0 * qm0 - dkm0 * km0 - dkb * kb, dqm1 * qm1 - dkm1 * km1 + dqb * qb], axis=0)
            dblast = jnp.sum(tke, axis=0, keepdims=True) + deb * fac["eb"][:, hs]
            db_ref[:, hs] = db + jnp.where(last_row, dblast, 0.0)
            dq_ref[:, hs] = dq
            dk_ref[:, hs] = dk
            dpa_ref[:, D + 128 * h:D + 128 * h + 128] = dv.astype(BF16)
        dg = _dot_hi(_triu(n), db_ref[...], 1)
        df = dg / f - dk_ref[...]
        dpa_ref[:, 3 * D:4 * D] = (df * (1.0 - lb) * sg * (1.0 - sg)).astype(BF16)
        dpa_ref[:, 0:D] = (dq_ref[...] * _dsilu(hq)).astype(BF16)
        dlb_ref[...] += jnp.sum(df * (1.0 - sg), axis=0, keepdims=True)

        @pl.when(i == nc - 1)
        def _():
            d0 = dlb_ref[...] * lb * (1.0 - lb)
            dhgl_ref[...] = jnp.concatenate([d0, -d0], axis=0)

    rev = lambda c: nc - 1 - c
    blk = lambda j: pl.BlockSpec((n, D), lambda c: (rev(c), j))
    payload = None if exchange is None else (_Exchange, exchange)
    args = [pa, phf, pa, pa, opre, sin, dob, hgl, nwx]
    in_specs, out_specs, out_shape, scratch = _carried_specs(
        payload,
        [blk(0), blk(0), blk(1), blk(2), blk(0), blk(0), blk(0), pl.BlockSpec((2, D), lambda c: (0, 0)),
         pl.BlockSpec((1, D), lambda c: (0, 0))],
        [pl.BlockSpec((n, 4 * D), lambda c: (rev(c), 0)), pl.BlockSpec((2, D), lambda c: (0, 0)),
         pl.BlockSpec((1, D), lambda c: (0, 0))],
        [jax.ShapeDtypeStruct((L, 4 * D), BF16), jax.ShapeDtypeStruct((2, D), F32), jax.ShapeDtypeStruct((1, D), F32)],
        [pltpu.VMEM((128, D), F32), pltpu.VMEM((n, D), F32), pltpu.VMEM((1, D), F32),
         pltpu.VMEM((n, D), F32), pltpu.VMEM((n, D), F32), pltpu.VMEM((n, D), F32)])
    return pl.pallas_call(
        _carried(body, len(args), 3, payload, _first_last(nc)), name="hgrn_bwd", grid=(nc,),
        in_specs=in_specs, out_specs=out_specs, out_shape=out_shape, scratch_shapes=scratch,
        compiler_params=_cparams(("arbitrary",)),
    )(*(args if exchange is None else args + [exchange]))


XA_SCALE = XA_DH ** -0.5


def _xa_probs(qh, kmh):
    sc = _dot_nt(qh, kmh) * XA_SCALE
    p = jnp.exp(sc - jnp.max(sc, axis=1, keepdims=True))
    return p * (1.0 / jnp.sum(p, axis=1, keepdims=True))


def _xattn_fwd(x1, nw, wq, kv, wo, *, tm):
    L = x1.shape[0]
    tm = min(tm, L)

    def body(x_ref, nw_ref, wq_ref, kv_ref, wo_ref, o_ref, ox_ref):
        x = x_ref[...]
        q = _dot(_rms_fwd(x, nw_ref[...]), wq_ref[...])
        for h in range(XA_HEADS):
            hs = slice(XA_DH * h, XA_DH * h + XA_DH)
            p = _xa_probs(q[:, hs], kv_ref[:, hs])
            ox_ref[:, hs] = _dot(p, kv_ref[:, D + XA_DH * h:D + XA_DH * h + XA_DH])
        o_ref[...] = x + _dot(ox_ref[...], wo_ref[...])

    full = lambda a: pl.BlockSpec(a.shape, lambda i: (0, 0))
    return pl.pallas_call(
        body, name="xattn_fwd", grid=(L // tm,),
        in_specs=[pl.BlockSpec((tm, D), lambda i: (i, 0)), full(nw), full(wq), full(kv), full(wo)],
        out_specs=pl.BlockSpec((tm, D), lambda i: (i, 0)),
        out_shape=jax.ShapeDtypeStruct((L, D), F32),
        scratch_shapes=[pltpu.VMEM((tm, D), F32)],
        compiler_params=_cparams(("parallel",)),
    )(x1, nw, wq, kv, wo)


def _xattn_bwd(x1, dx2, nw, wq, kv, wo, w_out, *, tm):
    L = x1.shape[0]
    tm = min(tm, L)

    def body(x_ref, dx2_ref, nw_ref, wq_ref, kv_ref, wo_ref, wout_ref, dx1_ref, dx1b_ref, h_ref, dq_ref, ox_ref,
             dya_ref, dob_ref, dkv_ref, dnw_ref, dqs_ref):
        @pl.when(pl.program_id(0) == 0)
        def _():
            dkv_ref[...] = jnp.zeros_like(dkv_ref)
            dnw_ref[...] = jnp.zeros_like(dnw_ref)

        x = x_ref[...]
        dx2 = dx2_ref[...]
        hn = _rms_fwd(x, nw_ref[...]).astype(BF16)
        h_ref[...] = hn
        q = _dot(hn, wq_ref[...])
        dox = _dot_nt(dx2, wo_ref[...])
        for h in range(XA_HEADS):
            hs = slice(XA_DH * h, XA_DH * h + XA_DH)
            vs = slice(D + XA_DH * h, D + XA_DH * h + XA_DH)
            qh, kmh, vmh, doxh = q[:, hs], kv_ref[:, hs], kv_ref[:, vs], dox[:, hs]
            p = _xa_probs(qh, kmh)
            ox_ref[:, hs] = _dot(p, vmh).astype(BF16)
            dp = _dot_nt(doxh, vmh)
            dkv_ref[:, vs] += _dot_tn(p, doxh)
            ds = p * (dp - jnp.sum(dp * p, axis=1, keepdims=True)) * XA_SCALE
            dqs_ref[:, hs] = _dot(ds, kmh)
            dkv_ref[:, hs] += _dot_tn(ds, qh)
        dq = dqs_ref[...]
        dq_ref[...] = dq.astype(BF16)
        dx, dnw = _rms_bwd(_dot_nt(dq, wq_ref[...]), x, nw_ref[...])
        dx1 = dx2 + dx
        dx1_ref[...] = dx1
        dx1b = dx1.astype(BF16)
        dx1b_ref[...] = dx1b
        dya_ref[...] = _dot_nt(dx1b, wout_ref[0:D, :]).astype(BF16)
        dob_ref[...] = _dot_nt(dx1b, wout_ref[D:2 * D, :]).astype(BF16)
        dnw_ref[...] += dnw

    full = lambda a: pl.BlockSpec(a.shape, lambda i: (0, 0))
    row = pl.BlockSpec((tm, D), lambda i: (i, 0))
    rowb = jax.ShapeDtypeStruct((L, D), BF16)
    return pl.pallas_call(
        body, name="xattn_bwd", grid=(L // tm,),
        in_specs=[row, row, full(nw), full(wq), full(kv), full(wo), full(w_out)],
        out_specs=[row] * 7 + [pl.BlockSpec((MEM_LEN, 2 * D), lambda i: (0, 0)), pl.BlockSpec((1, D), lambda i: (0, 0))],
        out_shape=[jax.ShapeDtypeStruct((L, D), F32)] + [rowb] * 6 + [
            jax.ShapeDtypeStruct((MEM_LEN, 2 * D), F32), jax.ShapeDtypeStruct((1, D), F32)],
        scratch_shapes=[pltpu.VMEM((tm, D), F32)],
        compiler_params=_cparams(("arbitrary",)),
    )(x1, dx2, nw, wq, kv, wo, w_out)


def _ffn_blk(tm, tf):
    return pl.BlockSpec((tm, tf), lambda i, j: (i, j))


def _interleave(wg, wu, tf):
    return jnp.stack([wg.reshape(FFN // tf, tf, D), wu.reshape(FFN // tf, tf, D)], axis=1).reshape(2 * FFN, D)


def _ffn_fwd(x2, nw, wgu, wd, *, tm, tf):
    L = x2.shape[0]
    tm = min(tm, L)
    nf = FFN // tf

    def body(x_ref, nw_ref, wgu_ref, wd_ref, o_ref, hn_ref, g_ref, u_ref, h_ref, acc_ref):
        j = pl.program_id(1)

        @pl.when(j == 0)
        def _():
            hn = _rms_fwd(x_ref[...], nw_ref[...]).astype(BF16)
            h_ref[...] = hn
            hn_ref[...] = hn
            acc_ref[...] = jnp.zeros_like(acc_ref)

        gu = _dot_nt(h_ref[...], wgu_ref[...]).astype(BF16)
        g, u = gu[:, 0:tf], gu[:, tf:2 * tf]
        g_ref[...] = g
        u_ref[...] = u
        acc_ref[...] += _dot(_silu(g.astype(F32)) * u.astype(F32), wd_ref[...])

        @pl.when(j == nf - 1)
        def _():
            o_ref[...] = x_ref[...] + acc_ref[...]

    row = pl.BlockSpec((tm, D), lambda i, j: (i, 0))
    wide = jax.ShapeDtypeStruct((L, FFN), BF16)
    return pl.pallas_call(
        body, name="ffn_fwd", grid=(L // tm, nf),
        in_specs=[row, pl.BlockSpec((1, D), lambda i, j: (0, 0)),
                  pl.BlockSpec((2 * tf, D), lambda i, j: (j, 0)), pl.BlockSpec((tf, D), lambda i, j: (j, 0))],
        out_specs=[row, row, _ffn_blk(tm, tf), _ffn_blk(tm, tf)],
        out_shape=[jax.ShapeDtypeStruct((L, D), F32), jax.ShapeDtypeStruct((L, D), BF16), wide, wide],
        scratch_shapes=[pltpu.VMEM((tm, D), BF16), pltpu.VMEM((tm, D), F32)],
        compiler_params=_cparams(("parallel", "arbitrary"), vmem_mb=58),
    )(x2, nw, wgu, wd)


def _ffn_bwd(x2, dx3, g, u, nw, wgu, wd, *, tm, tf):
    L = x2.shape[0]
    tm = min(tm, L)
    nf = FFN // tf

    def body(x_ref, dx3_ref, g_ref, u_ref, nw_ref, wgu_ref, wd_ref,
             dx2_ref, a_ref, dg_ref, du_ref, dnw_ref, d3_ref, acc_ref, da_ref, dgu_ref):
        i, j = pl.program_id(0), pl.program_id(1)

        @pl.when(j == 0)
        def _():
            d3_ref[...] = dx3_ref[...].astype(BF16)
            acc_ref[...] = jnp.zeros_like(acc_ref)
            da_ref[...] = jnp.zeros_like(da_ref)
            dgu_ref[...] = jnp.zeros_like(dgu_ref)

        @pl.when(jnp.logical_and(i == 0, j == 0))
        def _():
            dnw_ref[...] = jnp.zeros_like(dnw_ref)

        acc_ref[...] += _dot(dgu_ref[...], wgu_ref[...])
        g = g_ref[...].astype(F32)
        u = u_ref[...].astype(F32)
        s = _sigmoid_gate(g)
        sg = g * s
        da = da_ref[...]
        a_ref[...] = (sg * u).astype(BF16)
        dg = (da * u * (s + sg * (1.0 - s))).astype(BF16)
        du = (da * sg).astype(BF16)
        dg_ref[...] = dg
        du_ref[...] = du
        dgu_ref[...] = jnp.concatenate([dg, du], axis=1)
        da_ref[...] = _dot_nt(d3_ref[...], wd_ref[...])

        @pl.when(j == nf + 1)
        def _():
            dx, dnw = _rms_bwd(acc_ref[...], x_ref[...], nw_ref[...])
            dx2_ref[...] = dx3_ref[...] + dx
            dnw_ref[...] += dnw

    clamp = lambda j, d: jnp.clip(j - d, 0, nf - 1)
    row = pl.BlockSpec((tm, D), lambda i, j: (i, 0))
    blk = pl.BlockSpec((tm, tf), lambda i, j: (i, clamp(j, 1)))
    wide = jax.ShapeDtypeStruct((L, FFN), BF16)
    return pl.pallas_call(
        body, name="ffn_bwd", grid=(L // tm, nf + 2),
        in_specs=[row, row, blk, blk, pl.BlockSpec((1, D), lambda i, j: (0, 0)),
                  pl.BlockSpec((2 * tf, D), lambda i, j: (clamp(j, 2), 0)),
                  pl.BlockSpec((tf, D), lambda i, j: (clamp(j, 0), 0))],
        out_specs=[row, blk, blk, blk, pl.BlockSpec((1, D), lambda i, j: (0, 0))],
        out_shape=[jax.ShapeDtypeStruct((L, D), F32), wide, wide, wide, jax.ShapeDtypeStruct((1, D), F32)],
        scratch_shapes=[pltpu.VMEM((tm, D), BF16), pltpu.VMEM((tm, D), F32), pltpu.VMEM((tm, tf), F32),
                        pltpu.VMEM((tm, 2 * tf), BF16)],
        compiler_params=_cparams(("arbitrary", "arbitrary"), vmem_mb=56),
    )(x2, dx3, g, u, nw, wgu, wd)


def _final(x3, tgt, nw, *, tm):
    L = x3.shape[0]
    tm = min(tm, L)

    def body(x_ref, t_ref, nw_ref, dx_ref, dxb_ref, loss_ref, dnw_ref):
        @pl.when(pl.program_id(0) == 0)
        def _():
            loss_ref[...] = jnp.zeros_like(loss_ref)
            dnw_ref[...] = jnp.zeros_like(dnw_ref)

        x = x_ref[...]
        w = nw_ref[...]
        err = _rms_fwd(x, w) - t_ref[...]
        part = 0.5 * jnp.sum(jnp.sum(err * err, axis=1, keepdims=True), axis=0, keepdims=True) * (1.0 / D)
        loss_ref[...] += jnp.where(_iota2((1, 128), 1) == 0, part, 0.0)
        dx, dnw = _rms_bwd(err * (1.0 / D), x, w)
        dx_ref[...] = dx
        dxb_ref[...] = dx.astype(BF16)
        dnw_ref[...] += dnw

    row = pl.BlockSpec((tm, D), lambda i: (i, 0))
    return pl.pallas_call(
        body, name="final_loss", grid=(L // tm,),
        in_specs=[row, row, pl.BlockSpec((1, D), lambda i: (0, 0))],
        out_specs=[row, row, pl.BlockSpec((1, 128), lambda i: (0, 0)), pl.BlockSpec((1, D), lambda i: (0, 0))],
        out_shape=[jax.ShapeDtypeStruct((L, D), F32), jax.ShapeDtypeStruct((L, D), BF16),
                   jax.ShapeDtypeStruct((1, 128), F32), jax.ShapeDtypeStruct((1, D), F32)],
        compiler_params=_cparams(("arbitrary",)),
    )(x3, tgt, nw)


def _adam_update(g, w, m, v):
    c1 = 1.0 / (1.0 - ADAM_B1 ** ADAM_STEP)
    c2 = 1.0 / (1.0 - ADAM_B2 ** ADAM_STEP)
    nm = ADAM_B1 * m + (1.0 - ADAM_B1) * g
    nv = ADAM_B2 * v + (1.0 - ADAM_B2) * (g * g)
    return -ADAM_LR * ((nm * c1) / (jnp.sqrt(nv * c2) + ADAM_EPS) + ADAM_WD * w), nm, nv


def _adamw_small(tot, conv_w_grad, w, m, v):
    names = [n for n, _, _ in SMALL if n != "loss"]
    k = len(names)

    def body(tot_ref, cwg_ref, *refs):
        w_refs, m_refs, v_refs = refs[0:k], refs[k:2 * k], refs[2 * k:3 * k]
        g_refs, d_refs, nm_refs, nv_refs = (refs[(3 + j) * k:(4 + j) * k] for j in range(4))
        for i, n in enumerate(names):
            row, nr, nc = SMALL_AT[n]
            g = cwg_ref[...] if n == "conv_w" else tot_ref[row:row + nr, 0:nc]
            d, nm, nv = _adam_update(g, w_refs[i][...], m_refs[i][...], v_refs[i][...])
            g_refs[i][...] = g
            d_refs[i][...] = d
            nm_refs[i][...] = nm
            nv_refs[i][...] = nv

    sds = [jax.ShapeDtypeStruct(t.shape, F32) for t in w]
    res = pl.pallas_call(body, name="adamw_small", out_shape=sds * 4)(tot, conv_w_grad, *w, *m, *v)
    return res[0:k], res[k:2 * k], res[2 * k:3 * k], res[3 * k:4 * k]


def _adamw(parts, w, m, v, *, tr, name):
    n_parts, R, C = parts.shape
    tr = min(tr, R)

    def body(p_ref, w_ref, m_ref, v_ref, g_ref, d_ref, nm_ref, nv_ref):
        g = p_ref[0].astype(F32)
        for k in range(1, n_parts):
            g = g + p_ref[k].astype(F32)
        d, nm, nv = _adam_update(g, w_ref[...], m_ref[...], v_ref[...])
        g_ref[...] = g
        nm_ref[...] = nm
        nv_ref[...] = nv
        d_ref[...] = d

    blk = pl.BlockSpec((tr, C), lambda i: (i, 0))
    sds = jax.ShapeDtypeStruct((R, C), F32)
    return pl.pallas_call(
        body, name=name, grid=(R // tr,),
        in_specs=[pl.BlockSpec((n_parts, tr, C), lambda i: (0, i, 0)), blk, blk, blk],
        out_specs=[blk, blk, blk, blk], out_shape=[sds, sds, sds, sds],
        compiler_params=_cparams(("parallel",)),
    )(parts, w, m, v)


def _position():
    return lax.axis_index("x"), lax.axis_index("y"), lax.axis_index("c")


def _comm_scratch():
    return [pltpu.SemaphoreType.DMA((7,)), pltpu.SemaphoreType.DMA((7,)), pltpu.SemaphoreType.DMA]


class _Gather:
    def __init__(self, x_ref, out_ref, send_sems, recv_sems, local_sem):
        x, y, c = _position()
        me, sibling = (x, y, c), (x, y, 1 - c)
        chips = [(1 - x, y), (x, 1 - y), (1 - x, 1 - y)]

        def rows(px, py, pc):
            return out_ref.at[4 * px + 2 * py + pc]

        def copy(k, block, to, src=None):
            return pltpu.make_async_remote_copy(
                src_ref=rows(*block) if src is None else src, dst_ref=rows(*block),
                send_sem=send_sems.at[k], recv_sem=recv_sems.at[k], device_id=to, device_id_type=MESH)

        self.mine = pltpu.make_async_copy(x_ref, rows(*me), local_sem)
        self.first = [copy(0, me, sibling, src=x_ref)]
        self.first += [copy(1 + j, me, (*chip, c), src=x_ref) for j, chip in enumerate(chips)]
        self.passed = [copy(4 + j, (*chip, c), sibling) for j, chip in enumerate(chips)]
        self.from_chips = [copy(1 + j, (*chip, c), me) for j, chip in enumerate(chips)]
        self.from_sibling = [copy(0, sibling, me)] + [copy(4 + j, (*chip, 1 - c), me) for j, chip in enumerate(chips)]

    def start(self):
        self.mine.start()
        for cp in self.first:
            cp.start()

    def forward(self):
        for got, cp in zip(self.from_chips, self.passed):
            got.wait_recv()
            cp.start()

    def finish(self):
        for got in self.from_sibling:
            got.wait_recv()
        for cp in self.first + self.passed:
            cp.wait_send()
        self.mine.wait()


class _Exchange:
    def __init__(self, g_ref, out_ref, send_sems, recv_sems, local_sem):
        x, y, c = _position()
        me = 4 * x + 2 * y + c
        self.mine = pltpu.make_async_copy(g_ref.at[me], out_ref.at[me], local_sem)
        self.copies = []
        for k in range(1, N_DEV):
            px = 1 - x if k & 4 else x
            py = 1 - y if k & 2 else y
            pc = 1 - c if k & 1 else c
            self.copies.append(pltpu.make_async_remote_copy(
                src_ref=g_ref.at[4 * px + 2 * py + pc], dst_ref=out_ref.at[me],
                send_sem=send_sems.at[k - 1], recv_sem=recv_sems.at[k - 1],
                device_id=(px, py, pc), device_id_type=MESH))

    def start(self):
        self.mine.start()
        for cp in self.copies:
            cp.start()

    def finish(self):
        for cp in self.copies:
            cp.wait()
        self.mine.wait()


def _allgather(xp):
    R, C = xp.shape

    def body(x_ref, out_ref, send_sems, recv_sems, local_sem):
        g = _Gather(x_ref, out_ref, send_sems, recv_sems, local_sem)
        g.start()
        g.forward()
        g.finish()

    return pl.pallas_call(
        body, name="allgather_w_in",
        out_shape=jax.ShapeDtypeStruct((N_DEV, R, C), xp.dtype),
        in_specs=[pl.BlockSpec(memory_space=pltpu.HBM)], out_specs=pl.BlockSpec(memory_space=pltpu.HBM),
        scratch_shapes=_comm_scratch(),
    )(xp)


def _carried(body, n_in, n_out, payload, phases):
    if payload is None:
        return body
    kind = payload[0]

    def new_body(*refs):
        ins, src_ref = refs[:n_in], refs[n_in]
        outs, dst_ref = refs[n_in + 1:n_in + 1 + n_out], refs[n_in + 1 + n_out]
        scratch, sems = refs[n_in + 2 + n_out:-3], refs[-3:]

        def run(before):
            for when, action, is_before in phases:
                if is_before == before:
                    @pl.when(when())
                    def _():
                        action(kind(src_ref, dst_ref, *sems))

        run(True)
        body(*ins, *outs, *scratch)
        run(False)

    return new_body


def _carried_specs(payload, in_specs, out_specs, out_shape, scratch):
    if payload is None:
        return in_specs, out_specs, out_shape, scratch
    kind, arr = payload
    landing = (N_DEV,) + arr.shape if kind is _Gather else arr.shape
    hbm = pl.BlockSpec(memory_space=pltpu.HBM)
    return (in_specs + [hbm], out_specs + [hbm], out_shape + [jax.ShapeDtypeStruct(landing, arr.dtype)],
            scratch + _comm_scratch())


def _small_allreduce(sp):
    R, C = sp.shape

    def body(s_ref, out_ref, buf_ref, send_sems, recv_sems):
        x, y, c = _position()
        me = 4 * x + 2 * y + c
        buf_ref[me] = s_ref[...]
        copies = []
        for k in range(1, N_DEV):
            px = 1 - x if k & 4 else x
            py = 1 - y if k & 2 else y
            pc = 1 - c if k & 1 else c
            cp = pltpu.make_async_remote_copy(
                src_ref=s_ref, dst_ref=buf_ref.at[me], send_sem=send_sems.at[k - 1], recv_sem=recv_sems.at[k - 1],
                device_id=(px, py, pc), device_id_type=MESH)
            cp.start()
            copies.append(cp)
        for cp in copies:
            cp.wait()
        tot = buf_ref[0]
        for k in range(1, N_DEV):
            tot = tot + buf_ref[k]
        out_ref[...] = tot

    return pl.pallas_call(
        body, name="allreduce_small",
        out_shape=jax.ShapeDtypeStruct((R, C), F32),
        in_specs=[pl.BlockSpec(memory_space=pltpu.VMEM)], out_specs=pl.BlockSpec(memory_space=pltpu.VMEM),
        scratch_shapes=[pltpu.VMEM((N_DEV, R, C), F32), pltpu.SemaphoreType.DMA((7,)), pltpu.SemaphoreType.DMA((7,))],
    )(sp)


def _local_step(x, mem, tgt, wt, small, dist=None):
    w_in = wt["w_in"]
    zpad = jnp.zeros((WB - D - SSD_HEADS, D), BF16)
    w_a = jnp.concatenate([w_in[2576:3600], w_in[4624:6672], w_in[3600:4624]], axis=0)
    w_b = jnp.concatenate([w_in[0:D], w_in[2560:2576], zpad], axis=0)
    w_c = w_in[D:2560]
    a_log, d_skip = small["a_log"], small["d_skip"]
    avec = jnp.pad(-jnp.exp(a_log), ((0, 0), (0, 128 - SSD_HEADS)))
    aexp = jnp.repeat(-jnp.exp(a_log), SSD_P, axis=1)
    dexp = jnp.repeat(d_skip, SSD_P, axis=1)
    dtb = jnp.pad(small["dt_bias"], ((0, 0), (0, 128 - SSD_HEADS)))
    emat = (lax.broadcasted_iota(jnp.int32, (128, D), 0) == lax.broadcasted_iota(jnp.int32, (128, D), 1) // SSD_P
            ).astype(F32)
    hg_nwx = jnp.tile(small["hg_norm_w"], (1, HG_HEADS))
    hgl = small["hg_lower_bounds"]
    nfw = small["norm_final_w"].reshape(1, D)

    received = {}
    pieces = (lambda group, grads: None) if dist is None else dist["pieces"]
    pa, phf, pz, pdt, pc, hn_mix, *got = _inproj(x, small["norm_mix_w"], w_a, w_b, w_c, tm=256,
                                                 gather=None if dist is None else dist["rest_pack"])
    if got:
        wt = {**wt, **dist["unpack_rest"](got[0])}
    xbc = _conv_fwd(pc, small["conv_w"], small["conv_b"], tm=512)
    ya, ypre, ssd_sin = _ssd_fwd(xbc, pz, pdt, dtb, avec, aexp, dexp, small["ssd_norm_w"], emat)
    ob, opre, hg_sin = _hgrn_fwd(pa, phf, hgl, hg_nwx)
    x1 = _mm2_res(x, ya, ob, wt["w_out"], tm=512, name="outproj")
    kvb, mn = _norm_mm(mem, small["norm_mem_w"], wt["xa_wkv"], tm=256, tn=1024, name="mem_kv", emit_h=True,
                       out_dtype=BF16)
    x2 = _xattn_fwd(x1, small["norm_xa_w"], wt["xa_wq"], kvb, wt["xa_wo"], tm=512)
    x3, hn_ffn, gate, up = _ffn_fwd(x2, small["norm_ffn_w"], _interleave(wt["ffn_w_gate"], wt["ffn_w_up"], 1408),
                                    wt["ffn_w_down"], tm=512, tf=1408)

    dx3, dx3b, loss, g_nf = _final(x3, tgt, nfw, tm=512)
    dx2, act, dg, du, g_nffn = _ffn_bwd(x2, dx3, gate, up, small["norm_ffn_w"],
                                        _interleave(wt["ffn_w_gate"], wt["ffn_w_up"], 256), wt["ffn_w_down"],
                                        tm=1024, tf=256)
    g_wg = _dw(dg, hn_ffn, tM=1408, tN=1024, tl=2048, name="dw_gate")
    g_wu = _dw(du, hn_ffn, tM=1408, tN=1024, tl=2048, name="dw_up")
    g_wd = _dw(act, dx3b, tM=1408, tN=1024, tl=2048, name="dw_down")
    dx1, dx1b, hn_xa, dq, ox, dya, dob, dkv, g_nxa = _xattn_bwd(
        x1, dx2, small["norm_xa_w"], wt["xa_wq"], kvb, wt["xa_wo"], wt["w_out"], tm=512)
    dkvb = dkv.astype(BF16)
    g_wq = _dw(hn_xa, dq, tM=1024, tN=1024, tl=2048, name="dw_q")
    g_wo = _dw(ox, dx2, tM=1024, tN=1024, tl=2048, name="dw_o")
    g_wkv = _dw(dkvb, mn, tM=1024, tN=1024, tl=256, name="dw_kv")
    _, g_nmem = _mm_normbwd([dkvb], [wt["xa_wkv"]], mem, small["norm_mem_w"], None, tm=256, name="mem_bwd")
    g_wout = jnp.concatenate([_dw(ya, dx1b, tM=1024, tN=1024, tl=2048, name="dw_out_a"),
                              _dw(ob, dx1b, tM=1024, tN=1024, tl=2048, name="dw_out_b")], axis=0)
    ffn_grads = {"ffn_w_gate": g_wg, "ffn_w_up": g_wu, "ffn_w_down": g_wd}
    mid_grads = {"w_out": g_wout, "xa_wq": g_wq, "xa_wkv": g_wkv, "xa_wo": g_wo}
    dpa, g_hgl, g_hgn_x, *got = _hgrn_bwd(pa, phf, opre, hg_sin, dob, hgl, hg_nwx,
                                          exchange=pieces("ffn", ffn_grads))
    received["ffn"] = got[0] if got else None
    dpb, dxbc, g_ssdn, g_dtb, g_alog, g_dx, *got = _ssd_bwd(
        xbc, pz, pdt, ypre, ssd_sin, dya, dtb, avec, aexp, dexp, small["ssd_norm_w"], emat,
        exchange=pieces("mid", mid_grads))
    received["mid"] = got[0] if got else None
    dpc, g_cw, g_cb = _conv_bwd(pc, dxbc, small["conv_w"], small["conv_b"], tm=512)
    g_wa = _dw(dpa, hn_mix, tM=1024, tN=1024, tl=2048, name="dw_in_a")
    g_wb = _dw(dpb, hn_mix, tM=384, tN=1024, tl=2048, name="dw_in_b")
    g_wc = _dw(dpc, hn_mix, tM=512, tN=1024, tl=2048, name="dw_in_c")
    g_win = jnp.concatenate([g_wb[0:D], g_wc, g_wb[D:D + SSD_HEADS], g_wa[0:D], g_wa[3 * D:4 * D], g_wa[D:3 * D]],
                            axis=0)
    grad_x, g_nmix, *got = _mm_normbwd([dpa, dpb, dpc], [w_a, w_b, w_c], x, small["norm_mix_w"], dx1, tm=256,
                                         name="inproj_bwd", exchange=pieces("in", {"w_in": g_win}))
    received["in"] = got[0] if got else None

    big = {"w_in": g_win, **mid_grads, **ffn_grads}
    smallg = {
        "norm_mix_w": g_nmix, "conv_w": g_cw, "conv_b": g_cb, "dt_bias": g_dtb[:, 0:SSD_HEADS],
        "a_log": g_alog[:, 0:SSD_HEADS], "d_skip": g_dx.reshape(SSD_HEADS, SSD_P).sum(axis=1).reshape(1, SSD_HEADS),
        "ssd_norm_w": g_ssdn, "hg_lower_bounds": g_hgl,
        "hg_norm_w": g_hgn_x.reshape(HG_HEADS, HG_K).sum(axis=0).reshape(1, HG_K),
        "norm_xa_w": g_nxa, "norm_mem_w": g_nmem, "norm_ffn_w": g_nffn, "norm_final_w": g_nf,
        "loss": loss[:, 0:1]}
    return grad_x, big, smallg, received


COL_SHARDED = ("w_in", "xa_wkv", "ffn_w_gate", "ffn_w_up")


def _pad_rows(t, rows):
    return jnp.pad(t, [(0, 0)] * (t.ndim - 2) + [(0, rows - t.shape[-2]), (0, 0)])


def _group_fill(parts, group, lead, dtype):
    used = sum(p.shape[-2] for p in parts)
    if used < GROUP_ROWS[group]:
        parts.append(jnp.zeros(lead + (GROUP_ROWS[group] - used, D), dtype))
    return parts


def _pack_group(shards, group, dtype, extra=None):
    parts = [_pad_rows(shards[n].astype(dtype).reshape(r, D), _rows_padded(r)) for n, r in GROUPS[group]]
    if extra is not None:
        parts.append(extra)
    return jnp.concatenate(_group_fill(parts, group, (), dtype), axis=0)


def _unpack_group(packed, group, shapes):
    out, off = {}, 0
    for n, r in GROUPS[group]:
        t = packed[off:off + r]
        out[n] = (t.T if n in COL_SHARDED else t).reshape(shapes[n])
        off += _rows_padded(r)
    return out


def _row_shards(d):
    return {n: d[n][0].T if n in COL_SHARDED else d[n][0] for n in BIG}


def _unpack_gathered(gath, groups):
    out, base = {}, 0
    for group in groups:
        off = base
        for n, r in GROUPS[group]:
            out[n] = gath[:, off:off + r].reshape(N_DEV * r, D)
            off += _rows_padded(r)
        base += GROUP_ROWS[group]
    return out


def _grad_pieces(group, grads):
    dtype = PIECE_DTYPE[group]
    parts = [_pad_rows(grads[n].reshape(N_DEV, r, D).astype(dtype), _rows_padded(r)) for n, r in GROUPS[group]]
    return jnp.concatenate(_group_fill(parts, group, (N_DEV,), dtype), axis=1)


def _pack_small(vals):
    tot = None
    for n, r, c in SMALL:
        row = SMALL_AT[n][0]
        part = jnp.pad(vals[n].reshape(r, c), ((row, SMALL_ROWS - row - r), (0, SMALL_COLS - c)))
        tot = part if tot is None else tot + part
    return tot


WEIGHTS = ['norm_mix_w', 'w_in', 'conv_w', 'conv_b', 'dt_bias', 'a_log', 'd_skip', 'ssd_norm_w', 'hg_lower_bounds',
           'hg_norm_w', 'w_out', 'norm_xa_w', 'norm_mem_w', 'xa_wq', 'xa_wkv', 'xa_wo', 'norm_ffn_w', 'ffn_w_gate',
           'ffn_w_up', 'ffn_w_down', 'norm_final_w']
BIG = tuple(n for n, _ in PACK)


def kernel(x, mem, norm_mix_w, w_in, conv_w, conv_b, dt_bias, a_log, d_skip, ssd_norm_w, hg_lower_bounds, hg_norm_w, w_out, norm_xa_w, norm_mem_w, xa_wq, xa_wkv, xa_wo, norm_ffn_w, ffn_w_gate, ffn_w_up, ffn_w_down, norm_final_w, loss_target, m_norm_mix_w, m_w_in, m_conv_w, m_conv_b, m_dt_bias, m_a_log, m_d_skip, m_ssd_norm_w, m_hg_lower_bounds, m_hg_norm_w, m_w_out, m_norm_xa_w, m_norm_mem_w, m_xa_wq, m_xa_wkv, m_xa_wo, m_norm_ffn_w, m_ffn_w_gate, m_ffn_w_up, m_ffn_w_down, m_norm_final_w, v_norm_mix_w, v_w_in, v_conv_w, v_conv_b, v_dt_bias, v_a_log, v_d_skip, v_ssd_norm_w, v_hg_lower_bounds, v_hg_norm_w, v_w_out, v_norm_xa_w, v_norm_mem_w, v_xa_wq, v_xa_wkv, v_xa_wo, v_norm_ffn_w, v_ffn_w_gate, v_ffn_w_up, v_ffn_w_down, v_norm_final_w):
    args = dict(locals())
    w = {n: args[n] for n in WEIGHTS}
    mo = {n: args["m_" + n] for n in WEIGHTS}
    vo = {n: args["v_" + n] for n in WEIGHTS}
    me = 4 * lax.axis_index("x") + 2 * lax.axis_index("y") + lax.axis_index("c")

    big_sh = _row_shards(w)
    cw_bits = lax.bitcast_convert_type(conv_w[0], BF16).reshape(-1)
    cw_rows = jnp.pad(cw_bits, (0, CONV_BITS_ROWS * D - cw_bits.shape[0])).reshape(CONV_BITS_ROWS, D)
    gath = _allgather(_pack_group(big_sh, "in", BF16, extra=cw_rows))
    wt = _unpack_gathered(gath, ("in",))
    off = _rows_padded(GROUPS["in"][0][1])
    cw_all = lax.bitcast_convert_type(gath[:, off:off + 2].reshape(N_DEV, 2 * D)[:, 0:1536].reshape(N_DEV, 4, 192, 2),
                                      F32)
    conv_w_full = cw_all.transpose(1, 0, 2).reshape(4, 1536)

    small = {n: w[n][0] if w[n].ndim == 3 else w[n] for n in WEIGHTS if n not in BIG}
    small["conv_w"] = conv_w_full
    small["hg_lower_bounds"] = hg_lower_bounds
    dist = {"rest_pack": jnp.concatenate([_pack_group(big_sh, "mid", BF16), _pack_group(big_sh, "ffn", BF16)], axis=0),
            "unpack_rest": lambda g: _unpack_gathered(g, ("mid", "ffn")),
            "pieces": _grad_pieces}
    grad_x, _, gsmall, received = _local_step(x[0], mem[0], loss_target[0], wt, small, dist)

    shapes = {n: w[n].shape for n in BIG}
    m_sh, v_sh = _row_shards(mo), _row_shards(vo)
    out_g, out_d, out_m, out_v = {}, {}, {}, {}
    for group in GROUPS:
        wp = _pack_group(big_sh, group, F32)
        mp = _pack_group(m_sh, group, F32)
        vp = _pack_group(v_sh, group, F32)
        packed = _adamw(received[group], wp, mp, vp, tr=ADAM_ROWS[group], name="adamw_" + group)
        for dst, src in zip((out_g, out_d, out_m, out_v), packed):
            dst.update(_unpack_group(src, group, shapes))

    tot = _small_allreduce(_pack_small(gsmall))
    loss = tot[SMALL_AT["loss"][0], 0]
    cw_row = SMALL_AT["conv_w"][0]
    conv_w_grad = lax.dynamic_slice(tot, (cw_row, me * 192), (4, 192))
    names = [n for n, _, _ in SMALL if n != "loss"]
    as2d = lambda t: t.reshape(t.shape[-2:] if t.ndim > 1 else (1, t.shape[0]))
    small_out = _adamw_small(tot, conv_w_grad, *[[as2d(d[n]) for n in names] for d in (w, mo, vo)])
    for dst, src in zip((out_g, out_d, out_m, out_v), small_out):
        dst.update({n: t.reshape(w[n].shape) for n, t in zip(names, src)})
    return (loss, grad_x[None], *[out_g[n] for n in WEIGHTS], *[out_d[n] for n in WEIGHTS],
            *[out_m[n] for n in WEIGHTS], *[out_v[n] for n in WEIGHTS])
```

```python
import jax
import jax.numpy as jnp
from jax import lax
from jax.experimental import pallas as pl
from jax.experimental.pallas import tpu as pltpu

F32, BF16 = jnp.float32, jnp.bfloat16
MESH = pl.DeviceIdType.MESH

D = 1024
EPS = 1e-6
SSD_HEADS, SSD_P, SSD_N, SSD_Q = 16, 64, 128, 128
HG_HEADS, HG_K, HG_STEP, HG_SUB = 8, 128, 128, 64
XA_HEADS, XA_DH, MEM_LEN = 4, 256, 256
FFN = 2816
N_DEV = 8
WB, WC = 1152, 1536
ADAM_LR, ADAM_B1, ADAM_B2, ADAM_EPS, ADAM_WD, ADAM_STEP = 0.001, 0.9, 0.999, 1e-08, 0.01, 10
VMEM_MB = 2 ** 20

PACK = (("w_in", 834), ("w_out", 256), ("xa_wq", 128), ("xa_wkv", 256), ("xa_wo", 128),
        ("ffn_w_gate", 352), ("ffn_w_up", 352), ("ffn_w_down", 352))
ROW_TILE = 16
GROUPS = {"in": PACK[0:1], "mid": PACK[1:5], "ffn": PACK[5:8]}
GROUP_ROWS = {"in": 896, "mid": 768, "ffn": 1056}
ADAM_ROWS = {"in": 128, "mid": 128, "ffn": 176}
PIECE_DTYPE = {"in": BF16, "mid": F32, "ffn": F32}
CONV_BITS_ROWS = ROW_TILE


def _rows_padded(r):
    return -(-r // ROW_TILE) * ROW_TILE

SMALL = (("norm_mix_w", 1, 1024), ("conv_w", 4, 1536), ("conv_b", 1, 1536), ("dt_bias", 1, 16), ("a_log", 1, 16),
         ("d_skip", 1, 16), ("ssd_norm_w", 1, 1024), ("hg_lower_bounds", 2, 1024), ("hg_norm_w", 1, 128),
         ("norm_xa_w", 1, 1024), ("norm_mem_w", 1, 1024), ("norm_ffn_w", 1, 1024), ("norm_final_w", 1, 1024),
         ("loss", 1, 1))
SMALL_COLS = 1536
SMALL_ROWS = 24
SMALL_AT = {n: (sum(q for _, q, _ in SMALL[:i]), r, c) for i, (n, r, c) in enumerate(SMALL)}


def _cparams(sem=None, vmem_mb=48):
    return pltpu.CompilerParams(dimension_semantics=sem, vmem_limit_bytes=vmem_mb * VMEM_MB)


def _dot(a, b):
    return jnp.dot(a.astype(BF16), b.astype(BF16), preferred_element_type=F32)


def _dot_nt(a, b):
    return lax.dot_general(a.astype(BF16), b.astype(BF16), (((1,), (1,)), ((), ())), preferred_element_type=F32)


def _dot_tn(a, b):
    return lax.dot_general(a.astype(BF16), b.astype(BF16), (((0,), (0,)), ((), ())), preferred_element_type=F32)


def _split3(a):
    a1 = a.astype(BF16)
    r1 = a - a1.astype(F32)
    a2 = r1.astype(BF16)
    a3 = (r1 - a2.astype(F32)).astype(BF16)
    return a1, a2, a3


def _dot_hi(a, b, general=0):
    if general == 0:
        return sum(jnp.dot(t, b.astype(BF16), preferred_element_type=F32) for t in _split3(a))
    return sum(jnp.dot(a.astype(BF16), t, preferred_element_type=F32) for t in _split3(b))


def _dot_nt_hi(a, b):
    return sum(_dot_nt(t, b) for t in _split3(a))


def _sigmoid(x):
    return 1.0 / (1.0 + jnp.exp(-x))


def _sigmoid_gate(x):
    return pl.reciprocal(1.0 + jnp.exp(-x), approx=True)


def _silu(x):
    return x * _sigmoid_gate(x)


def _dsilu(x):
    s = _sigmoid_gate(x)
    return s * (1.0 + x * (1.0 - s))


def _softplus(x):
    return jnp.maximum(x, 0.0) + jnp.log(1.0 + jnp.exp(-jnp.abs(x)))


def _rms_fwd(x, w):
    r = lax.rsqrt(jnp.mean(x * x, axis=1, keepdims=True) + EPS)
    return x * r * w


def _rms_bwd(dy, x, w):
    r = lax.rsqrt(jnp.mean(x * x, axis=1, keepdims=True) + EPS)
    xh = x * r
    g = dy * w
    dx = r * (g - xh * jnp.mean(g * xh, axis=1, keepdims=True))
    return dx, jnp.sum(dy * xh, axis=0, keepdims=True)


def _iota2(shape, dim):
    return lax.broadcasted_iota(jnp.int32, shape, dim)


def _tril(n):
    return (_iota2((n, n), 0) >= _iota2((n, n), 1)).astype(F32)


def _triu(n):
    return (_iota2((n, n), 0) <= _iota2((n, n), 1)).astype(F32)


def _norm_mm(x, nw, w, *, tm, tn, name, emit_h=False, out_dtype=F32):
    L, K = x.shape
    N = w.shape[0]
    tm, tn = min(tm, L), min(tn, N)
    ni, nj = L // tm, N // tn

    def body(x_ref, nw_ref, w_ref, *rest):
        if emit_h:
            o_ref, h_ref, hs_ref = rest
        else:
            o_ref, hs_ref = rest

        @pl.when(pl.program_id(1) == 0)
        def _():
            h = _rms_fwd(x_ref[...], nw_ref[...]).astype(BF16)
            hs_ref[...] = h
            if emit_h:
                h_ref[...] = h

        o_ref[...] = _dot_nt(hs_ref[...], w_ref[...]).astype(out_dtype)

    out_shape = [jax.ShapeDtypeStruct((L, N), out_dtype)]
    out_specs = [pl.BlockSpec((tm, tn), lambda i, j: (i, j))]
    if emit_h:
        out_shape.append(jax.ShapeDtypeStruct((L, K), BF16))
        out_specs.append(pl.BlockSpec((tm, K), lambda i, j: (i, 0)))
    res = pl.pallas_call(
        body, name=name, grid=(ni, nj),
        in_specs=[pl.BlockSpec((tm, K), lambda i, j: (i, 0)), pl.BlockSpec((1, K), lambda i, j: (0, 0)),
                  pl.BlockSpec((tn, K), lambda i, j: (j, 0))],
        out_specs=out_specs, out_shape=out_shape, scratch_shapes=[pltpu.VMEM((tm, K), BF16)],
        compiler_params=_cparams(("parallel", "arbitrary")),
    )(x, nw, w)
    return res if len(res) > 1 else res[0]


def _inproj(x, nw, w_a, w_b, w_c, *, tm, gather=None):
    L, K = x.shape
    tm = min(tm, L)
    ni = L // tm

    def body(x_ref, nw_ref, wa_ref, wb_ref, wc_ref, pa_ref, hf_ref, z_ref, dt_ref, pc_ref, h_ref):
        h = _rms_fwd(x_ref[...], nw_ref[...]).astype(BF16)
        h_ref[...] = h
        pa_ref[...] = _dot_nt(h, wa_ref[0:3 * D, :]).astype(BF16)
        hf_ref[...] = _dot_nt(h, wa_ref[3 * D:4 * D, :])
        pb = _dot_nt(h, wb_ref[...])
        z_ref[...] = pb[:, 0:D].astype(BF16)
        dt_ref[...] = pb[:, D:D + 128]
        pc_ref[...] = _dot_nt(h, wc_ref[...]).astype(BF16)

    row = lambda n: pl.BlockSpec((tm, n), lambda i: (i, 0))
    full = lambda a: pl.BlockSpec(a.shape, lambda i: (0, 0))
    at = lambda i: lambda: pl.program_id(0) == i
    payload = None if gather is None else (_Gather, gather)
    phases = [(at(0), _Gather.start, True), (at(ni // 2), _Gather.forward, True), (at(ni - 1), _Gather.finish, False)]
    in_specs, out_specs, out_shape, scratch = _carried_specs(
        payload, [row(K), full(nw), full(w_a), full(w_b), full(w_c)],
        [row(3 * D), row(D), row(D), row(128), row(WC), row(K)],
        [jax.ShapeDtypeStruct((L, 3 * D), BF16), jax.ShapeDtypeStruct((L, D), F32),
         jax.ShapeDtypeStruct((L, D), BF16), jax.ShapeDtypeStruct((L, 128), F32),
         jax.ShapeDtypeStruct((L, WC), BF16), jax.ShapeDtypeStruct((L, K), BF16)], [])
    args = [x, nw, w_a, w_b, w_c]
    return pl.pallas_call(
        _carried(body, 5, 6, payload, phases), name="inproj", grid=(ni,),
        in_specs=in_specs, out_specs=out_specs, out_shape=out_shape, scratch_shapes=scratch,
        compiler_params=_cparams(("arbitrary",), vmem_mb=58),
    )(*(args if gather is None else args + [gather]))


def _mm2_res(res, a1, a2, w, *, tm, name):
    L, N = res.shape
    K = a1.shape[1]
    tm = min(tm, L)

    def body(r_ref, a1_ref, a2_ref, w_ref, o_ref):
        acc = jnp.dot(a1_ref[...], w_ref[0:K, :], preferred_element_type=F32)
        acc += jnp.dot(a2_ref[...], w_ref[K:2 * K, :], preferred_element_type=F32)
        o_ref[...] = r_ref[...] + acc

    return pl.pallas_call(
        body, name=name, grid=(L // tm,),
        in_specs=[pl.BlockSpec((tm, N), lambda i: (i, 0)), pl.BlockSpec((tm, K), lambda i: (i, 0)),
                  pl.BlockSpec((tm, K), lambda i: (i, 0)), pl.BlockSpec((2 * K, N), lambda i: (0, 0))],
        out_specs=pl.BlockSpec((tm, N), lambda i: (i, 0)),
        out_shape=jax.ShapeDtypeStruct((L, N), F32),
        compiler_params=_cparams(("parallel",)),
    )(res, a1, a2, w)


def _dw(a, b, *, tM, tN, tl, name):
    L, M = a.shape
    N = b.shape[1]
    tM, tN, tl = min(tM, M), min(tN, N), min(tl, L)

    def body(a_ref, b_ref, o_ref):
        @pl.when(pl.program_id(2) == 0)
        def _():
            o_ref[...] = jnp.zeros_like(o_ref)

        o_ref[...] += _dot_tn(a_ref[...], b_ref[...])

    return pl.pallas_call(
        body, name=name, grid=(M // tM, N // tN, L // tl),
        in_specs=[pl.BlockSpec((tl, tM), lambda i, j, l: (l, i)), pl.BlockSpec((tl, tN), lambda i, j, l: (l, j))],
        out_specs=pl.BlockSpec((tM, tN), lambda i, j, l: (i, j)),
        out_shape=jax.ShapeDtypeStruct((M, N), F32),
        compiler_params=_cparams(("parallel", "parallel", "arbitrary")),
    )(a, b)


def _first_last(n):
    return [(lambda: pl.program_id(0) == 0, _Exchange.start, True),
            (lambda: pl.program_id(0) == n - 1, _Exchange.finish, False)]


def _mm_normbwd(a_list, w_list, x, nw, res, *, tm, name, exchange=None):
    L, Dm = x.shape
    tm = min(tm, L)
    n = len(a_list)
    has_res = res is not None
    nt = L // tm

    def body(*refs):
        a_refs, w_refs = refs[:n], refs[n:2 * n]
        x_ref, nw_ref = refs[2 * n], refs[2 * n + 1]
        k = 2 * n + 2
        r_ref = refs[k] if has_res else None
        dx_ref, dnw_ref, dh_ref = refs[k + has_res], refs[k + has_res + 1], refs[k + has_res + 2]

        @pl.when(pl.program_id(0) == 0)
        def _():
            dnw_ref[...] = jnp.zeros_like(dnw_ref)
            dh_ref[...] = jnp.zeros_like(dh_ref)

        dx, dnw = _rms_bwd(dh_ref[...], x_ref[...], nw_ref[...])
        dx_ref[...] = dx + r_ref[...] if has_res else dx
        dnw_ref[...] += dnw
        dh = _dot(a_refs[0][...], w_refs[0][...])
        for a_ref, w_ref in zip(a_refs[1:], w_refs[1:]):
            dh += _dot(a_ref[...], w_ref[...])
        dh_ref[...] = dh

    ahead = lambda width: pl.BlockSpec((tm, width), lambda i: (jnp.minimum(i, nt - 1), 0))
    behind = pl.BlockSpec((tm, Dm), lambda i: (jnp.maximum(i - 1, 0), 0))
    in_specs = [ahead(a.shape[1]) for a in a_list]
    in_specs += [pl.BlockSpec(w.shape, lambda i: (0, 0)) for w in w_list]
    in_specs += [behind, pl.BlockSpec((1, Dm), lambda i: (0, 0))]
    args = [*a_list, *w_list, x, nw]
    if has_res:
        in_specs.append(behind)
        args.append(res)
    payload = None if exchange is None else (_Exchange, exchange)
    n_in = len(args)
    if exchange is not None:
        args.append(exchange)
    in_specs, out_specs, out_shape, scratch = _carried_specs(
        payload, in_specs, [behind, pl.BlockSpec((1, Dm), lambda i: (0, 0))],
        [jax.ShapeDtypeStruct((L, Dm), F32), jax.ShapeDtypeStruct((1, Dm), F32)], [pltpu.VMEM((tm, Dm), F32)])
    return pl.pallas_call(
        _carried(body, n_in, 2, payload, _first_last(nt + 1)), name=name, grid=(nt + 1,), in_specs=in_specs,
        out_specs=out_specs, out_shape=out_shape, scratch_shapes=scratch,
        compiler_params=_cparams(("arbitrary",), vmem_mb=56),
    )(*args)


CONV_TN = 512


HALO = 16


def _conv_pre(cat, w_ref, b_ref, rows):
    shifted = [pltpu.roll(cat, 3 - k, 0)[HALO:HALO + rows] for k in range(3)] + [cat[HALO:HALO + rows]]
    pre = b_ref[...] + w_ref[3:4, :] * shifted[3]
    for k in range(3):
        pre += w_ref[k:k + 1, :] * shifted[k]
    return pre, shifted


def _conv_fwd(pc, cw, cb, *, tm):
    L, C = pc.shape
    tm = min(tm, L)
    tn = CONV_TN

    def body(u_ref, halo_ref, w_ref, b_ref, o_ref):
        halo = jnp.where(pl.program_id(1) > 0, halo_ref[...].astype(F32), 0.0)
        cat = jnp.concatenate([halo, u_ref[...].astype(F32)], axis=0)
        pre, _ = _conv_pre(cat, w_ref, b_ref, tm)
        o_ref[...] = _silu(pre).astype(BF16)

    return pl.pallas_call(
        body, name="conv_fwd", grid=(C // tn, L // tm),
        in_specs=[pl.BlockSpec((tm, tn), lambda j, i: (i, j)),
                  pl.BlockSpec((HALO, tn), lambda j, i: (jnp.maximum(i * (tm // HALO) - 1, 0), j)),
                  pl.BlockSpec((4, tn), lambda j, i: (0, j)), pl.BlockSpec((1, tn), lambda j, i: (0, j))],
        out_specs=pl.BlockSpec((tm, tn), lambda j, i: (i, j)),
        out_shape=jax.ShapeDtypeStruct((L, C), BF16),
        compiler_params=_cparams(("parallel", "parallel")),
    )(pc, pc, cw, cb)


def _conv_bwd(pc, dact, cw, cb, *, tm):
    L, C = pc.shape
    tm = min(tm, L)
    tn = CONV_TN
    nt = L // tm

    def body(u_ref, halo_ref, unext_ref, da_ref, danext_ref, w_ref, b_ref, du_ref, dw_ref, db_ref):
        i = pl.program_id(1)
        halo = jnp.where(i > 0, halo_ref[...].astype(F32), 0.0)
        cat = jnp.concatenate([halo, u_ref[...].astype(F32), unext_ref[...].astype(F32)], axis=0)
        pre, shifted = _conv_pre(cat, w_ref, b_ref, tm + HALO)
        da = jnp.concatenate([da_ref[...].astype(F32),
                              jnp.where(i < nt - 1, danext_ref[...].astype(F32), 0.0)], axis=0)
        dpre = da * _dsilu(pre)
        du = w_ref[3:4, :] * dpre[0:tm]
        for k in range(3):
            du += w_ref[k:k + 1, :] * pltpu.roll(dpre, tm + HALO - (3 - k), 0)[0:tm]
        du_ref[...] = du.astype(BF16)

        @pl.when(i == 0)
        def _():
            dw_ref[...] = jnp.zeros_like(dw_ref)
            db_ref[...] = jnp.zeros_like(db_ref)

        dp = dpre[0:tm]
        dw_ref[...] += jnp.concatenate(
            [jnp.sum(dp * shifted[k][0:tm], axis=0, keepdims=True) for k in range(4)], axis=0)
        db_ref[...] += jnp.sum(dp, axis=0, keepdims=True)

    nb = L // HALO
    return pl.pallas_call(
        body, name="conv_bwd", grid=(C // tn, nt),
        in_specs=[pl.BlockSpec((tm, tn), lambda j, i: (i, j)),
                  pl.BlockSpec((HALO, tn), lambda j, i: (jnp.maximum(i * (tm // HALO) - 1, 0), j)),
                  pl.BlockSpec((HALO, tn), lambda j, i: (jnp.minimum((i + 1) * (tm // HALO), nb - 1), j)),
                  pl.BlockSpec((tm, tn), lambda j, i: (i, j)),
                  pl.BlockSpec((HALO, tn), lambda j, i: (jnp.minimum((i + 1) * (tm // HALO), nb - 1), j)),
                  pl.BlockSpec((4, tn), lambda j, i: (0, j)), pl.BlockSpec((1, tn), lambda j, i: (0, j))],
        out_specs=[pl.BlockSpec((tm, tn), lambda j, i: (i, j)), pl.BlockSpec((4, tn), lambda j, i: (0, j)),
                   pl.BlockSpec((1, tn), lambda j, i: (0, j))],
        out_shape=[jax.ShapeDtypeStruct((L, C), BF16), jax.ShapeDtypeStruct((4, C), F32),
                   jax.ShapeDtypeStruct((1, C), F32)],
        compiler_params=_cparams(("parallel", "arbitrary")),
    )(pc, pc, pc, dact, dact, cw, cb)


def _ssd_common(dtr_ref, dtb_ref, avec_ref, aexp_ref, e_ref, acx_ref, acol_ref, arow_ref):
    q = SSD_Q
    tril = _tril(q)
    dtpre = dtr_ref[...] + dtb_ref[...]
    dt = _softplus(dtpre)
    dtx = _dot_hi(dt, e_ref[...])
    acx_ref[...] = _dot_hi(tril, dtx * aexp_ref[...], 1)
    acol = _dot_hi(tril, dt * avec_ref[...], 1)
    acol_ref[...] = acol
    arow_ref[...] = acol.T
    return dtpre, dt, dtx


def _ssd_fwd(xbc, pz, pdt, dtb, avec, aexp, dexp, nw, emat):
    L = xbc.shape[0]
    q = SSD_Q
    nc = L // q

    def body(xbc_ref, z_ref, dtr_ref, dtb_ref, avec_ref, aexp_ref, dexp_ref, nw_ref, e_ref,
             ya_ref, ypre_ref, sin_ref, st_ref, acx_ref, acol_ref, arow_ref, xdt_ref, y_ref):
        @pl.when(pl.program_id(0) == 0)
        def _():
            st_ref[...] = jnp.zeros_like(st_ref)

        sin_ref[...] = st_ref[...].astype(BF16)
        _, _, dtx = _ssd_common(dtr_ref, dtb_ref, avec_ref, aexp_ref, e_ref, acx_ref, acol_ref, arow_ref)
        xs = xbc_ref[:, 0:D].astype(F32)
        xdt = xs * dtx
        xdt_ref[...] = xdt
        acx = acx_ref[...]
        alast = acx_ref[q - 1:q, :]
        xdtd = xdt * jnp.exp(alast - acx)
        eac = jnp.exp(acx)
        ealast = jnp.exp(alast)
        causal = _iota2((q, q), 0) >= _iota2((q, q), 1)
        for g in range(2):
            gs = slice(512 * g, 512 * g + 512)
            bm = xbc_ref[:, D + 128 * g:D + 128 * g + 128]
            cm = xbc_ref[:, D + 256 + 128 * g:D + 256 + 128 * g + 128]
            stg = st_ref[:, gs]
            yoff = _dot(cm, stg) * eac[:, gs]
            gmat = _dot_nt(cm, bm)
            for e in range(8):
                h = 8 * g + e
                hs = slice(64 * h, 64 * h + 64)
                col = acol_ref[:, h:h + 1]
                row = arow_ref[h:h + 1, :]
                lm = jnp.exp(jnp.where(causal, col - row, -1e30))
                y_ref[:, hs] = _dot(gmat * lm, xdt_ref[:, hs])
            y_ref[:, gs] += yoff + dexp_ref[:, gs] * xs[:, gs]
            st_ref[:, gs] = stg * ealast[:, gs] + _dot_tn(bm, xdtd[:, gs])
        ypre_ref[...] = y_ref[...].astype(BF16)
        for g in range(2):
            gs = slice(512 * g, 512 * g + 512)
            yz = y_ref[:, gs] * _silu(z_ref[:, gs].astype(F32))
            ya_ref[:, gs] = _rms_fwd(yz, nw_ref[:, gs]).astype(BF16)

    vec = lambda n: pl.BlockSpec((1, n), lambda c: (0, 0))
    return pl.pallas_call(
        body, name="ssd_fwd", grid=(nc,),
        in_specs=[pl.BlockSpec((q, 1536), lambda c: (c, 0)), pl.BlockSpec((q, D), lambda c: (c, 0)),
                  pl.BlockSpec((q, 128), lambda c: (c, 0)), vec(128), vec(128), vec(D), vec(D), vec(D),
                  pl.BlockSpec((128, D), lambda c: (0, 0))],
        out_specs=[pl.BlockSpec((q, D), lambda c: (c, 0)), pl.BlockSpec((q, D), lambda c: (c, 0)),
                   pl.BlockSpec((128, D), lambda c: (c, 0))],
        out_shape=[jax.ShapeDtypeStruct((L, D), BF16), jax.ShapeDtypeStruct((L, D), BF16),
                   jax.ShapeDtypeStruct((nc * 128, D), BF16)],
        scratch_shapes=[pltpu.VMEM((128, D), F32), pltpu.VMEM((q, D), F32), pltpu.VMEM((q, 128), F32),
                        pltpu.VMEM((128, q), F32), pltpu.VMEM((q, D), F32), pltpu.VMEM((q, D), F32)],
        compiler_params=_cparams(("arbitrary",)),
    )(xbc, pz, pdt, dtb, avec, aexp, dexp, nw, emat)


def _ssd_bwd(xbc, pz, pdt, ypre, sin, dya, dtb, avec, aexp, dexp, nw, emat, exchange=None):
    L = xbc.shape[0]
    q = SSD_Q
    nc = L // q

    def body(xbc_ref, z_ref, dtr_ref, ypre_ref, sin_ref, dya_ref, dtb_ref, avec_ref, aexp_ref, dexp_ref, nw_ref,
             e_ref, dpb_ref, dxbc_ref, dnw_ref, ddtb_ref, da_ref, ddx_ref,
             dst_ref, acx_ref, acol_ref, arow_ref, xdt_ref, dxdt_ref, dy_ref, dacx_ref):
        @pl.when(pl.program_id(0) == 0)
        def _():
            dst_ref[...] = jnp.zeros_like(dst_ref)
            dnw_ref[...] = jnp.zeros_like(dnw_ref)
            ddtb_ref[...] = jnp.zeros_like(ddtb_ref)
            da_ref[...] = jnp.zeros_like(da_ref)
            ddx_ref[...] = jnp.zeros_like(ddx_ref)

        dtpre, dt, dtx = _ssd_common(dtr_ref, dtb_ref, avec_ref, aexp_ref, e_ref, acx_ref, acol_ref, arow_ref)
        xs = xbc_ref[:, 0:D].astype(F32)
        xdt = xs * dtx
        xdt_ref[...] = xdt
        acx = acx_ref[...]
        alast = acx_ref[q - 1:q, :]
        dec_end = jnp.exp(alast - acx)
        xdtd = xdt * dec_end
        eac = jnp.exp(acx)
        ealast = jnp.exp(alast)
        for g in range(2):
            gs = slice(512 * g, 512 * g + 512)
            y = ypre_ref[:, gs].astype(F32)
            z = z_ref[:, gs].astype(F32)
            sz = _silu(z)
            dyz, dnw = _rms_bwd(dya_ref[:, gs].astype(F32), y * sz, nw_ref[:, gs])
            dnw_ref[:, gs] += dnw
            dy_ref[:, gs] = dyz * sz
            dpb_ref[:, gs] = (dyz * y * _dsilu(z)).astype(BF16)
        dy = dy_ref[...]
        ddx_ref[...] += jnp.sum(dy * xs, axis=0, keepdims=True)
        ri = _iota2((q, q), 0)
        ci = _iota2((q, q), 1)
        causal = ri >= ci
        causal_t = ri <= ci
        dacol = jnp.zeros((q, 128), F32)
        dacol_t = jnp.zeros((128, q), F32)
        last_row = _iota2((q, 512), 0) == q - 1
        for g in range(2):
            gs = slice(512 * g, 512 * g + 512)
            bm = xbc_ref[:, D + 128 * g:D + 128 * g + 128]
            cm = xbc_ref[:, D + 256 + 128 * g:D + 256 + 128 * g + 128]
            stg = sin_ref[:, gs].astype(F32)
            dstg = dst_ref[:, gs]
            dyg = dy[:, gs]
            yoff = _dot(cm, stg) * eac[:, gs]
            dwm = dyg * eac[:, gs]
            dcm = _dot_nt(dwm, stg)
            dstin = _dot_tn(cm, dwm)
            dacx_g = dyg * yoff
            dxdtd = _dot(bm, dstg)
            dbm = _dot_nt(xdtd[:, gs], dstg)
            t = dxdtd * xdtd[:, gs]
            dacx_g -= t
            dalast = jnp.sum(t, axis=0, keepdims=True) + jnp.sum(dstg * stg, axis=0, keepdims=True) * ealast[:, gs]
            dst_ref[:, gs] = dstin + dstg * ealast[:, gs]
            dacx_ref[:, gs] = dacx_g + jnp.where(last_row, dalast, 0.0)
            gmat = _dot_nt(cm, bm)
            gmat_t = _dot_nt(bm, cm)
            dg = jnp.zeros((q, q), F32)
            for e in range(8):
                h = 8 * g + e
                hs = slice(64 * h, 64 * h + 64)
                col = acol_ref[:, h:h + 1]
                row = arow_ref[h:h + 1, :]
                lm = jnp.exp(jnp.where(causal, col - row, -1e30))
                lm_t = jnp.exp(jnp.where(causal_t, row - col, -1e30))
                dyh = dy_ref[:, hs]
                dm = _dot_nt(dyh, xdt_ref[:, hs])
                dxdt_ref[:, hs] = _dot(gmat_t * lm_t, dyh)
                dml = dm * lm
                dg += dml
                p = dml * gmat
                dacol += jnp.where(ci == h, jnp.sum(p, axis=1, keepdims=True), 0.0)
                dacol_t -= jnp.where(ri == h, jnp.sum(p, axis=0, keepdims=True), 0.0)
            dcm += _dot(dg, bm)
            dbm += _dot_tn(dg, cm)
            dxbc_ref[:, D + 128 * g:D + 128 * g + 128] = dbm.astype(BF16)
            dxbc_ref[:, D + 256 + 128 * g:D + 256 + 128 * g + 128] = dcm.astype(BF16)
            dxdt_ref[:, gs] += dxdtd * dec_end[:, gs]
        dxdt = dxdt_ref[...]
        dacum = dacol + dacol_t.T + _dot_nt_hi(dacx_ref[...], e_ref[...])
        da = _dot_hi(_triu(q), dacum, 1)
        ddt = da * avec_ref[...] + _dot_nt_hi(dxdt * xs, e_ref[...])
        da_ref[...] += jnp.sum(da * dt, axis=0, keepdims=True) * avec_ref[...]
        dxbc_ref[:, 0:D] = (dexp_ref[...] * dy + dxdt * dtx).astype(BF16)
        ddtr = ddt * _sigmoid(dtpre)
        ddtb_ref[...] += jnp.sum(ddtr, axis=0, keepdims=True)
        dpb_ref[:, D:D + 128] = ddtr.astype(BF16)

    rev = lambda c: nc - 1 - c
    vec = lambda n: pl.BlockSpec((1, n), lambda c: (0, 0))
    payload = None if exchange is None else (_Exchange, exchange)
    args = [xbc, pz, pdt, ypre, sin, dya, dtb, avec, aexp, dexp, nw, emat]
    in_specs, out_specs, out_shape, scratch = _carried_specs(
        payload,
        [pl.BlockSpec((q, 1536), lambda c: (rev(c), 0)), pl.BlockSpec((q, D), lambda c: (rev(c), 0)),
         pl.BlockSpec((q, 128), lambda c: (rev(c), 0)), pl.BlockSpec((q, D), lambda c: (rev(c), 0)),
         pl.BlockSpec((128, D), lambda c: (rev(c), 0)), pl.BlockSpec((q, D), lambda c: (rev(c), 0)),
         vec(128), vec(128), vec(D), vec(D), vec(D), pl.BlockSpec((128, D), lambda c: (0, 0))],
        [pl.BlockSpec((q, WB), lambda c: (rev(c), 0)), pl.BlockSpec((q, 1536), lambda c: (rev(c), 0)),
         vec(D), vec(128), vec(128), vec(D)],
        [jax.ShapeDtypeStruct((L, WB), BF16), jax.ShapeDtypeStruct((L, 1536), BF16),
         jax.ShapeDtypeStruct((1, D), F32), jax.ShapeDtypeStruct((1, 128), F32),
         jax.ShapeDtypeStruct((1, 128), F32), jax.ShapeDtypeStruct((1, D), F32)],
        [pltpu.VMEM((128, D), F32), pltpu.VMEM((q, D), F32), pltpu.VMEM((q, 128), F32),
         pltpu.VMEM((128, q), F32), pltpu.VMEM((q, D), F32), pltpu.VMEM((q, D), F32),
         pltpu.VMEM((q, D), F32), pltpu.VMEM((q, D), F32)])
    return pl.pallas_call(
        _carried(body, len(args), 6, payload, _first_last(nc)), name="ssd_bwd", grid=(nc,),
        in_specs=in_specs, out_specs=out_specs, out_shape=out_shape, scratch_shapes=scratch,
        compiler_params=_cparams(("arbitrary",)),
    )(*(args if exchange is None else args + [exchange]))


def _hg_gates(hq, hf, hgl_ref, b_ref):
    lb = 1.0 / (1.0 + jnp.exp(hgl_ref[1:2, :] - hgl_ref[0:1, :]))
    qf = _silu(hq)
    sg = _sigmoid(hf)
    f = lb + (1.0 - lb) * sg
    b_ref[...] = _dot_hi(_tril(HG_STEP), jnp.log(f), 1)
    return lb, qf, sg, f


def _hg_factors(qf, kf, b_ref):
    s, n = HG_SUB, HG_STEP
    b = b_ref[...]
    blast = b_ref[n - 1:n, :]
    m0, mb, m1 = b_ref[s // 2 - 1:s // 2, :], b_ref[s - 1:s, :], b_ref[s + s // 2 - 1:s + s // 2, :]
    b0, b1 = b[0:s], b[s:n]
    q0, q1, k0, k1 = qf[0:s], qf[s:n], kf[0:s], kf[s:n]
    fac = dict(
        eb=jnp.exp(blast), eq=jnp.exp(b), ek=jnp.exp(blast - b),
        eq0=jnp.exp(b0 - m0), ek0=jnp.exp(m0 - b0), eq1=jnp.exp(b1 - m1), ek1=jnp.exp(m1 - b1),
        eqb=jnp.exp(b1 - mb), ekb=jnp.exp(mb - b0))
    rd = lambda t: t.astype(BF16).astype(F32)
    val = dict(qe=qf * fac["eq"], ke=kf * fac["ek"], qm0=rd(q0 * fac["eq0"]), km0=rd(k0 * fac["ek0"]),
               qm1=rd(q1 * fac["eq1"]), km1=rd(k1 * fac["ek1"]), qb=rd(q1 * fac["eqb"]), kb=rd(k0 * fac["ekb"]))
    return fac, val


def _hgrn_fwd(pa, phf, hgl, nwx):
    L = pa.shape[0]
    n, s = HG_STEP, HG_SUB
    nc = L // n

    def body(hq_ref, hf_ref, hi_ref, hg_ref, hgl_ref, nw_ref, ob_ref, opre_ref, sin_ref, st_ref, b_ref):
        @pl.when(pl.program_id(0) == 0)
        def _():
            st_ref[...] = jnp.zeros_like(st_ref)

        sin_ref[...] = st_ref[...].astype(BF16)
        _, qf, _, f = _hg_gates(hq_ref[...].astype(F32), hf_ref[...], hgl_ref, b_ref)
        fac, val = _hg_factors(qf, 1.0 - f, b_ref)
        causal = _iota2((s, s), 0) >= _iota2((s, s), 1)
        for h in range(HG_HEADS):
            hs = slice(128 * h, 128 * h + 128)
            sth = st_ref[:, hs]
            v = hi_ref[:, hs]
            v0, v1 = v[0:s], v[s:n]
            a00 = jnp.where(causal, _dot_nt(val["qm0"][:, hs], val["km0"][:, hs]), 0.0)
            a11 = jnp.where(causal, _dot_nt(val["qm1"][:, hs], val["km1"][:, hs]), 0.0)
            a10 = _dot_nt(val["qb"][:, hs], val["kb"][:, hs])
            o = _dot_nt(val["qe"][:, hs], sth) + jnp.concatenate(
                [_dot(a00, v0), _dot(a10, v0) + _dot(a11, v1)], axis=0)
            st_ref[:, hs] = sth * fac["eb"][:, hs] + _dot_tn(v, val["ke"][:, hs])
            opre_ref[:, hs] = o.astype(BF16)
            ob_ref[:, hs] = (_rms_fwd(o, nw_ref[:, hs]) * _silu(hg_ref[:, hs].astype(F32))).astype(BF16)

    blk = lambda j: pl.BlockSpec((n, D), lambda c: (c, j))
    return pl.pallas_call(
        body, name="hgrn_fwd", grid=(nc,),
        in_specs=[blk(0), blk(0), blk(1), blk(2), pl.BlockSpec((2, D), lambda c: (0, 0)),
                  pl.BlockSpec((1, D), lambda c: (0, 0))],
        out_specs=[blk(0), blk(0), blk(0)],
        out_shape=[jax.ShapeDtypeStruct((L, D), BF16), jax.ShapeDtypeStruct((L, D), BF16),
                   jax.ShapeDtypeStruct((nc * 128, D), BF16)],
        scratch_shapes=[pltpu.VMEM((128, D), F32), pltpu.VMEM((n, D), F32)],
        compiler_params=_cparams(("arbitrary",)),
    )(pa, phf, pa, pa, hgl, nwx)


def _hgrn_bwd(pa, phf, opre, sin, dob, hgl, nwx, exchange=None):
    L = pa.shape[0]
    n, s = HG_STEP, HG_SUB
    nc = L // n

    def body(hq_ref, hf_ref, hi_ref, hg_ref, opre_ref, sin_ref, dob_ref, hgl_ref, nw_ref,
             dpa_ref, dhgl_ref, dnw_ref, dst_ref, b_ref, dlb_ref, dq_ref, dk_ref, db_ref):
        i = pl.program_id(0)

        @pl.when(i == 0)
        def _():
            dst_ref[...] = jnp.zeros_like(dst_ref)
            dlb_ref[...] = jnp.zeros_like(dlb_ref)
            dnw_ref[...] = jnp.zeros_like(dnw_ref)

        hq = hq_ref[...].astype(F32)
        lb, qf, sg, f = _hg_gates(hq, hf_ref[...], hgl_ref, b_ref)
        kf = 1.0 - f
        fac, val = _hg_factors(qf, kf, b_ref)
        ri, ci = _iota2((s, s), 0), _iota2((s, s), 1)
        causal, causal_t = ri >= ci, ri <= ci
        last_row = _iota2((n, 128), 0) == n - 1
        for h in range(HG_HEADS):
            hs = slice(128 * h, 128 * h + 128)
            o = opre_ref[:, hs].astype(F32)
            gate = hg_ref[:, hs].astype(F32)
            dout = dob_ref[:, hs].astype(F32)
            sgate = _silu(gate)
            do, dnw = _rms_bwd(dout * sgate, o, nw_ref[:, hs])
            dnw_ref[:, hs] += dnw
            dpa_ref[:, 2 * D + 128 * h:2 * D + 128 * h + 128] = (
                dout * _rms_fwd(o, nw_ref[:, hs]) * _dsilu(gate)).astype(BF16)
            sth = sin_ref[:, hs].astype(F32)
            dsth = dst_ref[:, hs]
            v = hi_ref[:, hs]
            v0, v1 = v[0:s], v[s:n]
            do0, do1 = do[0:s], do[s:n]
            qe, ke = val["qe"][:, hs], val["ke"][:, hs]
            qm0, km0, qm1, km1 = val["qm0"][:, hs], val["km0"][:, hs], val["qm1"][:, hs], val["km1"][:, hs]
            qb, kb = val["qb"][:, hs], val["kb"][:, hs]
            dqe = _dot(do, sth)
            dstin = _dot_tn(do, qe)
            a00t = jnp.where(causal_t, _dot_nt(km0, qm0), 0.0)
            a11t = jnp.where(causal_t, _dot_nt(km1, qm1), 0.0)
            a10t = _dot_nt(kb, qb)
            dat00 = jnp.where(causal, _dot_nt(do0, v0), 0.0)
            dat11 = jnp.where(causal, _dot_nt(do1, v1), 0.0)
            dat10 = _dot_nt(do1, v0)
            dat00t = jnp.where(causal_t, _dot_nt(v0, do0), 0.0)
            dat11t = jnp.where(causal_t, _dot_nt(v1, do1), 0.0)
            dat10t = _dot_nt(v0, do1)
            dv = jnp.concatenate([_dot(a00t, do0) + _dot(a10t, do1), _dot(a11t, do1)], axis=0)
            dqm0, dkm0 = _dot(dat00, km0), _dot(dat00t, qm0)
            dqm1, dkm1 = _dot(dat11, km1), _dot(dat11t, qm1)
            dqb, dkb = _dot(dat10, kb), _dot(dat10t, qb)
            dke = _dot(v, dsth)
            dv += _dot_nt(ke, dsth)
            deb = jnp.sum(dsth * sth, axis=0, keepdims=True)
            dst_ref[:, hs] = dstin + dsth * fac["eb"][:, hs]
            dq = dqe * fac["eq"][:, hs] + jnp.concatenate(
                [dqm0 * fac["eq0"][:, hs], dqm1 * fac["eq1"][:, hs] + dqb * fac["eqb"][:, hs]], axis=0)
            dk = dke * fac["ek"][:, hs] + jnp.concatenate(
                [dkm0 * fac["ek0"][:, hs] + dkb * fac["ekb"][:, hs], dkm1 * fac["ek1"][:, hs]], axis=0)
            tke = dke * ke
            db = dqe * qe - tke + jnp.concatenate(
                [dqm0 * qm0 - dkm0 * km0 - dkb * kb, dqm1 * qm1 - dkm1 * km1 + dqb * qb], axis=0)
            dblast = jnp.sum(tke, axis=0, keepdims=True) + deb * fac["eb"][:, hs]
            db_ref[:, hs] = db + jnp.where(last_row, dblast, 0.0)
            dq_ref[:, hs] = dq
            dk_ref[:, hs] = dk
            dpa_ref[:, D + 128 * h:D + 128 * h + 128] = dv.astype(BF16)
        dg = _dot_hi(_triu(n), db_ref[...], 1)
        df = dg / f - dk_ref[...]
        dpa_ref[:, 3 * D:4 * D] = (df * (1.0 - lb) * sg * (1.0 - sg)).astype(BF16)
        dpa_ref[:, 0:D] = (dq_ref[...] * _dsilu(hq)).astype(BF16)
        dlb_ref[...] += jnp.sum(df * (1.0 - sg), axis=0, keepdims=True)

        @pl.when(i == nc - 1)
        def _():
            d0 = dlb_ref[...] * lb * (1.0 - lb)
            dhgl_ref[...] = jnp.concatenate([d0, -d0], axis=0)

    rev = lambda c: nc - 1 - c
    blk = lambda j: pl.BlockSpec((n, D), lambda c: (rev(c), j))
    payload = None if exchange is None else (_Exchange, exchange)
    args = [pa, phf, pa, pa, opre, sin, dob, hgl, nwx]
    in_specs, out_specs, out_shape, scratch = _carried_specs(
        payload,
        [blk(0), blk(0), blk(1), blk(2), blk(0), blk(0), blk(0), pl.BlockSpec((2, D), lambda c: (0, 0)),
         pl.BlockSpec((1, D), lambda c: (0, 0))],
        [pl.BlockSpec((n, 4 * D), lambda c: (rev(c), 0)), pl.BlockSpec((2, D), lambda c: (0, 0)),
         pl.BlockSpec((1, D), lambda c: (0, 0))],
        [jax.ShapeDtypeStruct((L, 4 * D), BF16), jax.ShapeDtypeStruct((2, D), F32), jax.ShapeDtypeStruct((1, D), F32)],
        [pltpu.VMEM((128, D), F32), pltpu.VMEM((n, D), F32), pltpu.VMEM((1, D), F32),
         pltpu.VMEM((n, D), F32), pltpu.VMEM((n, D), F32), pltpu.VMEM((n, D), F32)])
    return pl.pallas_call(
        _carried(body, len(args), 3, payload, _first_last(nc)), name="hgrn_bwd", grid=(nc,),
        in_specs=in_specs, out_specs=out_specs, out_shape=out_shape, scratch_shapes=scratch,
        compiler_params=_cparams(("arbitrary",)),
    )(*(args if exchange is None else args + [exchange]))


XA_SCALE = XA_DH ** -0.5


def _xa_probs(qh, kmh):
    sc = _dot_nt(qh, kmh) * XA_SCALE
    p = jnp.exp(sc - jnp.max(sc, axis=1, keepdims=True))
    return p * (1.0 / jnp.sum(p, axis=1, keepdims=True))


def _xattn_fwd(x1, nw, wq, kv, wo, *, tm):
    L = x1.shape[0]
    tm = min(tm, L)

    def body(x_ref, nw_ref, wq_ref, kv_ref, wo_ref, o_ref, ox_ref):
        x = x_ref[...]
        q = _dot(_rms_fwd(x, nw_ref[...]), wq_ref[...])
        for h in range(XA_HEADS):
            hs = slice(XA_DH * h, XA_DH * h + XA_DH)
            p = _xa_probs(q[:, hs], kv_ref[:, hs])
            ox_ref[:, hs] = _dot(p, kv_ref[:, D + XA_DH * h:D + XA_DH * h + XA_DH])
        o_ref[...] = x + _dot(ox_ref[...], wo_ref[...])

    full = lambda a: pl.BlockSpec(a.shape, lambda i: (0, 0))
    return pl.pallas_call(
        body, name="xattn_fwd", grid=(L // tm,),
        in_specs=[pl.BlockSpec((tm, D), lambda i: (i, 0)), full(nw), full(wq), full(kv), full(wo)],
        out_specs=pl.BlockSpec((tm, D), lambda i: (i, 0)),
        out_shape=jax.ShapeDtypeStruct((L, D), F32),
        scratch_shapes=[pltpu.VMEM((tm, D), F32)],
        compiler_params=_cparams(("parallel",)),
    )(x1, nw, wq, kv, wo)


def _xattn_bwd(x1, dx2, nw, wq, kv, wo, w_out, *, tm):
    L = x1.shape[0]
    tm = min(tm, L)

    def body(x_ref, dx2_ref, nw_ref, wq_ref, kv_ref, wo_ref, wout_ref, dx1_ref, dx1b_ref, h_ref, dq_ref, ox_ref,
             dya_ref, dob_ref, dkv_ref, dnw_ref, dqs_ref):
        @pl.when(pl.program_id(0) == 0)
        def _():
            dkv_ref[...] = jnp.zeros_like(dkv_ref)
            dnw_ref[...] = jnp.zeros_like(dnw_ref)

        x = x_ref[...]
        dx2 = dx2_ref[...]
        hn = _rms_fwd(x, nw_ref[...]).astype(BF16)
        h_ref[...] = hn
        q = _dot(hn, wq_ref[...])
        dox = _dot_nt(dx2, wo_ref[...])
        for h in range(XA_HEADS):
            hs = slice(XA_DH * h, XA_DH * h + XA_DH)
            vs = slice(D + XA_DH * h, D + XA_DH * h + XA_DH)
            qh, kmh, vmh, doxh = q[:, hs], kv_ref[:, hs], kv_ref[:, vs], dox[:, hs]
            p = _xa_probs(qh, kmh)
            ox_ref[:, hs] = _dot(p, vmh).astype(BF16)
            dp = _dot_nt(doxh, vmh)
            dkv_ref[:, vs] += _dot_tn(p, doxh)
            ds = p * (dp - jnp.sum(dp * p, axis=1, keepdims=True)) * XA_SCALE
            dqs_ref[:, hs] = _dot(ds, kmh)
            dkv_ref[:, hs] += _dot_tn(ds, qh)
        dq = dqs_ref[...]
        dq_ref[...] = dq.astype(BF16)
        dx, dnw = _rms_bwd(_dot_nt(dq, wq_ref[...]), x, nw_ref[...])
        dx1 = dx2 + dx
        dx1_ref[...] = dx1
        dx1b = dx1.astype(BF16)
        dx1b_ref[...] = dx1b
        dya_ref[...] = _dot_nt(dx1b, wout_ref[0:D, :]).astype(BF16)
        dob_ref[...] = _dot_nt(dx1b, wout_ref[D:2 * D, :]).astype(BF16)
        dnw_ref[...] += dnw

    full = lambda a: pl.BlockSpec(a.shape, lambda i: (0, 0))
    row = pl.BlockSpec((tm, D), lambda i: (i, 0))
    rowb = jax.ShapeDtypeStruct((L, D), BF16)
    return pl.pallas_call(
        body, name="xattn_bwd", grid=(L // tm,),
        in_specs=[row, row, full(nw), full(wq), full(kv), full(wo), full(w_out)],
        out_specs=[row] * 7 + [pl.BlockSpec((MEM_LEN, 2 * D), lambda i: (0, 0)), pl.BlockSpec((1, D), lambda i: (0, 0))],
        out_shape=[jax.ShapeDtypeStruct((L, D), F32)] + [rowb] * 6 + [
            jax.ShapeDtypeStruct((MEM_LEN, 2 * D), F32), jax.ShapeDtypeStruct((1, D), F32)],
        scratch_shapes=[pltpu.VMEM((tm, D), F32)],
        compiler_params=_cparams(("arbitrary",)),
    )(x1, dx2, nw, wq, kv, wo, w_out)


def _ffn_blk(tm, tf):
    return pl.BlockSpec((tm, tf), lambda i, j: (i, j))


def _interleave(wg, wu, tf):
    return jnp.stack([wg.reshape(FFN // tf, tf, D), wu.reshape(FFN // tf, tf, D)], axis=1).reshape(2 * FFN, D)


def _ffn_fwd(x2, nw, wgu, wd, *, tm, tf):
    L = x2.shape[0]
    tm = min(tm, L)
    nf = FFN // tf

    def body(x_ref, nw_ref, wgu_ref, wd_ref, o_ref, hn_ref, g_ref, u_ref, h_ref, acc_ref):
        j = pl.program_id(1)

        @pl.when(j == 0)
        def _():
            hn = _rms_fwd(x_ref[...], nw_ref[...]).astype(BF16)
            h_ref[...] = hn
            hn_ref[...] = hn
            acc_ref[...] = jnp.zeros_like(acc_ref)

        gu = _dot_nt(h_ref[...], wgu_ref[...]).astype(BF16)
        g, u = gu[:, 0:tf], gu[:, tf:2 * tf]
        g_ref[...] = g
        u_ref[...] = u
        acc_ref[...] += _dot(_silu(g.astype(F32)) * u.astype(F32), wd_ref[...])

        @pl.when(j == nf - 1)
        def _():
            o_ref[...] = x_ref[...] + acc_ref[...]

    row = pl.BlockSpec((tm, D), lambda i, j: (i, 0))
    wide = jax.ShapeDtypeStruct((L, FFN), BF16)
    return pl.pallas_call(
        body, name="ffn_fwd", grid=(L // tm, nf),
        in_specs=[row, pl.BlockSpec((1, D), lambda i, j: (0, 0)),
                  pl.BlockSpec((2 * tf, D), lambda i, j: (j, 0)), pl.BlockSpec((tf, D), lambda i, j: (j, 0))],
        out_specs=[row, row, _ffn_blk(tm, tf), _ffn_blk(tm, tf)],
        out_shape=[jax.ShapeDtypeStruct((L, D), F32), jax.ShapeDtypeStruct((L, D), BF16), wide, wide],
        scratch_shapes=[pltpu.VMEM((tm, D), BF16), pltpu.VMEM((tm, D), F32)],
        compiler_params=_cparams(("parallel", "arbitrary"), vmem_mb=58),
    )(x2, nw, wgu, wd)


def _ffn_bwd(x2, dx3, g, u, nw, wgu, wd, *, tm, tf):
    L = x2.shape[0]
    tm = min(tm, L)
    nf = FFN // tf

    def body(x_ref, dx3_ref, g_ref, u_ref, nw_ref, wgu_ref, wd_ref,
             dx2_ref, a_ref, dg_ref, du_ref, dnw_ref, d3_ref, acc_ref, da_ref, dgu_ref):
        i, j = pl.program_id(0), pl.program_id(1)

        @pl.when(j == 0)
        def _():
            d3_ref[...] = dx3_ref[...].astype(BF16)
            acc_ref[...] = jnp.zeros_like(acc_ref)
            da_ref[...] = jnp.zeros_like(da_ref)
            dgu_ref[...] = jnp.zeros_like(dgu_ref)

        @pl.when(jnp.logical_and(i == 0, j == 0))
        def _():
            dnw_ref[...] = jnp.zeros_like(dnw_ref)

        acc_ref[...] += _dot(dgu_ref[...], wgu_ref[...])
        g = g_ref[...].astype(F32)
        u = u_ref[...].astype(F32)
        s = _sigmoid_gate(g)
        sg = g * s
        da = da_ref[...]
        a_ref[...] = (sg * u).astype(BF16)
        dg = (da * u * (s + sg * (1.0 - s))).astype(BF16)
        du = (da * sg).astype(BF16)
        dg_ref[...] = dg
        du_ref[...] = du
        dgu_ref[...] = jnp.concatenate([dg, du], axis=1)
        da_ref[...] = _dot_nt(d3_ref[...], wd_ref[...])

        @pl.when(j == nf + 1)
        def _():
            dx, dnw = _rms_bwd(acc_ref[...], x_ref[...], nw_ref[...])
            dx2_ref[...] = dx3_ref[...] + dx
            dnw_ref[...] += dnw

    clamp = lambda j, d: jnp.clip(j - d, 0, nf - 1)
    row = pl.BlockSpec((tm, D), lambda i, j: (i, 0))
    blk = pl.BlockSpec((tm, tf), lambda i, j: (i, clamp(j, 1)))
    wide = jax.ShapeDtypeStruct((L, FFN), BF16)
    return pl.pallas_call(
        body, name="ffn_bwd", grid=(L // tm, nf + 2),
        in_specs=[row, row, blk, blk, pl.BlockSpec((1, D), lambda i, j: (0, 0)),
                  pl.BlockSpec((2 * tf, D), lambda i, j: (clamp(j, 2), 0)),
                  pl.BlockSpec((tf, D), lambda i, j: (clamp(j, 0), 0))],
        out_specs=[row, blk, blk, blk, pl.BlockSpec((1, D), lambda i, j: (0, 0))],
        out_shape=[jax.ShapeDtypeStruct((L, D), F32), wide, wide, wide, jax.ShapeDtypeStruct((1, D), F32)],
        scratch_shapes=[pltpu.VMEM((tm, D), BF16), pltpu.VMEM((tm, D), F32), pltpu.VMEM((tm, tf), F32),
                        pltpu.VMEM((tm, 2 * tf), BF16)],
        compiler_params=_cparams(("arbitrary", "arbitrary"), vmem_mb=56),
    )(x2, dx3, g, u, nw, wgu, wd)


def _final(x3, tgt, nw, *, tm):
    L = x3.shape[0]
    tm = min(tm, L)

    def body(x_ref, t_ref, nw_ref, dx_ref, dxb_ref, loss_ref, dnw_ref):
        @pl.when(pl.program_id(0) == 0)
        def _():
            loss_ref[...] = jnp.zeros_like(loss_ref)
            dnw_ref[...] = jnp.zeros_like(dnw_ref)

        x = x_ref[...]
        w = nw_ref[...]
        err = _rms_fwd(x, w) - t_ref[...]
        part = 0.5 * jnp.sum(jnp.sum(err * err, axis=1, keepdims=True), axis=0, keepdims=True) * (1.0 / D)
        loss_ref[...] += jnp.where(_iota2((1, 128), 1) == 0, part, 0.0)
        dx, dnw = _rms_bwd(err * (1.0 / D), x, w)
        dx_ref[...] = dx
        dxb_ref[...] = dx.astype(BF16)
        dnw_ref[...] += dnw

    row = pl.BlockSpec((tm, D), lambda i: (i, 0))
    return pl.pallas_call(
        body, name="final_loss", grid=(L // tm,),
        in_specs=[row, row, pl.BlockSpec((1, D), lambda i: (0, 0))],
        out_specs=[row, row, pl.BlockSpec((1, 128), lambda i: (0, 0)), pl.BlockSpec((1, D), lambda i: (0, 0))],
        out_shape=[jax.ShapeDtypeStruct((L, D), F32), jax.ShapeDtypeStruct((L, D), BF16),
                   jax.ShapeDtypeStruct((1, 128), F32), jax.ShapeDtypeStruct((1, D), F32)],
        compiler_params=_cparams(("arbitrary",)),
    )(x3, tgt, nw)


def _adam_update(g, w, m, v):
    c1 = 1.0 / (1.0 - ADAM_B1 ** ADAM_STEP)
    c2 = 1.0 / (1.0 - ADAM_B2 ** ADAM_STEP)
    nm = ADAM_B1 * m + (1.0 - ADAM_B1) * g
    nv = ADAM_B2 * v + (1.0 - ADAM_B2) * (g * g)
    return -ADAM_LR * ((nm * c1) / (jnp.sqrt(nv * c2) + ADAM_EPS) + ADAM_WD * w), nm, nv


def _adamw_small(tot, conv_w_grad, w, m, v):
    names = [n for n, _, _ in SMALL if n != "loss"]
    k = len(names)

    def body(tot_ref, cwg_ref, *refs):
        w_refs, m_refs, v_refs = refs[0:k], refs[k:2 * k], refs[2 * k:3 * k]
        g_refs, d_refs, nm_refs, nv_refs = (refs[(3 + j) * k:(4 + j) * k] for j in range(4))
        for i, n in enumerate(names):
            row, nr, nc = SMALL_AT[n]
            g = cwg_ref[...] if n == "conv_w" else tot_ref[row:row + nr, 0:nc]
            d, nm, nv = _adam_update(g, w_refs[i][...], m_refs[i][...], v_refs[i][...])
            g_refs[i][...] = g
            d_refs[i][...] = d
            nm_refs[i][...] = nm
            nv_refs[i][...] = nv

    sds = [jax.ShapeDtypeStruct(t.shape, F32) for t in w]
    res = pl.pallas_call(body, name="adamw_small", out_shape=sds * 4)(tot, conv_w_grad, *w, *m, *v)
    return res[0:k], res[k:2 * k], res[2 * k:3 * k], res[3 * k:4 * k]


def _adamw(parts, w, m, v, *, tr, name):
    n_parts, R, C = parts.shape
    tr = min(tr, R)

    def body(p_ref, w_ref, m_ref, v_ref, g_ref, d_ref, nm_ref, nv_ref):
        g = p_ref[0].astype(F32)
        for k in range(1, n_parts):
            g = g + p_ref[k].astype(F32)
        d, nm, nv = _adam_update(g, w_ref[...], m_ref[...], v_ref[...])
        g_ref[...] = g
        nm_ref[...] = nm
        nv_ref[...] = nv
        d_ref[...] = d

    blk = pl.BlockSpec((tr, C), lambda i: (i, 0))
    sds = jax.ShapeDtypeStruct((R, C), F32)
    return pl.pallas_call(
        body, name=name, grid=(R // tr,),
        in_specs=[pl.BlockSpec((n_parts, tr, C), lambda i: (0, i, 0)), blk, blk, blk],
        out_specs=[blk, blk, blk, blk], out_shape=[sds, sds, sds, sds],
        compiler_params=_cparams(("parallel",)),
    )(parts, w, m, v)


def _position():
    return lax.axis_index("x"), lax.axis_index("y"), lax.axis_index("c")


def _comm_scratch():
    return [pltpu.SemaphoreType.DMA((7,)), pltpu.SemaphoreType.DMA((7,)), pltpu.SemaphoreType.DMA]


class _Gather:
    def __init__(self, x_ref, out_ref, send_sems, recv_sems, local_sem):
        x, y, c = _position()
        me, sibling = (x, y, c), (x, y, 1 - c)
        chips = [(1 - x, y), (x, 1 - y), (1 - x, 1 - y)]

        def rows(px, py, pc):
            return out_ref.at[4 * px + 2 * py + pc]

        def copy(k, block, to, src=None):
            return pltpu.make_async_remote_copy(
                src_ref=rows(*block) if src is None else src, dst_ref=rows(*block),
                send_sem=send_sems.at[k], recv_sem=recv_sems.at[k], device_id=to, device_id_type=MESH)

        self.mine = pltpu.make_async_copy(x_ref, rows(*me), local_sem)
        self.first = [copy(0, me, sibling, src=x_ref)]
        self.first += [copy(1 + j, me, (*chip, c), src=x_ref) for j, chip in enumerate(chips)]
        self.passed = [copy(4 + j, (*chip, c), sibling) for j, chip in enumerate(chips)]
        self.from_chips = [copy(1 + j, (*chip, c), me) for j, chip in enumerate(chips)]
        self.from_sibling = [copy(0, sibling, me)] + [copy(4 + j, (*chip, 1 - c), me) for j, chip in enumerate(chips)]

    def start(self):
        self.mine.start()
        for cp in self.first:
            cp.start()

    def forward(self):
        for got, cp in zip(self.from_chips, self.passed):
            got.wait_recv()
            cp.start()

    def finish(self):
        for got in self.from_sibling:
            got.wait_recv()
        for cp in self.first + self.passed:
            cp.wait_send()
        self.mine.wait()


class _Exchange:
    def __init__(self, g_ref, out_ref, send_sems, recv_sems, local_sem):
        x, y, c = _position()
        me = 4 * x + 2 * y + c
        self.mine = pltpu.make_async_copy(g_ref.at[me], out_ref.at[me], local_sem)
        self.copies = []
        for k in range(1, N_DEV):
            px = 1 - x if k & 4 else x
            py = 1 - y if k & 2 else y
            pc = 1 - c if k & 1 else c
            self.copies.append(pltpu.make_async_remote_copy(
                src_ref=g_ref.at[4 * px + 2 * py + pc], dst_ref=out_ref.at[me],
                send_sem=send_sems.at[k - 1], recv_sem=recv_sems.at[k - 1],
                device_id=(px, py, pc), device_id_type=MESH))

    def start(self):
        self.mine.start()
        for cp in self.copies:
            cp.start()

    def finish(self):
        for cp in self.copies:
            cp.wait()
        self.mine.wait()


def _allgather(xp):
    R, C = xp.shape

    def body(x_ref, out_ref, send_sems, recv_sems, local_sem):
        g = _Gather(x_ref, out_ref, send_sems, recv_sems, local_sem)
        g.start()
        g.forward()
        g.finish()

    return pl.pallas_call(
        body, name="allgather_w_in",
        out_shape=jax.ShapeDtypeStruct((N_DEV, R, C), xp.dtype),
        in_specs=[pl.BlockSpec(memory_space=pltpu.HBM)], out_specs=pl.BlockSpec(memory_space=pltpu.HBM),
        scratch_shapes=_comm_scratch(),
    )(xp)


def _carried(body, n_in, n_out, payload, phases):
    if payload is None:
        return body
    kind = payload[0]

    def new_body(*refs):
        ins, src_ref = refs[:n_in], refs[n_in]
        outs, dst_ref = refs[n_in + 1:n_in + 1 + n_out], refs[n_in + 1 + n_out]
        scratch, sems = refs[n_in + 2 + n_out:-3], refs[-3:]

        def run(before):
            for when, action, is_before in phases:
                if is_before == before:
                    @pl.when(when())
                    def _():
                        action(kind(src_ref, dst_ref, *sems))

        run(True)
        body(*ins, *outs, *scratch)
        run(False)

    return new_body


def _carried_specs(payload, in_specs, out_specs, out_shape, scratch):
    if payload is None:
        return in_specs, out_specs, out_shape, scratch
    kind, arr = payload
    landing = (N_DEV,) + arr.shape if kind is _Gather else arr.shape
    hbm = pl.BlockSpec(memory_space=pltpu.HBM)
    return (in_specs + [hbm], out_specs + [hbm], out_shape + [jax.ShapeDtypeStruct(landing, arr.dtype)],
            scratch + _comm_scratch())


def _small_allreduce(sp):
    R, C = sp.shape

    def body(s_ref, out_ref, buf_ref, send_sems, recv_sems):
        x, y, c = _position()
        me = 4 * x + 2 * y + c
        buf_ref[me] = s_ref[...]
        copies = []
        for k in range(1, N_DEV):
            px = 1 - x if k & 4 else x
            py = 1 - y if k & 2 else y
            pc = 1 - c if k & 1 else c
            cp = pltpu.make_async_remote_copy(
                src_ref=s_ref, dst_ref=buf_ref.at[me], send_sem=send_sems.at[k - 1], recv_sem=recv_sems.at[k - 1],
                device_id=(px, py, pc), device_id_type=MESH)
            cp.start()
            copies.append(cp)
        for cp in copies:
            cp.wait()
        tot = buf_ref[0]
        for k in range(1, N_DEV):
            tot = tot + buf_ref[k]
        out_ref[...] = tot

    return pl.pallas_call(
        body, name="allreduce_small",
        out_shape=jax.ShapeDtypeStruct((R, C), F32),
        in_specs=[pl.BlockSpec(memory_space=pltpu.VMEM)], out_specs=pl.BlockSpec(memory_space=pltpu.VMEM),
        scratch_shapes=[pltpu.VMEM((N_DEV, R, C), F32), pltpu.SemaphoreType.DMA((7,)), pltpu.SemaphoreType.DMA((7,))],
    )(sp)


def _local_step(x, mem, tgt, wt, small, dist=None):
    w_in = wt["w_in"]
    zpad = jnp.zeros((WB - D - SSD_HEADS, D), BF16)
    w_a = jnp.concatenate([w_in[2576:3600], w_in[4624:6672], w_in[3600:4624]], axis=0)
    w_b = jnp.concatenate([w_in[0:D], w_in[2560:2576], zpad], axis=0)
    w_c = w_in[D:2560]
    a_log, d_skip = small["a_log"], small["d_skip"]
    avec = jnp.pad(-jnp.exp(a_log), ((0, 0), (0, 128 - SSD_HEADS)))
    aexp = jnp.repeat(-jnp.exp(a_log), SSD_P, axis=1)
    dexp = jnp.repeat(d_skip, SSD_P, axis=1)
    dtb = jnp.pad(small["dt_bias"], ((0, 0), (0, 128 - SSD_HEADS)))
    emat = (lax.broadcasted_iota(jnp.int32, (128, D), 0) == lax.broadcasted_iota(jnp.int32, (128, D), 1) // SSD_P
            ).astype(F32)
    hg_nwx = jnp.tile(small["hg_norm_w"], (1, HG_HEADS))
    hgl = small["hg_lower_bounds"]
    nfw = small["norm_final_w"].reshape(1, D)

    received = {}
    pieces = (lambda group, grads: None) if dist is None else dist["pieces"]
    pa, phf, pz, pdt, pc, hn_mix, *got = _inproj(x, small["norm_mix_w"], w_a, w_b, w_c, tm=256,
                                                 gather=None if dist is None else dist["rest_pack"])
    if got:
        wt = {**wt, **dist["unpack_rest"](got[0])}
    xbc = _conv_fwd(pc, small["conv_w"], small["conv_b"], tm=1024)
    ya, ypre, ssd_sin = _ssd_fwd(xbc, pz, pdt, dtb, avec, aexp, dexp, small["ssd_norm_w"], emat)
    ob, opre, hg_sin = _hgrn_fwd(pa, phf, hgl, hg_nwx)
    x1 = _mm2_res(x, ya, ob, wt["w_out"], tm=512, name="outproj")
    kvb, mn = _norm_mm(mem, small["norm_mem_w"], wt["xa_wkv"], tm=256, tn=1024, name="mem_kv", emit_h=True,
                       out_dtype=BF16)
    x2 = _xattn_fwd(x1, small["norm_xa_w"], wt["xa_wq"], kvb, wt["xa_wo"], tm=1024)
    x3, hn_ffn, gate, up = _ffn_fwd(x2, small["norm_ffn_w"], _interleave(wt["ffn_w_gate"], wt["ffn_w_up"], 1408),
                                    wt["ffn_w_down"], tm=512, tf=1408)

    dx3, dx3b, loss, g_nf = _final(x3, tgt, nfw, tm=1024)
    dx2, act, dg, du, g_nffn = _ffn_bwd(x2, dx3, gate, up, small["norm_ffn_w"],
                                        _interleave(wt["ffn_w_gate"], wt["ffn_w_up"], 256), wt["ffn_w_down"],
                                        tm=1024, tf=256)
    g_wg = _dw(dg, hn_ffn, tM=1408, tN=1024, tl=2048, name="dw_gate")
    g_wu = _dw(du, hn_ffn, tM=1408, tN=1024, tl=2048, name="dw_up")
    g_wd = _dw(act, dx3b, tM=1408, tN=1024, tl=2048, name="dw_down")
    dx1, dx1b, hn_xa, dq, ox, dya, dob, dkv, g_nxa = _xattn_bwd(
        x1, dx2, small["norm_xa_w"], wt["xa_wq"], kvb, wt["xa_wo"], wt["w_out"], tm=512)
    dkvb = dkv.astype(BF16)
    g_wq = _dw(hn_xa, dq, tM=1024, tN=1024, tl=2048, name="dw_q")
    g_wo = _dw(ox, dx2, tM=1024, tN=1024, tl=2048, name="dw_o")
    g_wkv = _dw(dkvb, mn, tM=1024, tN=1024, tl=256, name="dw_kv")
    _, g_nmem = _mm_normbwd([dkvb], [wt["xa_wkv"]], mem, small["norm_mem_w"], None, tm=256, name="mem_bwd")
    g_wout = jnp.concatenate([_dw(ya, dx1b, tM=1024, tN=1024, tl=2048, name="dw_out_a"),
                              _dw(ob, dx1b, tM=1024, tN=1024, tl=2048, name="dw_out_b")], axis=0)
    ffn_grads = {"ffn_w_gate": g_wg, "ffn_w_up": g_wu, "ffn_w_down": g_wd}
    mid_grads = {"w_out": g_wout, "xa_wq": g_wq, "xa_wkv": g_wkv, "xa_wo": g_wo}
    dpa, g_hgl, g_hgn_x, *got = _hgrn_bwd(pa, phf, opre, hg_sin, dob, hgl, hg_nwx,
                                          exchange=pieces("ffn", ffn_grads))
    received["ffn"] = got[0] if got else None
    dpb, dxbc, g_ssdn, g_dtb, g_alog, g_dx, *got = _ssd_bwd(
        xbc, pz, pdt, ypre, ssd_sin, dya, dtb, avec, aexp, dexp, small["ssd_norm_w"], emat,
        exchange=pieces("mid", mid_grads))
    received["mid"] = got[0] if got else None
    dpc, g_cw, g_cb = _conv_bwd(pc, dxbc, small["conv_w"], small["conv_b"], tm=1024)
    g_wa = _dw(dpa, hn_mix, tM=1024, tN=1024, tl=2048, name="dw_in_a")
    g_wb = _dw(dpb, hn_mix, tM=384, tN=1024, tl=2048, name="dw_in_b")
    g_wc = _dw(dpc, hn_mix, tM=512, tN=1024, tl=2048, name="dw_in_c")
    g_win = jnp.concatenate([g_wb[0:D], g_wc, g_wb[D:D + SSD_HEADS], g_wa[0:D], g_wa[3 * D:4 * D], g_wa[D:3 * D]],
                            axis=0)
    grad_x, g_nmix, *got = _mm_normbwd([dpa, dpb, dpc], [w_a, w_b, w_c], x, small["norm_mix_w"], dx1, tm=256,
                                         name="inproj_bwd", exchange=pieces("in", {"w_in": g_win}))
    received["in"] = got[0] if got else None

    big = {"w_in": g_win, **mid_grads, **ffn_grads}
    smallg = {
        "norm_mix_w": g_nmix, "conv_w": g_cw, "conv_b": g_cb, "dt_bias": g_dtb[:, 0:SSD_HEADS],
        "a_log": g_alog[:, 0:SSD_HEADS], "d_skip": g_dx.reshape(SSD_HEADS, SSD_P).sum(axis=1).reshape(1, SSD_HEADS),
        "ssd_norm_w": g_ssdn, "hg_lower_bounds": g_hgl,
        "hg_norm_w": g_hgn_x.reshape(HG_HEADS, HG_K).sum(axis=0).reshape(1, HG_K),
        "norm_xa_w": g_nxa, "norm_mem_w": g_nmem, "norm_ffn_w": g_nffn, "norm_final_w": g_nf,
        "loss": loss[:, 0:1]}
    return grad_x, big, smallg, received


COL_SHARDED = ("w_in", "xa_wkv", "ffn_w_gate", "ffn_w_up")


def _pad_rows(t, rows):
    return jnp.pad(t, [(0, 0)] * (t.ndim - 2) + [(0, rows - t.shape[-2]), (0, 0)])


def _group_fill(parts, group, lead, dtype):
    used = sum(p.shape[-2] for p in parts)
    if used < GROUP_ROWS[group]:
        parts.append(jnp.zeros(lead + (GROUP_ROWS[group] - used, D), dtype))
    return parts


def _pack_group(shards, group, dtype, extra=None):
    parts = [_pad_rows(shards[n].astype(dtype).reshape(r, D), _rows_padded(r)) for n, r in GROUPS[group]]
    if extra is not None:
        parts.append(extra)
    return jnp.concatenate(_group_fill(parts, group, (), dtype), axis=0)


def _unpack_group(packed, group, shapes):
    out, off = {}, 0
    for n, r in GROUPS[group]:
        t = packed[off:off + r]
        out[n] = (t.T if n in COL_SHARDED else t).reshape(shapes[n])
        off += _rows_padded(r)
    return out


def _row_shards(d):
    return {n: d[n][0].T if n in COL_SHARDED else d[n][0] for n in BIG}


def _unpack_gathered(gath, groups):
    out, base = {}, 0
    for group in groups:
        off = base
        for n, r in GROUPS[group]:
            out[n] = gath[:, off:off + r].reshape(N_DEV * r, D)
            off += _rows_padded(r)
        base += GROUP_ROWS[group]
    return out


def _grad_pieces(group, grads):
    dtype = PIECE_DTYPE[group]
    parts = [_pad_rows(grads[n].reshape(N_DEV, r, D).astype(dtype), _rows_padded(r)) for n, r in GROUPS[group]]
    return jnp.concatenate(_group_fill(parts, group, (N_DEV,), dtype), axis=1)


def _pack_small(vals):
    tot = None
    for n, r, c in SMALL:
        row = SMALL_AT[n][0]
        part = jnp.pad(vals[n].reshape(r, c), ((row, SMALL_ROWS - row - r), (0, SMALL_COLS - c)))
        tot = part if tot is None else tot + part
    return tot


WEIGHTS = ['norm_mix_w', 'w_in', 'conv_w', 'conv_b', 'dt_bias', 'a_log', 'd_skip', 'ssd_norm_w', 'hg_lower_bounds',
           'hg_norm_w', 'w_out', 'norm_xa_w', 'norm_mem_w', 'xa_wq', 'xa_wkv', 'xa_wo', 'norm_ffn_w', 'ffn_w_gate',
           'ffn_w_up', 'ffn_w_down', 'norm_final_w']
BIG = tuple(n for n, _ in PACK)


def kernel(x, mem, norm_mix_w, w_in, conv_w, conv_b, dt_bias, a_log, d_skip, ssd_norm_w, hg_lower_bounds, hg_norm_w, w_out, norm_xa_w, norm_mem_w, xa_wq, xa_wkv, xa_wo, norm_ffn_w, ffn_w_gate, ffn_w_up, ffn_w_down, norm_final_w, loss_target, m_norm_mix_w, m_w_in, m_conv_w, m_conv_b, m_dt_bias, m_a_log, m_d_skip, m_ssd_norm_w, m_hg_lower_bounds, m_hg_norm_w, m_w_out, m_norm_xa_w, m_norm_mem_w, m_xa_wq, m_xa_wkv, m_xa_wo, m_norm_ffn_w, m_ffn_w_gate, m_ffn_w_up, m_ffn_w_down, m_norm_final_w, v_norm_mix_w, v_w_in, v_conv_w, v_conv_b, v_dt_bias, v_a_log, v_d_skip, v_ssd_norm_w, v_hg_lower_bounds, v_hg_norm_w, v_w_out, v_norm_xa_w, v_norm_mem_w, v_xa_wq, v_xa_wkv, v_xa_wo, v_norm_ffn_w, v_ffn_w_gate, v_ffn_w_up, v_ffn_w_down, v_norm_final_w):
    args = dict(locals())
    w = {n: args[n] for n in WEIGHTS}
    mo = {n: args["m_" + n] for n in WEIGHTS}
    vo = {n: args["v_" + n] for n in WEIGHTS}
    me = 4 * lax.axis_index("x") + 2 * lax.axis_index("y") + lax.axis_index("c")

    big_sh = _row_shards(w)
    cw_bits = lax.bitcast_convert_type(conv_w[0], BF16).reshape(-1)
    cw_rows = jnp.pad(cw_bits, (0, CONV_BITS_ROWS * D - cw_bits.shape[0])).reshape(CONV_BITS_ROWS, D)
    gath = _allgather(_pack_group(big_sh, "in", BF16, extra=cw_rows))
    wt = _unpack_gathered(gath, ("in",))
    off = _rows_padded(GROUPS["in"][0][1])
    cw_all = lax.bitcast_convert_type(gath[:, off:off + 2].reshape(N_DEV, 2 * D)[:, 0:1536].reshape(N_DEV, 4, 192, 2),
                                      F32)
    conv_w_full = cw_all.transpose(1, 0, 2).reshape(4, 1536)

    small = {n: w[n][0] if w[n].ndim == 3 else w[n] for n in WEIGHTS if n not in BIG}
    small["conv_w"] = conv_w_full
    small["hg_lower_bounds"] = hg_lower_bounds
    dist = {"rest_pack": jnp.concatenate([_pack_group(big_sh, "mid", BF16), _pack_group(big_sh, "ffn", BF16)], axis=0),
            "unpack_rest": lambda g: _unpack_gathered(g, ("mid", "ffn")),
            "pieces": _grad_pieces}
    grad_x, _, gsmall, received = _local_step(x[0], mem[0], loss_target[0], wt, small, dist)

    shapes = {n: w[n].shape for n in BIG}
    m_sh, v_sh = _row_shards(mo), _row_shards(vo)
    out_g, out_d, out_m, out_v = {}, {}, {}, {}
    for group in GROUPS:
        wp = _pack_group(big_sh, group, F32)
        mp = _pack_group(m_sh, group, F32)
        vp = _pack_group(v_sh, group, F32)
        packed = _adamw(received[group], wp, mp, vp, tr=ADAM_ROWS[group], name="adamw_" + group)
        for dst, src in zip((out_g, out_d, out_m, out_v), packed):
            dst.update(_unpack_group(src, group, shapes))

    tot = _small_allreduce(_pack_small(gsmall))
    loss = tot[SMALL_AT["loss"][0], 0]
    cw_row = SMALL_AT["conv_w"][0]
    conv_w_grad = lax.dynamic_slice(tot, (cw_row, me * 192), (4, 192))
    names = [n for n, _, _ in SMALL if n != "loss"]
    as2d = lambda t: t.reshape(t.shape[-2:] if t.ndim > 1 else (1, t.shape[0]))
    small_out = _adamw_small(tot, conv_w_grad, *[[as2d(d[n]) for n in names] for d in (w, mo, vo)])
    for dst, src in zip((out_g, out_d, out_m, out_v), small_out):
        dst.update({n: t.reshape(w[n].shape) for n, t in zip(names, src)})
    return (loss, grad_x[None], *[out_g[n] for n in WEIGHTS], *[out_d[n] for n in WEIGHTS],
            *[out_m[n] for n in WEIGHTS], *[out_v[n] for n in WEIGHTS])
```

```python
import jax
import jax.numpy as jnp
from jax import lax
from jax.experimental import pallas as pl
from jax.experimental.pallas import tpu as pltpu

F32, BF16 = jnp.float32, jnp.bfloat16
MESH = pl.DeviceIdType.MESH

D = 1024
EPS = 1e-6
SSD_HEADS, SSD_P, SSD_N, SSD_Q = 16, 64, 128, 128
HG_HEADS, HG_K, HG_STEP, HG_SUB = 8, 128, 128, 64
XA_HEADS, XA_DH, MEM_LEN = 4, 256, 256
FFN = 2816
N_DEV = 8
WB, WC = 1152, 1536
ADAM_LR, ADAM_B1, ADAM_B2, ADAM_EPS, ADAM_WD, ADAM_STEP = 0.001, 0.9, 0.999, 1e-08, 0.01, 10
VMEM_MB = 2 ** 20

PACK = (("w_in", 834), ("w_out", 256), ("xa_wq", 128), ("xa_wkv", 256), ("xa_wo", 128),
        ("ffn_w_gate", 352), ("ffn_w_up", 352), ("ffn_w_down", 352))
ROW_TILE = 16
GROUPS = {"in": PACK[0:1], "mid": PACK[1:5], "ffn": PACK[5:8]}
GROUP_ROWS = {"in": 896, "mid": 768, "ffn": 1056}
ADAM_ROWS = {"in": 128, "mid": 128, "ffn": 176}
PIECE_DTYPE = {"in": BF16, "mid": F32, "ffn": F32}
CONV_BITS_ROWS = ROW_TILE


def _rows_padded(r):
    return -(-r // ROW_TILE) * ROW_TILE

SMALL = (("norm_mix_w", 1, 1024), ("conv_w", 4, 1536), ("conv_b", 1, 1536), ("dt_bias", 1, 16), ("a_log", 1, 16),
         ("d_skip", 1, 16), ("ssd_norm_w", 1, 1024), ("hg_lower_bounds", 2, 1024), ("hg_norm_w", 1, 128),
         ("norm_xa_w", 1, 1024), ("norm_mem_w", 1, 1024), ("norm_ffn_w", 1, 1024), ("norm_final_w", 1, 1024),
         ("loss", 1, 1))
SMALL_COLS = 1536
SMALL_ROWS = 24
SMALL_AT = {n: (sum(q for _, q, _ in SMALL[:i]), r, c) for i, (n, r, c) in enumerate(SMALL)}


def _cparams(sem=None, vmem_mb=48):
    return pltpu.CompilerParams(dimension_semantics=sem, vmem_limit_bytes=vmem_mb * VMEM_MB)


def _dot(a, b):
    return jnp.dot(a.astype(BF16), b.astype(BF16), preferred_element_type=F32)


def _dot_nt(a, b):
    return lax.dot_general(a.astype(BF16), b.astype(BF16), (((1,), (1,)), ((), ())), preferred_element_type=F32)


def _dot_tn(a, b):
    return lax.dot_general(a.astype(BF16), b.astype(BF16), (((0,), (0,)), ((), ())), preferred_element_type=F32)


def _split3(a):
    a1 = a.astype(BF16)
    r1 = a - a1.astype(F32)
    a2 = r1.astype(BF16)
    a3 = (r1 - a2.astype(F32)).astype(BF16)
    return a1, a2, a3


def _dot_hi(a, b, general=0):
    if general == 0:
        return sum(jnp.dot(t, b.astype(BF16), preferred_element_type=F32) for t in _split3(a))
    return sum(jnp.dot(a.astype(BF16), t, preferred_element_type=F32) for t in _split3(b))


def _dot_nt_hi(a, b):
    return sum(_dot_nt(t, b) for t in _split3(a))


def _sigmoid(x):
    return 1.0 / (1.0 + jnp.exp(-x))


def _sigmoid_gate(x):
    return pl.reciprocal(1.0 + jnp.exp(-x), approx=True)


def _silu(x):
    return x * _sigmoid_gate(x)


def _dsilu(x):
    s = _sigmoid_gate(x)
    return s * (1.0 + x * (1.0 - s))


def _softplus(x):
    return jnp.maximum(x, 0.0) + jnp.log(1.0 + jnp.exp(-jnp.abs(x)))


def _rms_fwd(x, w):
    r = lax.rsqrt(jnp.mean(x * x, axis=1, keepdims=True) + EPS)
    return x * r * w


def _rms_bwd(dy, x, w):
    r = lax.rsqrt(jnp.mean(x * x, axis=1, keepdims=True) + EPS)
    xh = x * r
    g = dy * w
    dx = r * (g - xh * jnp.mean(g * xh, axis=1, keepdims=True))
    return dx, jnp.sum(dy * xh, axis=0, keepdims=True)


def _iota2(shape, dim):
    return lax.broadcasted_iota(jnp.int32, shape, dim)


def _tril(n):
    return (_iota2((n, n), 0) >= _iota2((n, n), 1)).astype(F32)


def _triu(n):
    return (_iota2((n, n), 0) <= _iota2((n, n), 1)).astype(F32)


def _norm_mm(x, nw, w, *, tm, tn, name, emit_h=False, out_dtype=F32):
    L, K = x.shape
    N = w.shape[0]
    tm, tn = min(tm, L), min(tn, N)
    ni, nj = L // tm, N // tn

    def body(x_ref, nw_ref, w_ref, *rest):
        if emit_h:
            o_ref, h_ref, hs_ref = rest
        else:
            o_ref, hs_ref = rest

        @pl.when(pl.program_id(1) == 0)
        def _():
            h = _rms_fwd(x_ref[...], nw_ref[...]).astype(BF16)
            hs_ref[...] = h
            if emit_h:
                h_ref[...] = h

        o_ref[...] = _dot_nt(hs_ref[...], w_ref[...]).astype(out_dtype)

    out_shape = [jax.ShapeDtypeStruct((L, N), out_dtype)]
    out_specs = [pl.BlockSpec((tm, tn), lambda i, j: (i, j))]
    if emit_h:
        out_shape.append(jax.ShapeDtypeStruct((L, K), BF16))
        out_specs.append(pl.BlockSpec((tm, K), lambda i, j: (i, 0)))
    res = pl.pallas_call(
        body, name=name, grid=(ni, nj),
        in_specs=[pl.BlockSpec((tm, K), lambda i, j: (i, 0)), pl.BlockSpec((1, K), lambda i, j: (0, 0)),
                  pl.BlockSpec((tn, K), lambda i, j: (j, 0))],
        out_specs=out_specs, out_shape=out_shape, scratch_shapes=[pltpu.VMEM((tm, K), BF16)],
        compiler_params=_cparams(("parallel", "arbitrary")),
    )(x, nw, w)
    return res if len(res) > 1 else res[0]


def _inproj(x, nw, w_a, w_b, w_c, *, tm, gather=None):
    L, K = x.shape
    tm = min(tm, L)
    ni = L // tm

    def body(x_ref, nw_ref, wa_ref, wb_ref, wc_ref, pa_ref, hf_ref, z_ref, dt_ref, pc_ref, h_ref):
        h = _rms_fwd(x_ref[...], nw_ref[...]).astype(BF16)
        h_ref[...] = h
        pa_ref[...] = _dot_nt(h, wa_ref[0:3 * D, :]).astype(BF16)
        hf_ref[...] = _dot_nt(h, wa_ref[3 * D:4 * D, :])
        pb = _dot_nt(h, wb_ref[...])
        z_ref[...] = pb[:, 0:D].astype(BF16)
        dt_ref[...] = pb[:, D:D + 128]
        pc_ref[...] = _dot_nt(h, wc_ref[...]).astype(BF16)

    row = lambda n: pl.BlockSpec((tm, n), lambda i: (i, 0))
    full = lambda a: pl.BlockSpec(a.shape, lambda i: (0, 0))
    at = lambda i: lambda: pl.program_id(0) == i
    payload = None if gather is None else (_Gather, gather)
    phases = [(at(0), _Gather.start, True), (at(ni // 2), _Gather.forward, True), (at(ni - 1), _Gather.finish, False)]
    in_specs, out_specs, out_shape, scratch = _carried_specs(
        payload, [row(K), full(nw), full(w_a), full(w_b), full(w_c)],
        [row(3 * D), row(D), row(D), row(128), row(WC), row(K)],
        [jax.ShapeDtypeStruct((L, 3 * D), BF16), jax.ShapeDtypeStruct((L, D), F32),
         jax.ShapeDtypeStruct((L, D), BF16), jax.ShapeDtypeStruct((L, 128), F32),
         jax.ShapeDtypeStruct((L, WC), BF16), jax.ShapeDtypeStruct((L, K), BF16)], [])
    args = [x, nw, w_a, w_b, w_c]
    return pl.pallas_call(
        _carried(body, 5, 6, payload, phases), name="inproj", grid=(ni,),
        in_specs=in_specs, out_specs=out_specs, out_shape=out_shape, scratch_shapes=scratch,
        compiler_params=_cparams(("arbitrary",), vmem_mb=58),
    )(*(args if gather is None else args + [gather]))


def _mm2_res(res, a1, a2, w, *, tm, name):
    L, N = res.shape
    K = a1.shape[1]
    tm = min(tm, L)

    def body(r_ref, a1_ref, a2_ref, w_ref, o_ref):
        acc = jnp.dot(a1_ref[...], w_ref[0:K, :], preferred_element_type=F32)
        acc += jnp.dot(a2_ref[...], w_ref[K:2 * K, :], preferred_element_type=F32)
        o_ref[...] = r_ref[...] + acc

    return pl.pallas_call(
        body, name=name, grid=(L // tm,),
        in_specs=[pl.BlockSpec((tm, N), lambda i: (i, 0)), pl.BlockSpec((tm, K), lambda i: (i, 0)),
                  pl.BlockSpec((tm, K), lambda i: (i, 0)), pl.BlockSpec((2 * K, N), lambda i: (0, 0))],
        out_specs=pl.BlockSpec((tm, N), lambda i: (i, 0)),
        out_shape=jax.ShapeDtypeStruct((L, N), F32),
        compiler_params=_cparams(("parallel",)),
    )(res, a1, a2, w)


def _dw(a, b, *, tM, tN, tl, name):
    L, M = a.shape
    N = b.shape[1]
    tM, tN, tl = min(tM, M), min(tN, N), min(tl, L)

    def body(a_ref, b_ref, o_ref):
        @pl.when(pl.program_id(2) == 0)
        def _():
            o_ref[...] = jnp.zeros_like(o_ref)

        o_ref[...] += _dot_tn(a_ref[...], b_ref[...])

    return pl.pallas_call(
        body, name=name, grid=(M // tM, N // tN, L // tl),
        in_specs=[pl.BlockSpec((tl, tM), lambda i, j, l: (l, i)), pl.BlockSpec((tl, tN), lambda i, j, l: (l, j))],
        out_specs=pl.BlockSpec((tM, tN), lambda i, j, l: (i, j)),
        out_shape=jax.ShapeDtypeStruct((M, N), F32),
        compiler_params=_cparams(("parallel", "parallel", "arbitrary")),
    )(a, b)


def _first_last(n):
    return [(lambda: pl.program_id(0) == 0, _Exchange.start, True),
            (lambda: pl.program_id(0) == n - 1, _Exchange.finish, False)]


def _mm_normbwd(a_list, w_list, x, nw, res, *, tm, name, exchange=None):
    L, Dm = x.shape
    tm = min(tm, L)
    n = len(a_list)
    has_res = res is not None
    nt = L // tm

    def body(*refs):
        a_refs, w_refs = refs[:n], refs[n:2 * n]
        x_ref, nw_ref = refs[2 * n], refs[2 * n + 1]
        k = 2 * n + 2
        r_ref = refs[k] if has_res else None
        dx_ref, dnw_ref, dh_ref = refs[k + has_res], refs[k + has_res + 1], refs[k + has_res + 2]

        @pl.when(pl.program_id(0) == 0)
        def _():
            dnw_ref[...] = jnp.zeros_like(dnw_ref)
            dh_ref[...] = jnp.zeros_like(dh_ref)

        dx, dnw = _rms_bwd(dh_ref[...], x_ref[...], nw_ref[...])
        dx_ref[...] = dx + r_ref[...] if has_res else dx
        dnw_ref[...] += dnw
        dh = _dot(a_refs[0][...], w_refs[0][...])
        for a_ref, w_ref in zip(a_refs[1:], w_refs[1:]):
            dh += _dot(a_ref[...], w_ref[...])
        dh_ref[...] = dh

    ahead = lambda width: pl.BlockSpec((tm, width), lambda i: (jnp.minimum(i, nt - 1), 0))
    behind = pl.BlockSpec((tm, Dm), lambda i: (jnp.maximum(i - 1, 0), 0))
    in_specs = [ahead(a.shape[1]) for a in a_list]
    in_specs += [pl.BlockSpec(w.shape, lambda i: (0, 0)) for w in w_list]
    in_specs += [behind, pl.BlockSpec((1, Dm), lambda i: (0, 0))]
    args = [*a_list, *w_list, x, nw]
    if has_res:
        in_specs.append(behind)
        args.append(res)
    payload = None if exchange is None else (_Exchange, exchange)
    n_in = len(args)
    if exchange is not None:
        args.append(exchange)
    in_specs, out_specs, out_shape, scratch = _carried_specs(
        payload, in_specs, [behind, pl.BlockSpec((1, Dm), lambda i: (0, 0))],
        [jax.ShapeDtypeStruct((L, Dm), F32), jax.ShapeDtypeStruct((1, Dm), F32)], [pltpu.VMEM((tm, Dm), F32)])
    return pl.pallas_call(
        _carried(body, n_in, 2, payload, _first_last(nt + 1)), name=name, grid=(nt + 1,), in_specs=in_specs,
        out_specs=out_specs, out_shape=out_shape, scratch_shapes=scratch,
        compiler_params=_cparams(("arbitrary",), vmem_mb=56),
    )(*args)


CONV_TN = 512


HALO = 16


def _conv_pre(cat, w_ref, b_ref, rows):
    shifted = [pltpu.roll(cat, 3 - k, 0)[HALO:HALO + rows] for k in range(3)] + [cat[HALO:HALO + rows]]
    pre = b_ref[...] + w_ref[3:4, :] * shifted[3]
    for k in range(3):
        pre += w_ref[k:k + 1, :] * shifted[k]
    return pre, shifted


def _conv_fwd(pc, cw, cb, *, tm):
    L, C = pc.shape
    tm = min(tm, L)
    tn = CONV_TN

    def body(u_ref, halo_ref, w_ref, b_ref, o_ref):
        halo = jnp.where(pl.program_id(1) > 0, halo_ref[...].astype(F32), 0.0)
        cat = jnp.concatenate([halo, u_ref[...].astype(F32)], axis=0)
        pre, _ = _conv_pre(cat, w_ref, b_ref, tm)
        o_ref[...] = _silu(pre).astype(BF16)

    return pl.pallas_call(
        body, name="conv_fwd", grid=(C // tn, L // tm),
        in_specs=[pl.BlockSpec((tm, tn), lambda j, i: (i, j)),
                  pl.BlockSpec((HALO, tn), lambda j, i: (jnp.maximum(i * (tm // HALO) - 1, 0), j)),
                  pl.BlockSpec((4, tn), lambda j, i: (0, j)), pl.BlockSpec((1, tn), lambda j, i: (0, j))],
        out_specs=pl.BlockSpec((tm, tn), lambda j, i: (i, j)),
        out_shape=jax.ShapeDtypeStruct((L, C), BF16),
        compiler_params=_cparams(("parallel", "parallel")),
    )(pc, pc, cw, cb)


def _conv_bwd(pc, dact, cw, cb, *, tm):
    L, C = pc.shape
    tm = min(tm, L)
    tn = CONV_TN
    nt = L // tm

    def body(u_ref, halo_ref, unext_ref, da_ref, danext_ref, w_ref, b_ref, du_ref, dw_ref, db_ref):
        i = pl.program_id(1)
        halo = jnp.where(i > 0, halo_ref[...].astype(F32), 0.0)
        cat = jnp.concatenate([halo, u_ref[...].astype(F32), unext_ref[...].astype(F32)], axis=0)
        pre, shifted = _conv_pre(cat, w_ref, b_ref, tm + HALO)
        da = jnp.concatenate([da_ref[...].astype(F32),
                              jnp.where(i < nt - 1, danext_ref[...].astype(F32), 0.0)], axis=0)
        dpre = da * _dsilu(pre)
        du = w_ref[3:4, :] * dpre[0:tm]
        for k in range(3):
            du += w_ref[k:k + 1, :] * pltpu.roll(dpre, tm + HALO - (3 - k), 0)[0:tm]
        du_ref[...] = du.astype(BF16)

        @pl.when(i == 0)
        def _():
            dw_ref[...] = jnp.zeros_like(dw_ref)
            db_ref[...] = jnp.zeros_like(db_ref)

        dp = dpre[0:tm]
        dw_ref[...] += jnp.concatenate(
            [jnp.sum(dp * shifted[k][0:tm], axis=0, keepdims=True) for k in range(4)], axis=0)
        db_ref[...] += jnp.sum(dp, axis=0, keepdims=True)

    nb = L // HALO
    return pl.pallas_call(
        body, name="conv_bwd", grid=(C // tn, nt),
        in_specs=[pl.BlockSpec((tm, tn), lambda j, i: (i, j)),
                  pl.BlockSpec((HALO, tn), lambda j, i: (jnp.maximum(i * (tm // HALO) - 1, 0), j)),
                  pl.BlockSpec((HALO, tn), lambda j, i: (jnp.minimum((i + 1) * (tm // HALO), nb - 1), j)),
                  pl.BlockSpec((tm, tn), lambda j, i: (i, j)),
                  pl.BlockSpec((HALO, tn), lambda j, i: (jnp.minimum((i + 1) * (tm // HALO), nb - 1), j)),
                  pl.BlockSpec((4, tn), lambda j, i: (0, j)), pl.BlockSpec((1, tn), lambda j, i: (0, j))],
        out_specs=[pl.BlockSpec((tm, tn), lambda j, i: (i, j)), pl.BlockSpec((4, tn), lambda j, i: (0, j)),
                   pl.BlockSpec((1, tn), lambda j, i: (0, j))],
        out_shape=[jax.ShapeDtypeStruct((L, C), BF16), jax.ShapeDtypeStruct((4, C), F32),
                   jax.ShapeDtypeStruct((1, C), F32)],
        compiler_params=_cparams(("parallel", "arbitrary")),
    )(pc, pc, pc, dact, dact, cw, cb)


def _ssd_common(dtr_ref, dtb_ref, avec_ref, aexp_ref, e_ref, acx_ref, acol_ref, arow_ref):
    q = SSD_Q
    tril = _tril(q)
    dtpre = dtr_ref[...] + dtb_ref[...]
    dt = _softplus(dtpre)
    dtx = _dot_hi(dt, e_ref[...])
    acx_ref[...] = _dot_hi(tril, dtx * aexp_ref[...], 1)
    acol = _dot_hi(tril, dt * avec_ref[...], 1)
    acol_ref[...] = acol
    arow_ref[...] = acol.T
    return dtpre, dt, dtx


def _ssd_fwd(xbc, pz, pdt, dtb, avec, aexp, dexp, nw, emat):
    L = xbc.shape[0]
    q = SSD_Q
    nc = L // q

    def body(xbc_ref, z_ref, dtr_ref, dtb_ref, avec_ref, aexp_ref, dexp_ref, nw_ref, e_ref,
             ya_ref, ypre_ref, sin_ref, st_ref, acx_ref, acol_ref, arow_ref, xdt_ref, y_ref):
        @pl.when(pl.program_id(0) == 0)
        def _():
            st_ref[...] = jnp.zeros_like(st_ref)

        sin_ref[...] = st_ref[...].astype(BF16)
        _, _, dtx = _ssd_common(dtr_ref, dtb_ref, avec_ref, aexp_ref, e_ref, acx_ref, acol_ref, arow_ref)
        xs = xbc_ref[:, 0:D].astype(F32)
        xdt = xs * dtx
        xdt_ref[...] = xdt
        acx = acx_ref[...]
        alast = acx_ref[q - 1:q, :]
        xdtd = xdt * jnp.exp(alast - acx)
        eac = jnp.exp(acx)
        ealast = jnp.exp(alast)
        causal = _iota2((q, q), 0) >= _iota2((q, q), 1)
        for g in range(2):
            gs = slice(512 * g, 512 * g + 512)
            bm = xbc_ref[:, D + 128 * g:D + 128 * g + 128]
            cm = xbc_ref[:, D + 256 + 128 * g:D + 256 + 128 * g + 128]
            stg = st_ref[:, gs]
            yoff = _dot(cm, stg) * eac[:, gs]
            gmat = _dot_nt(cm, bm)
            for e in range(8):
                h = 8 * g + e
                hs = slice(64 * h, 64 * h + 64)
                col = acol_ref[:, h:h + 1]
                row = arow_ref[h:h + 1, :]
                lm = jnp.exp(jnp.where(causal, col - row, -1e30))
                y_ref[:, hs] = _dot(gmat * lm, xdt_ref[:, hs])
            y_ref[:, gs] += yoff + dexp_ref[:, gs] * xs[:, gs]
            st_ref[:, gs] = stg * ealast[:, gs] + _dot_tn(bm, xdtd[:, gs])
        ypre_ref[...] = y_ref[...].astype(BF16)
        for g in range(2):
            gs = slice(512 * g, 512 * g + 512)
            yz = y_ref[:, gs] * _silu(z_ref[:, gs].astype(F32))
            ya_ref[:, gs] = _rms_fwd(yz, nw_ref[:, gs]).astype(BF16)

    vec = lambda n: pl.BlockSpec((1, n), lambda c: (0, 0))
    return pl.pallas_call(
        body, name="ssd_fwd", grid=(nc,),
        in_specs=[pl.BlockSpec((q, 1536), lambda c: (c, 0)), pl.BlockSpec((q, D), lambda c: (c, 0)),
                  pl.BlockSpec((q, 128), lambda c: (c, 0)), vec(128), vec(128), vec(D), vec(D), vec(D),
                  pl.BlockSpec((128, D), lambda c: (0, 0))],
        out_specs=[pl.BlockSpec((q, D), lambda c: (c, 0)), pl.BlockSpec((q, D), lambda c: (c, 0)),
                   pl.BlockSpec((128, D), lambda c: (c, 0))],
        out_shape=[jax.ShapeDtypeStruct((L, D), BF16), jax.ShapeDtypeStruct((L, D), BF16),
                   jax.ShapeDtypeStruct((nc * 128, D), BF16)],
        scratch_shapes=[pltpu.VMEM((128, D), F32), pltpu.VMEM((q, D), F32), pltpu.VMEM((q, 128), F32),
                        pltpu.VMEM((128, q), F32), pltpu.VMEM((q, D), F32), pltpu.VMEM((q, D), F32)],
        compiler_params=_cparams(("arbitrary",)),
    )(xbc, pz, pdt, dtb, avec, aexp, dexp, nw, emat)


def _ssd_bwd(xbc, pz, pdt, ypre, sin, dya, dtb, avec, aexp, dexp, nw, emat, exchange=None):
    L = xbc.shape[0]
    q = SSD_Q
    nc = L // q

    def body(xbc_ref, z_ref, dtr_ref, ypre_ref, sin_ref, dya_ref, dtb_ref, avec_ref, aexp_ref, dexp_ref, nw_ref,
             e_ref, dpb_ref, dxbc_ref, dnw_ref, ddtb_ref, da_ref, ddx_ref,
             dst_ref, acx_ref, acol_ref, arow_ref, xdt_ref, dxdt_ref, dy_ref, dacx_ref):
        @pl.when(pl.program_id(0) == 0)
        def _():
            dst_ref[...] = jnp.zeros_like(dst_ref)
            dnw_ref[...] = jnp.zeros_like(dnw_ref)
            ddtb_ref[...] = jnp.zeros_like(ddtb_ref)
            da_ref[...] = jnp.zeros_like(da_ref)
            ddx_ref[...] = jnp.zeros_like(ddx_ref)

        dtpre, dt, dtx = _ssd_common(dtr_ref, dtb_ref, avec_ref, aexp_ref, e_ref, acx_ref, acol_ref, arow_ref)
        xs = xbc_ref[:, 0:D].astype(F32)
        xdt = xs * dtx
        xdt_ref[...] = xdt
        acx = acx_ref[...]
        alast = acx_ref[q - 1:q, :]
        dec_end = jnp.exp(alast - acx)
        xdtd = xdt * dec_end
        eac = jnp.exp(acx)
        ealast = jnp.exp(alast)
        for g in range(2):
            gs = slice(512 * g, 512 * g + 512)
            y = ypre_ref[:, gs].astype(F32)
            z = z_ref[:, gs].astype(F32)
            sz = _silu(z)
            dyz, dnw = _rms_bwd(dya_ref[:, gs].astype(F32), y * sz, nw_ref[:, gs])
            dnw_ref[:, gs] += dnw
            dy_ref[:, gs] = dyz * sz
            dpb_ref[:, gs] = (dyz * y * _dsilu(z)).astype(BF16)
        dy = dy_ref[...]
        ddx_ref[...] += jnp.sum(dy * xs, axis=0, keepdims=True)
        ri = _iota2((q, q), 0)
        ci = _iota2((q, q), 1)
        causal = ri >= ci
        causal_t = ri <= ci
        dacol = jnp.zeros((q, 128), F32)
        dacol_t = jnp.zeros((128, q), F32)
        last_row = _iota2((q, 512), 0) == q - 1
        for g in range(2):
            gs = slice(512 * g, 512 * g + 512)
            bm = xbc_ref[:, D + 128 * g:D + 128 * g + 128]
            cm = xbc_ref[:, D + 256 + 128 * g:D + 256 + 128 * g + 128]
            stg = sin_ref[:, gs].astype(F32)
            dstg = dst_ref[:, gs]
            dyg = dy[:, gs]
            yoff = _dot(cm, stg) * eac[:, gs]
            dwm = dyg * eac[:, gs]
            dcm = _dot_nt(dwm, stg)
            dstin = _dot_tn(cm, dwm)
            dacx_g = dyg * yoff
            dxdtd = _dot(bm, dstg)
            dbm = _dot_nt(xdtd[:, gs], dstg)
            t = dxdtd * xdtd[:, gs]
            dacx_g -= t
            dalast = jnp.sum(t, axis=0, keepdims=True) + jnp.sum(dstg * stg, axis=0, keepdims=True) * ealast[:, gs]
            dst_ref[:, gs] = dstin + dstg * ealast[:, gs]
            dacx_ref[:, gs] = dacx_g + jnp.where(last_row, dalast, 0.0)
            gmat = _dot_nt(cm, bm)
            gmat_t = _dot_nt(bm, cm)
            dg = jnp.zeros((q, q), F32)
            for e in range(8):
                h = 8 * g + e
                hs = slice(64 * h, 64 * h + 64)
                col = acol_ref[:, h:h + 1]
                row = arow_ref[h:h + 1, :]
                lm = jnp.exp(jnp.where(causal, col - row, -1e30))
                lm_t = jnp.exp(jnp.where(causal_t, row - col, -1e30))
                dyh = dy_ref[:, hs]
                dm = _dot_nt(dyh, xdt_ref[:, hs])
                dxdt_ref[:, hs] = _dot(gmat_t * lm_t, dyh)
                dml = dm * lm
                dg += dml
                p = dml * gmat
                dacol += jnp.where(ci == h, jnp.sum(p, axis=1, keepdims=True), 0.0)
                dacol_t -= jnp.where(ri == h, jnp.sum(p, axis=0, keepdims=True), 0.0)
            dcm += _dot(dg, bm)
            dbm += _dot_tn(dg, cm)
            dxbc_ref[:, D + 128 * g:D + 128 * g + 128] = dbm.astype(BF16)
            dxbc_ref[:, D + 256 + 128 * g:D + 256 + 128 * g + 128] = dcm.astype(BF16)
            dxdt_ref[:, gs] += dxdtd * dec_end[:, gs]
        dxdt = dxdt_ref[...]
        dacum = dacol + dacol_t.T + _dot_nt_hi(dacx_ref[...], e_ref[...])
        da = _dot_hi(_triu(q), dacum, 1)
        ddt = da * avec_ref[...] + _dot_nt_hi(dxdt * xs, e_ref[...])
        da_ref[...] += jnp.sum(da * dt, axis=0, keepdims=True) * avec_ref[...]
        dxbc_ref[:, 0:D] = (dexp_ref[...] * dy + dxdt * dtx).astype(BF16)
        ddtr = ddt * _sigmoid(dtpre)
        ddtb_ref[...] += jnp.sum(ddtr, axis=0, keepdims=True)
        dpb_ref[:, D:D + 128] = ddtr.astype(BF16)

    rev = lambda c: nc - 1 - c
    vec = lambda n: pl.BlockSpec((1, n), lambda c: (0, 0))
    payload = None if exchange is None else (_Exchange, exchange)
    args = [xbc, pz, pdt, ypre, sin, dya, dtb, avec, aexp, dexp, nw, emat]
    in_specs, out_specs, out_shape, scratch = _carried_specs(
        payload,
        [pl.BlockSpec((q, 1536), lambda c: (rev(c), 0)), pl.BlockSpec((q, D), lambda c: (rev(c), 0)),
         pl.BlockSpec((q, 128), lambda c: (rev(c), 0)), pl.BlockSpec((q, D), lambda c: (rev(c), 0)),
         pl.BlockSpec((128, D), lambda c: (rev(c), 0)), pl.BlockSpec((q, D), lambda c: (rev(c), 0)),
         vec(128), vec(128), vec(D), vec(D), vec(D), pl.BlockSpec((128, D), lambda c: (0, 0))],
        [pl.BlockSpec((q, WB), lambda c: (rev(c), 0)), pl.BlockSpec((q, 1536), lambda c: (rev(c), 0)),
         vec(D), vec(128), vec(128), vec(D)],
        [jax.ShapeDtypeStruct((L, WB), BF16), jax.ShapeDtypeStruct((L, 1536), BF16),
         jax.ShapeDtypeStruct((1, D), F32), jax.ShapeDtypeStruct((1, 128), F32),
         jax.ShapeDtypeStruct((1, 128), F32), jax.ShapeDtypeStruct((1, D), F32)],
        [pltpu.VMEM((128, D), F32), pltpu.VMEM((q, D), F32), pltpu.VMEM((q, 128), F32),
         pltpu.VMEM((128, q), F32), pltpu.VMEM((q, D), F32), pltpu.VMEM((q, D), F32),
         pltpu.VMEM((q, D), F32), pltpu.VMEM((q, D), F32)])
    return pl.pallas_call(
        _carried(body, len(args), 6, payload, _first_last(nc)), name="ssd_bwd", grid=(nc,),
        in_specs=in_specs, out_specs=out_specs, out_shape=out_shape, scratch_shapes=scratch,
        compiler_params=_cparams(("arbitrary",)),
    )(*(args if exchange is None else args + [exchange]))


def _hg_gates(hq, hf, hgl_ref, b_ref, ls=slice(None)):
    lb = 1.0 / (1.0 + jnp.exp(hgl_ref[1:2, ls] - hgl_ref[0:1, ls]))
    qf = _silu(hq)
    sg = _sigmoid(hf)
    f = lb + (1.0 - lb) * sg
    b_ref[:, ls] = _dot_hi(_tril(HG_STEP), jnp.log(f), 1)
    return lb, qf, sg, f


def _hg_factors(qf, kf, b_ref, ls=slice(None)):
    s, n = HG_SUB, HG_STEP
    b = b_ref[:, ls]
    blast = b_ref[n - 1:n, ls]
    m0, mb, m1 = b_ref[s // 2 - 1:s // 2, ls], b_ref[s - 1:s, ls], b_ref[s + s // 2 - 1:s + s // 2, ls]
    b0, b1 = b[0:s], b[s:n]
    q0, q1, k0, k1 = qf[0:s], qf[s:n], kf[0:s], kf[s:n]
    fac = dict(
        eb=jnp.exp(blast), eq=jnp.exp(b), ek=jnp.exp(blast - b),
        eq0=jnp.exp(b0 - m0), ek0=jnp.exp(m0 - b0), eq1=jnp.exp(b1 - m1), ek1=jnp.exp(m1 - b1),
        eqb=jnp.exp(b1 - mb), ekb=jnp.exp(mb - b0))
    rd = lambda t: t.astype(BF16).astype(F32)
    val = dict(qe=qf * fac["eq"], ke=kf * fac["ek"], qm0=rd(q0 * fac["eq0"]), km0=rd(k0 * fac["ek0"]),
               qm1=rd(q1 * fac["eq1"]), km1=rd(k1 * fac["ek1"]), qb=rd(q1 * fac["eqb"]), kb=rd(k0 * fac["ekb"]))
    return fac, val


def _hgrn_fwd(pa, phf, hgl, nwx):
    L = pa.shape[0]
    n, s = HG_STEP, HG_SUB
    nc = L // n

    def body(hq_ref, hf_ref, hi_ref, hg_ref, hgl_ref, nw_ref, ob_ref, opre_ref, sin_ref, st_ref, b_ref):
        @pl.when(pl.program_id(0) == 0)
        def _():
            st_ref[...] = jnp.zeros_like(st_ref)

        sin_ref[...] = st_ref[...].astype(BF16)
        causal = _iota2((s, s), 0) >= _iota2((s, s), 1)
        for half in range(2):
            ls = slice(512 * half, 512 * half + 512)
            _, qf, _, f = _hg_gates(hq_ref[:, ls].astype(F32), hf_ref[:, ls], hgl_ref, b_ref, ls)
            fac, val = _hg_factors(qf, 1.0 - f, b_ref, ls)
            for hh in range(HG_HEADS // 2):
                hl = slice(128 * hh, 128 * hh + 128)
                hs = slice(512 * half + 128 * hh, 512 * half + 128 * hh + 128)
                sth = st_ref[:, hs]
                v = hi_ref[:, hs]
                v0, v1 = v[0:s], v[s:n]
                a00 = jnp.where(causal, _dot_nt(val["qm0"][:, hl], val["km0"][:, hl]), 0.0)
                a11 = jnp.where(causal, _dot_nt(val["qm1"][:, hl], val["km1"][:, hl]), 0.0)
                a10 = _dot_nt(val["qb"][:, hl], val["kb"][:, hl])
                o = _dot_nt(val["qe"][:, hl], sth) + jnp.concatenate(
                    [_dot(a00, v0), _dot(a10, v0) + _dot(a11, v1)], axis=0)
                st_ref[:, hs] = sth * fac["eb"][:, hl] + _dot_tn(v, val["ke"][:, hl])
                opre_ref[:, hs] = o.astype(BF16)
                ob_ref[:, hs] = (_rms_fwd(o, nw_ref[:, hs]) * _silu(hg_ref[:, hs].astype(F32))).astype(BF16)

    blk = lambda j: pl.BlockSpec((n, D), lambda c: (c, j))
    return pl.pallas_call(
        body, name="hgrn_fwd", grid=(nc,),
        in_specs=[blk(0), blk(0), blk(1), blk(2), pl.BlockSpec((2, D), lambda c: (0, 0)),
                  pl.BlockSpec((1, D), lambda c: (0, 0))],
        out_specs=[blk(0), blk(0), blk(0)],
        out_shape=[jax.ShapeDtypeStruct((L, D), BF16), jax.ShapeDtypeStruct((L, D), BF16),
                   jax.ShapeDtypeStruct((nc * 128, D), BF16)],
        scratch_shapes=[pltpu.VMEM((128, D), F32), pltpu.VMEM((n, D), F32)],
        compiler_params=_cparams(("arbitrary",)),
    )(pa, phf, pa, pa, hgl, nwx)


def _hgrn_bwd(pa, phf, opre, sin, dob, hgl, nwx, exchange=None):
    L = pa.shape[0]
    n, s = HG_STEP, HG_SUB
    nc = L // n

    def body(hq_ref, hf_ref, hi_ref, hg_ref, opre_ref, sin_ref, dob_ref, hgl_ref, nw_ref,
             dpa_ref, dhgl_ref, dnw_ref, dst_ref, b_ref, dlb_ref, dq_ref, dk_ref, db_ref):
        i = pl.program_id(0)

        @pl.when(i == 0)
        def _():
            dst_ref[...] = jnp.zeros_like(dst_ref)
            dlb_ref[...] = jnp.zeros_like(dlb_ref)
            dnw_ref[...] = jnp.zeros_like(dnw_ref)

        hq = hq_ref[...].astype(F32)
        lb, qf, sg, f = _hg_gates(hq, hf_ref[...], hgl_ref, b_ref)
        kf = 1.0 - f
        fac, val = _hg_factors(qf, kf, b_ref)
        ri, ci = _iota2((s, s), 0), _iota2((s, s), 1)
        causal, causal_t = ri >= ci, ri <= ci
        last_row = _iota2((n, 128), 0) == n - 1
        for h in range(HG_HEADS):
            hs = slice(128 * h, 128 * h + 128)
            o = opre_ref[:, hs].astype(F32)
            gate = hg_ref[:, hs].astype(F32)
            dout = dob_ref[:, hs].astype(F32)
            sgate = _silu(gate)
            do, dnw = _rms_bwd(dout * sgate, o, nw_ref[:, hs])
            dnw_ref[:, hs] += dnw
            dpa_ref[:, 2 * D + 128 * h:2 * D + 128 * h + 128] = (
                dout * _rms_fwd(o, nw_ref[:, hs]) * _dsilu(gate)).astype(BF16)
            sth = sin_ref[:, hs].astype(F32)
            dsth = dst_ref[:, hs]
            v = hi_ref[:, hs]
            v0, v1 = v[0:s], v[s:n]
            do0, do1 = do[0:s], do[s:n]
            qe, ke = val["qe"][:, hs], val["ke"][:, hs]
            qm0, km0, qm1, km1 = val["qm0"][:, hs], val["km0"][:, hs], val["qm1"][:, hs], val["km1"][:, hs]
            qb, kb = val["qb"][:, hs], val["kb"][:, hs]
            dqe = _dot(do, sth)
            dstin = _dot_tn(do, qe)
            a00t = jnp.where(causal_t, _dot_nt(km0, qm0), 0.0)
            a11t = jnp.where(causal_t, _dot_nt(km1, qm1), 0.0)
            a10t = _dot_nt(kb, qb)
            dat00 = jnp.where(causal, _dot_nt(do0, v0), 0.0)
            dat11 = jnp.where(causal, _dot_nt(do1, v1), 0.0)
            dat10 = _dot_nt(do1, v0)
            dat00t = jnp.where(causal_t, _dot_nt(v0, do0), 0.0)
            dat11t = jnp.where(causal_t, _dot_nt(v1, do1), 0.0)
            dat10t = _dot_nt(v0, do1)
            dv = jnp.concatenate([_dot(a00t, do0) + _dot(a10t, do1), _dot(a11t, do1)], axis=0)
            dqm0, dkm0 = _dot(dat00, km0), _dot(dat00t, qm0)
            dqm1, dkm1 = _dot(dat11, km1), _dot(dat11t, qm1)
            dqb, dkb = _dot(dat10, kb), _dot(dat10t, qb)
            dke = _dot(v, dsth)
            dv += _dot_nt(ke, dsth)
            deb = jnp.sum(dsth * sth, axis=0, keepdims=True)
            dst_ref[:, hs] = dstin + dsth * fac["eb"][:, hs]
            dq = dqe * fac["eq"][:, hs] + jnp.concatenate(
                [dqm0 * fac["eq0"][:, hs], dqm1 * fac["eq1"][:, hs] + dqb * fac["eqb"][:, hs]], axis=0)
            dk = dke * fac["ek"][:, hs] + jnp.concatenate(
                [dkm0 * fac["ek0"][:, hs] + dkb * fac["ekb"][:, hs], dkm1 * fac["ek1"][:, hs]], axis=0)
            tke = dke * ke
            db = dqe * qe - tke + jnp.concatenate(
                [dqm0 * qm0 - dkm0 * km0 - dkb * kb, dqm1 * qm1 - dkm1 * km1 + dqb * qb], axis=0)
            dblast = jnp.sum(tke, axis=0, keepdims=True) + deb * fac["eb"][:, hs]
            db_ref[:, hs] = db + jnp.where(last_row, dblast, 0.0)
            dq_ref[:, hs] = dq
            dk_ref[:, hs] = dk
            dpa_ref[:, D + 128 * h:D + 128 * h + 128] = dv.astype(BF16)
        dg = _dot_hi(_triu(n), db_ref[...], 1)
        df = dg / f - dk_ref[...]
        dpa_ref[:, 3 * D:4 * D] = (df * (1.0 - lb) * sg * (1.0 - sg)).astype(BF16)
        dpa_ref[:, 0:D] = (dq_ref[...] * _dsilu(hq)).astype(BF16)
        dlb_ref[...] += jnp.sum(df * (1.0 - sg), axis=0, keepdims=True)

        @pl.when(i == nc - 1)
        def _():
            d0 = dlb_ref[...] * lb * (1.0 - lb)
            dhgl_ref[...] = jnp.concatenate([d0, -d0], axis=0)

    rev = lambda c: nc - 1 - c
    blk = lambda j: pl.BlockSpec((n, D), lambda c: (rev(c), j))
    payload = None if exchange is None else (_Exchange, exchange)
    args = [pa, phf, pa, pa, opre, sin, dob, hgl, nwx]
    in_specs, out_specs, out_shape, scratch = _carried_specs(
        payload,
        [blk(0), blk(0), blk(1), blk(2), blk(0), blk(0), blk(0), pl.BlockSpec((2, D), lambda c: (0, 0)),
         pl.BlockSpec((1, D), lambda c: (0, 0))],
        [pl.BlockSpec((n, 4 * D), lambda c: (rev(c), 0)), pl.BlockSpec((2, D), lambda c: (0, 0)),
         pl.BlockSpec((1, D), lambda c: (0, 0))],
        [jax.ShapeDtypeStruct((L, 4 * D), BF16), jax.ShapeDtypeStruct((2, D), F32), jax.ShapeDtypeStruct((1, D), F32)],
        [pltpu.VMEM((128, D), F32), pltpu.VMEM((n, D), F32), pltpu.VMEM((1, D), F32),
         pltpu.VMEM((n, D), F32), pltpu.VMEM((n, D), F32), pltpu.VMEM((n, D), F32)])
    return pl.pallas_call(
        _carried(body, len(args), 3, payload, _first_last(nc)), name="hgrn_bwd", grid=(nc,),
        in_specs=in_specs, out_specs=out_specs, out_shape=out_shape, scratch_shapes=scratch,
        compiler_params=_cparams(("arbitrary",)),
    )(*(args if exchange is None else args + [exchange]))


XA_SCALE = XA_DH ** -0.5


def _xa_probs(qh, kmh):
    sc = _dot_nt(qh, kmh) * XA_SCALE
    p = jnp.exp(sc - jnp.max(sc, axis=1, keepdims=True))
    return p * (1.0 / jnp.sum(p, axis=1, keepdims=True))


def _xattn_fwd(x1, nw, wq, kv, wo, *, tm):
    L = x1.shape[0]
    tm = min(tm, L)

    def body(x_ref, nw_ref, wq_ref, kv_ref, wo_ref, o_ref, ox_ref):
        x = x_ref[...]
        q = _dot(_rms_fwd(x, nw_ref[...]), wq_ref[...])
        for h in range(XA_HEADS):
            hs = slice(XA_DH * h, XA_DH * h + XA_DH)
            p = _xa_probs(q[:, hs], kv_ref[:, hs])
            ox_ref[:, hs] = _dot(p, kv_ref[:, D + XA_DH * h:D + XA_DH * h + XA_DH])
        o_ref[...] = x + _dot(ox_ref[...], wo_ref[...])

    full = lambda a: pl.BlockSpec(a.shape, lambda i: (0, 0))
    return pl.pallas_call(
        body, name="xattn_fwd", grid=(L // tm,),
        in_specs=[pl.BlockSpec((tm, D), lambda i: (i, 0)), full(nw), full(wq), full(kv), full(wo)],
        out_specs=pl.BlockSpec((tm, D), lambda i: (i, 0)),
        out_shape=jax.ShapeDtypeStruct((L, D), F32),
        scratch_shapes=[pltpu.VMEM((tm, D), F32)],
        compiler_params=_cparams(("parallel",)),
    )(x1, nw, wq, kv, wo)


def _xattn_bwd(x1, dx2, nw, wq, kv, wo, w_out, *, tm):
    L = x1.shape[0]
    tm = min(tm, L)

    def body(x_ref, dx2_ref, nw_ref, wq_ref, kv_ref, wo_ref, wout_ref, dx1_ref, dx1b_ref, h_ref, dq_ref, ox_ref,
             dya_ref, dob_ref, dkv_ref, dnw_ref, dqs_ref):
        @pl.when(pl.program_id(0) == 0)
        def _():
            dkv_ref[...] = jnp.zeros_like(dkv_ref)
            dnw_ref[...] = jnp.zeros_like(dnw_ref)

        x = x_ref[...]
        dx2 = dx2_ref[...]
        hn = _rms_fwd(x, nw_ref[...]).astype(BF16)
        h_ref[...] = hn
        q = _dot(hn, wq_ref[...])
        dox = _dot_nt(dx2, wo_ref[...])
        for h in range(XA_HEADS):
            hs = slice(XA_DH * h, XA_DH * h + XA_DH)
            vs = slice(D + XA_DH * h, D + XA_DH * h + XA_DH)
            qh, kmh, vmh, doxh = q[:, hs], kv_ref[:, hs], kv_ref[:, vs], dox[:, hs]
            p = _xa_probs(qh, kmh)
            ox_ref[:, hs] = _dot(p, vmh).astype(BF16)
            dp = _dot_nt(doxh, vmh)
            dkv_ref[:, vs] += _dot_tn(p, doxh)
            ds = p * (dp - jnp.sum(dp * p, axis=1, keepdims=True)) * XA_SCALE
            dqs_ref[:, hs] = _dot(ds, kmh)
            dkv_ref[:, hs] += _dot_tn(ds, qh)
        dq = dqs_ref[...]
        dq_ref[...] = dq.astype(BF16)
        dx, dnw = _rms_bwd(_dot_nt(dq, wq_ref[...]), x, nw_ref[...])
        dx1 = dx2 + dx
        dx1_ref[...] = dx1
        dx1b = dx1.astype(BF16)
        dx1b_ref[...] = dx1b
        dya_ref[...] = _dot_nt(dx1b, wout_ref[0:D, :]).astype(BF16)
        dob_ref[...] = _dot_nt(dx1b, wout_ref[D:2 * D, :]).astype(BF16)
        dnw_ref[...] += dnw

    full = lambda a: pl.BlockSpec(a.shape, lambda i: (0, 0))
    row = pl.BlockSpec((tm, D), lambda i: (i, 0))
    rowb = jax.ShapeDtypeStruct((L, D), BF16)
    return pl.pallas_call(
        body, name="xattn_bwd", grid=(L // tm,),
        in_specs=[row, row, full(nw), full(wq), full(kv), full(wo), full(w_out)],
        out_specs=[row] * 7 + [pl.BlockSpec((MEM_LEN, 2 * D), lambda i: (0, 0)), pl.BlockSpec((1, D), lambda i: (0, 0))],
        out_shape=[jax.ShapeDtypeStruct((L, D), F32)] + [rowb] * 6 + [
            jax.ShapeDtypeStruct((MEM_LEN, 2 * D), F32), jax.ShapeDtypeStruct((1, D), F32)],
        scratch_shapes=[pltpu.VMEM((tm, D), F32)],
        compiler_params=_cparams(("arbitrary",)),
    )(x1, dx2, nw, wq, kv, wo, w_out)


def _ffn_blk(tm, tf):
    return pl.BlockSpec((tm, tf), lambda i, j: (i, j))


def _interleave(wg, wu, tf):
    return jnp.stack([wg.reshape(FFN // tf, tf, D), wu.reshape(FFN // tf, tf, D)], axis=1).reshape(2 * FFN, D)


def _ffn_fwd(x2, nw, wgu, wd, *, tm, tf):
    L = x2.shape[0]
    tm = min(tm, L)
    nf = FFN // tf

    def body(x_ref, nw_ref, wgu_ref, wd_ref, o_ref, hn_ref, g_ref, u_ref, h_ref, acc_ref):
        j = pl.program_id(1)

        @pl.when(j == 0)
        def _():
            hn = _rms_fwd(x_ref[...], nw_ref[...]).astype(BF16)
            h_ref[...] = hn
            hn_ref[...] = hn
            acc_ref[...] = jnp.zeros_like(acc_ref)

        gu = _dot_nt(h_ref[...], wgu_ref[...]).astype(BF16)
        g, u = gu[:, 0:tf], gu[:, tf:2 * tf]
        g_ref[...] = g
        u_ref[...] = u
        acc_ref[...] += _dot(_silu(g.astype(F32)) * u.astype(F32), wd_ref[...])

        @pl.when(j == nf - 1)
        def _():
            o_ref[...] = x_ref[...] + acc_ref[...]

    row = pl.BlockSpec((tm, D), lambda i, j: (i, 0))
    wide = jax.ShapeDtypeStruct((L, FFN), BF16)
    return pl.pallas_call(
        body, name="ffn_fwd", grid=(L // tm, nf),
        in_specs=[row, pl.BlockSpec((1, D), lambda i, j: (0, 0)),
                  pl.BlockSpec((2 * tf, D), lambda i, j: (j, 0)), pl.BlockSpec((tf, D), lambda i, j: (j, 0))],
        out_specs=[row, row, _ffn_blk(tm, tf), _ffn_blk(tm, tf)],
        out_shape=[jax.ShapeDtypeStruct((L, D), F32), jax.ShapeDtypeStruct((L, D), BF16), wide, wide],
        scratch_shapes=[pltpu.VMEM((tm, D), BF16), pltpu.VMEM((tm, D), F32)],
        compiler_params=_cparams(("parallel", "arbitrary"), vmem_mb=58),
    )(x2, nw, wgu, wd)


def _ffn_bwd(x2, dx3, g, u, nw, wgu, wd, *, tm, tf):
    L = x2.shape[0]
    tm = min(tm, L)
    nf = FFN // tf

    def body(x_ref, dx3_ref, g_ref, u_ref, nw_ref, wgu_ref, wd_ref,
             dx2_ref, a_ref, dg_ref, du_ref, dnw_ref, d3_ref, acc_ref, da_ref, dgu_ref):
        i, j = pl.program_id(0), pl.program_id(1)

        @pl.when(j == 0)
        def _():
            d3_ref[...] = dx3_ref[...].astype(BF16)
            acc_ref[...] = jnp.zeros_like(acc_ref)
            da_ref[...] = jnp.zeros_like(da_ref)
            dgu_ref[...] = jnp.zeros_like(dgu_ref)

        @pl.when(jnp.logical_and(i == 0, j == 0))
        def _():
            dnw_ref[...] = jnp.zeros_like(dnw_ref)

        acc_ref[...] += _dot(dgu_ref[...], wgu_ref[...])
        g = g_ref[...].astype(F32)
        u = u_ref[...].astype(F32)
        s = _sigmoid_gate(g)
        sg = g * s
        da = da_ref[...]
        a_ref[...] = (sg * u).astype(BF16)
        dg = (da * u * (s + sg * (1.0 - s))).astype(BF16)
        du = (da * sg).astype(BF16)
        dg_ref[...] = dg
        du_ref[...] = du
        dgu_ref[...] = jnp.concatenate([dg, du], axis=1)
        da_ref[...] = _dot_nt(d3_ref[...], wd_ref[...])

        @pl.when(j == nf + 1)
        def _():
            dx, dnw = _rms_bwd(acc_ref[...], x_ref[...], nw_ref[...])
            dx2_ref[...] = dx3_ref[...] + dx
            dnw_ref[...] += dnw

    clamp = lambda j, d: jnp.clip(j - d, 0, nf - 1)
    row = pl.BlockSpec((tm, D), lambda i, j: (i, 0))
    blk = pl.BlockSpec((tm, tf), lambda i, j: (i, clamp(j, 1)))
    wide = jax.ShapeDtypeStruct((L, FFN), BF16)
    return pl.pallas_call(
        body, name="ffn_bwd", grid=(L // tm, nf + 2),
        in_specs=[row, row, blk, blk, pl.BlockSpec((1, D), lambda i, j: (0, 0)),
                  pl.BlockSpec((2 * tf, D), lambda i, j: (clamp(j, 2), 0)),
                  pl.BlockSpec((tf, D), lambda i, j: (clamp(j, 0), 0))],
        out_specs=[row, blk, blk, blk, pl.BlockSpec((1, D), lambda i, j: (0, 0))],
        out_shape=[jax.ShapeDtypeStruct((L, D), F32), wide, wide, wide, jax.ShapeDtypeStruct((1, D), F32)],
        scratch_shapes=[pltpu.VMEM((tm, D), BF16), pltpu.VMEM((tm, D), F32), pltpu.VMEM((tm, tf), F32),
                        pltpu.VMEM((tm, 2 * tf), BF16)],
        compiler_params=_cparams(("arbitrary", "arbitrary"), vmem_mb=56),
    )(x2, dx3, g, u, nw, wgu, wd)


def _final(x3, tgt, nw, *, tm):
    L = x3.shape[0]
    tm = min(tm, L)

    def body(x_ref, t_ref, nw_ref, dx_ref, dxb_ref, loss_ref, dnw_ref):
        @pl.when(pl.program_id(0) == 0)
        def _():
            loss_ref[...] = jnp.zeros_like(loss_ref)
            dnw_ref[...] = jnp.zeros_like(dnw_ref)

        x = x_ref[...]
        w = nw_ref[...]
        err = _rms_fwd(x, w) - t_ref[...]
        part = 0.5 * jnp.sum(jnp.sum(err * err, axis=1, keepdims=True), axis=0, keepdims=True) * (1.0 / D)
        loss_ref[...] += jnp.where(_iota2((1, 128), 1) == 0, part, 0.0)
        dx, dnw = _rms_bwd(err * (1.0 / D), x, w)
        dx_ref[...] = dx
        dxb_ref[...] = dx.astype(BF16)
        dnw_ref[...] += dnw

    row = pl.BlockSpec((tm, D), lambda i: (i, 0))
    return pl.pallas_call(
        body, name="final_loss", grid=(L // tm,),
        in_specs=[row, row, pl.BlockSpec((1, D), lambda i: (0, 0))],
        out_specs=[row, row, pl.BlockSpec((1, 128), lambda i: (0, 0)), pl.BlockSpec((1, D), lambda i: (0, 0))],
        out_shape=[jax.ShapeDtypeStruct((L, D), F32), jax.ShapeDtypeStruct((L, D), BF16),
                   jax.ShapeDtypeStruct((1, 128), F32), jax.ShapeDtypeStruct((1, D), F32)],
        compiler_params=_cparams(("arbitrary",)),
    )(x3, tgt, nw)


def _adam_update(g, w, m, v):
    c1 = 1.0 / (1.0 - ADAM_B1 ** ADAM_STEP)
    c2 = 1.0 / (1.0 - ADAM_B2 ** ADAM_STEP)
    nm = ADAM_B1 * m + (1.0 - ADAM_B1) * g
    nv = ADAM_B2 * v + (1.0 - ADAM_B2) * (g * g)
    return -ADAM_LR * ((nm * c1) / (jnp.sqrt(nv * c2) + ADAM_EPS) + ADAM_WD * w), nm, nv


def _adamw_small(tot, conv_w_grad, w, m, v):
    names = [n for n, _, _ in SMALL if n != "loss"]
    k = len(names)

    def body(tot_ref, cwg_ref, *refs):
        w_refs, m_refs, v_refs = refs[0:k], refs[k:2 * k], refs[2 * k:3 * k]
        g_refs, d_refs, nm_refs, nv_refs = (refs[(3 + j) * k:(4 + j) * k] for j in range(4))
        for i, n in enumerate(names):
            row, nr, nc = SMALL_AT[n]
            g = cwg_ref[...] if n == "conv_w" else tot_ref[row:row + nr, 0:nc]
            d, nm, nv = _adam_update(g, w_refs[i][...], m_refs[i][...], v_refs[i][...])
            g_refs[i][...] = g
            d_refs[i][...] = d
            nm_refs[i][...] = nm
            nv_refs[i][...] = nv

    sds = [jax.ShapeDtypeStruct(t.shape, F32) for t in w]
    res = pl.pallas_call(body, name="adamw_small", out_shape=sds * 4)(tot, conv_w_grad, *w, *m, *v)
    return res[0:k], res[k:2 * k], res[2 * k:3 * k], res[3 * k:4 * k]


def _adamw(parts, w, m, v, *, tr, name):
    n_parts, R, C = parts.shape
    tr = min(tr, R)

    def body(p_ref, w_ref, m_ref, v_ref, g_ref, d_ref, nm_ref, nv_ref):
        g = p_ref[0].astype(F32)
        for k in range(1, n_parts):
            g = g + p_ref[k].astype(F32)
        d, nm, nv = _adam_update(g, w_ref[...], m_ref[...], v_ref[...])
        g_ref[...] = g
        nm_ref[...] = nm
        nv_ref[...] = nv
        d_ref[...] = d

    blk = pl.BlockSpec((tr, C), lambda i: (i, 0))
    sds = jax.ShapeDtypeStruct((R, C), F32)
    return pl.pallas_call(
        body, name=name, grid=(R // tr,),
        in_specs=[pl.BlockSpec((n_parts, tr, C), lambda i: (0, i, 0)), blk, blk, blk],
        out_specs=[blk, blk, blk, blk], out_shape=[sds, sds, sds, sds],
        compiler_params=_cparams(("parallel",)),
    )(parts, w, m, v)


def _position():
    return lax.axis_index("x"), lax.axis_index("y"), lax.axis_index("c")


def _comm_scratch():
    return [pltpu.SemaphoreType.DMA((7,)), pltpu.SemaphoreType.DMA((7,)), pltpu.SemaphoreType.DMA]


class _Gather:
    def __init__(self, x_ref, out_ref, send_sems, recv_sems, local_sem):
        x, y, c = _position()
        me, sibling = (x, y, c), (x, y, 1 - c)
        chips = [(1 - x, y), (x, 1 - y), (1 - x, 1 - y)]

        def rows(px, py, pc):
            return out_ref.at[4 * px + 2 * py + pc]

        def copy(k, block, to, src=None):
            return pltpu.make_async_remote_copy(
                src_ref=rows(*block) if src is None else src, dst_ref=rows(*block),
                send_sem=send_sems.at[k], recv_sem=recv_sems.at[k], device_id=to, device_id_type=MESH)

        self.mine = pltpu.make_async_copy(x_ref, rows(*me), local_sem)
        self.first = [copy(0, me, sibling, src=x_ref)]
        self.first += [copy(1 + j, me, (*chip, c), src=x_ref) for j, chip in enumerate(chips)]
        self.passed = [copy(4 + j, (*chip, c), sibling) for j, chip in enumerate(chips)]
        self.from_chips = [copy(1 + j, (*chip, c), me) for j, chip in enumerate(chips)]
        self.from_sibling = [copy(0, sibling, me)] + [copy(4 + j, (*chip, 1 - c), me) for j, chip in enumerate(chips)]

    def start(self):
        self.mine.start()
        for cp in self.first:
            cp.start()

    def forward(self):
        for got, cp in zip(self.from_chips, self.passed):
            got.wait_recv()
            cp.start()

    def finish(self):
        for got in self.from_sibling:
            got.wait_recv()
        for cp in self.first + self.passed:
            cp.wait_send()
        self.mine.wait()


class _Exchange:
    def __init__(self, g_ref, out_ref, send_sems, recv_sems, local_sem):
        x, y, c = _position()
        me = 4 * x + 2 * y + c
        self.mine = pltpu.make_async_copy(g_ref.at[me], out_ref.at[me], local_sem)
        self.copies = []
        for k in range(1, N_DEV):
            px = 1 - x if k & 4 else x
            py = 1 - y if k & 2 else y
            pc = 1 - c if k & 1 else c
            self.copies.append(pltpu.make_async_remote_copy(
                src_ref=g_ref.at[4 * px + 2 * py + pc], dst_ref=out_ref.at[me],
                send_sem=send_sems.at[k - 1], recv_sem=recv_sems.at[k - 1],
                device_id=(px, py, pc), device_id_type=MESH))

    def start(self):
        self.mine.start()
        for cp in self.copies:
            cp.start()

    def finish(self):
        for cp in self.copies:
            cp.wait()
        self.mine.wait()


def _allgather(xp):
    R, C = xp.shape

    def body(x_ref, out_ref, send_sems, recv_sems, local_sem):
        g = _Gather(x_ref, out_ref, send_sems, recv_sems, local_sem)
        g.start()
        g.forward()
        g.finish()

    return pl.pallas_call(
        body, name="allgather_w_in",
        out_shape=jax.ShapeDtypeStruct((N_DEV, R, C), xp.dtype),
        in_specs=[pl.BlockSpec(memory_space=pltpu.HBM)], out_specs=pl.BlockSpec(memory_space=pltpu.HBM),
        scratch_shapes=_comm_scratch(),
    )(xp)


def _carried(body, n_in, n_out, payload, phases):
    if payload is None:
        return body
    kind = payload[0]

    def new_body(*refs):
        ins, src_ref = refs[:n_in], refs[n_in]
        outs, dst_ref = refs[n_in + 1:n_in + 1 + n_out], refs[n_in + 1 + n_out]
        scratch, sems = refs[n_in + 2 + n_out:-3], refs[-3:]

        def run(before):
            for when, action, is_before in phases:
                if is_before == before:
                    @pl.when(when())
                    def _():
                        action(kind(src_ref, dst_ref, *sems))

        run(True)
        body(*ins, *outs, *scratch)
        run(False)

    return new_body


def _carried_specs(payload, in_specs, out_specs, out_shape, scratch):
    if payload is None:
        return in_specs, out_specs, out_shape, scratch
    kind, arr = payload
    landing = (N_DEV,) + arr.shape if kind is _Gather else arr.shape
    hbm = pl.BlockSpec(memory_space=pltpu.HBM)
    return (in_specs + [hbm], out_specs + [hbm], out_shape + [jax.ShapeDtypeStruct(landing, arr.dtype)],
            scratch + _comm_scratch())


def _small_allreduce(sp):
    R, C = sp.shape

    def body(s_ref, out_ref, buf_ref, send_sems, recv_sems):
        x, y, c = _position()
        me = 4 * x + 2 * y + c
        buf_ref[me] = s_ref[...]
        copies = []
        for k in range(1, N_DEV):
            px = 1 - x if k & 4 else x
            py = 1 - y if k & 2 else y
            pc = 1 - c if k & 1 else c
            cp = pltpu.make_async_remote_copy(
                src_ref=s_ref, dst_ref=buf_ref.at[me], send_sem=send_sems.at[k - 1], recv_sem=recv_sems.at[k - 1],
                device_id=(px, py, pc), device_id_type=MESH)
            cp.start()
            copies.append(cp)
        for cp in copies:
            cp.wait()
        tot = buf_ref[0]
        for k in range(1, N_DEV):
            tot = tot + buf_ref[k]
        out_ref[...] = tot

    return pl.pallas_call(
        body, name="allreduce_small",
        out_shape=jax.ShapeDtypeStruct((R, C), F32),
        in_specs=[pl.BlockSpec(memory_space=pltpu.VMEM)], out_specs=pl.BlockSpec(memory_space=pltpu.VMEM),
        scratch_shapes=[pltpu.VMEM((N_DEV, R, C), F32), pltpu.SemaphoreType.DMA((7,)), pltpu.SemaphoreType.DMA((7,))],
    )(sp)


def _local_step(x, mem, tgt, wt, small, dist=None):
    w_in = wt["w_in"]
    zpad = jnp.zeros((WB - D - SSD_HEADS, D), BF16)
    w_a = jnp.concatenate([w_in[2576:3600], w_in[4624:6672], w_in[3600:4624]], axis=0)
    w_b = jnp.concatenate([w_in[0:D], w_in[2560:2576], zpad], axis=0)
    w_c = w_in[D:2560]
    a_log, d_skip = small["a_log"], small["d_skip"]
    avec = jnp.pad(-jnp.exp(a_log), ((0, 0), (0, 128 - SSD_HEADS)))
    aexp = jnp.repeat(-jnp.exp(a_log), SSD_P, axis=1)
    dexp = jnp.repeat(d_skip, SSD_P, axis=1)
    dtb = jnp.pad(small["dt_bias"], ((0, 0), (0, 128 - SSD_HEADS)))
    emat = (lax.broadcasted_iota(jnp.int32, (128, D), 0) == lax.broadcasted_iota(jnp.int32, (128, D), 1) // SSD_P
            ).astype(F32)
    hg_nwx = jnp.tile(small["hg_norm_w"], (1, HG_HEADS))
    hgl = small["hg_lower_bounds"]
    nfw = small["norm_final_w"].reshape(1, D)

    received = {}
    pieces = (lambda group, grads: None) if dist is None else dist["pieces"]
    pa, phf, pz, pdt, pc, hn_mix, *got = _inproj(x, small["norm_mix_w"], w_a, w_b, w_c, tm=256,
                                                 gather=None if dist is None else dist["rest_pack"])
    if got:
        wt = {**wt, **dist["unpack_rest"](got[0])}
    xbc = _conv_fwd(pc, small["conv_w"], small["conv_b"], tm=1024)
    ya, ypre, ssd_sin = _ssd_fwd(xbc, pz, pdt, dtb, avec, aexp, dexp, small["ssd_norm_w"], emat)
    ob, opre, hg_sin = _hgrn_fwd(pa, phf, hgl, hg_nwx)
    x1 = _mm2_res(x, ya, ob, wt["w_out"], tm=512, name="outproj")
    kvb, mn = _norm_mm(mem, small["norm_mem_w"], wt["xa_wkv"], tm=256, tn=1024, name="mem_kv", emit_h=True,
                       out_dtype=BF16)
    x2 = _xattn_fwd(x1, small["norm_xa_w"], wt["xa_wq"], kvb, wt["xa_wo"], tm=1024)
    x3, hn_ffn, gate, up = _ffn_fwd(x2, small["norm_ffn_w"], _interleave(wt["ffn_w_gate"], wt["ffn_w_up"], 1408),
                                    wt["ffn_w_down"], tm=512, tf=1408)

    dx3, dx3b, loss, g_nf = _final(x3, tgt, nfw, tm=1024)
    dx2, act, dg, du, g_nffn = _ffn_bwd(x2, dx3, gate, up, small["norm_ffn_w"],
                                        _interleave(wt["ffn_w_gate"], wt["ffn_w_up"], 256), wt["ffn_w_down"],
                                        tm=1024, tf=256)
    g_wg = _dw(dg, hn_ffn, tM=1408, tN=1024, tl=2048, name="dw_gate")
    g_wu = _dw(du, hn_ffn, tM=1408, tN=1024, tl=2048, name="dw_up")
    g_wd = _dw(act, dx3b, tM=1408, tN=1024, tl=2048, name="dw_down")
    dx1, dx1b, hn_xa, dq, ox, dya, dob, dkv, g_nxa = _xattn_bwd(
        x1, dx2, small["norm_xa_w"], wt["xa_wq"], kvb, wt["xa_wo"], wt["w_out"], tm=512)
    dkvb = dkv.astype(BF16)
    g_wq = _dw(hn_xa, dq, tM=1024, tN=1024, tl=2048, name="dw_q")
    g_wo = _dw(ox, dx2, tM=1024, tN=1024, tl=2048, name="dw_o")
    g_wkv = _dw(dkvb, mn, tM=1024, tN=1024, tl=256, name="dw_kv")
    _, g_nmem = _mm_normbwd([dkvb], [wt["xa_wkv"]], mem, small["norm_mem_w"], None, tm=256, name="mem_bwd")
    g_wout = jnp.concatenate([_dw(ya, dx1b, tM=1024, tN=1024, tl=2048, name="dw_out_a"),
                              _dw(ob, dx1b, tM=1024, tN=1024, tl=2048, name="dw_out_b")], axis=0)
    ffn_grads = {"ffn_w_gate": g_wg, "ffn_w_up": g_wu, "ffn_w_down": g_wd}
    mid_grads = {"w_out": g_wout, "xa_wq": g_wq, "xa_wkv": g_wkv, "xa_wo": g_wo}
    dpa, g_hgl, g_hgn_x, *got = _hgrn_bwd(pa, phf, opre, hg_sin, dob, hgl, hg_nwx,
                                          exchange=pieces("ffn", ffn_grads))
    received["ffn"] = got[0] if got else None
    dpb, dxbc, g_ssdn, g_dtb, g_alog, g_dx, *got = _ssd_bwd(
        xbc, pz, pdt, ypre, ssd_sin, dya, dtb, avec, aexp, dexp, small["ssd_norm_w"], emat,
        exchange=pieces("mid", mid_grads))
    received["mid"] = got[0] if got else None
    dpc, g_cw, g_cb = _conv_bwd(pc, dxbc, small["conv_w"], small["conv_b"], tm=1024)
    g_wa = _dw(dpa, hn_mix, tM=1024, tN=1024, tl=2048, name="dw_in_a")
    g_wb = _dw(dpb, hn_mix, tM=384, tN=1024, tl=2048, name="dw_in_b")
    g_wc = _dw(dpc, hn_mix, tM=512, tN=1024, tl=2048, name="dw_in_c")
    g_win = jnp.concatenate([g_wb[0:D], g_wc, g_wb[D:D + SSD_HEADS], g_wa[0:D], g_wa[3 * D:4 * D], g_wa[D:3 * D]],
                            axis=0)
    grad_x, g_nmix, *got = _mm_normbwd([dpa, dpb, dpc], [w_a, w_b, w_c], x, small["norm_mix_w"], dx1, tm=256,
                                         name="inproj_bwd", exchange=pieces("in", {"w_in": g_win}))
    received["in"] = got[0] if got else None

    big = {"w_in": g_win, **mid_grads, **ffn_grads}
    smallg = {
        "norm_mix_w": g_nmix, "conv_w": g_cw, "conv_b": g_cb, "dt_bias": g_dtb[:, 0:SSD_HEADS],
        "a_log": g_alog[:, 0:SSD_HEADS], "d_skip": g_dx.reshape(SSD_HEADS, SSD_P).sum(axis=1).reshape(1, SSD_HEADS),
        "ssd_norm_w": g_ssdn, "hg_lower_bounds": g_hgl,
        "hg_norm_w": g_hgn_x.reshape(HG_HEADS, HG_K).sum(axis=0).reshape(1, HG_K),
        "norm_xa_w": g_nxa, "norm_mem_w": g_nmem, "norm_ffn_w": g_nffn, "norm_final_w": g_nf,
        "loss": loss[:, 0:1]}
    return grad_x, big, smallg, received


COL_SHARDED = ("w_in", "xa_wkv", "ffn_w_gate", "ffn_w_up")


def _pad_rows(t, rows):
    return jnp.pad(t, [(0, 0)] * (t.ndim - 2) + [(0, rows - t.shape[-2]), (0, 0)])


def _group_fill(parts, group, lead, dtype):
    used = sum(p.shape[-2] for p in parts)
    if used < GROUP_ROWS[group]:
        parts.append(jnp.zeros(lead + (GROUP_ROWS[group] - used, D), dtype))
    return parts


def _pack_group(shards, group, dtype, extra=None):
    parts = [_pad_rows(shards[n].astype(dtype).reshape(r, D), _rows_padded(r)) for n, r in GROUPS[group]]
    if extra is not None:
        parts.append(extra)
    return jnp.concatenate(_group_fill(parts, group, (), dtype), axis=0)


def _unpack_group(packed, group, shapes):
    out, off = {}, 0
    for n, r in GROUPS[group]:
        t = packed[off:off + r]
        out[n] = (t.T if n in COL_SHARDED else t).reshape(shapes[n])
        off += _rows_padded(r)
    return out


def _row_shards(d):
    return {n: d[n][0].T if n in COL_SHARDED else d[n][0] for n in BIG}


def _unpack_gathered(gath, groups):
    out, base = {}, 0
    for group in groups:
        off = base
        for n, r in GROUPS[group]:
            out[n] = gath[:, off:off + r].reshape(N_DEV * r, D)
            off += _rows_padded(r)
        base += GROUP_ROWS[group]
    return out


def _grad_pieces(group, grads):
    dtype = PIECE_DTYPE[group]
    parts = [_pad_rows(grads[n].reshape(N_DEV, r, D).astype(dtype), _rows_padded(r)) for n, r in GROUPS[group]]
    return jnp.concatenate(_group_fill(parts, group, (N_DEV,), dtype), axis=1)


def _pack_small(vals):
    tot = None
    for n, r, c in SMALL:
        row = SMALL_AT[n][0]
        part = jnp.pad(vals[n].reshape(r, c), ((row, SMALL_ROWS - row - r), (0, SMALL_COLS - c)))
        tot = part if tot is None else tot + part
    return tot


WEIGHTS = ['norm_mix_w', 'w_in', 'conv_w', 'conv_b', 'dt_bias', 'a_log', 'd_skip', 'ssd_norm_w', 'hg_lower_bounds',
           'hg_norm_w', 'w_out', 'norm_xa_w', 'norm_mem_w', 'xa_wq', 'xa_wkv', 'xa_wo', 'norm_ffn_w', 'ffn_w_gate',
           'ffn_w_up', 'ffn_w_down', 'norm_final_w']
BIG = tuple(n for n, _ in PACK)


def kernel(x, mem, norm_mix_w, w_in, conv_w, conv_b, dt_bias, a_log, d_skip, ssd_norm_w, hg_lower_bounds, hg_norm_w, w_out, norm_xa_w, norm_mem_w, xa_wq, xa_wkv, xa_wo, norm_ffn_w, ffn_w_gate, ffn_w_up, ffn_w_down, norm_final_w, loss_target, m_norm_mix_w, m_w_in, m_conv_w, m_conv_b, m_dt_bias, m_a_log, m_d_skip, m_ssd_norm_w, m_hg_lower_bounds, m_hg_norm_w, m_w_out, m_norm_xa_w, m_norm_mem_w, m_xa_wq, m_xa_wkv, m_xa_wo, m_norm_ffn_w, m_ffn_w_gate, m_ffn_w_up, m_ffn_w_down, m_norm_final_w, v_norm_mix_w, v_w_in, v_conv_w, v_conv_b, v_dt_bias, v_a_log, v_d_skip, v_ssd_norm_w, v_hg_lower_bounds, v_hg_norm_w, v_w_out, v_norm_xa_w, v_norm_mem_w, v_xa_wq, v_xa_wkv, v_xa_wo, v_norm_ffn_w, v_ffn_w_gate, v_ffn_w_up, v_ffn_w_down, v_norm_final_w):
    args = dict(locals())
    w = {n: args[n] for n in WEIGHTS}
    mo = {n: args["m_" + n] for n in WEIGHTS}
    vo = {n: args["v_" + n] for n in WEIGHTS}
    me = 4 * lax.axis_index("x") + 2 * lax.axis_index("y") + lax.axis_index("c")

    big_sh = _row_shards(w)
    cw_bits = lax.bitcast_convert_type(conv_w[0], BF16).reshape(-1)
    cw_rows = jnp.pad(cw_bits, (0, CONV_BITS_ROWS * D - cw_bits.shape[0])).reshape(CONV_BITS_ROWS, D)
    gath = _allgather(_pack_group(big_sh, "in", BF16, extra=cw_rows))
    wt = _unpack_gathered(gath, ("in",))
    off = _rows_padded(GROUPS["in"][0][1])
    cw_all = lax.bitcast_convert_type(gath[:, off:off + 2].reshape(N_DEV, 2 * D)[:, 0:1536].reshape(N_DEV, 4, 192, 2),
                                      F32)
    conv_w_full = cw_all.transpose(1, 0, 2).reshape(4, 1536)

    small = {n: w[n][0] if w[n].ndim == 3 else w[n] for n in WEIGHTS if n not in BIG}
    small["conv_w"] = conv_w_full
    small["hg_lower_bounds"] = hg_lower_bounds
    dist = {"rest_pack": jnp.concatenate([_pack_group(big_sh, "mid", BF16), _pack_group(big_sh, "ffn", BF16)], axis=0),
            "unpack_rest": lambda g: _unpack_gathered(g, ("mid", "ffn")),
            "pieces": _grad_pieces}
    grad_x, _, gsmall, received = _local_step(x[0], mem[0], loss_target[0], wt, small, dist)

    shapes = {n: w[n].shape for n in BIG}
    m_sh, v_sh = _row_shards(mo), _row_shards(vo)
    out_g, out_d, out_m, out_v = {}, {}, {}, {}
    for group in GROUPS:
        wp = _pack_group(big_sh, group, F32)
        mp = _pack_group(m_sh, group, F32)
        vp = _pack_group(v_sh, group, F32)
        packed = _adamw(received[group], wp, mp, vp, tr=ADAM_ROWS[group], name="adamw_" + group)
        for dst, src in zip((out_g, out_d, out_m, out_v), packed):
            dst.update(_unpack_group(src, group, shapes))

    tot = _small_allreduce(_pack_small(gsmall))
    loss = tot[SMALL_AT["loss"][0], 0]
    cw_row = SMALL_AT["conv_w"][0]
    conv_w_grad = lax.dynamic_slice(tot, (cw_row, me * 192), (4, 192))
    names = [n for n, _, _ in SMALL if n != "loss"]
    as2d = lambda t: t.reshape(t.shape[-2:] if t.ndim > 1 else (1, t.shape[0]))
    small_out = _adamw_small(tot, conv_w_grad, *[[as2d(d[n]) for n in names] for d in (w, mo, vo)])
    for dst, src in zip((out_g, out_d, out_m, out_v), small_out):
        dst.update({n: t.reshape(w[n].shape) for n, t in zip(names, src)})
    return (loss, grad_x[None], *[out_g[n] for n in WEIGHTS], *[out_d[n] for n in WEIGHTS],
            *[out_m[n] for n in WEIGHTS], *[out_v[n] for n in WEIGHTS])
```
